```python
import jax
import jax.numpy as jnp
from jax import lax
import numpy as np

D_MODEL = 1024
BATCH = 8
SEQ = 2048
DEPTH = 2

GRID_W = 64
CTX_LEN = 256

N_MLSTM_HEADS = 4
MLSTM_WIDTH = D_MODEL // 2
MLSTM_HEAD_DIM = MLSTM_WIDTH // N_MLSTM_HEADS
N_ROW_HEADS = N_MLSTM_HEADS // 2
MLSTM_CHUNK = 64
CONV_CH = D_MODEL // 4
CONV_K = 31
FOURIER_CH = D_MODEL // 4
N_FOURIER_GROUPS = 4
FOURIER_GROUP_CH = FOURIER_CH // N_FOURIER_GROUPS
MIX_WIDTH = MLSTM_WIDTH + CONV_CH + FOURIER_CH
N_GATES = 4
QKVG_COLS = 3 * MLSTM_WIDTH + N_GATES * N_MLSTM_HEADS
IN_COLS = QKVG_COLS + MLSTM_WIDTH + 2 * CONV_CH + FOURIER_CH
N_EXPERTS = 16
EC_CAPACITY_FACTOR = 2
EXPERT_HIDDEN = D_MODEL
N_ADA = 6
EPS = 1e-6

kernel_name = 'hybrid_mlstm_conv_fourier_ecmoe_dit'


def _rms(x, gain):
    xf = x.astype(jnp.float32)
    return xf * lax.rsqrt(jnp.mean(xf * xf, axis=-1, keepdims=True) + EPS) * gain.astype(jnp.float32)


def modulated_rmsnorm(x, gain, shift, scale):
    y = _rms(x, gain) * (1.0 + scale.astype(jnp.float32)) + shift.astype(jnp.float32)
    return y.astype(x.dtype)


def layernorm(x, gain, bias):
    xf = x.astype(jnp.float32)
    mu = jnp.mean(xf, axis=-1, keepdims=True)
    var = jnp.mean(jnp.square(xf - mu), axis=-1, keepdims=True)
    y = (xf - mu) * lax.rsqrt(var + EPS) * gain.astype(jnp.float32) + bias.astype(jnp.float32)
    return y.astype(x.dtype)


def _raster_to_colmajor(a, rows):
    b, h, t, f = a.shape
    return a.reshape(b, h, rows, GRID_W, f).swapaxes(2, 3).reshape(b, h, t, f)


def _colmajor_to_raster(a, rows):
    b, h, t, f = a.shape
    return a.reshape(b, h, GRID_W, rows, f).swapaxes(2, 3).reshape(b, h, t, f)


def _to_chunks(a):
    b, h, t = a.shape[:3]
    a = a.reshape((b, h, t // MLSTM_CHUNK, MLSTM_CHUNK) + a.shape[3:])
    return jnp.moveaxis(a, 2, 0)


def _from_chunks(a):
    a = jnp.moveaxis(a, 0, 2)
    return a.reshape((a.shape[0], a.shape[1], -1) + a.shape[4:])


def _chunk_outputs(q, k, v, i_pre, b, c_st, n_st, m_st):
    L = q.shape[-2]
    lower = jnp.tril(jnp.ones((L, L), dtype=bool))
    dlog = jnp.where(lower, b[..., :, None] - b[..., None, :] + i_pre[..., None, :], -jnp.inf)
    inter = b + m_st[..., None]
    m_t = jnp.maximum(inter, jnp.max(dlog, axis=-1))
    s = jnp.einsum('...td,...sd->...ts', q, k) * jnp.exp(dlog - m_t[..., None])
    w_inter = jnp.exp(inter - m_t)
    num = w_inter[..., None] * jnp.einsum('...ed,...td->...te', c_st, q) + jnp.einsum('...ts,...se->...te', s, v)
    den = w_inter * jnp.einsum('...td,...d->...t', q, n_st) + jnp.sum(s, axis=-1)
    return num / jnp.maximum(jnp.abs(den), jnp.exp(-m_t))[..., None]


def mlstm_direction(q, k, v, i_pre, log_f, state0, with_outputs):
    qc, kc, vc, ic = (_to_chunks(a) for a in (q, k, v, i_pre))
    bc = jnp.cumsum(_to_chunks(log_f), axis=-1)

    def step(state, inp):
        c_st, n_st, m_st = state
        kx, vx, ix, bx = inp
        b_last = bx[..., -1]
        wlog = b_last[..., None] - bx + ix
        m_new = jnp.maximum(b_last + m_st, jnp.max(wlog, axis=-1))
        decay = jnp.exp(b_last + m_st - m_new)
        w = jnp.exp(wlog - m_new[..., None])
        c_new = decay[..., None, None] * c_st + jnp.einsum('bhs,bhse,bhsd->bhed', w, vx, kx)
        n_new = decay[..., None] * n_st + jnp.einsum('bhs,bhsd->bhd', w, kx)
        return (c_new, n_new, m_new), (state if with_outputs else None)

    final, starts = lax.scan(step, state0, (kc, vc, ic, bc))
    if not with_outputs:
        return None, final
    h = _chunk_outputs(qc, kc, vc, ic, bc, *starts)
    return _from_chunks(h), final


def mlstm_bidir(q, k, v, gates, init_fwd, init_bwd, with_outputs):
    flip = lambda a: jnp.flip(a, axis=2)
    h_f, st_f = mlstm_direction(q, k, v, gates[..., 0], jax.nn.log_sigmoid(gates[..., 1]), init_fwd, with_outputs)
    h_b, st_b = mlstm_direction(flip(q), flip(k), flip(v), flip(gates[..., 2]),
                                flip(jax.nn.log_sigmoid(gates[..., 3])), init_bwd, with_outputs)
    h = h_f + flip(h_b) if with_outputs else None
    return h, st_f, st_b


def zero_state(bsz):
    return (jnp.zeros((bsz, N_MLSTM_HEADS, MLSTM_HEAD_DIM, MLSTM_HEAD_DIM), jnp.float32),
            jnp.zeros((bsz, N_MLSTM_HEADS, MLSTM_HEAD_DIM), jnp.float32),
            jnp.zeros((bsz, N_MLSTM_HEADS), jnp.float32))


def mlstm_heads(u_qkvg, b_gates, rows):
    bsz, t, _ = u_qkvg.shape
    per_head = lambda a: a.reshape(bsz, t, N_MLSTM_HEADS, -1).transpose(0, 2, 1, 3)
    q, k, v, g = jnp.split(u_qkvg.astype(jnp.float32), [MLSTM_WIDTH, 2 * MLSTM_WIDTH, 3 * MLSTM_WIDTH], axis=-1)
    z = jnp.concatenate([per_head(q), per_head(k) * MLSTM_HEAD_DIM ** -0.5, per_head(v),
                         per_head(g) + b_gates.astype(jnp.float32)[None, :, None, :]], axis=-1)
    if rows is not None:
        z = jnp.concatenate([z[:, :N_ROW_HEADS], _raster_to_colmajor(z[:, N_ROW_HEADS:], rows)], axis=1)
    return jnp.split(z, [MLSTM_HEAD_DIM, 2 * MLSTM_HEAD_DIM, 3 * MLSTM_HEAD_DIM], axis=-1)


def conformer_conv(a, g, conv_w, conv_b, ln_g, ln_b):
    y = a * jax.nn.sigmoid(g)
    y = lax.conv_general_dilated(y, conv_w[:, None, :], window_strides=(1,),
                                 padding=[(CONV_K // 2, CONV_K // 2)],
                                 dimension_numbers=('NWC', 'WIO', 'NWC'),
                                 feature_group_count=y.shape[-1]) + conv_b
    return jax.nn.silu(layernorm(y, ln_g, ln_b))


def fourier_mix(f):
    bsz, t, _ = f.shape
    z = f.astype(jnp.float32).reshape(bsz, t, N_FOURIER_GROUPS, FOURIER_GROUP_CH)
    z = jnp.fft.fft2(z, axes=(1, 3), norm='ortho').real
    return z.reshape(bsz, t, FOURIER_CH).astype(f.dtype)


def token_mixer(u, b_gates, g_hnorm, conv_w, conv_b, conv_ln_g, conv_ln_b, init_fwd, init_bwd, rows):
    bsz, t, _ = u.shape
    o0 = QKVG_COLS + MLSTM_WIDTH
    u_qkvg, o, ca, cg, fr = jnp.split(u, [QKVG_COLS, o0, o0 + CONV_CH, o0 + 2 * CONV_CH], axis=-1)
    q, k, v, g = mlstm_heads(u_qkvg, b_gates, rows)
    h, st_f, st_b = mlstm_bidir(q, k, v, g, init_fwd, init_bwd, True)
    if rows is not None:
        h = jnp.concatenate([h[:, :N_ROW_HEADS], _colmajor_to_raster(h[:, N_ROW_HEADS:], rows)], axis=1)
    h = h * lax.rsqrt(jnp.mean(h * h, axis=-1, keepdims=True) + EPS) \
        * g_hnorm.astype(jnp.float32).reshape(N_MLSTM_HEADS, 1, MLSTM_HEAD_DIM)
    h = h.transpose(0, 2, 1, 3).reshape(bsz, t, MLSTM_WIDTH).astype(u.dtype) * jax.nn.sigmoid(o)
    y = jnp.concatenate([h, conformer_conv(ca, cg, conv_w, conv_b, conv_ln_g, conv_ln_b), fourier_mix(fr)], axis=-1)
    return y, st_f, st_b


def expert_choice_ffn(xn, w_router, w_gate, w_up, w_down):
    bsz, n, _ = xn.shape
    cap = EC_CAPACITY_FACTOR * n // N_EXPERTS
    aff = jax.nn.softmax(jnp.einsum('bnd,de->bne', xn, w_router).astype(jnp.float32), axis=-1)
    gate, idx = lax.top_k(jnp.swapaxes(aff, 1, 2), cap)
    bidx = jnp.arange(bsz)[:, None, None]
    xs = xn[bidx, idx]
    hid = jax.nn.silu(jnp.einsum('becd,edf->becf', xs, w_gate)) * jnp.einsum('becd,edf->becf', xs, w_up)
    ys = jnp.einsum('becf,efd->becd', hid, w_down) * gate[..., None].astype(xn.dtype)
    return jnp.zeros_like(xn).at[bidx, idx].add(ys)


def setup_inputs(seed: int = 0) -> dict:
    key = jax.random.key(seed)
    ks = jax.random.split(key, 24)
    nrm = lambda k, shape, s: jax.random.normal(k, shape, jnp.float32) * s
    D, L, H = D_MODEL, DEPTH, N_MLSTM_HEADS
    ig = nrm(ks[8], (L, H, 2), 0.1)
    fg = jnp.linspace(3.0, 6.0, H)[None, :, None] + nrm(ks[9], (L, H, 2), 0.1)
    b_gates = jnp.stack([ig[..., 0], fg[..., 0], ig[..., 1], fg[..., 1]], axis=-1)
    return {
        'x': nrm(ks[0], (BATCH, SEQ, D), 1.0),
        'c': nrm(ks[1], (BATCH, D), 1.0),
        'ctx': nrm(ks[2], (BATCH, CTX_LEN, D), 1.0),
        'c_ctx': nrm(ks[3], (D,), 1.0),
        'w_ada': nrm(ks[4], (L, D, N_ADA * D), 0.5 * D ** -0.5),
        'b_ada': nrm(ks[5], (L, N_ADA * D), 0.02),
        'g_norm1': 1.0 + nrm(ks[6], (L, D), 0.02),
        'w_in': nrm(ks[7], (L, D, IN_COLS), D ** -0.5),
        'b_gates': b_gates,
        'g_hnorm': 1.0 + nrm(ks[10], (L, MLSTM_WIDTH), 0.02),
        'conv_w': nrm(ks[11], (L, CONV_K, CONV_CH), CONV_K ** -0.5),
        'conv_b': nrm(ks[12], (L, CONV_CH), 0.02),
        'conv_ln_g': 1.0 + nrm(ks[13], (L, CONV_CH), 0.02),
        'conv_ln_b': nrm(ks[14], (L, CONV_CH), 0.02),
        'w_out': nrm(ks[15], (L, MIX_WIDTH, D), MIX_WIDTH ** -0.5),
        'g_norm2': 1.0 + nrm(ks[16], (L, D), 0.02),
        'w_router': nrm(ks[17], (L, D, N_EXPERTS), D ** -0.5),
        'w_e_gate': nrm(ks[18], (L, N_EXPERTS, D, EXPERT_HIDDEN), D ** -0.5),
        'w_e_up': nrm(ks[19], (L, N_EXPERTS, D, EXPERT_HIDDEN), D ** -0.5),
        'w_e_down': nrm(ks[20], (L, N_EXPERTS, EXPERT_HIDDEN, D), EXPERT_HIDDEN ** -0.5),
        'g_final': 1.0 + nrm(ks[21], (D,), 0.02),
    }


def reference(x, c, ctx, c_ctx, w_ada, b_ada, g_norm1, w_in, b_gates, g_hnorm, conv_w, conv_b,
              conv_ln_g, conv_ln_b, w_out, g_norm2, w_router, w_e_gate, w_e_up, w_e_down, g_final):
    bsz, t, _ = x.shape
    rows = t // GRID_W
    h_ctx = ctx
    silu_c = jax.nn.silu(c)
    silu_cc = jax.nn.silu(c_ctx)
    for l in range(DEPTH):
        last = l == DEPTH - 1
        mod = silu_c @ w_ada[l] + b_ada[l]
        mod_c = silu_cc @ w_ada[l] + b_ada[l]
        sh1, sc1, gt1, sh2, sc2, gt2 = jnp.split(mod[:, None, :], N_ADA, axis=-1)
        csh1, csc1, cgt1, csh2, csc2, cgt2 = jnp.split(mod_c, N_ADA)
        init = zero_state(h_ctx.shape[0])
        xn_c = modulated_rmsnorm(h_ctx, g_norm1[l], csh1, csc1)
        if last:
            q, k, v, g = mlstm_heads(xn_c @ w_in[l][:, :QKVG_COLS], b_gates[l], None)
            _, st_f, st_b = mlstm_bidir(q, k, v, g, init, init, False)
        else:
            mix_c, st_f, st_b = token_mixer(xn_c @ w_in[l], b_gates[l], g_hnorm[l], conv_w[l], conv_b[l],
                                            conv_ln_g[l], conv_ln_b[l], init, init, None)
            h_ctx = h_ctx + cgt1 * (mix_c @ w_out[l])
            h_ctx = h_ctx + cgt2 * expert_choice_ffn(modulated_rmsnorm(h_ctx, g_norm2[l], csh2, csc2),
                                                     w_router[l], w_e_gate[l], w_e_up[l], w_e_down[l])
        xn = modulated_rmsnorm(x, g_norm1[l], sh1, sc1)
        mix_x, _, _ = token_mixer(xn @ w_in[l], b_gates[l], g_hnorm[l], conv_w[l], conv_b[l],
                                  conv_ln_g[l], conv_ln_b[l], st_f, st_b, rows)
        x = x + gt1 * (mix_x @ w_out[l])
        x = x + gt2 * expert_choice_ffn(modulated_rmsnorm(x, g_norm2[l], sh2, sc2),
                                        w_router[l], w_e_gate[l], w_e_up[l], w_e_down[l])
    return _rms(x, g_final).astype(x.dtype)
```

```python
import functools

import jax
import jax.numpy as jnp
import numpy as np
from jax import lax
from jax.experimental import pallas as pl
from jax.experimental.pallas import tpu as pltpu

F32 = jnp.float32
BF16 = jnp.bfloat16
I32 = jnp.int32

D_MODEL = 1024
GRID_W = 64
N_HEADS = 4
HEAD_DIM = 128
MLSTM_WIDTH = N_HEADS * HEAD_DIM
N_ROW_HEADS = N_HEADS // 2
CONV_CH = 256
CONV_K = 31
FOURIER_CH = 256
FOURIER_GROUP_CH = 64
N_GATES = 4
N_EXPERTS = 16
EC_CAPACITY_FACTOR = 2
N_ADA = 6
EPS = 1e-6

LANES = 128
MXU_DIM = 256
MLSTM_CHUNK = 128
CONV_HALO = 16
MIB = 1024 * 1024

SEC_QKV = (0, 3 * MLSTM_WIDTH)
SEC_G = (SEC_QKV[1], SEC_QKV[1] + LANES)
SEC_O = (SEC_G[1], SEC_G[1] + MLSTM_WIDTH)
SEC_CACG = (SEC_O[1], SEC_O[1] + 2 * CONV_CH)
SEC_FR = (SEC_CACG[1], SEC_CACG[1] + FOURIER_CH)
IN_COLS_PAD = SEC_FR[1]

ST_NF = 2 * HEAD_DIM
ST_NB = ST_NF + 16
ST_ROWS = ST_NB + 16
UP_ROWS = 2 * HEAD_DIM + 16


def _dot(a, b):
    return jnp.dot(a, b, preferred_element_type=F32)


def _dot_nt(a, b):
    return lax.dot_general(a, b, (((1,), (1,)), ((), ())), preferred_element_type=F32)


def _split_bf16(a):
    hi = a.astype(BF16)
    lo = (a - hi.astype(F32)).astype(BF16)
    return hi, lo


def _dot3(a, w):
    ah, al = _split_bf16(a)
    wh, wl = _split_bf16(w)
    return _dot(ah, wh) + (_dot(al, wh) + _dot(ah, wl))


def _modnorm(x, g, sh, sc):
    ms = jnp.mean(x * x, axis=-1, keepdims=True)
    return x * lax.rsqrt(ms + EPS) * g * (1.0 + sc) + sh


def _rows(n_rows, width, *vals):
    rid = lax.broadcasted_iota(I32, (n_rows, width), 0)
    out = jnp.zeros((n_rows, width), F32)
    for i, v in enumerate(vals):
        out = jnp.where(rid == i, v, out)
    return out


def _params(sem, vmem_mib):
    return pltpu.CompilerParams(dimension_semantics=sem, vmem_limit_bytes=vmem_mib * MIB)


def _ada_kernel(c_ref, w_ref, b_ref, o_ref):
    c = c_ref[...]
    o_ref[...] = _dot3(c * jax.nn.sigmoid(c), w_ref[...]) + b_ref[...]


def _ada(cs, w_ada, b_ada):
    n_layers, d, n6 = w_ada.shape
    r = cs.shape[0]
    tn = 1536
    return pl.pallas_call(
        _ada_kernel,
        grid=(n_layers, n6 // tn),
        in_specs=[
            pl.BlockSpec((r, d), lambda l, j: (0, 0)),
            pl.BlockSpec((None, d, tn), lambda l, j: (l, 0, j)),
            pl.BlockSpec((None, 1, tn), lambda l, j: (l, 0, j)),
        ],
        out_specs=pl.BlockSpec((None, r, tn), lambda l, j: (l, 0, j)),
        out_shape=jax.ShapeDtypeStruct((n_layers, r, n6), F32),
        compiler_params=_params(("arbitrary", "arbitrary"), 40),
        name="ada_mod",
    )(cs, w_ada, b_ada.reshape(n_layers, 1, n6))


def _inproj_kernel(x_ref, sh_ref, sc_ref, g_ref, w_ref, bg_ref, *out_refs, full, colmajor):
    qkv_refs, gates_ref = out_refs[:6], out_refs[6]
    xn = _modnorm(x_ref[...], g_ref[...], sh_ref[...], sc_ref[...]).astype(BF16)
    tm = xn.shape[0]
    w = MLSTM_WIDTH
    for i in range(3):
        r = _dot(xn, w_ref[:, i * w:(i + 1) * w])
        if i == 1:
            r = r * HEAD_DIM ** -0.5
        row_ref, col_ref = qkv_refs[2 * i], qkv_refs[2 * i + 1]
        for hh in range(N_ROW_HEADS):
            row_ref[hh] = r[:, hh * HEAD_DIM:(hh + 1) * HEAD_DIM].astype(BF16)
            piece = r[:, (N_ROW_HEADS + hh) * HEAD_DIM:(N_ROW_HEADS + hh + 1) * HEAD_DIM]
            if colmajor:
                piece = jnp.swapaxes(piece.reshape(tm // GRID_W, GRID_W, HEAD_DIM), 0, 1)
            col_ref[hh] = piece.astype(BF16)
    gates_ref[...] = _dot(xn, w_ref[:, SEC_G[0]:SEC_G[1]]) + bg_ref[...]
    if full:
        o_ref, cacg_ref, fr_ref = out_refs[7:]
        o_ref[...] = _dot(xn, w_ref[:, SEC_O[0]:SEC_O[1]])
        cacg_ref[...] = _dot(xn, w_ref[:, SEC_CACG[0]:SEC_CACG[1]])
        fr_ref[...] = _dot(xn, w_ref[:, SEC_FR[0]:SEC_FR[1]]).astype(BF16)


def _inproj(x, sh, sc, g, w, bg, full, rows):
    bsz, t, d = x.shape
    tm = min(t, 1024)
    nh = N_ROW_HEADS
    ncols = IN_COLS_PAD if full else SEC_G[1]
    tok = lambda n: pl.BlockSpec((None, tm, n), lambda b, i: (b, i, 0))
    per_b = pl.BlockSpec((None, 1, d), lambda b, i: (b, 0, 0))
    row_shape = jax.ShapeDtypeStruct((bsz, nh, t, HEAD_DIM), BF16)
    row_spec = pl.BlockSpec((None, nh, tm, HEAD_DIM), lambda b, i: (b, 0, i, 0))
    if rows is None:
        col_shape, col_spec = row_shape, row_spec
    else:
        col_shape = jax.ShapeDtypeStruct((bsz, nh, GRID_W, rows, HEAD_DIM), BF16)
        col_spec = pl.BlockSpec((None, nh, GRID_W, tm // GRID_W, HEAD_DIM), lambda b, i: (b, 0, 0, i, 0))
    out_shape = [row_shape, col_shape] * 3 + [jax.ShapeDtypeStruct((bsz, t, LANES), F32)]
    out_specs = [row_spec, col_spec] * 3 + [tok(LANES)]
    if full:
        out_shape += [jax.ShapeDtypeStruct((bsz, t, MLSTM_WIDTH), F32),
                      jax.ShapeDtypeStruct((bsz, t, 2 * CONV_CH), F32),
                      jax.ShapeDtypeStruct((bsz, t, FOURIER_CH), BF16)]
        out_specs += [tok(MLSTM_WIDTH), tok(2 * CONV_CH), tok(FOURIER_CH)]
    outs = pl.pallas_call(
        functools.partial(_inproj_kernel, full=full, colmajor=rows is not None),
        grid=(bsz, t // tm),
        in_specs=[tok(d), per_b, per_b,
                  pl.BlockSpec((1, d), lambda b, i: (0, 0)),
                  pl.BlockSpec((d, ncols), lambda b, i: (0, 0)),
                  pl.BlockSpec((1, LANES), lambda b, i: (0, 0))],
        out_specs=out_specs,
        out_shape=out_shape,
        compiler_params=_params(("parallel", "parallel"), 56),
        name="inproj_full" if full else "inproj_qkvg",
    )(x, sh, sc, g.reshape(1, d), w, bg)
    qkv = [a.reshape(bsz, nh, t, HEAD_DIM) for a in outs[:6]]
    return qkv, outs[6], outs[7:]


def _logsig(x):
    return jnp.minimum(x, 0.0) - jnp.log1p(jnp.exp(-jnp.abs(x)))


def _tri_masks(n):
    a0 = lax.broadcasted_iota(I32, (n, n), 0)
    a1 = lax.broadcasted_iota(I32, (n, n), 1)
    return a0 <= a1, a0 >= a1


def _mlstm_increments(c, k_ref, v_ref, gr_ref, gc_ref, up_scr, stat_scr, out_scr):
    lc = MLSTM_CHUNK
    s = pl.multiple_of(c * lc, lc)
    k = k_ref[pl.ds(s, lc), :]
    v_t = v_ref[pl.ds(s, lc), :].astype(F32).T
    gr = gr_ref[c]
    gc = gc_ref[c]
    le, ge = _tri_masks(lc)
    lf_f_row, lf_b_row = _logsig(gr[1:2]), _logsig(gr[3:4])
    lf_f_col, lf_b_col = _logsig(gc[:, 1:2]), _logsig(gc[:, 3:4])
    b_f_row = jnp.sum(jnp.where(le, lf_f_col, 0.0), axis=0, keepdims=True)
    b_b_row = jnp.sum(jnp.where(ge, lf_b_col, 0.0), axis=0, keepdims=True)
    bl_f = jnp.sum(lf_f_row, axis=1, keepdims=True)
    bl_b = jnp.sum(lf_b_row, axis=1, keepdims=True)
    wl_f = gr[0:1] - b_f_row + bl_f
    wl_b = gr[2:3] - b_b_row + bl_b
    ml_f = jnp.max(wl_f, axis=1, keepdims=True)
    ml_b = jnp.max(wl_b, axis=1, keepdims=True)
    w_f = jnp.exp(wl_f - ml_f)
    w_b = jnp.exp(wl_b - ml_b)
    lhs = jnp.concatenate([(v_t * w_f).astype(BF16), (v_t * w_b).astype(BF16),
                           _rows(16, lc, w_f, w_b).astype(BF16)], axis=0)
    up_scr[c] = _dot(lhs, k)
    stat_scr[c] = _rows(8, HEAD_DIM, bl_f, bl_b, ml_f, ml_b)
    if out_scr is not None:
        brow_scr, cif_scr, cib_scr, vt_scr = out_scr
        b_f_col = jnp.sum(jnp.where(ge, lf_f_row, 0.0), axis=1, keepdims=True)
        b_b_col = jnp.sum(jnp.where(le, lf_b_row, 0.0), axis=1, keepdims=True)
        brow_scr[c] = _rows(8, lc, b_f_row, b_b_row)
        cif_scr[c] = jnp.broadcast_to(gc[:, 0:1] - b_f_col, (lc, lc))
        cib_scr[c] = jnp.broadcast_to(gc[:, 2:3] - b_b_col, (lc, lc))
        vt_scr[c] = v_t.astype(BF16)


def _mlstm_state_step(cc, carry, up_scr, stat_scr, st_scr, mst_scr, fwd):
    cmat, n, m = carry
    off, row = (0, 0) if fwd else (HEAD_DIM, 1)
    if st_scr is not None:
        st_scr[cc, off:off + HEAD_DIM, :] = cmat.astype(BF16)
        nrow = ST_NF if fwd else ST_NB
        st_scr[cc, nrow:nrow + 16, :] = _rows(16, HEAD_DIM, n).astype(BF16)
        mst_scr[cc, row:row + 1, :] = jnp.broadcast_to(m, (1, HEAD_DIM))
    st = stat_scr[cc]
    bl, ml = st[row:row + 1, 0:1], st[2 + row:3 + row, 0:1]
    m_new = jnp.maximum(bl + m, ml)
    keep = jnp.exp(bl + m - m_new)
    gain = jnp.exp(ml - m_new)
    cmat = keep * cmat + gain * up_scr[cc, off:off + HEAD_DIM, :]
    n = keep * n + gain * up_scr[cc, 2 * HEAD_DIM + row:2 * HEAD_DIM + row + 1, :]
    return cmat, n, m_new


def _mlstm_outputs(c, q_ref, k_ref, gh_ref, h_ref, st_scr, mst_scr, out_scr, cm_rows):
    lc = MLSTM_CHUNK
    brow_scr, cif_scr, cib_scr, vt_scr = out_scr
    s = pl.multiple_of(c * lc, lc)
    q = q_ref[pl.ds(s, lc), :]
    qk_t = _dot_nt(k_ref[pl.ds(s, lc), :], q)
    g_t = _dot_nt(st_scr[c], q)
    ms = mst_scr[c]
    br = brow_scr[c]
    le, ge = _tri_masks(lc)

    def direction(valid, b_row, ci, m_st, qn):
        dlog = jnp.where(valid, b_row + ci, -jnp.inf)
        inter = b_row + m_st
        m_t = jnp.maximum(inter, jnp.max(dlog, axis=0, keepdims=True))
        s_t = qk_t * jnp.exp(dlog - m_t)
        w_inter = jnp.exp(inter - m_t)
        den = w_inter * qn + jnp.sum(s_t, axis=0, keepdims=True)
        r = 1.0 / jnp.maximum(jnp.abs(den), jnp.exp(-m_t))
        return r * s_t, r * w_inter

    sf, cf = direction(le, br[0:1], cif_scr[c], ms[0:1, 0:1], g_t[ST_NF:ST_NF + 1])
    sb, cb = direction(ge, br[1:2], cib_scr[c], ms[1:2, 0:1], g_t[ST_NB:ST_NB + 1])
    h_t = cf * g_t[0:HEAD_DIM] + cb * g_t[HEAD_DIM:2 * HEAD_DIM] + _dot(vt_scr[c], (sf + sb).astype(BF16))
    h_t = h_t * lax.rsqrt(jnp.mean(h_t * h_t, axis=0, keepdims=True) + EPS) * gh_ref[...]
    h = h_t.T
    if cm_rows is None:
        h_ref[pl.ds(s, lc), :] = h
    else:
        cols = lc // cm_rows
        for j in range(cols):
            h_ref[pl.ds(c * cols + j, cm_rows, stride=GRID_W), :] = h[j * cm_rows:(j + 1) * cm_rows]


def _mlstm_kernel(*refs, n_chunks, with_outputs, cm_rows):
    n_slot_in = 8
    slots_in = [refs[:n_slot_in], refs[n_slot_in:2 * n_slot_in]]
    rest = refs[2 * n_slot_in:]
    if with_outputs:
        h_refs, rest = rest[:2], rest[2:]
    state_out, scratch = rest[:4], rest[4:]
    n_scr = 8 if with_outputs else 2
    slots_scr = [scratch[:n_scr], scratch[n_scr:2 * n_scr]]

    def increments(c, _):
        for (q_ref, k_ref, v_ref, gr_ref, gc_ref, c0_ref, n0_ref, gh_ref), scr in zip(slots_in, slots_scr):
            _mlstm_increments(c, k_ref, v_ref, gr_ref, gc_ref, scr[0], scr[1], scr[4:] if with_outputs else None)
        return 0

    lax.fori_loop(0, n_chunks, increments, 0, unroll=min(4, n_chunks))

    def states(i, carry):
        out = []
        for slot, scr in enumerate(slots_scr):
            st_scr, mst_scr = (scr[2], scr[3]) if with_outputs else (None, None)
            out.append(_mlstm_state_step(i, carry[2 * slot], scr[0], scr[1], st_scr, mst_scr, True))
            out.append(_mlstm_state_step(n_chunks - 1 - i, carry[2 * slot + 1], scr[0], scr[1], st_scr, mst_scr, False))
        return tuple(out)

    init = []
    for (q_ref, k_ref, v_ref, gr_ref, gc_ref, c0_ref, n0_ref, gh_ref) in slots_in:
        init.append((c0_ref[0], n0_ref[0:1, :], n0_ref[2:3, 0:1]))
        init.append((c0_ref[1], n0_ref[1:2, :], n0_ref[3:4, 0:1]))
    final = lax.fori_loop(0, n_chunks, states, tuple(init))
    for slot in range(2):
        c_out, n_out = state_out[2 * slot], state_out[2 * slot + 1]
        (cf, nf, mf), (cb, nb, mb) = final[2 * slot], final[2 * slot + 1]
        c_out[0] = cf
        c_out[1] = cb
        n_out[...] = _rows(8, HEAD_DIM, nf, nb, mf, mb)

    if with_outputs:
        def outputs(c, _):
            for slot, ((q_ref, k_ref, v_ref, gr_ref, gc_ref, c0_ref, n0_ref, gh_ref), scr) in enumerate(zip(slots_in, slots_scr)):
                _mlstm_outputs(c, q_ref, k_ref, gh_ref, h_refs[slot], scr[2], scr[3], scr[4:],
                               cm_rows if slot == 1 else None)
            return 0

        lax.fori_loop(0, n_chunks, outputs, 0, unroll=min(4, n_chunks))


def _mlstm(qkv, gr, gc, states, gh, with_outputs, cm_rows):
    bsz, npair, t, dh = qkv[0].shape
    nc = t // MLSTM_CHUNK
    lc = MLSTM_CHUNK
    seq = pl.BlockSpec((None, None, t, dh), lambda b, p: (b, p, 0, 0))
    cst = pl.BlockSpec((None, None, 2, dh, dh), lambda b, p: (b, p, 0, 0, 0))
    nst = pl.BlockSpec((None, None, 8, dh), lambda b, p: (b, p, 0, 0))

    def slot_specs(head_off):
        return [seq, seq, seq,
                pl.BlockSpec((None, None, nc, N_GATES, lc), lambda b, p: (b, p + head_off, 0, 0, 0)),
                pl.BlockSpec((None, None, nc, lc, N_GATES), lambda b, p: (b, p + head_off, 0, 0, 0)),
                cst, nst,
                pl.BlockSpec((None, dh, 1), lambda b, p: (p + head_off, 0, 0))]

    st_shapes = [jax.ShapeDtypeStruct((bsz, npair, 2, dh, dh), F32), jax.ShapeDtypeStruct((bsz, npair, 8, dh), F32)] * 2
    out_shape, out_specs = st_shapes, [cst, nst, cst, nst]
    slot_scratch = [pltpu.VMEM((nc, UP_ROWS, dh), F32), pltpu.VMEM((nc, 8, dh), F32)]
    if with_outputs:
        h_shape = jax.ShapeDtypeStruct((bsz, t, npair * dh), F32)
        h_spec = pl.BlockSpec((None, t, dh), lambda b, p: (b, 0, p))
        out_shape = [h_shape, h_shape] + out_shape
        out_specs = [h_spec, h_spec] + out_specs
        slot_scratch += [pltpu.VMEM((nc, ST_ROWS, dh), BF16), pltpu.VMEM((nc, 8, dh), F32),
                         pltpu.VMEM((nc, 8, lc), F32), pltpu.VMEM((nc, lc, lc), F32), pltpu.VMEM((nc, lc, lc), F32),
                         pltpu.VMEM((nc, dh, lc), BF16)]
    q_r, q_c, k_r, k_c, v_r, v_c = qkv
    c_r, n_r, c_c, n_c = states
    return pl.pallas_call(
        functools.partial(_mlstm_kernel, n_chunks=nc, with_outputs=with_outputs, cm_rows=cm_rows),
        grid=(bsz, npair),
        in_specs=slot_specs(0) + slot_specs(N_ROW_HEADS),
        out_specs=out_specs,
        out_shape=out_shape,
        scratch_shapes=slot_scratch * 2,
        compiler_params=_params(("parallel", "parallel"), 48),
        name="mlstm_full" if with_outputs else "mlstm_states",
    )(q_r, k_r, v_r, gr, gc, c_r, n_r, gh, q_c, k_c, v_c, gr, gc, c_c, n_c, gh)


def _conv_kernel(cacg_ref, w_ref, cb_ref, lg_ref, lb_ref, o_ref, ypad, *, t):
    rt = 64
    zeros = jnp.zeros((CONV_HALO, CONV_CH), F32)
    ypad[0:CONV_HALO, :] = zeros
    ypad[CONV_HALO + t:2 * CONV_HALO + t, :] = zeros

    def fill(r, _):
        s = pl.multiple_of(r * rt, rt)
        a = cacg_ref[pl.ds(s, rt), 0:CONV_CH]
        g = cacg_ref[pl.ds(s, rt), CONV_CH:2 * CONV_CH]
        ypad[pl.ds(CONV_HALO + s, rt), :] = a * jax.nn.sigmoid(g)
        return 0

    lax.fori_loop(0, t // rt, fill, 0)

    def tile(r, _):
        s = pl.multiple_of(r * rt, rt)
        win_rows = rt + 2 * CONV_HALO
        win = ypad[pl.ds(s, win_rows), :]
        acc = jnp.zeros((rt, CONV_CH), F32)
        for res in range(8):
            rolled = win if res == 0 else pltpu.roll(win, win_rows - res, axis=0)
            for off in range(res, 2 * CONV_HALO, 8):
                kk = off - (CONV_HALO - CONV_K // 2)
                if 0 <= kk < CONV_K:
                    acc = acc + w_ref[kk:kk + 1, :] * rolled[off - res:off - res + rt, :]
        y = acc + cb_ref[...]
        mu = jnp.mean(y, axis=-1, keepdims=True)
        yc = y - mu
        var = jnp.mean(yc * yc, axis=-1, keepdims=True)
        z = yc * lax.rsqrt(var + EPS) * lg_ref[...] + lb_ref[...]
        o_ref[pl.ds(s, rt), :] = (z * jax.nn.sigmoid(z)).astype(BF16)
        return 0

    lax.fori_loop(0, t // rt, tile, 0)


def _conv(cacg, conv_w, conv_b, ln_g, ln_b):
    bsz, t, _ = cacg.shape
    row = lambda a: a.reshape(1, CONV_CH)
    vec = pl.BlockSpec((1, CONV_CH), lambda b: (0, 0))
    return pl.pallas_call(
        functools.partial(_conv_kernel, t=t),
        grid=(bsz,),
        in_specs=[pl.BlockSpec((None, t, 2 * CONV_CH), lambda b: (b, 0, 0)),
                  pl.BlockSpec((CONV_K + 1, CONV_CH), lambda b: (0, 0)), vec, vec, vec],
        out_specs=pl.BlockSpec((None, t, CONV_CH), lambda b: (b, 0, 0)),
        out_shape=jax.ShapeDtypeStruct((bsz, t, CONV_CH), BF16),
        scratch_shapes=[pltpu.VMEM((t + 2 * CONV_HALO, CONV_CH), F32)],
        compiler_params=_params(("parallel",), 32),
        name="conformer_conv",
    )(cacg, jnp.pad(conv_w, ((0, 1), (0, 0))), row(conv_b), row(ln_g), row(ln_b))


@functools.lru_cache(maxsize=None)
def _dft_tables(t):
    j = np.arange(t, dtype=np.int64)
    ang = 2.0 * np.pi * ((j[:, None] * j[None, :]) % t).astype(np.float64) / t
    wt = np.concatenate([np.cos(ang), -np.sin(ang)], axis=1).astype(np.float32)
    c = np.arange(FOURIER_CH, dtype=np.int64)
    grp, idx = c // FOURIER_GROUP_CH, c % FOURIER_GROUP_CH
    same = grp[:, None] == grp[None, :]
    angc = 2.0 * np.pi * ((idx[:, None] * idx[None, :]) % FOURIER_GROUP_CH).astype(np.float64) / FOURIER_GROUP_CH
    cc = np.where(same, np.cos(angc), 0.0).astype(np.float32)
    sc = np.where(same, np.sin(angc), 0.0).astype(np.float32)
    return wt, cc, sc


def _fourier_kernel(z_ref, cc_ref, sc_ref, wt_ref, o_ref, zcs, *, t, scale):
    @pl.when(pl.program_id(1) == 0)
    def _():
        z = z_ref[...]
        zcs[0:t, :] = _dot(z, cc_ref[...]).astype(BF16)
        zcs[t:2 * t, :] = _dot(z, sc_ref[...]).astype(BF16)

    o_ref[...] = (_dot(wt_ref[...], zcs[...]) * scale).astype(BF16)


def _fourier(z):
    bsz, t, ch = z.shape
    wt, cc, sc = _dft_tables(t)
    tm = min(t, 512)
    mat = pl.BlockSpec((ch, ch), lambda b, i: (0, 0))
    return pl.pallas_call(
        functools.partial(_fourier_kernel, t=t, scale=float((t * FOURIER_GROUP_CH) ** -0.5)),
        grid=(bsz, t // tm),
        in_specs=[pl.BlockSpec((None, t, ch), lambda b, i: (b, 0, 0)), mat, mat,
                  pl.BlockSpec((tm, 2 * t), lambda b, i: (i, 0))],
        out_specs=pl.BlockSpec((None, tm, ch), lambda b, i: (b, i, 0)),
        out_shape=jax.ShapeDtypeStruct((bsz, t, ch), BF16),
        scratch_shapes=[pltpu.VMEM((2 * t, ch), BF16)],
        compiler_params=_params(("parallel", "arbitrary"), 32),
        name="fourier_mix",
    )(z, jnp.asarray(cc).astype(BF16), jnp.asarray(sc).astype(BF16), jnp.asarray(wt).astype(BF16))


def _outproj_kernel(x_ref, hr_ref, hc_ref, o_ref, cv_ref, fo_ref, w_ref, gt_ref, g2_ref, sh_ref, sc_ref, wr_ref,
                    x1_ref, xn_ref, aff_ref):
    half = MLSTM_WIDTH // 2
    a, b = MLSTM_WIDTH, MLSTM_WIDTH + CONV_CH
    og = jax.nn.sigmoid(o_ref[...])
    y = (_dot((hr_ref[...] * og[:, 0:half]).astype(BF16), w_ref[0:half, :])
         + _dot((hc_ref[...] * og[:, half:a]).astype(BF16), w_ref[half:a, :])
         + _dot(cv_ref[...], w_ref[a:b, :]) + _dot(fo_ref[...], w_ref[b:, :]))
    x1 = x_ref[...] + gt_ref[...] * y
    x1_ref[...] = x1
    xn = _modnorm(x1, g2_ref[...], sh_ref[...], sc_ref[...])
    xn_ref[...] = xn.astype(BF16)
    logits = _dot3(xn, wr_ref[...])
    lane = lax.broadcasted_iota(I32, logits.shape, 1)
    logits = jnp.where(lane < N_EXPERTS, logits, -jnp.inf)
    e = jnp.exp(logits - jnp.max(logits, axis=-1, keepdims=True))
    aff_ref[...] = e / jnp.sum(e, axis=-1, keepdims=True)


def _outproj(x, hr, hc, o, cv, fo, w_out, gt, g2, sh, sc, wr):
    bsz, t, d = x.shape
    tm = min(t, 512)
    tok = lambda n: pl.BlockSpec((None, tm, n), lambda b, i: (b, i, 0))
    per_b = pl.BlockSpec((None, 1, d), lambda b, i: (b, 0, 0))
    return pl.pallas_call(
        _outproj_kernel,
        grid=(bsz, t // tm),
        in_specs=[tok(d), tok(MLSTM_WIDTH // 2), tok(MLSTM_WIDTH // 2), tok(MLSTM_WIDTH), tok(CONV_CH), tok(FOURIER_CH),
                  pl.BlockSpec((d, d), lambda b, i: (0, 0)), per_b,
                  pl.BlockSpec((1, d), lambda b, i: (0, 0)), per_b, per_b,
                  pl.BlockSpec((d, LANES), lambda b, i: (0, 0))],
        out_specs=[tok(d), tok(d), tok(LANES)],
        out_shape=[jax.ShapeDtypeStruct((bsz, t, d), F32), jax.ShapeDtypeStruct((bsz, t, d), BF16),
                   jax.ShapeDtypeStruct((bsz, t, LANES), F32)],
        compiler_params=_params(("parallel", "parallel"), 40),
        name="outproj_router",
    )(x, hr, hc, o, cv, fo, w_out, gt, g2.reshape(1, d), sh, sc, wr)


def _select_kernel(a_ref, sp_ref, *, cap):
    a = a_ref[...]
    n = a.shape[0]
    blk = min(n, MXU_DIM)

    def count(mask):
        return jnp.sum(jnp.where(mask, 1.0, 0.0), axis=0, keepdims=True)

    def search(i, thr):
        cand = thr | jnp.left_shift(jnp.int32(1), 30 - i)
        return jnp.where(count(a >= lax.bitcast_convert_type(cand, F32)) >= cap, cand, thr)

    thr = lax.bitcast_convert_type(lax.fori_loop(0, 31, search, jnp.zeros((1, LANES), I32)), F32)
    gt = a > thr
    eq = a == thr
    before = jnp.where(lax.broadcasted_iota(I32, (blk, blk), 1) < lax.broadcasted_iota(I32, (blk, blk), 0), 1.0, 0.0).astype(BF16)

    def excl_cumsum(mask):
        x = jnp.where(mask, 1.0, 0.0)
        run = jnp.zeros((1, LANES), F32)
        parts = []
        for j in range(n // blk):
            xb = x[j * blk:(j + 1) * blk]
            parts.append(_dot(before, xb.astype(BF16)) + run)
            run = run + jnp.sum(xb, axis=0, keepdims=True)
        return jnp.concatenate(parts, axis=0)

    sel = gt | (eq & (excl_cumsum(eq) < cap - count(gt)))
    sp_ref[...] = jnp.where(sel, excl_cumsum(sel).astype(I32), -1)


def _select(aff_sets, cap):
    ng, n, _ = aff_sets.shape
    blk = pl.BlockSpec((None, n, LANES), lambda g: (g, 0, 0))
    return pl.pallas_call(
        functools.partial(_select_kernel, cap=cap),
        grid=(ng,),
        in_specs=[blk],
        out_specs=blk,
        out_shape=jax.ShapeDtypeStruct((ng, n, LANES), I32),
        compiler_params=_params(("parallel",), 32),
        name="ec_select",
    )(aff_sets)


def _ffn_kernel(sp_ref, ar_ref, xn_ref, wg_ref, wu_ref, wd_ref, ys_ref, wg_b, wu_b, wd_b, *, cap):
    @pl.when(pl.program_id(1) == 0)
    def _():
        rt = 256
        for src, dst in ((wg_ref, wg_b), (wu_ref, wu_b), (wd_ref, wd_b)):
            for r in range(0, src.shape[0], rt):
                dst[r:r + rt, :] = src[r:r + rt, :].astype(BF16)

    nb, _, n = sp_ref.shape
    slot = lax.broadcasted_iota(I32, (cap, n), 0)
    xs, gates = [], []
    for bi in range(nb):
        hit = slot == sp_ref[bi]
        gates.append(jnp.sum(jnp.where(hit, ar_ref[bi], 0.0), axis=1, keepdims=True))
        xs.append(_dot(jnp.where(hit, 1.0, 0.0).astype(BF16), xn_ref[bi]).astype(BF16))
    xs = xs[0] if nb == 1 else jnp.concatenate(xs, axis=0)
    gate = gates[0] if nb == 1 else jnp.concatenate(gates, axis=0)
    h1 = _dot(xs, wg_b[...])
    hid = (h1 * jax.nn.sigmoid(h1) * _dot(xs, wu_b[...])).astype(BF16)
    ys = (_dot(hid, wd_b[...]) * gate).astype(BF16)
    for bi in range(nb):
        ys_ref[bi] = ys[bi * cap:(bi + 1) * cap]


def _ffn(sp_row, aff_row, xn, wg, wu, wd, cap):
    bsz, n, d = xn.shape
    ne, _, hid = wg.shape
    nb = min(bsz, max(1, MXU_DIM // cap))
    row = pl.BlockSpec((nb, None, 1, n), lambda e, j: (j, e, 0, 0))
    return pl.pallas_call(
        functools.partial(_ffn_kernel, cap=cap),
        grid=(ne, bsz // nb),
        in_specs=[row, row,
                  pl.BlockSpec((nb, n, d), lambda e, j: (j, 0, 0)),
                  pl.BlockSpec((None, d, hid), lambda e, j: (e, 0, 0)),
                  pl.BlockSpec((None, d, hid), lambda e, j: (e, 0, 0)),
                  pl.BlockSpec((None, hid, d), lambda e, j: (e, 0, 0))],
        out_specs=pl.BlockSpec((nb, None, cap, d), lambda e, j: (j, e, 0, 0)),
        out_shape=jax.ShapeDtypeStruct((bsz, ne, cap, d), BF16),
        scratch_shapes=[pltpu.VMEM((d, hid), BF16), pltpu.VMEM((d, hid), BF16), pltpu.VMEM((hid, d), BF16)],
        compiler_params=_params(("arbitrary", "arbitrary"), 58),
        name="ec_ffn",
    )(sp_row, aff_row, xn, wg, wu, wd)


def _combine_kernel(x_ref, spc_ref, ys_ref, gt_ref, gf_ref, o_ref, *, cap, final):
    spc = spc_ref[...]
    tm = spc.shape[0]
    slot = lax.broadcasted_iota(I32, (tm, cap), 1)
    acc = jnp.zeros(x_ref.shape, F32)
    for e in range(N_EXPERTS):
        hit = jnp.where(spc[:, e:e + 1] == slot, 1.0, 0.0).astype(BF16)
        acc = acc + _dot(hit, ys_ref[e])
    out = x_ref[...] + gt_ref[...] * acc
    if final:
        out = out * lax.rsqrt(jnp.mean(out * out, axis=-1, keepdims=True) + EPS) * gf_ref[...]
    o_ref[...] = out


def _combine(x, sp_col, ys, gt, g_final, cap, final):
    bsz, n, d = x.shape
    tm = min(n, 512)
    return pl.pallas_call(
        functools.partial(_combine_kernel, cap=cap, final=final),
        grid=(bsz, n // tm),
        in_specs=[pl.BlockSpec((None, tm, d), lambda b, i: (b, i, 0)),
                  pl.BlockSpec((None, tm, N_EXPERTS), lambda b, i: (b, i, 0)),
                  pl.BlockSpec((None, N_EXPERTS, cap, d), lambda b, i: (b, 0, 0, 0)),
                  pl.BlockSpec((None, 1, d), lambda b, i: (b, 0, 0)),
                  pl.BlockSpec((1, d), lambda b, i: (0, 0))],
        out_specs=pl.BlockSpec((None, tm, d), lambda b, i: (b, i, 0)),
        out_shape=jax.ShapeDtypeStruct((bsz, n, d), F32),
        compiler_params=_params(("parallel", "parallel"), 40),
        name="ec_combine_final" if final else "ec_combine",
    )(x, sp_col, ys, gt, g_final.reshape(1, d))


def _gates_to_scan(gates, rows):
    bsz, t, _ = gates.shape
    g = gates[:, :, :N_HEADS * N_GATES].reshape(bsz, t, N_HEADS, N_GATES).transpose(0, 2, 1, 3)
    if rows is not None:
        nh = N_HEADS - N_ROW_HEADS
        cm = g[:, N_ROW_HEADS:].reshape(bsz, nh, rows, GRID_W, N_GATES).swapaxes(2, 3).reshape(bsz, nh, t, N_GATES)
        g = jnp.concatenate([g[:, :N_ROW_HEADS], cm], axis=1)
    gc = g.reshape(bsz, N_HEADS, t // MLSTM_CHUNK, MLSTM_CHUNK, N_GATES)
    return gc.swapaxes(3, 4), gc


def _moe(x1, xn2, aff, wg, wu, wd, gt2, g_final, final):
    bsz, n, _ = x1.shape
    cap = EC_CAPACITY_FACTOR * n // N_EXPERTS
    per_group = LANES // N_EXPERTS
    ng = -(-bsz // per_group)
    aff_e = aff[:, :, :N_EXPERTS]
    sets = jnp.pad(aff_e, ((0, ng * per_group - bsz), (0, 0), (0, 0)))
    sets = sets.reshape(ng, per_group, n, N_EXPERTS).transpose(0, 2, 1, 3).reshape(ng, n, LANES)
    selpos = _select(sets, cap)
    sp_col = selpos.reshape(ng, n, per_group, N_EXPERTS).transpose(0, 2, 1, 3).reshape(ng * per_group, n, N_EXPERTS)[:bsz]
    sp_row = sp_col.transpose(0, 2, 1).reshape(bsz, N_EXPERTS, 1, n)
    aff_row = aff_e.transpose(0, 2, 1).reshape(bsz, N_EXPERTS, 1, n)
    ys = _ffn(sp_row, aff_row, xn2, wg, wu, wd, cap)
    return _combine(x1, sp_col, ys, gt2, g_final, cap, final)


def kernel(x, c, ctx, c_ctx, w_ada, b_ada, g_norm1, w_in, b_gates, g_hnorm, conv_w, conv_b, conv_ln_g, conv_ln_b,
           w_out, g_norm2, w_router, w_e_gate, w_e_up, w_e_down, g_final):
    bsz, t, d = x.shape
    depth = w_ada.shape[0]
    rows = t // GRID_W
    n_cond = 16
    cs = jnp.concatenate([c, c_ctx[None, :], jnp.zeros((n_cond - bsz - 1, d), F32)], axis=0)
    mods = _ada(cs, w_ada, b_ada)

    npair = N_HEADS // 2
    zero_states = (jnp.zeros((bsz, npair, 2, HEAD_DIM, HEAD_DIM), F32), jnp.zeros((bsz, npair, 8, HEAD_DIM), F32)) * 2
    h_ctx = ctx
    for l in range(depth):
        last = l == depth - 1
        lat = [mods[l, :bsz, i * d:(i + 1) * d].reshape(bsz, 1, d) for i in range(N_ADA)]
        cmod = [jnp.broadcast_to(mods[l, bsz, i * d:(i + 1) * d].reshape(1, 1, d), (bsz, 1, d)) for i in range(N_ADA)]
        w_l = w_in[l]
        qkvg_cols = 3 * MLSTM_WIDTH + N_GATES * N_HEADS
        w_p = jnp.concatenate([w_l[:, :3 * MLSTM_WIDTH],
                               jnp.pad(w_l[:, 3 * MLSTM_WIDTH:qkvg_cols], ((0, 0), (0, LANES - N_GATES * N_HEADS))),
                               w_l[:, qkvg_cols:]], axis=1).astype(BF16)
        bg = jnp.pad(b_gates[l].reshape(1, N_HEADS * N_GATES), ((0, 0), (0, LANES - N_HEADS * N_GATES)))
        w_o = w_out[l].astype(BF16)
        w_r = jnp.pad(w_router[l], ((0, 0), (0, LANES - N_EXPERTS)))
        gh = g_hnorm[l].reshape(N_HEADS, HEAD_DIM, 1)

        def mlstm(xs, md, states, grid_rows, full):
            qkv, gates, extra = _inproj(xs, md[0], md[1], g_norm1[l], w_p, bg, full, grid_rows)
            gr, gc = _gates_to_scan(gates, grid_rows)
            return _mlstm(qkv, gr, gc, states, gh, full, grid_rows), extra

        def sublayers(xs, md, states, grid_rows, final):
            (hr, hc, *st), (o, cacg, fr) = mlstm(xs, md, states, grid_rows, True)
            cv = _conv(cacg, conv_w[l], conv_b[l], conv_ln_g[l], conv_ln_b[l])
            fo = _fourier(fr)
            x1, xn2, aff = _outproj(xs, hr, hc, o, cv, fo, w_o, md[2], g_norm2[l], md[3], md[4], w_r)
            return _moe(x1, xn2, aff, w_e_gate[l], w_e_up[l], w_e_down[l], md[5], g_final, final), st

        if last:
            st, _ = mlstm(h_ctx, cmod, zero_states, None, False)
        else:
            h_ctx, st = sublayers(h_ctx, cmod, zero_states, None, False)
        x, _ = sublayers(x, lat, tuple(st), rows, last)
    return x
```

```python
import functools

import jax
import jax.numpy as jnp
import numpy as np
from jax import lax
from jax.experimental import pallas as pl
from jax.experimental.pallas import tpu as pltpu

F32 = jnp.float32
BF16 = jnp.bfloat16
I32 = jnp.int32

D_MODEL = 1024
GRID_W = 64
N_HEADS = 4
HEAD_DIM = 128
MLSTM_WIDTH = N_HEADS * HEAD_DIM
N_ROW_HEADS = N_HEADS // 2
CONV_CH = 256
CONV_K = 31
FOURIER_CH = 256
FOURIER_GROUP_CH = 64
N_GATES = 4
N_EXPERTS = 16
EC_CAPACITY_FACTOR = 2
N_ADA = 6
EPS = 1e-6

LANES = 128
MXU_DIM = 256
MLSTM_CHUNK = 128
MAX_CHUNKS = 16
CONV_HALO = 16
CONV_TAIL = 128
MIB = 1024 * 1024

SEC_QKV = (0, 3 * MLSTM_WIDTH)
SEC_G = (SEC_QKV[1], SEC_QKV[1] + LANES)
SEC_O = (SEC_G[1], SEC_G[1] + MLSTM_WIDTH)
SEC_CACG = (SEC_O[1], SEC_O[1] + 2 * CONV_CH)
SEC_FR = (SEC_CACG[1], SEC_CACG[1] + FOURIER_CH)
IN_COLS_PAD = SEC_FR[1]

ST_NF = 2 * HEAD_DIM
ST_NB = ST_NF + 16
ST_ROWS = ST_NB + 16
UP_ROWS = 2 * HEAD_DIM + 16


def _dot(a, b):
    return jnp.dot(a, b, preferred_element_type=F32)


def _dot_nt(a, b):
    return lax.dot_general(a, b, (((1,), (1,)), ((), ())), preferred_element_type=F32)


def _split_bf16(a):
    hi = a.astype(BF16)
    lo = (a - hi.astype(F32)).astype(BF16)
    return hi, lo


def _dot3(a, w):
    ah, al = _split_bf16(a)
    wh, wl = _split_bf16(w)
    return _dot(ah, wh) + (_dot(al, wh) + _dot(ah, wl))


def _modnorm(x, g, sh, sc):
    ms = jnp.mean(x * x, axis=-1, keepdims=True)
    return x * lax.rsqrt(ms + EPS) * g * (1.0 + sc) + sh


def _rows(n_rows, width, *vals):
    rid = lax.broadcasted_iota(I32, (n_rows, width), 0)
    out = jnp.zeros((n_rows, width), F32)
    for i, v in enumerate(vals):
        out = jnp.where(rid == i, v, out)
    return out


def _params(sem, vmem_mib):
    return pltpu.CompilerParams(dimension_semantics=sem, vmem_limit_bytes=vmem_mib * MIB)


def _ada_kernel(c_ref, w_ref, b_ref, o_ref):
    c = c_ref[...]
    o_ref[...] = _dot3(c * jax.nn.sigmoid(c), w_ref[...]) + b_ref[...]


def _ada(cs, w_ada, b_ada):
    n_layers, d, n6 = w_ada.shape
    r = cs.shape[0]
    tn = 1536
    return pl.pallas_call(
        _ada_kernel,
        grid=(n_layers, n6 // tn),
        in_specs=[
            pl.BlockSpec((r, d), lambda l, j: (0, 0)),
            pl.BlockSpec((None, d, tn), lambda l, j: (l, 0, j)),
            pl.BlockSpec((None, 1, tn), lambda l, j: (l, 0, j)),
        ],
        out_specs=pl.BlockSpec((None, r, tn), lambda l, j: (l, 0, j)),
        out_shape=jax.ShapeDtypeStruct((n_layers, r, n6), F32),
        compiler_params=_params(("arbitrary", "arbitrary"), 40),
        name="ada_mod",
    )(cs, w_ada, b_ada.reshape(n_layers, 1, n6))


def _inproj_kernel(x_ref, sh_ref, sc_ref, g_ref, w_ref, bg_ref, *out_refs, full, colmajor):
    qkv_refs, gates_ref = out_refs[:6], out_refs[6]
    xn = _modnorm(x_ref[...], g_ref[...], sh_ref[...], sc_ref[...]).astype(BF16)
    tm = xn.shape[0]
    w = MLSTM_WIDTH
    for i in range(3):
        r = _dot(xn, w_ref[:, i * w:(i + 1) * w])
        if i == 1:
            r = r * HEAD_DIM ** -0.5
        row_ref, col_ref = qkv_refs[2 * i], qkv_refs[2 * i + 1]
        for hh in range(N_ROW_HEADS):
            row_ref[hh] = r[:, hh * HEAD_DIM:(hh + 1) * HEAD_DIM].astype(BF16)
            piece = r[:, (N_ROW_HEADS + hh) * HEAD_DIM:(N_ROW_HEADS + hh + 1) * HEAD_DIM]
            if colmajor:
                piece = jnp.swapaxes(piece.reshape(tm // GRID_W, GRID_W, HEAD_DIM), 0, 1)
            col_ref[hh] = piece.astype(BF16)
    gates_ref[...] = _dot(xn, w_ref[:, SEC_G[0]:SEC_G[1]]) + bg_ref[...]
    if full:
        o_ref, cacg_ref, fr_ref = out_refs[7:]
        o_ref[...] = _dot(xn, w_ref[:, SEC_O[0]:SEC_O[1]])
        cacg_ref[...] = _dot(xn, w_ref[:, SEC_CACG[0]:SEC_CACG[1]])
        fr_ref[...] = _dot(xn, w_ref[:, SEC_FR[0]:SEC_FR[1]]).astype(BF16)


def _inproj(x, sh, sc, g, w, bg, full, rows):
    bsz, t, d = x.shape
    tm = min(t, 1024)
    nh = N_ROW_HEADS
    ncols = IN_COLS_PAD if full else SEC_G[1]
    tok = lambda n: pl.BlockSpec((None, tm, n), lambda b, i: (b, i, 0))
    per_b = pl.BlockSpec((None, 1, d), lambda b, i: (b, 0, 0))
    row_shape = jax.ShapeDtypeStruct((bsz, nh, t, HEAD_DIM), BF16)
    row_spec = pl.BlockSpec((None, nh, tm, HEAD_DIM), lambda b, i: (b, 0, i, 0))
    if rows is None:
        col_shape, col_spec = row_shape, row_spec
    else:
        col_shape = jax.ShapeDtypeStruct((bsz, nh, GRID_W, rows, HEAD_DIM), BF16)
        col_spec = pl.BlockSpec((None, nh, GRID_W, tm // GRID_W, HEAD_DIM), lambda b, i: (b, 0, 0, i, 0))
    out_shape = [row_shape, col_shape] * 3 + [jax.ShapeDtypeStruct((bsz, t, LANES), F32)]
    out_specs = [row_spec, col_spec] * 3 + [tok(LANES)]
    if full:
        out_shape += [jax.ShapeDtypeStruct((bsz, t, MLSTM_WIDTH), F32),
                      jax.ShapeDtypeStruct((bsz, t, 2 * CONV_CH), F32),
                      jax.ShapeDtypeStruct((bsz, t, FOURIER_CH), BF16)]
        out_specs += [tok(MLSTM_WIDTH), tok(2 * CONV_CH), tok(FOURIER_CH)]
    outs = pl.pallas_call(
        functools.partial(_inproj_kernel, full=full, colmajor=rows is not None),
        grid=(bsz, t // tm),
        in_specs=[tok(d), per_b, per_b,
                  pl.BlockSpec((1, d), lambda b, i: (0, 0)),
                  pl.BlockSpec((d, ncols), lambda b, i: (0, 0)),
                  pl.BlockSpec((1, LANES), lambda b, i: (0, 0))],
        out_specs=out_specs,
        out_shape=out_shape,
        compiler_params=_params(("parallel", "parallel"), 56),
        name="inproj_full" if full else "inproj_qkvg",
    )(x, sh, sc, g.reshape(1, d), w, bg)
    qkv = [a.reshape(bsz, nh, t, HEAD_DIM) for a in outs[:6]]
    return qkv, outs[6], outs[7:]


def _logsig(x):
    return jnp.minimum(x, 0.0) - jnp.log1p(jnp.exp(-jnp.abs(x)))


def _tri_masks(n):
    a0 = lax.broadcasted_iota(I32, (n, n), 0)
    a1 = lax.broadcasted_iota(I32, (n, n), 1)
    return a0 <= a1, a0 >= a1


def _split3(x):
    hi = x.astype(BF16).astype(F32)
    r1 = x - hi
    mid = r1.astype(BF16).astype(F32)
    return hi, mid, r1 - mid


def _mlstm_consts(tri_scr, ones_scr):
    lc = MLSTM_CHUNK
    u = lax.broadcasted_iota(I32, (lc, 2 * lc), 0)
    t2 = lax.broadcasted_iota(I32, (lc, 2 * lc), 1)
    tri_scr[...] = jnp.where(((t2 < lc) & (u <= t2)) | ((t2 >= lc) & (u >= t2 - lc)), 1.0, 0.0).astype(BF16)
    r = lax.broadcasted_iota(I32, (16, 2 * lc), 0)
    t16 = lax.broadcasted_iota(I32, (16, 2 * lc), 1)
    ones_scr[...] = jnp.where(((r < 3) & (t16 < lc)) | ((r >= 3) & (r < 6) & (t16 >= lc)), 1.0, 0.0).astype(BF16)


def _mlstm_gates(head, g_ref, tri_scr, rowg_scr, stat_scr, n_chunks, cm_rows):
    lc = MLSTM_CHUNK
    pick = jnp.where(lax.broadcasted_iota(I32, (16, LANES), 1) == N_GATES * head + lax.broadcasted_iota(I32, (16, LANES), 0),
                     1.0, 0.0).astype(BF16)
    chunk_id = lax.broadcasted_iota(I32, (MAX_CHUNKS, lc), 0)

    def gather(c, acc):
        if cm_rows is None:
            g = g_ref[pl.ds(pl.multiple_of(c * lc, lc), lc), :]
        else:
            cols = lc // cm_rows
            g = jnp.concatenate([g_ref[pl.ds(c * cols + j, cm_rows, stride=GRID_W), :] for j in range(cols)], axis=0)
        rows = sum(_dot_nt(pick, part.astype(BF16)) for part in _split3(g))
        return tuple(jnp.where(chunk_id == c, rows[i:i + 1], a) for i, a in enumerate(acc))

    zeros = jnp.zeros((MAX_CHUNKS, lc), F32)
    i_f, f_f, i_b, f_b = lax.fori_loop(0, n_chunks, gather, (zeros,) * N_GATES, unroll=min(4, n_chunks))
    lf_f, lf_b = _logsig(f_f), _logsig(f_b)
    b_f = sum(_dot(part.astype(BF16), tri_scr[:, 0:lc]) for part in _split3(lf_f))
    b_b = sum(_dot(part.astype(BF16), tri_scr[:, lc:]) for part in _split3(lf_b))
    bl_f = jnp.sum(lf_f, axis=1, keepdims=True)
    bl_b = jnp.sum(lf_b, axis=1, keepdims=True)
    ci_f, ci_b = i_f - b_f, i_b - b_b
    ml_f = jnp.max(ci_f + bl_f, axis=1, keepdims=True)
    ml_b = jnp.max(ci_b + bl_b, axis=1, keepdims=True)
    for r, val in enumerate((jnp.exp(ci_f + bl_f - ml_f), jnp.exp(ci_b + bl_b - ml_b), b_f, b_b, ci_f, ci_b)):
        rowg_scr[:, r, :] = val
    for r, val in enumerate((bl_f, bl_b, ml_f, ml_b)):
        stat_scr[:, r, :] = jnp.broadcast_to(val, (MAX_CHUNKS, HEAD_DIM))


def _mlstm_increment(c, k_ref, v_ref, rowg_scr, up_scr, vt_scr):
    lc = MLSTM_CHUNK
    s = pl.multiple_of(c * lc, lc)
    v_t = v_ref[pl.ds(s, lc), :].astype(F32).T
    rg = rowg_scr[c]
    w_f, w_b = rg[0:1], rg[1:2]
    lhs = jnp.concatenate([(v_t * w_f).astype(BF16), (v_t * w_b).astype(BF16),
                           _rows(16, lc, w_f, w_b).astype(BF16)], axis=0)
    up_scr[c] = _dot(lhs, k_ref[pl.ds(s, lc), :])
    if vt_scr is not None:
        vt_scr[c] = v_t.astype(BF16)


def _mlstm_state_step(cc, carry, up_scr, stat_scr, st_scr, mst_scr, fwd):
    cmat, n, m = carry
    off, row = (0, 0) if fwd else (HEAD_DIM, 1)
    if st_scr is not None:
        st_scr[cc, off:off + HEAD_DIM, :] = cmat.astype(BF16)
        nrow = ST_NF if fwd else ST_NB
        st_scr[cc, nrow:nrow + 16, :] = _rows(16, HEAD_DIM, n).astype(BF16)
        mst_scr[cc, row:row + 1, :] = jnp.broadcast_to(m, (1, HEAD_DIM))
    st = stat_scr[cc]
    bl, ml = st[row:row + 1, 0:1], st[2 + row:3 + row, 0:1]
    m_new = jnp.maximum(bl + m, ml)
    keep = jnp.exp(bl + m - m_new)
    gain = jnp.exp(ml - m_new)
    cmat = keep * cmat + gain * up_scr[cc, off:off + HEAD_DIM, :]
    n = keep * n + gain * up_scr[cc, 2 * HEAD_DIM + row:2 * HEAD_DIM + row + 1, :]
    return cmat, n, m_new


def _mlstm_outputs(c, chains, ones_scr):
    lc = MLSTM_CHUNK
    s = pl.multiple_of(c * lc, lc)
    le, ge = _tri_masks(lc)
    qs = [ch[0][pl.ds(s, lc), :] for ch in chains]
    qk_ts = [_dot_nt(ch[1][pl.ds(s, lc), :], q) for ch, q in zip(chains, qs)]
    g_ts = [_dot_nt(ch[5][c], q) for ch, q in zip(chains, qs)]
    rgs = [ch[4][c] for ch in chains]
    cis = [lax.dot_general(_rows(16, lc, *_split3(rg[4:5]), *_split3(rg[5:6])).astype(BF16), ones_scr[...],
                           (((0,), (0,)), ((), ())), preferred_element_type=F32) for rg in rgs]

    def direction(qk_t, valid, b_row, ci, m_st, qn):
        dlog = jnp.where(valid, b_row + ci, -jnp.inf)
        inter = b_row + m_st
        m_t = jnp.maximum(inter, jnp.max(dlog, axis=0, keepdims=True))
        s_t = qk_t * jnp.exp(dlog - m_t)
        w_inter = jnp.exp(inter - m_t)
        den = w_inter * qn + jnp.sum(s_t, axis=0, keepdims=True)
        r = 1.0 / jnp.maximum(jnp.abs(den), jnp.exp(-m_t))
        return r * s_t, r * w_inter

    mixed = []
    for ch, qk_t, g_t, rg, ci in zip(chains, qk_ts, g_ts, rgs, cis):
        ms = ch[6][c]
        sf, cf = direction(qk_t, le, rg[2:3], ci[:, 0:lc], ms[0:1, 0:1], g_t[ST_NF:ST_NF + 1])
        sb, cb = direction(qk_t, ge, rg[3:4], ci[:, lc:], ms[1:2, 0:1], g_t[ST_NB:ST_NB + 1])
        mixed.append(((sf + sb).astype(BF16), cf * g_t[0:HEAD_DIM] + cb * g_t[HEAD_DIM:2 * HEAD_DIM]))
    h_ts = [inter_part + _dot(ch[7][c], s_mix) for ch, (s_mix, inter_part) in zip(chains, mixed)]
    for ch, h_t in zip(chains, h_ts):
        gh_ref, h_ref, cm_rows = ch[2], ch[3], ch[8]
        h = (h_t * lax.rsqrt(jnp.mean(h_t * h_t, axis=0, keepdims=True) + EPS) * gh_ref[...]).T
        if cm_rows is None:
            h_ref[pl.ds(s, lc), :] = h
        else:
            cols = lc // cm_rows
            for j in range(cols):
                h_ref[pl.ds(c * cols + j, cm_rows, stride=GRID_W), :] = h[j * cm_rows:(j + 1) * cm_rows]


def _mlstm_kernel(*refs, n_chunks, with_outputs, cm_rows):
    n_slot_in = 6
    g_ref = refs[0]
    slots_in = [refs[1:1 + n_slot_in], refs[1 + n_slot_in:1 + 2 * n_slot_in]]
    rest = refs[1 + 2 * n_slot_in:]
    if with_outputs:
        h_refs, rest = rest[:2], rest[2:]
    state_out, scratch = rest[:4], rest[4:]
    tri_scr, ones_scr, scratch = scratch[0], scratch[1], scratch[2:]
    n_scr = 6 if with_outputs else 3
    slots_scr = [scratch[:n_scr], scratch[n_scr:2 * n_scr]]
    slot_rows = [None, cm_rows]
    _mlstm_consts(tri_scr, ones_scr)
    for slot, scr in enumerate(slots_scr):
        _mlstm_gates(pl.program_id(1) + slot * N_ROW_HEADS, g_ref, tri_scr, scr[0], scr[1], n_chunks, slot_rows[slot])

    def increments(c, _):
        for (q_ref, k_ref, v_ref, c0_ref, n0_ref, gh_ref), scr in zip(slots_in, slots_scr):
            _mlstm_increment(c, k_ref, v_ref, scr[0], scr[2], scr[5] if with_outputs else None)
        return 0

    lax.fori_loop(0, n_chunks, increments, 0, unroll=min(4, n_chunks))

    def states(i, carry):
        out = []
        for slot, scr in enumerate(slots_scr):
            st_scr, mst_scr = (scr[3], scr[4]) if with_outputs else (None, None)
            out.append(_mlstm_state_step(i, carry[2 * slot], scr[2], scr[1], st_scr, mst_scr, True))
            out.append(_mlstm_state_step(n_chunks - 1 - i, carry[2 * slot + 1], scr[2], scr[1], st_scr, mst_scr, False))
        return tuple(out)

    init = []
    for (q_ref, k_ref, v_ref, c0_ref, n0_ref, gh_ref) in slots_in:
        init.append((c0_ref[0], n0_ref[0:1, :], n0_ref[2:3, 0:1]))
        init.append((c0_ref[1], n0_ref[1:2, :], n0_ref[3:4, 0:1]))
    final = lax.fori_loop(0, n_chunks, states, tuple(init))
    for slot in range(2):
        c_out, n_out = state_out[2 * slot], state_out[2 * slot + 1]
        (cf, nf, mf), (cb, nb, mb) = final[2 * slot], final[2 * slot + 1]
        c_out[0] = cf
        c_out[1] = cb
        n_out[...] = _rows(8, HEAD_DIM, nf, nb, mf, mb)

    if with_outputs:
        chains = [(q_ref, k_ref, gh_ref, h_refs[slot], scr[0], scr[3], scr[4], scr[5], slot_rows[slot])
                  for slot, ((q_ref, k_ref, v_ref, c0_ref, n0_ref, gh_ref), scr) in enumerate(zip(slots_in, slots_scr))]

        def outputs(c, _):
            _mlstm_outputs(c, chains, ones_scr)
            return 0

        lax.fori_loop(0, n_chunks, outputs, 0, unroll=min(2, n_chunks))


def _mlstm(qkv, gates, states, gh, with_outputs, cm_rows):
    bsz, npair, t, dh = qkv[0].shape
    nc = t // MLSTM_CHUNK
    lc = MLSTM_CHUNK
    seq = pl.BlockSpec((None, None, t, dh), lambda b, p: (b, p, 0, 0))
    cst = pl.BlockSpec((None, None, 2, dh, dh), lambda b, p: (b, p, 0, 0, 0))
    nst = pl.BlockSpec((None, None, 8, dh), lambda b, p: (b, p, 0, 0))

    def slot_specs(head_off):
        return [seq, seq, seq, cst, nst, pl.BlockSpec((None, dh, 1), lambda b, p: (p + head_off, 0, 0))]

    st_shapes = [jax.ShapeDtypeStruct((bsz, npair, 2, dh, dh), F32), jax.ShapeDtypeStruct((bsz, npair, 8, dh), F32)] * 2
    out_shape, out_specs = st_shapes, [cst, nst, cst, nst]
    assert nc <= MAX_CHUNKS
    slot_scratch = [pltpu.VMEM((MAX_CHUNKS, 8, lc), F32), pltpu.VMEM((MAX_CHUNKS, 8, dh), F32),
                    pltpu.VMEM((nc, UP_ROWS, dh), F32)]
    if with_outputs:
        h_shape = jax.ShapeDtypeStruct((bsz, t, npair * dh), F32)
        h_spec = pl.BlockSpec((None, t, dh), lambda b, p: (b, 0, p))
        out_shape = [h_shape, h_shape] + out_shape
        out_specs = [h_spec, h_spec] + out_specs
        slot_scratch += [pltpu.VMEM((nc, ST_ROWS, dh), BF16), pltpu.VMEM((nc, 8, dh), F32), pltpu.VMEM((nc, dh, lc), BF16)]
    q_r, q_c, k_r, k_c, v_r, v_c = qkv
    c_r, n_r, c_c, n_c = states
    return pl.pallas_call(
        functools.partial(_mlstm_kernel, n_chunks=nc, with_outputs=with_outputs, cm_rows=cm_rows),
        grid=(bsz, npair),
        in_specs=[pl.BlockSpec((None, t, LANES), lambda b, p: (b, 0, 0))] + slot_specs(0) + slot_specs(N_ROW_HEADS),
        out_specs=out_specs,
        out_shape=out_shape,
        scratch_shapes=[pltpu.VMEM((lc, 2 * lc), BF16), pltpu.VMEM((16, 2 * lc), BF16)] + slot_scratch * 2,
        compiler_params=_params(("parallel", "parallel"), 48),
        name="mlstm_full" if with_outputs else "mlstm_states",
    )(gates, q_r, k_r, v_r, c_r, n_r, gh, q_c, k_c, v_c, c_c, n_c, gh)


def _conv_kernel(cacg_ref, w_ref, cb_ref, lg_ref, lb_ref, o_ref, ysh, yconv, *, t):
    rt = 64
    ysh[0, 0:CONV_HALO, :] = jnp.zeros((CONV_HALO, CONV_CH), F32)
    ysh[0, CONV_HALO + t:t + CONV_TAIL, :] = jnp.zeros((CONV_TAIL - CONV_HALO, CONV_CH), F32)

    def fill(r, _):
        s = pl.multiple_of(r * rt, rt)
        a = cacg_ref[pl.ds(s, rt), 0:CONV_CH]
        g = cacg_ref[pl.ds(s, rt), CONV_CH:2 * CONV_CH]
        ysh[0, pl.ds(CONV_HALO + s, rt), :] = a * jax.nn.sigmoid(g)
        return 0

    lax.fori_loop(0, t // rt, fill, 0)

    def shift(r, _):
        s = pl.multiple_of(r * rt, rt)
        win = ysh[0, pl.ds(s, rt + 8), :]
        for res in range(1, 8):
            ysh[res, pl.ds(s, rt), :] = pltpu.roll(win, rt + 8 - res, axis=0)[0:rt, :]
        return 0

    lax.fori_loop(0, t // rt + 1, shift, 0)

    def tile(r, _):
        s = pl.multiple_of(r * rt, rt)
        acc = jnp.zeros((rt, CONV_CH), F32)
        for kk in range(CONV_K):
            off = kk + CONV_HALO - CONV_K // 2
            rows = ysh[off % 8, pl.ds(pl.multiple_of(s + 8 * (off // 8), 8), rt), :]
            acc = acc + w_ref[kk:kk + 1, :] * rows
        yconv[pl.ds(s, rt), :] = acc + cb_ref[...]
        return 0

    lax.fori_loop(0, t // rt, tile, 0)

    def norm(r, _):
        s = pl.multiple_of(r * rt, rt)
        y = yconv[pl.ds(s, rt), :]
        mu = jnp.mean(y, axis=-1, keepdims=True)
        yc = y - mu
        var = jnp.mean(yc * yc, axis=-1, keepdims=True)
        z = yc * lax.rsqrt(var + EPS) * lg_ref[...] + lb_ref[...]
        o_ref[pl.ds(s, rt), :] = (z * jax.nn.sigmoid(z)).astype(BF16)
        return 0

    lax.fori_loop(0, t // rt, norm, 0, unroll=4)


def _conv(cacg, conv_w, conv_b, ln_g, ln_b):
    bsz, t, _ = cacg.shape
    row = lambda a: a.reshape(1, CONV_CH)
    vec = pl.BlockSpec((1, CONV_CH), lambda b: (0, 0))
    return pl.pallas_call(
        functools.partial(_conv_kernel, t=t),
        grid=(bsz,),
        in_specs=[pl.BlockSpec((None, t, 2 * CONV_CH), lambda b: (b, 0, 0)),
                  pl.BlockSpec((CONV_K + 1, CONV_CH), lambda b: (0, 0)), vec, vec, vec],
        out_specs=pl.BlockSpec((None, t, CONV_CH), lambda b: (b, 0, 0)),
        out_shape=jax.ShapeDtypeStruct((bsz, t, CONV_CH), BF16),
        scratch_shapes=[pltpu.VMEM((8, t + CONV_TAIL, CONV_CH), F32), pltpu.VMEM((t, CONV_CH), F32)],
        compiler_params=_params(("parallel",), 40),
        name="conformer_conv",
    )(cacg, jnp.pad(conv_w, ((0, 1), (0, 0))), row(conv_b), row(ln_g), row(ln_b))


@functools.lru_cache(maxsize=None)
def _dft_tables(t):
    j = np.arange(t, dtype=np.int64)
    ang = 2.0 * np.pi * ((j[:, None] * j[None, :]) % t).astype(np.float64) / t
    wt = np.concatenate([np.cos(ang), -np.sin(ang)], axis=1).astype(np.float32)
    c = np.arange(FOURIER_CH, dtype=np.int64)
    grp, idx = c // FOURIER_GROUP_CH, c % FOURIER_GROUP_CH
    same = grp[:, None] == grp[None, :]
    angc = 2.0 * np.pi * ((idx[:, None] * idx[None, :]) % FOURIER_GROUP_CH).astype(np.float64) / FOURIER_GROUP_CH
    cc = np.where(same, np.cos(angc), 0.0).astype(np.float32)
    sc = np.where(same, np.sin(angc), 0.0).astype(np.float32)
    return wt, cc, sc


def _fourier_kernel(z_ref, cc_ref, sc_ref, wt_ref, o_ref, zcs, *, t, scale):
    @pl.when(pl.program_id(1) == 0)
    def _():
        z = z_ref[...]
        zcs[0:t, :] = _dot(z, cc_ref[...]).astype(BF16)
        zcs[t:2 * t, :] = _dot(z, sc_ref[...]).astype(BF16)

    o_ref[...] = (_dot(wt_ref[...], zcs[...]) * scale).astype(BF16)


def _fourier(z):
    bsz, t, ch = z.shape
    wt, cc, sc = _dft_tables(t)
    tm = min(t, 512)
    mat = pl.BlockSpec((ch, ch), lambda b, i: (0, 0))
    return pl.pallas_call(
        functools.partial(_fourier_kernel, t=t, scale=float((t * FOURIER_GROUP_CH) ** -0.5)),
        grid=(bsz, t // tm),
        in_specs=[pl.BlockSpec((None, t, ch), lambda b, i: (b, 0, 0)), mat, mat,
                  pl.BlockSpec((tm, 2 * t), lambda b, i: (i, 0))],
        out_specs=pl.BlockSpec((None, tm, ch), lambda b, i: (b, i, 0)),
        out_shape=jax.ShapeDtypeStruct((bsz, t, ch), BF16),
        scratch_shapes=[pltpu.VMEM((2 * t, ch), BF16)],
        compiler_params=_params(("parallel", "arbitrary"), 32),
        name="fourier_mix",
    )(z, jnp.asarray(cc).astype(BF16), jnp.asarray(sc).astype(BF16), jnp.asarray(wt).astype(BF16))


def _outproj_kernel(x_ref, hr_ref, hc_ref, o_ref, cv_ref, fo_ref, w_ref, gt_ref, g2_ref, sh_ref, sc_ref, wr_ref,
                    x1_ref, xn_ref, aff_ref):
    half = MLSTM_WIDTH // 2
    a, b = MLSTM_WIDTH, MLSTM_WIDTH + CONV_CH
    og = jax.nn.sigmoid(o_ref[...])
    y = (_dot((hr_ref[...] * og[:, 0:half]).astype(BF16), w_ref[0:half, :])
         + _dot((hc_ref[...] * og[:, half:a]).astype(BF16), w_ref[half:a, :])
         + _dot(cv_ref[...], w_ref[a:b, :]) + _dot(fo_ref[...], w_ref[b:, :]))
    x1 = x_ref[...] + gt_ref[...] * y
    x1_ref[...] = x1
    xn = _modnorm(x1, g2_ref[...], sh_ref[...], sc_ref[...])
    xn_ref[...] = xn.astype(BF16)
    tm = xn.shape[0]
    r = _dot(jnp.concatenate(_split_bf16(xn), axis=0), jnp.concatenate(_split_bf16(wr_ref[...]), axis=1))
    logits = (r[:tm, :LANES] + r[:tm, LANES:]) + (r[tm:, :LANES] + r[tm:, LANES:])
    lane = lax.broadcasted_iota(I32, logits.shape, 1)
    logits = jnp.where(lane < N_EXPERTS, logits, -jnp.inf)
    e = jnp.exp(logits - jnp.max(logits, axis=-1, keepdims=True))
    aff_ref[...] = e / jnp.sum(e, axis=-1, keepdims=True)


def _outproj(x, hr, hc, o, cv, fo, w_out, gt, g2, sh, sc, wr):
    bsz, t, d = x.shape
    tm = min(t, 512)
    tok = lambda n: pl.BlockSpec((None, tm, n), lambda b, i: (b, i, 0))
    per_b = pl.BlockSpec((None, 1, d), lambda b, i: (b, 0, 0))
    return pl.pallas_call(
        _outproj_kernel,
        grid=(bsz, t // tm),
        in_specs=[tok(d), tok(MLSTM_WIDTH // 2), tok(MLSTM_WIDTH // 2), tok(MLSTM_WIDTH), tok(CONV_CH), tok(FOURIER_CH),
                  pl.BlockSpec((d, d), lambda b, i: (0, 0)), per_b,
                  pl.BlockSpec((1, d), lambda b, i: (0, 0)), per_b, per_b,
                  pl.BlockSpec((d, LANES), lambda b, i: (0, 0))],
        out_specs=[tok(d), tok(d), tok(LANES)],
        out_shape=[jax.ShapeDtypeStruct((bsz, t, d), F32), jax.ShapeDtypeStruct((bsz, t, d), BF16),
                   jax.ShapeDtypeStruct((bsz, t, LANES), F32)],
        compiler_params=_params(("parallel", "parallel"), 40),
        name="outproj_router",
    )(x, hr, hc, o, cv, fo, w_out, gt, g2.reshape(1, d), sh, sc, wr)


def _select_kernel(a_ref, sp_ref, *, cap):
    a = a_ref[...]
    n = a.shape[0]
    blk = min(n, MXU_DIM)

    def count(mask):
        return jnp.sum(jnp.where(mask, 1.0, 0.0), axis=0, keepdims=True)

    def search(i, thr):
        cand = thr | jnp.left_shift(jnp.int32(1), 30 - i)
        return jnp.where(count(a >= lax.bitcast_convert_type(cand, F32)) >= cap, cand, thr)

    thr = lax.bitcast_convert_type(lax.fori_loop(0, 31, search, jnp.zeros((1, LANES), I32)), F32)
    gt = a > thr
    eq = a == thr
    before = jnp.where(lax.broadcasted_iota(I32, (blk, blk), 1) < lax.broadcasted_iota(I32, (blk, blk), 0), 1.0, 0.0).astype(BF16)

    def excl_cumsum(mask):
        x = jnp.where(mask, 1.0, 0.0)
        run = jnp.zeros((1, LANES), F32)
        parts = []
        for j in range(n // blk):
            xb = x[j * blk:(j + 1) * blk]
            parts.append(_dot(before, xb.astype(BF16)) + run)
            run = run + jnp.sum(xb, axis=0, keepdims=True)
        return jnp.concatenate(parts, axis=0)

    sel = gt | (eq & (excl_cumsum(eq) < cap - count(gt)))
    sp_ref[...] = jnp.where(sel, excl_cumsum(sel).astype(I32), -1)


def _select(aff_sets, cap):
    ng, n, _ = aff_sets.shape
    blk = pl.BlockSpec((None, n, LANES), lambda g: (g, 0, 0))
    return pl.pallas_call(
        functools.partial(_select_kernel, cap=cap),
        grid=(ng,),
        in_specs=[blk],
        out_specs=blk,
        out_shape=jax.ShapeDtypeStruct((ng, n, LANES), I32),
        compiler_params=_params(("parallel",), 32),
        name="ec_select",
    )(aff_sets)


def _ffn_kernel(sp_ref, ar_ref, xn_ref, wg_ref, wu_ref, wd_ref, ys_ref, wg_b, wu_b, wd_b, *, cap):
    @pl.when(pl.program_id(1) == 0)
    def _():
        rt = 256
        for src, dst in ((wg_ref, wg_b), (wu_ref, wu_b), (wd_ref, wd_b)):
            for r in range(0, src.shape[0], rt):
                dst[r:r + rt, :] = src[r:r + rt, :].astype(BF16)

    nb, _, n = sp_ref.shape
    slot = lax.broadcasted_iota(I32, (cap, n), 0)
    xs, gates = [], []
    for bi in range(nb):
        hit = slot == sp_ref[bi]
        gates.append(jnp.sum(jnp.where(hit, ar_ref[bi], 0.0), axis=1, keepdims=True))
        xs.append(_dot(jnp.where(hit, 1.0, 0.0).astype(BF16), xn_ref[bi]).astype(BF16))
    xs = xs[0] if nb == 1 else jnp.concatenate(xs, axis=0)
    gate = gates[0] if nb == 1 else jnp.concatenate(gates, axis=0)
    h1 = _dot(xs, wg_b[...])
    hid = (h1 * jax.nn.sigmoid(h1) * _dot(xs, wu_b[...])).astype(BF16)
    ys = (_dot(hid, wd_b[...]) * gate).astype(BF16)
    for bi in range(nb):
        ys_ref[bi] = ys[bi * cap:(bi + 1) * cap]


def _ffn(sp_row, aff_row, xn, wg, wu, wd, layer, cap):
    bsz, n, d = xn.shape
    _, ne, _, hid = wg.shape
    nb = min(bsz, max(1, MXU_DIM // cap))
    row = pl.BlockSpec((nb, None, 1, n), lambda e, j: (j, e, 0, 0))
    return pl.pallas_call(
        functools.partial(_ffn_kernel, cap=cap),
        grid=(ne, bsz // nb),
        in_specs=[row, row,
                  pl.BlockSpec((nb, n, d), lambda e, j: (j, 0, 0)),
                  pl.BlockSpec((None, None, d, hid), lambda e, j: (layer, e, 0, 0)),
                  pl.BlockSpec((None, None, d, hid), lambda e, j: (layer, e, 0, 0)),
                  pl.BlockSpec((None, None, hid, d), lambda e, j: (layer, e, 0, 0))],
        out_specs=pl.BlockSpec((nb, None, cap, d), lambda e, j: (j, e, 0, 0)),
        out_shape=jax.ShapeDtypeStruct((bsz, ne, cap, d), BF16),
        scratch_shapes=[pltpu.VMEM((d, hid), BF16), pltpu.VMEM((d, hid), BF16), pltpu.VMEM((hid, d), BF16)],
        compiler_params=_params(("arbitrary", "arbitrary"), 58),
        name="ec_ffn",
    )(sp_row, aff_row, xn, wg, wu, wd)


def _combine_kernel(x_ref, spc_ref, ys_ref, gt_ref, gf_ref, o_ref, *, cap, final):
    spc = spc_ref[...]
    tm = spc.shape[0]
    slot = lax.broadcasted_iota(I32, (tm, cap), 1)
    acc = jnp.zeros(x_ref.shape, F32)
    for e in range(N_EXPERTS):
        hit = jnp.where(spc[:, e:e + 1] == slot, 1.0, 0.0).astype(BF16)
        acc = acc + _dot(hit, ys_ref[e])
    out = x_ref[...] + gt_ref[...] * acc
    if final:
        out = out * lax.rsqrt(jnp.mean(out * out, axis=-1, keepdims=True) + EPS) * gf_ref[...]
    o_ref[...] = out


def _combine(x, sp_col, ys, gt, g_final, cap, final):
    bsz, n, d = x.shape
    tm = min(n, 512)
    return pl.pallas_call(
        functools.partial(_combine_kernel, cap=cap, final=final),
        grid=(bsz, n // tm),
        in_specs=[pl.BlockSpec((None, tm, d), lambda b, i: (b, i, 0)),
                  pl.BlockSpec((None, tm, N_EXPERTS), lambda b, i: (b, i, 0)),
                  pl.BlockSpec((None, N_EXPERTS, cap, d), lambda b, i: (b, 0, 0, 0)),
                  pl.BlockSpec((None, 1, d), lambda b, i: (b, 0, 0)),
                  pl.BlockSpec((1, d), lambda b, i: (0, 0))],
        out_specs=pl.BlockSpec((None, tm, d), lambda b, i: (b, i, 0)),
        out_shape=jax.ShapeDtypeStruct((bsz, n, d), F32),
        compiler_params=_params(("parallel", "parallel"), 40),
        name="ec_combine_final" if final else "ec_combine",
    )(x, sp_col, ys, gt, g_final.reshape(1, d))


def _moe(x1, xn2, aff, wg, wu, wd, layer, gt2, g_final, final):
    bsz, n, _ = x1.shape
    cap = EC_CAPACITY_FACTOR * n // N_EXPERTS
    per_group = LANES // N_EXPERTS
    ng = -(-bsz // per_group)
    aff_e = aff[:, :, :N_EXPERTS]
    sets = jnp.pad(aff_e, ((0, ng * per_group - bsz), (0, 0), (0, 0)))
    sets = sets.reshape(ng, per_group, n, N_EXPERTS).transpose(0, 2, 1, 3).reshape(ng, n, LANES)
    selpos = _select(sets, cap)
    sp_col = selpos.reshape(ng, n, per_group, N_EXPERTS).transpose(0, 2, 1, 3).reshape(ng * per_group, n, N_EXPERTS)[:bsz]
    sp_row = sp_col.transpose(0, 2, 1).reshape(bsz, N_EXPERTS, 1, n)
    aff_row = aff_e.transpose(0, 2, 1).reshape(bsz, N_EXPERTS, 1, n)
    ys = _ffn(sp_row, aff_row, xn2, wg, wu, wd, layer, cap)
    return _combine(x1, sp_col, ys, gt2, g_final, cap, final)


def kernel(x, c, ctx, c_ctx, w_ada, b_ada, g_norm1, w_in, b_gates, g_hnorm, conv_w, conv_b, conv_ln_g, conv_ln_b,
           w_out, g_norm2, w_router, w_e_gate, w_e_up, w_e_down, g_final):
    bsz, t, d = x.shape
    depth = w_ada.shape[0]
    rows = t // GRID_W
    n_cond = 16
    cs = jnp.concatenate([c, c_ctx[None, :], jnp.zeros((n_cond - bsz - 1, d), F32)], axis=0)
    mods = _ada(cs, w_ada, b_ada)

    npair = N_HEADS // 2
    zero_states = (jnp.zeros((bsz, npair, 2, HEAD_DIM, HEAD_DIM), F32), jnp.zeros((bsz, npair, 8, HEAD_DIM), F32)) * 2
    h_ctx = ctx
    for l in range(depth):
        last = l == depth - 1
        lat = [mods[l, :bsz, i * d:(i + 1) * d].reshape(bsz, 1, d) for i in range(N_ADA)]
        cmod = [jnp.broadcast_to(mods[l, bsz, i * d:(i + 1) * d].reshape(1, 1, d), (bsz, 1, d)) for i in range(N_ADA)]
        w_l = w_in[l]
        qkvg_cols = 3 * MLSTM_WIDTH + N_GATES * N_HEADS
        w_p = jnp.concatenate([w_l[:, :3 * MLSTM_WIDTH],
                               jnp.pad(w_l[:, 3 * MLSTM_WIDTH:qkvg_cols], ((0, 0), (0, LANES - N_GATES * N_HEADS))),
                               w_l[:, qkvg_cols:]], axis=1).astype(BF16)
        bg = jnp.pad(b_gates[l].reshape(1, N_HEADS * N_GATES), ((0, 0), (0, LANES - N_HEADS * N_GATES)))
        w_o = w_out[l].astype(BF16)
        w_r = jnp.pad(w_router[l], ((0, 0), (0, LANES - N_EXPERTS)))
        gh = g_hnorm[l].reshape(N_HEADS, HEAD_DIM, 1)

        def mlstm(xs, md, states, grid_rows, full):
            qkv, gates, extra = _inproj(xs, md[0], md[1], g_norm1[l], w_p, bg, full, grid_rows)
            return _mlstm(qkv, gates, states, gh, full, grid_rows), extra

        def sublayers(xs, md, states, grid_rows, final):
            (hr, hc, *st), (o, cacg, fr) = mlstm(xs, md, states, grid_rows, True)
            cv = _conv(cacg, conv_w[l], conv_b[l], conv_ln_g[l], conv_ln_b[l])
            fo = _fourier(fr)
            x1, xn2, aff = _outproj(xs, hr, hc, o, cv, fo, w_o, md[2], g_norm2[l], md[3], md[4], w_r)
            return _moe(x1, xn2, aff, w_e_gate, w_e_up, w_e_down, l, md[5], g_final, final), st

        if last:
            st, _ = mlstm(h_ctx, cmod, zero_states, None, False)
        else:
            h_ctx, st = sublayers(h_ctx, cmod, zero_states, None, False)
        x, _ = sublayers(x, lat, tuple(st), rows, last)
    return x
```

```python
import functools

import jax
import jax.numpy as jnp
import numpy as np
from jax import lax
from jax.experimental import pallas as pl
from jax.experimental.pallas import tpu as pltpu

F32 = jnp.float32
BF16 = jnp.bfloat16
I32 = jnp.int32

D_MODEL = 1024
GRID_W = 64
N_HEADS = 4
HEAD_DIM = 128
MLSTM_WIDTH = N_HEADS * HEAD_DIM
N_ROW_HEADS = N_HEADS // 2
CONV_CH = 256
CONV_K = 31
FOURIER_CH = 256
FOURIER_GROUP_CH = 64
N_GATES = 4
N_EXPERTS = 16
EC_CAPACITY_FACTOR = 2
N_ADA = 6
EPS = 1e-6

LANES = 128
MXU_DIM = 256
MLSTM_CHUNK = 128
MAX_CHUNKS = 16
CONV_HALO = 16
CONV_TAIL = 128
MIB = 1024 * 1024

SEC_QKV = (0, 3 * MLSTM_WIDTH)
SEC_G = (SEC_QKV[1], SEC_QKV[1] + LANES)
SEC_O = (SEC_G[1], SEC_G[1] + MLSTM_WIDTH)
SEC_CACG = (SEC_O[1], SEC_O[1] + 2 * CONV_CH)
SEC_FR = (SEC_CACG[1], SEC_CACG[1] + FOURIER_CH)
IN_COLS_PAD = SEC_FR[1]

ST_NF = 2 * HEAD_DIM
ST_NB = ST_NF + 16
ST_ROWS = ST_NB + 16
UP_ROWS = 2 * HEAD_DIM + 16


def _dot(a, b):
    return jnp.dot(a, b, preferred_element_type=F32)


def _dot_nt(a, b):
    return lax.dot_general(a, b, (((1,), (1,)), ((), ())), preferred_element_type=F32)


def _split_bf16(a):
    hi = a.astype(BF16)
    lo = (a - hi.astype(F32)).astype(BF16)
    return hi, lo


def _dot3(a, w):
    ah, al = _split_bf16(a)
    wh, wl = _split_bf16(w)
    return _dot(ah, wh) + (_dot(al, wh) + _dot(ah, wl))


def _modnorm(x, g, sh, sc):
    ms = jnp.mean(x * x, axis=-1, keepdims=True)
    return x * lax.rsqrt(ms + EPS) * g * (1.0 + sc) + sh


def _rows(n_rows, width, *vals):
    rid = lax.broadcasted_iota(I32, (n_rows, width), 0)
    out = jnp.zeros((n_rows, width), F32)
    for i, v in enumerate(vals):
        out = jnp.where(rid == i, v, out)
    return out


def _params(sem, vmem_mib):
    return pltpu.CompilerParams(dimension_semantics=sem, vmem_limit_bytes=vmem_mib * MIB)


def _ada_kernel(c_ref, w_ref, b_ref, o_ref):
    c = c_ref[...]
    o_ref[...] = _dot3(c * jax.nn.sigmoid(c), w_ref[...]) + b_ref[...]


def _ada(cs, w_ada, b_ada):
    n_layers, d, n6 = w_ada.shape
    r = cs.shape[0]
    tn = 1536
    return pl.pallas_call(
        _ada_kernel,
        grid=(n_layers, n6 // tn),
        in_specs=[
            pl.BlockSpec((r, d), lambda l, j: (0, 0)),
            pl.BlockSpec((None, d, tn), lambda l, j: (l, 0, j)),
            pl.BlockSpec((None, 1, tn), lambda l, j: (l, 0, j)),
        ],
        out_specs=pl.BlockSpec((None, r, tn), lambda l, j: (l, 0, j)),
        out_shape=jax.ShapeDtypeStruct((n_layers, r, n6), F32),
        compiler_params=_params(("arbitrary", "arbitrary"), 40),
        name="ada_mod",
    )(cs, w_ada, b_ada.reshape(n_layers, 1, n6))


def _inproj_kernel(x_ref, sh_ref, sc_ref, g_ref, w_ref, bg_ref, *out_refs, full, colmajor):
    qkv_refs, gates_ref = out_refs[:6], out_refs[6]
    xn = _modnorm(x_ref[...], g_ref[...], sh_ref[...], sc_ref[...]).astype(BF16)
    tm = xn.shape[0]
    w = MLSTM_WIDTH
    for i in range(3):
        r = _dot(xn, w_ref[:, i * w:(i + 1) * w])
        if i == 1:
            r = r * HEAD_DIM ** -0.5
        row_ref, col_ref = qkv_refs[2 * i], qkv_refs[2 * i + 1]
        for hh in range(N_ROW_HEADS):
            row_ref[hh] = r[:, hh * HEAD_DIM:(hh + 1) * HEAD_DIM].astype(BF16)
            piece = r[:, (N_ROW_HEADS + hh) * HEAD_DIM:(N_ROW_HEADS + hh + 1) * HEAD_DIM]
            if colmajor:
                piece = jnp.swapaxes(piece.reshape(tm // GRID_W, GRID_W, HEAD_DIM), 0, 1)
            col_ref[hh] = piece.astype(BF16)
    gates_ref[...] = _dot(xn, w_ref[:, SEC_G[0]:SEC_G[1]]) + bg_ref[...]
    if full:
        o_ref, cacg_ref, fr_ref = out_refs[7:]
        o_ref[...] = _dot(xn, w_ref[:, SEC_O[0]:SEC_O[1]])
        cacg_ref[...] = _dot(xn, w_ref[:, SEC_CACG[0]:SEC_CACG[1]])
        fr_ref[...] = _dot(xn, w_ref[:, SEC_FR[0]:SEC_FR[1]]).astype(BF16)


def _inproj(x, sh, sc, g, w, bg, full, rows):
    bsz, t, d = x.shape
    tm = min(t, 1024)
    nh = N_ROW_HEADS
    ncols = IN_COLS_PAD if full else SEC_G[1]
    tok = lambda n: pl.BlockSpec((None, tm, n), lambda b, i: (b, i, 0))
    per_b = pl.BlockSpec((None, 1, d), lambda b, i: (b, 0, 0))
    row_shape = jax.ShapeDtypeStruct((bsz, nh, t, HEAD_DIM), BF16)
    row_spec = pl.BlockSpec((None, nh, tm, HEAD_DIM), lambda b, i: (b, 0, i, 0))
    if rows is None:
        col_shape, col_spec = row_shape, row_spec
    else:
        col_shape = jax.ShapeDtypeStruct((bsz, nh, GRID_W, rows, HEAD_DIM), BF16)
        col_spec = pl.BlockSpec((None, nh, GRID_W, tm // GRID_W, HEAD_DIM), lambda b, i: (b, 0, 0, i, 0))
    out_shape = [row_shape, col_shape] * 3 + [jax.ShapeDtypeStruct((bsz, t, LANES), F32)]
    out_specs = [row_spec, col_spec] * 3 + [tok(LANES)]
    if full:
        out_shape += [jax.ShapeDtypeStruct((bsz, t, MLSTM_WIDTH), F32),
                      jax.ShapeDtypeStruct((bsz, t, 2 * CONV_CH), F32),
                      jax.ShapeDtypeStruct((bsz, t, FOURIER_CH), BF16)]
        out_specs += [tok(MLSTM_WIDTH), tok(2 * CONV_CH), tok(FOURIER_CH)]
    outs = pl.pallas_call(
        functools.partial(_inproj_kernel, full=full, colmajor=rows is not None),
        grid=(bsz, t // tm),
        in_specs=[tok(d), per_b, per_b,
                  pl.BlockSpec((1, d), lambda b, i: (0, 0)),
                  pl.BlockSpec((d, ncols), lambda b, i: (0, 0)),
                  pl.BlockSpec((1, LANES), lambda b, i: (0, 0))],
        out_specs=out_specs,
        out_shape=out_shape,
        compiler_params=_params(("parallel", "parallel"), 56),
        name="inproj_full" if full else "inproj_qkvg",
    )(x, sh, sc, g.reshape(1, d), w, bg)
    qkv = [a.reshape(bsz, nh, t, HEAD_DIM) for a in outs[:6]]
    return qkv, outs[6], outs[7:]


def _logsig(x):
    return jnp.minimum(x, 0.0) - jnp.log1p(jnp.exp(-jnp.abs(x)))


def _tri_masks(n):
    a0 = lax.broadcasted_iota(I32, (n, n), 0)
    a1 = lax.broadcasted_iota(I32, (n, n), 1)
    return a0 <= a1, a0 >= a1


def _split3(x):
    hi = x.astype(BF16).astype(F32)
    r1 = x - hi
    mid = r1.astype(BF16).astype(F32)
    return hi, mid, r1 - mid


def _mlstm_consts(tri_scr, ones_scr):
    lc = MLSTM_CHUNK
    u = lax.broadcasted_iota(I32, (lc, 2 * lc), 0)
    t2 = lax.broadcasted_iota(I32, (lc, 2 * lc), 1)
    tri_scr[...] = jnp.where(((t2 < lc) & (u <= t2)) | ((t2 >= lc) & (u >= t2 - lc)), 1.0, 0.0).astype(BF16)
    r = lax.broadcasted_iota(I32, (16, 2 * lc), 0)
    t16 = lax.broadcasted_iota(I32, (16, 2 * lc), 1)
    ones_scr[...] = jnp.where(((r < 3) & (t16 < lc)) | ((r >= 3) & (r < 6) & (t16 >= lc)), 1.0, 0.0).astype(BF16)


def _mlstm_gates(head, g_ref, tri_scr, rowg_scr, stat_scr, n_chunks, cm_rows):
    lc = MLSTM_CHUNK
    pick = jnp.where(lax.broadcasted_iota(I32, (16, LANES), 1) == N_GATES * head + lax.broadcasted_iota(I32, (16, LANES), 0),
                     1.0, 0.0).astype(BF16)
    chunk_id = lax.broadcasted_iota(I32, (MAX_CHUNKS, lc), 0)

    def gather(c, acc):
        if cm_rows is None:
            g = g_ref[pl.ds(pl.multiple_of(c * lc, lc), lc), :]
        else:
            cols = lc // cm_rows
            g = jnp.concatenate([g_ref[pl.ds(c * cols + j, cm_rows, stride=GRID_W), :] for j in range(cols)], axis=0)
        rows = sum(_dot_nt(pick, part.astype(BF16)) for part in _split3(g))
        return tuple(jnp.where(chunk_id == c, rows[i:i + 1], a) for i, a in enumerate(acc))

    zeros = jnp.zeros((MAX_CHUNKS, lc), F32)
    i_f, f_f, i_b, f_b = lax.fori_loop(0, n_chunks, gather, (zeros,) * N_GATES, unroll=min(4, n_chunks))
    lf_f, lf_b = _logsig(f_f), _logsig(f_b)
    b_f = sum(_dot(part.astype(BF16), tri_scr[:, 0:lc]) for part in _split3(lf_f))
    b_b = sum(_dot(part.astype(BF16), tri_scr[:, lc:]) for part in _split3(lf_b))
    bl_f = jnp.sum(lf_f, axis=1, keepdims=True)
    bl_b = jnp.sum(lf_b, axis=1, keepdims=True)
    ci_f, ci_b = i_f - b_f, i_b - b_b
    ml_f = jnp.max(ci_f + bl_f, axis=1, keepdims=True)
    ml_b = jnp.max(ci_b + bl_b, axis=1, keepdims=True)
    for r, val in enumerate((jnp.exp(ci_f + bl_f - ml_f), jnp.exp(ci_b + bl_b - ml_b), b_f, b_b, ci_f, ci_b)):
        rowg_scr[:, r, :] = val
    for r, val in enumerate((bl_f, bl_b, ml_f, ml_b)):
        stat_scr[:, r, :] = jnp.broadcast_to(val, (MAX_CHUNKS, HEAD_DIM))


def _mlstm_increment(c, k_ref, v_ref, rowg_scr, up_scr, vt_scr):
    lc = MLSTM_CHUNK
    s = pl.multiple_of(c * lc, lc)
    v_t = v_ref[pl.ds(s, lc), :].astype(F32).T
    rg = rowg_scr[c]
    w_f, w_b = rg[0:1], rg[1:2]
    lhs = jnp.concatenate([(v_t * w_f).astype(BF16), (v_t * w_b).astype(BF16),
                           _rows(16, lc, w_f, w_b).astype(BF16)], axis=0)
    up_scr[c] = _dot(lhs, k_ref[pl.ds(s, lc), :])
    if vt_scr is not None:
        vt_scr[c] = v_t.astype(BF16)


def _mlstm_state_step(cc, carry, up_scr, stat_scr, st_scr, mst_scr, fwd):
    cmat, n, m = carry
    off, row = (0, 0) if fwd else (HEAD_DIM, 1)
    if st_scr is not None:
        st_scr[cc, off:off + HEAD_DIM, :] = cmat.astype(BF16)
        nrow = ST_NF if fwd else ST_NB
        st_scr[cc, nrow:nrow + 16, :] = _rows(16, HEAD_DIM, n).astype(BF16)
        mst_scr[cc, row:row + 1, :] = jnp.broadcast_to(m, (1, HEAD_DIM))
    st = stat_scr[cc]
    bl, ml = st[row:row + 1, 0:1], st[2 + row:3 + row, 0:1]
    m_new = jnp.maximum(bl + m, ml)
    keep = jnp.exp(bl + m - m_new)
    gain = jnp.exp(ml - m_new)
    cmat = keep * cmat + gain * up_scr[cc, off:off + HEAD_DIM, :]
    n = keep * n + gain * up_scr[cc, 2 * HEAD_DIM + row:2 * HEAD_DIM + row + 1, :]
    return cmat, n, m_new


def _mlstm_outputs(c, chains, ones_scr):
    lc = MLSTM_CHUNK
    s = pl.multiple_of(c * lc, lc)
    le, ge = _tri_masks(lc)
    qs = [ch[0][pl.ds(s, lc), :] for ch in chains]
    qk_ts = [_dot_nt(ch[1][pl.ds(s, lc), :], q) for ch, q in zip(chains, qs)]
    g_ts = [_dot_nt(ch[5][c], q) for ch, q in zip(chains, qs)]
    rgs = [ch[4][c] for ch in chains]
    cis = [lax.dot_general(_rows(16, lc, *_split3(rg[4:5]), *_split3(rg[5:6])).astype(BF16), ones_scr[...],
                           (((0,), (0,)), ((), ())), preferred_element_type=F32) for rg in rgs]

    def direction(qk_t, valid, b_row, ci, m_st, qn):
        dlog = jnp.where(valid, b_row + ci, -jnp.inf)
        inter = b_row + m_st
        m_t = jnp.maximum(inter, jnp.max(dlog, axis=0, keepdims=True))
        s_t = qk_t * jnp.exp(dlog - m_t)
        w_inter = jnp.exp(inter - m_t)
        den = w_inter * qn + jnp.sum(s_t, axis=0, keepdims=True)
        r = 1.0 / jnp.maximum(jnp.abs(den), jnp.exp(-m_t))
        return r * s_t, r * w_inter

    mixed = []
    for ch, qk_t, g_t, rg, ci in zip(chains, qk_ts, g_ts, rgs, cis):
        ms = ch[6][c]
        sf, cf = direction(qk_t, le, rg[2:3], ci[:, 0:lc], ms[0:1, 0:1], g_t[ST_NF:ST_NF + 1])
        sb, cb = direction(qk_t, ge, rg[3:4], ci[:, lc:], ms[1:2, 0:1], g_t[ST_NB:ST_NB + 1])
        mixed.append(((sf + sb).astype(BF16), cf * g_t[0:HEAD_DIM] + cb * g_t[HEAD_DIM:2 * HEAD_DIM]))
    h_ts = [inter_part + _dot(ch[7][c], s_mix) for ch, (s_mix, inter_part) in zip(chains, mixed)]
    for ch, h_t in zip(chains, h_ts):
        gh_ref, h_ref, cm_rows = ch[2], ch[3], ch[8]
        h = (h_t * lax.rsqrt(jnp.mean(h_t * h_t, axis=0, keepdims=True) + EPS) * gh_ref[...]).T
        if cm_rows is None:
            h_ref[pl.ds(s, lc), :] = h
        else:
            cols = lc // cm_rows
            for j in range(cols):
                h_ref[pl.ds(c * cols + j, cm_rows, stride=GRID_W), :] = h[j * cm_rows:(j + 1) * cm_rows]


def _mlstm_kernel(*refs, n_chunks, with_outputs, cm_rows):
    n_slot_in = 6
    g_ref = refs[0]
    slots_in = [refs[1:1 + n_slot_in], refs[1 + n_slot_in:1 + 2 * n_slot_in]]
    rest = refs[1 + 2 * n_slot_in:]
    if with_outputs:
        h_refs, rest = rest[:2], rest[2:]
    state_out, scratch = rest[:4], rest[4:]
    tri_scr, ones_scr, scratch = scratch[0], scratch[1], scratch[2:]
    n_scr = 6 if with_outputs else 3
    slots_scr = [scratch[:n_scr], scratch[n_scr:2 * n_scr]]
    slot_rows = [None, cm_rows]
    _mlstm_consts(tri_scr, ones_scr)
    for slot, scr in enumerate(slots_scr):
        _mlstm_gates(pl.program_id(1) + slot * N_ROW_HEADS, g_ref, tri_scr, scr[0], scr[1], n_chunks, slot_rows[slot])

    def increments(c, _):
        for (q_ref, k_ref, v_ref, c0_ref, n0_ref, gh_ref), scr in zip(slots_in, slots_scr):
            _mlstm_increment(c, k_ref, v_ref, scr[0], scr[2], scr[5] if with_outputs else None)
        return 0

    lax.fori_loop(0, n_chunks, increments, 0, unroll=min(4, n_chunks))

    def states(i, carry):
        out = []
        for slot, scr in enumerate(slots_scr):
            st_scr, mst_scr = (scr[3], scr[4]) if with_outputs else (None, None)
            out.append(_mlstm_state_step(i, carry[2 * slot], scr[2], scr[1], st_scr, mst_scr, True))
            out.append(_mlstm_state_step(n_chunks - 1 - i, carry[2 * slot + 1], scr[2], scr[1], st_scr, mst_scr, False))
        return tuple(out)

    init = []
    for (q_ref, k_ref, v_ref, c0_ref, n0_ref, gh_ref) in slots_in:
        init.append((c0_ref[0], n0_ref[0:1, :], n0_ref[2:3, 0:1]))
        init.append((c0_ref[1], n0_ref[1:2, :], n0_ref[3:4, 0:1]))
    final = lax.fori_loop(0, n_chunks, states, tuple(init))
    for slot in range(2):
        c_out, n_out = state_out[2 * slot], state_out[2 * slot + 1]
        (cf, nf, mf), (cb, nb, mb) = final[2 * slot], final[2 * slot + 1]
        c_out[0] = cf
        c_out[1] = cb
        n_out[...] = _rows(8, HEAD_DIM, nf, nb, mf, mb)

    if with_outputs:
        chains = [(q_ref, k_ref, gh_ref, h_refs[slot], scr[0], scr[3], scr[4], scr[5], slot_rows[slot])
                  for slot, ((q_ref, k_ref, v_ref, c0_ref, n0_ref, gh_ref), scr) in enumerate(zip(slots_in, slots_scr))]

        def outputs(c, _):
            _mlstm_outputs(c, chains, ones_scr)
            return 0

        lax.fori_loop(0, n_chunks, outputs, 0, unroll=min(2, n_chunks))


def _mlstm(qkv, gates, states, gh, with_outputs, cm_rows):
    bsz, npair, t, dh = qkv[0].shape
    nc = t // MLSTM_CHUNK
    lc = MLSTM_CHUNK
    seq = pl.BlockSpec((None, None, t, dh), lambda b, p: (b, p, 0, 0))
    cst = pl.BlockSpec((None, None, 2, dh, dh), lambda b, p: (b, p, 0, 0, 0))
    nst = pl.BlockSpec((None, None, 8, dh), lambda b, p: (b, p, 0, 0))

    def slot_specs(head_off):
        return [seq, seq, seq, cst, nst, pl.BlockSpec((None, dh, 1), lambda b, p: (p + head_off, 0, 0))]

    st_shapes = [jax.ShapeDtypeStruct((bsz, npair, 2, dh, dh), F32), jax.ShapeDtypeStruct((bsz, npair, 8, dh), F32)] * 2
    out_shape, out_specs = st_shapes, [cst, nst, cst, nst]
    assert nc <= MAX_CHUNKS
    slot_scratch = [pltpu.VMEM((MAX_CHUNKS, 8, lc), F32), pltpu.VMEM((MAX_CHUNKS, 8, dh), F32),
                    pltpu.VMEM((nc, UP_ROWS, dh), F32)]
    if with_outputs:
        h_shape = jax.ShapeDtypeStruct((bsz, t, npair * dh), F32)
        h_spec = pl.BlockSpec((None, t, dh), lambda b, p: (b, 0, p))
        out_shape = [h_shape, h_shape] + out_shape
        out_specs = [h_spec, h_spec] + out_specs
        slot_scratch += [pltpu.VMEM((nc, ST_ROWS, dh), BF16), pltpu.VMEM((nc, 8, dh), F32), pltpu.VMEM((nc, dh, lc), BF16)]
    q_r, q_c, k_r, k_c, v_r, v_c = qkv
    c_r, n_r, c_c, n_c = states
    return pl.pallas_call(
        functools.partial(_mlstm_kernel, n_chunks=nc, with_outputs=with_outputs, cm_rows=cm_rows),
        grid=(bsz, npair),
        in_specs=[pl.BlockSpec((None, t, LANES), lambda b, p: (b, 0, 0))] + slot_specs(0) + slot_specs(N_ROW_HEADS),
        out_specs=out_specs,
        out_shape=out_shape,
        scratch_shapes=[pltpu.VMEM((lc, 2 * lc), BF16), pltpu.VMEM((16, 2 * lc), BF16)] + slot_scratch * 2,
        compiler_params=_params(("parallel", "parallel"), 48),
        name="mlstm_full" if with_outputs else "mlstm_states",
    )(gates, q_r, k_r, v_r, c_r, n_r, gh, q_c, k_c, v_c, c_c, n_c, gh)


def _conv_kernel(cacg_ref, w_ref, cb_ref, lg_ref, lb_ref, o_ref, ysh, yconv, *, t):
    rt = 64
    ysh[0, 0:CONV_HALO, :] = jnp.zeros((CONV_HALO, CONV_CH), F32)
    ysh[0, CONV_HALO + t:t + CONV_TAIL, :] = jnp.zeros((CONV_TAIL - CONV_HALO, CONV_CH), F32)

    def fill(r, _):
        s = pl.multiple_of(r * rt, rt)
        a = cacg_ref[pl.ds(s, rt), 0:CONV_CH]
        g = cacg_ref[pl.ds(s, rt), CONV_CH:2 * CONV_CH]
        ysh[0, pl.ds(CONV_HALO + s, rt), :] = a * jax.nn.sigmoid(g)
        return 0

    lax.fori_loop(0, t // rt, fill, 0)

    def shift(r, _):
        s = pl.multiple_of(r * rt, rt)
        win = ysh[0, pl.ds(s, rt + 8), :]
        for res in range(1, 8):
            ysh[res, pl.ds(s, rt), :] = pltpu.roll(win, rt + 8 - res, axis=0)[0:rt, :]
        return 0

    lax.fori_loop(0, t // rt + 1, shift, 0)

    def tile(r, _):
        s = pl.multiple_of(r * rt, rt)
        acc = jnp.zeros((rt, CONV_CH), F32)
        for kk in range(CONV_K):
            off = kk + CONV_HALO - CONV_K // 2
            rows = ysh[off % 8, pl.ds(pl.multiple_of(s + 8 * (off // 8), 8), rt), :]
            acc = acc + w_ref[kk:kk + 1, :] * rows
        yconv[pl.ds(s, rt), :] = acc + cb_ref[...]
        return 0

    lax.fori_loop(0, t // rt, tile, 0)

    def norm(r, _):
        s = pl.multiple_of(r * rt, rt)
        y = yconv[pl.ds(s, rt), :]
        mu = jnp.mean(y, axis=-1, keepdims=True)
        yc = y - mu
        var = jnp.mean(yc * yc, axis=-1, keepdims=True)
        z = yc * lax.rsqrt(var + EPS) * lg_ref[...] + lb_ref[...]
        o_ref[pl.ds(s, rt), :] = (z * jax.nn.sigmoid(z)).astype(BF16)
        return 0

    lax.fori_loop(0, t // rt, norm, 0, unroll=4)


def _conv(cacg, conv_w, conv_b, ln_g, ln_b):
    bsz, t, _ = cacg.shape
    row = lambda a: a.reshape(1, CONV_CH)
    vec = pl.BlockSpec((1, CONV_CH), lambda b: (0, 0))
    return pl.pallas_call(
        functools.partial(_conv_kernel, t=t),
        grid=(bsz,),
        in_specs=[pl.BlockSpec((None, t, 2 * CONV_CH), lambda b: (b, 0, 0)),
                  pl.BlockSpec((CONV_K + 1, CONV_CH), lambda b: (0, 0)), vec, vec, vec],
        out_specs=pl.BlockSpec((None, t, CONV_CH), lambda b: (b, 0, 0)),
        out_shape=jax.ShapeDtypeStruct((bsz, t, CONV_CH), BF16),
        scratch_shapes=[pltpu.VMEM((8, t + CONV_TAIL, CONV_CH), F32), pltpu.VMEM((t, CONV_CH), F32)],
        compiler_params=_params(("parallel",), 40),
        name="conformer_conv",
    )(cacg, jnp.pad(conv_w, ((0, 1), (0, 0))), row(conv_b), row(ln_g), row(ln_b))


@functools.lru_cache(maxsize=None)
def _dft_tables(t):
    j = np.arange(t, dtype=np.int64)
    ang = 2.0 * np.pi * ((j[:, None] * j[None, :]) % t).astype(np.float64) / t
    wt = np.concatenate([np.cos(ang), -np.sin(ang)], axis=1).astype(np.float32)
    c = np.arange(FOURIER_CH, dtype=np.int64)
    grp, idx = c // FOURIER_GROUP_CH, c % FOURIER_GROUP_CH
    same = grp[:, None] == grp[None, :]
    angc = 2.0 * np.pi * ((idx[:, None] * idx[None, :]) % FOURIER_GROUP_CH).astype(np.float64) / FOURIER_GROUP_CH
    cc = np.where(same, np.cos(angc), 0.0).astype(np.float32)
    sc = np.where(same, np.sin(angc), 0.0).astype(np.float32)
    return wt, cc, sc


def _fourier_kernel(z_ref, cc_ref, sc_ref, wt_ref, o_ref, zcs, *, t, scale):
    bsz, _, ch = z_ref.shape

    @pl.when(pl.program_id(0) == 0)
    def _():
        for b in range(bsz):
            zcs[0:t, b * ch:(b + 1) * ch] = _dot(z_ref[b], cc_ref[...]).astype(BF16)
            zcs[t:2 * t, b * ch:(b + 1) * ch] = _dot(z_ref[b], sc_ref[...]).astype(BF16)

    r = _dot(wt_ref[...], zcs[...]) * scale
    for b in range(bsz):
        o_ref[b] = r[:, b * ch:(b + 1) * ch].astype(BF16)


def _fourier(z):
    bsz, t, ch = z.shape
    wt, cc, sc = _dft_tables(t)
    tm = min(t, 512)
    mat = pl.BlockSpec((ch, ch), lambda i: (0, 0))
    return pl.pallas_call(
        functools.partial(_fourier_kernel, t=t, scale=float((t * FOURIER_GROUP_CH) ** -0.5)),
        grid=(t // tm,),
        in_specs=[pl.BlockSpec((bsz, t, ch), lambda i: (0, 0, 0)), mat, mat,
                  pl.BlockSpec((tm, 2 * t), lambda i: (i, 0))],
        out_specs=pl.BlockSpec((bsz, tm, ch), lambda i: (0, i, 0)),
        out_shape=jax.ShapeDtypeStruct((bsz, t, ch), BF16),
        scratch_shapes=[pltpu.VMEM((2 * t, bsz * ch), BF16)],
        compiler_params=_params(("arbitrary",), 56),
        name="fourier_mix",
    )(z, jnp.asarray(cc).astype(BF16), jnp.asarray(sc).astype(BF16), jnp.asarray(wt).astype(BF16))


def _outproj_kernel(x_ref, hr_ref, hc_ref, o_ref, cv_ref, fo_ref, w_ref, gt_ref, g2_ref, sh_ref, sc_ref, wr_ref,
                    x1_ref, xn_ref, aff_ref):
    half = MLSTM_WIDTH // 2
    a, b = MLSTM_WIDTH, MLSTM_WIDTH + CONV_CH
    og = jax.nn.sigmoid(o_ref[...])
    y = (_dot((hr_ref[...] * og[:, 0:half]).astype(BF16), w_ref[0:half, :])
         + _dot((hc_ref[...] * og[:, half:a]).astype(BF16), w_ref[half:a, :])
         + _dot(cv_ref[...], w_ref[a:b, :]) + _dot(fo_ref[...], w_ref[b:, :]))
    x1 = x_ref[...] + gt_ref[...] * y
    x1_ref[...] = x1
    xn = _modnorm(x1, g2_ref[...], sh_ref[...], sc_ref[...])
    xn_ref[...] = xn.astype(BF16)
    tm = xn.shape[0]
    r = _dot(jnp.concatenate(_split_bf16(xn), axis=0), jnp.concatenate(_split_bf16(wr_ref[...]), axis=1))
    logits = (r[:tm, :LANES] + r[:tm, LANES:]) + (r[tm:, :LANES] + r[tm:, LANES:])
    lane = lax.broadcasted_iota(I32, logits.shape, 1)
    logits = jnp.where(lane < N_EXPERTS, logits, -jnp.inf)
    e = jnp.exp(logits - jnp.max(logits, axis=-1, keepdims=True))
    aff_ref[...] = e / jnp.sum(e, axis=-1, keepdims=True)


def _outproj(x, hr, hc, o, cv, fo, w_out, gt, g2, sh, sc, wr):
    bsz, t, d = x.shape
    tm = min(t, 512)
    tok = lambda n: pl.BlockSpec((None, tm, n), lambda b, i: (b, i, 0))
    per_b = pl.BlockSpec((None, 1, d), lambda b, i: (b, 0, 0))
    return pl.pallas_call(
        _outproj_kernel,
        grid=(bsz, t // tm),
        in_specs=[tok(d), tok(MLSTM_WIDTH // 2), tok(MLSTM_WIDTH // 2), tok(MLSTM_WIDTH), tok(CONV_CH), tok(FOURIER_CH),
                  pl.BlockSpec((d, d), lambda b, i: (0, 0)), per_b,
                  pl.BlockSpec((1, d), lambda b, i: (0, 0)), per_b, per_b,
                  pl.BlockSpec((d, LANES), lambda b, i: (0, 0))],
        out_specs=[tok(d), tok(d), tok(LANES)],
        out_shape=[jax.ShapeDtypeStruct((bsz, t, d), F32), jax.ShapeDtypeStruct((bsz, t, d), BF16),
                   jax.ShapeDtypeStruct((bsz, t, LANES), F32)],
        compiler_params=_params(("parallel", "parallel"), 40),
        name="outproj_router",
    )(x, hr, hc, o, cv, fo, w_out, gt, g2.reshape(1, d), sh, sc, wr)


def _select_kernel(a_ref, sp_ref, *, cap):
    a = a_ref[...]
    n = a.shape[0]
    blk = min(n, MXU_DIM)

    def count(mask):
        return jnp.sum(jnp.where(mask, 1.0, 0.0), axis=0, keepdims=True)

    def search(i, thr):
        cand = thr | jnp.left_shift(jnp.int32(1), 30 - i)
        return jnp.where(count(a >= lax.bitcast_convert_type(cand, F32)) >= cap, cand, thr)

    thr = lax.bitcast_convert_type(lax.fori_loop(0, 31, search, jnp.zeros((1, LANES), I32)), F32)
    gt = a > thr
    eq = a == thr
    before = jnp.where(lax.broadcasted_iota(I32, (blk, blk), 1) < lax.broadcasted_iota(I32, (blk, blk), 0), 1.0, 0.0).astype(BF16)

    def excl_cumsum(mask):
        x = jnp.where(mask, 1.0, 0.0)
        run = jnp.zeros((1, LANES), F32)
        parts = []
        for j in range(n // blk):
            xb = x[j * blk:(j + 1) * blk]
            parts.append(_dot(before, xb.astype(BF16)) + run)
            run = run + jnp.sum(xb, axis=0, keepdims=True)
        return jnp.concatenate(parts, axis=0)

    sel = gt | (eq & (excl_cumsum(eq) < cap - count(gt)))
    sp_ref[...] = jnp.where(sel, excl_cumsum(sel).astype(I32), -1)


def _select(aff_sets, cap):
    ng, n, _ = aff_sets.shape
    blk = pl.BlockSpec((None, n, LANES), lambda g: (g, 0, 0))
    return pl.pallas_call(
        functools.partial(_select_kernel, cap=cap),
        grid=(ng,),
        in_specs=[blk],
        out_specs=blk,
        out_shape=jax.ShapeDtypeStruct((ng, n, LANES), I32),
        compiler_params=_params(("parallel",), 32),
        name="ec_select",
    )(aff_sets)


def _ffn_kernel(sp_ref, ar_ref, xn_ref, wg_ref, wu_ref, wd_ref, ys_ref, wgu_b, wd_b, *, cap, n_groups):
    r, j = pl.program_id(0), pl.program_id(1)

    def cast_slice():
        rt = min(256, wg_ref.shape[0] // n_groups)
        dst = lax.rem(r, 2)
        for piece in range(wg_ref.shape[0] // n_groups // rt):
            r0 = pl.multiple_of(j * (wg_ref.shape[0] // n_groups) + piece * rt, rt)
            wgu_b[dst, 0, pl.ds(r0, rt), :] = wg_ref[pl.ds(r0, rt), :].astype(BF16)
            wgu_b[dst, 1, pl.ds(r0, rt), :] = wu_ref[pl.ds(r0, rt), :].astype(BF16)
            wd_b[dst, pl.ds(r0, rt), :] = wd_ref[pl.ds(r0, rt), :].astype(BF16)

    @pl.when(r == 0)
    def _():
        cast_slice()

    @pl.when(r > 0)
    def _():
        cast_slice()
        src = lax.rem(r + 1, 2)
        nb, _, n = sp_ref.shape
        slot = lax.broadcasted_iota(I32, (cap, n), 0)
        xs, gates = [], []
        for bi in range(nb):
            hit = slot == sp_ref[bi]
            gates.append(jnp.sum(jnp.where(hit, ar_ref[bi], 0.0), axis=1, keepdims=True))
            xs.append(_dot(jnp.where(hit, 1.0, 0.0).astype(BF16), xn_ref[bi]).astype(BF16))
        xs = xs[0] if nb == 1 else jnp.concatenate(xs, axis=0)
        gate = gates[0] if nb == 1 else jnp.concatenate(gates, axis=0)
        h1 = _dot(xs, wgu_b[src, 0])
        hid = (h1 * jax.nn.sigmoid(h1) * _dot(xs, wgu_b[src, 1])).astype(BF16)
        ys = (_dot(hid, wd_b[src]) * gate).astype(BF16)
        for bi in range(nb):
            ys_ref[bi] = ys[bi * cap:(bi + 1) * cap]


def _ffn(sp_row, aff_row, xn, wg, wu, wd, layer, cap):
    bsz, n, d = xn.shape
    _, ne, _, hid = wg.shape
    assert d == hid
    nb = min(bsz, max(1, MXU_DIM // cap))
    ng = bsz // nb
    set_idx = lambda r, j: (jnp.where(r == 0, 0, j), jnp.maximum(r - 1, 0), 0, 0)
    row = pl.BlockSpec((nb, None, 1, n), set_idx)
    wspec = lambda a, b: pl.BlockSpec((None, None, a, b), lambda r, j: (layer, jnp.minimum(r, ne - 1), 0, 0))
    return pl.pallas_call(
        functools.partial(_ffn_kernel, cap=cap, n_groups=ng),
        grid=(ne + 1, ng),
        in_specs=[row, row, pl.BlockSpec((nb, n, d), lambda r, j: (jnp.where(r == 0, 0, j), 0, 0)),
                  wspec(d, hid), wspec(d, hid), wspec(hid, d)],
        out_specs=pl.BlockSpec((nb, None, cap, d), set_idx),
        out_shape=jax.ShapeDtypeStruct((bsz, ne, cap, d), BF16),
        scratch_shapes=[pltpu.VMEM((2, 2, d, hid), BF16), pltpu.VMEM((2, hid, d), BF16)],
        compiler_params=_params(("arbitrary", "arbitrary"), 58),
        name="ec_ffn",
    )(sp_row, aff_row, xn, wg, wu, wd)


def _combine_kernel(x_ref, spc_ref, ys_ref, gt_ref, gf_ref, o_ref, *, cap, final):
    spc = spc_ref[...]
    tm = spc.shape[0]
    slot = lax.broadcasted_iota(I32, (tm, cap), 1)
    acc = jnp.zeros(x_ref.shape, F32)
    for e in range(N_EXPERTS):
        hit = jnp.where(spc[:, e:e + 1] == slot, 1.0, 0.0).astype(BF16)
        acc = acc + _dot(hit, ys_ref[e])
    out = x_ref[...] + gt_ref[...] * acc
    if final:
        out = out * lax.rsqrt(jnp.mean(out * out, axis=-1, keepdims=True) + EPS) * gf_ref[...]
    o_ref[...] = out


def _combine(x, sp_col, ys, gt, g_final, cap, final):
    bsz, n, d = x.shape
    tm = min(n, 1024)
    return pl.pallas_call(
        functools.partial(_combine_kernel, cap=cap, final=final),
        grid=(bsz, n // tm),
        in_specs=[pl.BlockSpec((None, tm, d), lambda b, i: (b, i, 0)),
                  pl.BlockSpec((None, tm, N_EXPERTS), lambda b, i: (b, i, 0)),
                  pl.BlockSpec((None, N_EXPERTS, cap, d), lambda b, i: (b, 0, 0, 0)),
                  pl.BlockSpec((None, 1, d), lambda b, i: (b, 0, 0)),
                  pl.BlockSpec((1, d), lambda b, i: (0, 0))],
        out_specs=pl.BlockSpec((None, tm, d), lambda b, i: (b, i, 0)),
        out_shape=jax.ShapeDtypeStruct((bsz, n, d), F32),
        compiler_params=_params(("parallel", "parallel"), 52),
        name="ec_combine_final" if final else "ec_combine",
    )(x, sp_col, ys, gt, g_final.reshape(1, d))


def _moe(x1, xn2, aff, wg, wu, wd, layer, gt2, g_final, final):
    bsz, n, _ = x1.shape
    cap = EC_CAPACITY_FACTOR * n // N_EXPERTS
    per_group = LANES // N_EXPERTS
    ng = -(-bsz // per_group)
    aff_e = aff[:, :, :N_EXPERTS]
    sets = jnp.pad(aff_e, ((0, ng * per_group - bsz), (0, 0), (0, 0)))
    sets = sets.reshape(ng, per_group, n, N_EXPERTS).transpose(0, 2, 1, 3).reshape(ng, n, LANES)
    selpos = _select(sets, cap)
    sp_col = selpos.reshape(ng, n, per_group, N_EXPERTS).transpose(0, 2, 1, 3).reshape(ng * per_group, n, N_EXPERTS)[:bsz]
    sp_row = sp_col.transpose(0, 2, 1).reshape(bsz, N_EXPERTS, 1, n)
    aff_row = aff_e.transpose(0, 2, 1).reshape(bsz, N_EXPERTS, 1, n)
    ys = _ffn(sp_row, aff_row, xn2, wg, wu, wd, layer, cap)
    return _combine(x1, sp_col, ys, gt2, g_final, cap, final)


def kernel(x, c, ctx, c_ctx, w_ada, b_ada, g_norm1, w_in, b_gates, g_hnorm, conv_w, conv_b, conv_ln_g, conv_ln_b,
           w_out, g_norm2, w_router, w_e_gate, w_e_up, w_e_down, g_final):
    bsz, t, d = x.shape
    depth = w_ada.shape[0]
    rows = t // GRID_W
    n_cond = 16
    cs = jnp.concatenate([c, c_ctx[None, :], jnp.zeros((n_cond - bsz - 1, d), F32)], axis=0)
    mods = _ada(cs, w_ada, b_ada)

    npair = N_HEADS // 2
    zero_states = (jnp.zeros((bsz, npair, 2, HEAD_DIM, HEAD_DIM), F32), jnp.zeros((bsz, npair, 8, HEAD_DIM), F32)) * 2
    h_ctx = ctx
    for l in range(depth):
        last = l == depth - 1
        lat = [mods[l, :bsz, i * d:(i + 1) * d].reshape(bsz, 1, d) for i in range(N_ADA)]
        cmod = [jnp.broadcast_to(mods[l, bsz, i * d:(i + 1) * d].reshape(1, 1, d), (bsz, 1, d)) for i in range(N_ADA)]
        w_l = w_in[l]
        qkvg_cols = 3 * MLSTM_WIDTH + N_GATES * N_HEADS
        w_p = jnp.concatenate([w_l[:, :3 * MLSTM_WIDTH],
                               jnp.pad(w_l[:, 3 * MLSTM_WIDTH:qkvg_cols], ((0, 0), (0, LANES - N_GATES * N_HEADS))),
                               w_l[:, qkvg_cols:]], axis=1).astype(BF16)
        bg = jnp.pad(b_gates[l].reshape(1, N_HEADS * N_GATES), ((0, 0), (0, LANES - N_HEADS * N_GATES)))
        w_o = w_out[l].astype(BF16)
        w_r = jnp.pad(w_router[l], ((0, 0), (0, LANES - N_EXPERTS)))
        gh = g_hnorm[l].reshape(N_HEADS, HEAD_DIM, 1)

        def mlstm(xs, md, states, grid_rows, full):
            qkv, gates, extra = _inproj(xs, md[0], md[1], g_norm1[l], w_p, bg, full, grid_rows)
            return _mlstm(qkv, gates, states, gh, full, grid_rows), extra

        def sublayers(xs, md, states, grid_rows, final):
            (hr, hc, *st), (o, cacg, fr) = mlstm(xs, md, states, grid_rows, True)
            cv = _conv(cacg, conv_w[l], conv_b[l], conv_ln_g[l], conv_ln_b[l])
            fo = _fourier(fr)
            x1, xn2, aff = _outproj(xs, hr, hc, o, cv, fo, w_o, md[2], g_norm2[l], md[3], md[4], w_r)
            return _moe(x1, xn2, aff, w_e_gate, w_e_up, w_e_down, l, md[5], g_final, final), st

        if last:
            st, _ = mlstm(h_ctx, cmod, zero_states, None, False)
        else:
            h_ctx, st = sublayers(h_ctx, cmod, zero_states, None, False)
        x, _ = sublayers(x, lat, tuple(st), rows, last)
    return x
```

```python
import functools

import jax
import jax.numpy as jnp
import numpy as np
from jax import lax
from jax.experimental import pallas as pl
from jax.experimental.pallas import tpu as pltpu

F32 = jnp.float32
BF16 = jnp.bfloat16
I32 = jnp.int32

D_MODEL = 1024
GRID_W = 64
N_HEADS = 4
HEAD_DIM = 128
MLSTM_WIDTH = N_HEADS * HEAD_DIM
N_ROW_HEADS = N_HEADS // 2
CONV_CH = 256
CONV_K = 31
FOURIER_CH = 256
FOURIER_GROUP_CH = 64
N_GATES = 4
N_EXPERTS = 16
EC_CAPACITY_FACTOR = 2
N_ADA = 6
EPS = 1e-6

LANES = 128
MXU_DIM = 256
MLSTM_CHUNK = 128
MAX_CHUNKS = 16
CONV_HALO = 16
CONV_TAIL = 128
MIB = 1024 * 1024

SEC_QKV = (0, 3 * MLSTM_WIDTH)
SEC_G = (SEC_QKV[1], SEC_QKV[1] + LANES)
SEC_O = (SEC_G[1], SEC_G[1] + MLSTM_WIDTH)
SEC_CACG = (SEC_O[1], SEC_O[1] + 2 * CONV_CH)
SEC_FR = (SEC_CACG[1], SEC_CACG[1] + FOURIER_CH)
IN_COLS_PAD = SEC_FR[1]

ST_NF = 2 * HEAD_DIM
ST_NB = ST_NF + 16
ST_ROWS = ST_NB + 16
UP_ROWS = 2 * HEAD_DIM + 16


def _dot(a, b):
    return jnp.dot(a, b, preferred_element_type=F32)


def _dot_nt(a, b):
    return lax.dot_general(a, b, (((1,), (1,)), ((), ())), preferred_element_type=F32)


def _split_bf16(a):
    hi = a.astype(BF16)
    lo = (a - hi.astype(F32)).astype(BF16)
    return hi, lo


def _dot3(a, w):
    ah, al = _split_bf16(a)
    wh, wl = _split_bf16(w)
    return _dot(ah, wh) + (_dot(al, wh) + _dot(ah, wl))


def _modnorm(x, g, sh, sc):
    ms = jnp.mean(x * x, axis=-1, keepdims=True)
    return x * lax.rsqrt(ms + EPS) * g * (1.0 + sc) + sh


def _rows(n_rows, width, *vals):
    rid = lax.broadcasted_iota(I32, (n_rows, width), 0)
    out = jnp.zeros((n_rows, width), F32)
    for i, v in enumerate(vals):
        out = jnp.where(rid == i, v, out)
    return out


def _mod_spec(mod):
    mods, layer, row = mod
    _, _, n_ada, d = mods.shape
    return pl.BlockSpec((None, None, n_ada, d), lambda b, i: (layer, b if row is None else row, 0, 0))


def _params(sem, vmem_mib):
    return pltpu.CompilerParams(dimension_semantics=sem, vmem_limit_bytes=vmem_mib * MIB)


def _ada_kernel(c_ref, w_ref, b_ref, o_ref):
    c = c_ref[...]
    o_ref[...] = _dot3(c * jax.nn.sigmoid(c), w_ref[...]) + b_ref[...]


def _ada(cs, w_ada, b_ada):
    n_layers, d, n6 = w_ada.shape
    r = cs.shape[0]
    tn = 1536
    return pl.pallas_call(
        _ada_kernel,
        grid=(n_layers, n6 // tn),
        in_specs=[
            pl.BlockSpec((r, d), lambda l, j: (0, 0)),
            pl.BlockSpec((None, d, tn), lambda l, j: (l, 0, j)),
            pl.BlockSpec((None, 1, tn), lambda l, j: (l, 0, j)),
        ],
        out_specs=pl.BlockSpec((None, r, tn), lambda l, j: (l, 0, j)),
        out_shape=jax.ShapeDtypeStruct((n_layers, r, n6), F32),
        compiler_params=_params(("arbitrary", "arbitrary"), 40),
        name="ada_mod",
    )(cs, w_ada, b_ada.reshape(n_layers, 1, n6))


def _inproj_kernel(x_ref, mod_ref, g_ref, w_ref, bg_ref, *out_refs, full, colmajor):
    qkv_refs, gates_ref = out_refs[:6], out_refs[6]
    xn = _modnorm(x_ref[...], g_ref[...], mod_ref[0:1, :], mod_ref[1:2, :]).astype(BF16)
    tm = xn.shape[0]
    w = MLSTM_WIDTH
    for i in range(3):
        r = _dot(xn, w_ref[:, i * w:(i + 1) * w])
        if i == 1:
            r = r * HEAD_DIM ** -0.5
        row_ref, col_ref = qkv_refs[2 * i], qkv_refs[2 * i + 1]
        for hh in range(N_ROW_HEADS):
            row_ref[hh] = r[:, hh * HEAD_DIM:(hh + 1) * HEAD_DIM].astype(BF16)
            piece = r[:, (N_ROW_HEADS + hh) * HEAD_DIM:(N_ROW_HEADS + hh + 1) * HEAD_DIM]
            if colmajor:
                piece = jnp.swapaxes(piece.reshape(tm // GRID_W, GRID_W, HEAD_DIM), 0, 1)
            col_ref[hh] = piece.astype(BF16)
    gates_ref[...] = _dot(xn, w_ref[:, SEC_G[0]:SEC_G[1]]) + bg_ref[...]
    if full:
        o_ref, cacg_ref, fr_ref = out_refs[7:]
        o_ref[...] = _dot(xn, w_ref[:, SEC_O[0]:SEC_O[1]])
        cacg_ref[...] = _dot(xn, w_ref[:, SEC_CACG[0]:SEC_CACG[1]])
        fr_ref[...] = _dot(xn, w_ref[:, SEC_FR[0]:SEC_FR[1]]).astype(BF16)


def _inproj(x, mod, g, w, bg, full, rows):
    bsz, t, d = x.shape
    tm = min(t, 1024)
    nh = N_ROW_HEADS
    ncols = IN_COLS_PAD if full else SEC_G[1]
    tok = lambda n: pl.BlockSpec((None, tm, n), lambda b, i: (b, i, 0))
    row_shape = jax.ShapeDtypeStruct((bsz, nh, t, HEAD_DIM), BF16)
    row_spec = pl.BlockSpec((None, nh, tm, HEAD_DIM), lambda b, i: (b, 0, i, 0))
    if rows is None:
        col_shape, col_spec = row_shape, row_spec
    else:
        col_shape = jax.ShapeDtypeStruct((bsz, nh, GRID_W, rows, HEAD_DIM), BF16)
        col_spec = pl.BlockSpec((None, nh, GRID_W, tm // GRID_W, HEAD_DIM), lambda b, i: (b, 0, 0, i, 0))
    out_shape = [row_shape, col_shape] * 3 + [jax.ShapeDtypeStruct((bsz, t, LANES), F32)]
    out_specs = [row_spec, col_spec] * 3 + [tok(LANES)]
    if full:
        out_shape += [jax.ShapeDtypeStruct((bsz, t, MLSTM_WIDTH), F32),
                      jax.ShapeDtypeStruct((bsz, t, 2 * CONV_CH), F32),
                      jax.ShapeDtypeStruct((bsz, t, FOURIER_CH), BF16)]
        out_specs += [tok(MLSTM_WIDTH), tok(2 * CONV_CH), tok(FOURIER_CH)]
    outs = pl.pallas_call(
        functools.partial(_inproj_kernel, full=full, colmajor=rows is not None),
        grid=(bsz, t // tm),
        in_specs=[tok(d), _mod_spec(mod),
                  pl.BlockSpec((1, d), lambda b, i: (0, 0)),
                  pl.BlockSpec((d, ncols), lambda b, i: (0, 0)),
                  pl.BlockSpec((1, LANES), lambda b, i: (0, 0))],
        out_specs=out_specs,
        out_shape=out_shape,
        compiler_params=_params(("parallel", "parallel"), 56),
        name="inproj_full" if full else "inproj_qkvg",
    )(x, mod[0], g.reshape(1, d), w, bg)
    qkv = [a.reshape(bsz, nh, t, HEAD_DIM) for a in outs[:6]]
    return qkv, outs[6], outs[7:]


def _logsig(x):
    return jnp.minimum(x, 0.0) - jnp.log1p(jnp.exp(-jnp.abs(x)))


def _tri_masks(n):
    a0 = lax.broadcasted_iota(I32, (n, n), 0)
    a1 = lax.broadcasted_iota(I32, (n, n), 1)
    return a0 <= a1, a0 >= a1


def _split3(x):
    hi = x.astype(BF16).astype(F32)
    r1 = x - hi
    mid = r1.astype(BF16).astype(F32)
    return hi, mid, r1 - mid


def _mlstm_consts(tri_scr, ones_scr):
    lc = MLSTM_CHUNK
    u = lax.broadcasted_iota(I32, (lc, 2 * lc), 0)
    t2 = lax.broadcasted_iota(I32, (lc, 2 * lc), 1)
    tri_scr[...] = jnp.where(((t2 < lc) & (u <= t2)) | ((t2 >= lc) & (u >= t2 - lc)), 1.0, 0.0).astype(BF16)
    r = lax.broadcasted_iota(I32, (16, 2 * lc), 0)
    t16 = lax.broadcasted_iota(I32, (16, 2 * lc), 1)
    ones_scr[...] = jnp.where(((r < 3) & (t16 < lc)) | ((r >= 3) & (r < 6) & (t16 >= lc)), 1.0, 0.0).astype(BF16)


def _mlstm_gates(head, g_ref, tri_scr, rowg_scr, stat_scr, n_chunks, cm_rows):
    lc = MLSTM_CHUNK
    pick = jnp.where(lax.broadcasted_iota(I32, (16, LANES), 1) == N_GATES * head + lax.broadcasted_iota(I32, (16, LANES), 0),
                     1.0, 0.0).astype(BF16)
    chunk_id = lax.broadcasted_iota(I32, (MAX_CHUNKS, lc), 0)

    def gather(c, acc):
        if cm_rows is None:
            g = g_ref[pl.ds(pl.multiple_of(c * lc, lc), lc), :]
        else:
            cols = lc // cm_rows
            g = jnp.concatenate([g_ref[pl.ds(c * cols + j, cm_rows, stride=GRID_W), :] for j in range(cols)], axis=0)
        rows = sum(_dot_nt(pick, part.astype(BF16)) for part in _split3(g))
        return tuple(jnp.where(chunk_id == c, rows[i:i + 1], a) for i, a in enumerate(acc))

    zeros = jnp.zeros((MAX_CHUNKS, lc), F32)
    i_f, f_f, i_b, f_b = lax.fori_loop(0, n_chunks, gather, (zeros,) * N_GATES, unroll=min(4, n_chunks))
    lf_f, lf_b = _logsig(f_f), _logsig(f_b)
    b_f = sum(_dot(part.astype(BF16), tri_scr[:, 0:lc]) for part in _split3(lf_f))
    b_b = sum(_dot(part.astype(BF16), tri_scr[:, lc:]) for part in _split3(lf_b))
    bl_f = jnp.sum(lf_f, axis=1, keepdims=True)
    bl_b = jnp.sum(lf_b, axis=1, keepdims=True)
    ci_f, ci_b = i_f - b_f, i_b - b_b
    ml_f = jnp.max(ci_f + bl_f, axis=1, keepdims=True)
    ml_b = jnp.max(ci_b + bl_b, axis=1, keepdims=True)
    for r, val in enumerate((jnp.exp(ci_f + bl_f - ml_f), jnp.exp(ci_b + bl_b - ml_b), b_f, b_b, ci_f, ci_b)):
        rowg_scr[:, r, :] = val
    for r, val in enumerate((bl_f, bl_b, ml_f, ml_b)):
        stat_scr[:, r, :] = jnp.broadcast_to(val, (MAX_CHUNKS, HEAD_DIM))


def _mlstm_increment(c, k_ref, v_ref, rowg_scr, up_scr, vt_scr):
    lc = MLSTM_CHUNK
    s = pl.multiple_of(c * lc, lc)
    v_t = v_ref[pl.ds(s, lc), :].astype(F32).T
    rg = rowg_scr[c]
    w_f, w_b = rg[0:1], rg[1:2]
    lhs = jnp.concatenate([(v_t * w_f).astype(BF16), (v_t * w_b).astype(BF16),
                           _rows(16, lc, w_f, w_b).astype(BF16)], axis=0)
    up_scr[c] = _dot(lhs, k_ref[pl.ds(s, lc), :])
    if vt_scr is not None:
        vt_scr[c] = v_t.astype(BF16)


def _mlstm_state_step(cc, carry, up_scr, stat_scr, st_scr, mst_scr, fwd):
    cmat, n, m = carry
    off, row = (0, 0) if fwd else (HEAD_DIM, 1)
    if st_scr is not None:
        st_scr[cc, off:off + HEAD_DIM, :] = cmat.astype(BF16)
        nrow = ST_NF if fwd else ST_NB
        st_scr[cc, nrow:nrow + 16, :] = _rows(16, HEAD_DIM, n).astype(BF16)
        mst_scr[cc, row:row + 1, :] = jnp.broadcast_to(m, (1, HEAD_DIM))
    st = stat_scr[cc]
    bl, ml = st[row:row + 1, 0:1], st[2 + row:3 + row, 0:1]
    m_new = jnp.maximum(bl + m, ml)
    keep = jnp.exp(bl + m - m_new)
    gain = jnp.exp(ml - m_new)
    cmat = keep * cmat + gain * up_scr[cc, off:off + HEAD_DIM, :]
    n = keep * n + gain * up_scr[cc, 2 * HEAD_DIM + row:2 * HEAD_DIM + row + 1, :]
    return cmat, n, m_new


def _mlstm_outputs(c, chains, ones_scr):
    lc = MLSTM_CHUNK
    s = pl.multiple_of(c * lc, lc)
    le, ge = _tri_masks(lc)
    qs = [ch[0][pl.ds(s, lc), :] for ch in chains]
    qk_ts = [_dot_nt(ch[1][pl.ds(s, lc), :], q) for ch, q in zip(chains, qs)]
    g_ts = [_dot_nt(ch[5][c], q) for ch, q in zip(chains, qs)]
    rgs = [ch[4][c] for ch in chains]
    cis = [lax.dot_general(_rows(16, lc, *_split3(rg[4:5]), *_split3(rg[5:6])).astype(BF16), ones_scr[...],
                           (((0,), (0,)), ((), ())), preferred_element_type=F32) for rg in rgs]

    def direction(qk_t, valid, b_row, ci, m_st, qn):
        dlog = jnp.where(valid, b_row + ci, -jnp.inf)
        inter = b_row + m_st
        m_t = jnp.maximum(inter, jnp.max(dlog, axis=0, keepdims=True))
        s_t = qk_t * jnp.exp(dlog - m_t)
        w_inter = jnp.exp(inter - m_t)
        den = w_inter * qn + jnp.sum(s_t, axis=0, keepdims=True)
        r = 1.0 / jnp.maximum(jnp.abs(den), jnp.exp(-m_t))
        return r * s_t, r * w_inter

    mixed = []
    for ch, qk_t, g_t, rg, ci in zip(chains, qk_ts, g_ts, rgs, cis):
        ms = ch[6][c]
        sf, cf = direction(qk_t, le, rg[2:3], ci[:, 0:lc], ms[0:1, 0:1], g_t[ST_NF:ST_NF + 1])
        sb, cb = direction(qk_t, ge, rg[3:4], ci[:, lc:], ms[1:2, 0:1], g_t[ST_NB:ST_NB + 1])
        mixed.append(((sf + sb).astype(BF16), cf * g_t[0:HEAD_DIM] + cb * g_t[HEAD_DIM:2 * HEAD_DIM]))
    h_ts = [inter_part + _dot(ch[7][c], s_mix) for ch, (s_mix, inter_part) in zip(chains, mixed)]
    for ch, h_t in zip(chains, h_ts):
        gh_ref, h_ref, cm_rows = ch[2], ch[3], ch[8]
        h = (h_t * lax.rsqrt(jnp.mean(h_t * h_t, axis=0, keepdims=True) + EPS) * gh_ref[...]).T
        if cm_rows is None:
            h_ref[pl.ds(s, lc), :] = h
        else:
            cols = lc // cm_rows
            for j in range(cols):
                h_ref[pl.ds(c * cols + j, cm_rows, stride=GRID_W), :] = h[j * cm_rows:(j + 1) * cm_rows]


def _mlstm_kernel(*refs, n_chunks, with_outputs, cm_rows):
    n_slot_in = 6
    g_ref = refs[0]
    slots_in = [refs[1:1 + n_slot_in], refs[1 + n_slot_in:1 + 2 * n_slot_in]]
    rest = refs[1 + 2 * n_slot_in:]
    if with_outputs:
        h_refs, rest = rest[:2], rest[2:]
    state_out, scratch = rest[:4], rest[4:]
    tri_scr, ones_scr, scratch = scratch[0], scratch[1], scratch[2:]
    n_scr = 6 if with_outputs else 3
    slots_scr = [scratch[:n_scr], scratch[n_scr:2 * n_scr]]
    slot_rows = [None, cm_rows]
    _mlstm_consts(tri_scr, ones_scr)
    for slot, scr in enumerate(slots_scr):
        _mlstm_gates(pl.program_id(1) + slot * N_ROW_HEADS, g_ref, tri_scr, scr[0], scr[1], n_chunks, slot_rows[slot])

    def increments(c, _):
        for (q_ref, k_ref, v_ref, c0_ref, n0_ref, gh_ref), scr in zip(slots_in, slots_scr):
            _mlstm_increment(c, k_ref, v_ref, scr[0], scr[2], scr[5] if with_outputs else None)
        return 0

    lax.fori_loop(0, n_chunks, increments, 0, unroll=min(4, n_chunks))

    def states(i, carry):
        out = []
        for slot, scr in enumerate(slots_scr):
            st_scr, mst_scr = (scr[3], scr[4]) if with_outputs else (None, None)
            out.append(_mlstm_state_step(i, carry[2 * slot], scr[2], scr[1], st_scr, mst_scr, True))
            out.append(_mlstm_state_step(n_chunks - 1 - i, carry[2 * slot + 1], scr[2], scr[1], st_scr, mst_scr, False))
        return tuple(out)

    init = []
    for (q_ref, k_ref, v_ref, c0_ref, n0_ref, gh_ref) in slots_in:
        init.append((c0_ref[0], n0_ref[0:1, :], n0_ref[2:3, 0:1]))
        init.append((c0_ref[1], n0_ref[1:2, :], n0_ref[3:4, 0:1]))
    final = lax.fori_loop(0, n_chunks, states, tuple(init))
    for slot in range(2):
        c_out, n_out = state_out[2 * slot], state_out[2 * slot + 1]
        (cf, nf, mf), (cb, nb, mb) = final[2 * slot], final[2 * slot + 1]
        c_out[0] = cf
        c_out[1] = cb
        n_out[...] = _rows(8, HEAD_DIM, nf, nb, mf, mb)

    if with_outputs:
        chains = [(q_ref, k_ref, gh_ref, h_refs[slot], scr[0], scr[3], scr[4], scr[5], slot_rows[slot])
                  for slot, ((q_ref, k_ref, v_ref, c0_ref, n0_ref, gh_ref), scr) in enumerate(zip(slots_in, slots_scr))]

        def outputs(c, _):
            _mlstm_outputs(c, chains, ones_scr)
            return 0

        lax.fori_loop(0, n_chunks, outputs, 0, unroll=min(4, n_chunks))


def _mlstm(qkv, gates, states, gh, with_outputs, cm_rows):
    bsz, npair, t, dh = qkv[0].shape
    nc = t // MLSTM_CHUNK
    lc = MLSTM_CHUNK
    seq = pl.BlockSpec((None, None, t, dh), lambda b, p: (b, p, 0, 0))
    cst = pl.BlockSpec((None, None, 2, dh, dh), lambda b, p: (b, p, 0, 0, 0))
    nst = pl.BlockSpec((None, None, 8, dh), lambda b, p: (b, p, 0, 0))

    def slot_specs(head_off):
        return [seq, seq, seq, cst, nst, pl.BlockSpec((None, dh, 1), lambda b, p: (p + head_off, 0, 0))]

    st_shapes = [jax.ShapeDtypeStruct((bsz, npair, 2, dh, dh), F32), jax.ShapeDtypeStruct((bsz, npair, 8, dh), F32)] * 2
    out_shape, out_specs = st_shapes, [cst, nst, cst, nst]
    assert nc <= MAX_CHUNKS
    slot_scratch = [pltpu.VMEM((MAX_CHUNKS, 8, lc), F32), pltpu.VMEM((MAX_CHUNKS, 8, dh), F32),
                    pltpu.VMEM((nc, UP_ROWS, dh), F32)]
    if with_outputs:
        h_shape = jax.ShapeDtypeStruct((bsz, t, npair * dh), F32)
        h_spec = pl.BlockSpec((None, t, dh), lambda b, p: (b, 0, p))
        out_shape = [h_shape, h_shape] + out_shape
        out_specs = [h_spec, h_spec] + out_specs
        slot_scratch += [pltpu.VMEM((nc, ST_ROWS, dh), BF16), pltpu.VMEM((nc, 8, dh), F32), pltpu.VMEM((nc, dh, lc), BF16)]
    q_r, q_c, k_r, k_c, v_r, v_c = qkv
    c_r, n_r, c_c, n_c = states
    return pl.pallas_call(
        functools.partial(_mlstm_kernel, n_chunks=nc, with_outputs=with_outputs, cm_rows=cm_rows),
        grid=(bsz, npair),
        in_specs=[pl.BlockSpec((None, t, LANES), lambda b, p: (b, 0, 0))] + slot_specs(0) + slot_specs(N_ROW_HEADS),
        out_specs=out_specs,
        out_shape=out_shape,
        scratch_shapes=[pltpu.VMEM((lc, 2 * lc), BF16), pltpu.VMEM((16, 2 * lc), BF16)] + slot_scratch * 2,
        compiler_params=_params(("parallel", "parallel"), 48),
        name="mlstm_full" if with_outputs else "mlstm_states",
    )(gates, q_r, k_r, v_r, c_r, n_r, gh, q_c, k_c, v_c, c_c, n_c, gh)


def _conv_kernel(cacg_ref, w_ref, cb_ref, lg_ref, lb_ref, o_ref, ysh, yconv, *, t):
    rt = 64
    ysh[0, 0:CONV_HALO, :] = jnp.zeros((CONV_HALO, CONV_CH), F32)
    ysh[0, CONV_HALO + t:t + CONV_TAIL, :] = jnp.zeros((CONV_TAIL - CONV_HALO, CONV_CH), F32)

    def fill(r, _):
        s = pl.multiple_of(r * rt, rt)
        a = cacg_ref[pl.ds(s, rt), 0:CONV_CH]
        g = cacg_ref[pl.ds(s, rt), CONV_CH:2 * CONV_CH]
        ysh[0, pl.ds(CONV_HALO + s, rt), :] = a * jax.nn.sigmoid(g)
        return 0

    lax.fori_loop(0, t // rt, fill, 0)

    def shift(r, _):
        s = pl.multiple_of(r * rt, rt)
        win = ysh[0, pl.ds(s, rt + 8), :]
        for res in range(1, 8):
            ysh[res, pl.ds(s, rt), :] = pltpu.roll(win, rt + 8 - res, axis=0)[0:rt, :]
        return 0

    lax.fori_loop(0, t // rt + 1, shift, 0)

    def tile(r, _):
        s = pl.multiple_of(r * rt, rt)
        acc = jnp.zeros((rt, CONV_CH), F32)
        for kk in range(CONV_K):
            off = kk + CONV_HALO - CONV_K // 2
            rows = ysh[off % 8, pl.ds(pl.multiple_of(s + 8 * (off // 8), 8), rt), :]
            acc = acc + w_ref[kk:kk + 1, :] * rows
        yconv[pl.ds(s, rt), :] = acc + cb_ref[...]
        return 0

    lax.fori_loop(0, t // rt, tile, 0)

    def norm(r, _):
        s = pl.multiple_of(r * rt, rt)
        y = yconv[pl.ds(s, rt), :]
        mu = jnp.mean(y, axis=-1, keepdims=True)
        yc = y - mu
        var = jnp.mean(yc * yc, axis=-1, keepdims=True)
        z = yc * lax.rsqrt(var + EPS) * lg_ref[...] + lb_ref[...]
        o_ref[pl.ds(s, rt), :] = (z * jax.nn.sigmoid(z)).astype(BF16)
        return 0

    lax.fori_loop(0, t // rt, norm, 0, unroll=4)


def _conv(cacg, conv_w, conv_b, ln_g, ln_b):
    bsz, t, _ = cacg.shape
    row = lambda a: a.reshape(1, CONV_CH)
    vec = pl.BlockSpec((1, CONV_CH), lambda b: (0, 0))
    return pl.pallas_call(
        functools.partial(_conv_kernel, t=t),
        grid=(bsz,),
        in_specs=[pl.BlockSpec((None, t, 2 * CONV_CH), lambda b: (b, 0, 0)),
                  pl.BlockSpec((CONV_K + 1, CONV_CH), lambda b: (0, 0)), vec, vec, vec],
        out_specs=pl.BlockSpec((None, t, CONV_CH), lambda b: (b, 0, 0)),
        out_shape=jax.ShapeDtypeStruct((bsz, t, CONV_CH), BF16),
        scratch_shapes=[pltpu.VMEM((8, t + CONV_TAIL, CONV_CH), F32), pltpu.VMEM((t, CONV_CH), F32)],
        compiler_params=_params(("parallel",), 40),
        name="conformer_conv",
    )(cacg, jnp.pad(conv_w, ((0, 1), (0, 0))), row(conv_b), row(ln_g), row(ln_b))


@functools.lru_cache(maxsize=None)
def _dft_tables(t):
    j = np.arange(t, dtype=np.int64)
    ang = 2.0 * np.pi * ((j[:, None] * j[None, :]) % t).astype(np.float64) / t
    wt = np.concatenate([np.cos(ang), -np.sin(ang)], axis=1).astype(np.float32)
    c = np.arange(FOURIER_CH, dtype=np.int64)
    grp, idx = c // FOURIER_GROUP_CH, c % FOURIER_GROUP_CH
    same = grp[:, None] == grp[None, :]
    angc = 2.0 * np.pi * ((idx[:, None] * idx[None, :]) % FOURIER_GROUP_CH).astype(np.float64) / FOURIER_GROUP_CH
    cc = np.where(same, np.cos(angc), 0.0).astype(np.float32)
    sc = np.where(same, np.sin(angc), 0.0).astype(np.float32)
    return wt, cc, sc


def _fourier_kernel(z_ref, cc_ref, sc_ref, wt_ref, o_ref, zcs, *, t, scale):
    bsz, _, ch = z_ref.shape

    @pl.when(pl.program_id(0) == 0)
    def _():
        for b in range(bsz):
            zcs[0:t, b * ch:(b + 1) * ch] = _dot(z_ref[b], cc_ref[...]).astype(BF16)
            zcs[t:2 * t, b * ch:(b + 1) * ch] = _dot(z_ref[b], sc_ref[...]).astype(BF16)

    r = _dot(wt_ref[...], zcs[...]) * scale
    for b in range(bsz):
        o_ref[b] = r[:, b * ch:(b + 1) * ch].astype(BF16)


def _fourier(z):
    bsz, t, ch = z.shape
    wt, cc, sc = _dft_tables(t)
    tm = min(t, 512)
    mat = pl.BlockSpec((ch, ch), lambda i: (0, 0))
    return pl.pallas_call(
        functools.partial(_fourier_kernel, t=t, scale=float((t * FOURIER_GROUP_CH) ** -0.5)),
        grid=(t // tm,),
        in_specs=[pl.BlockSpec((bsz, t, ch), lambda i: (0, 0, 0)), mat, mat,
                  pl.BlockSpec((tm, 2 * t), lambda i: (i, 0))],
        out_specs=pl.BlockSpec((bsz, tm, ch), lambda i: (0, i, 0)),
        out_shape=jax.ShapeDtypeStruct((bsz, t, ch), BF16),
        scratch_shapes=[pltpu.VMEM((2 * t, bsz * ch), BF16)],
        compiler_params=_params(("arbitrary",), 56),
        name="fourier_mix",
    )(z, jnp.asarray(cc).astype(BF16), jnp.asarray(sc).astype(BF16), jnp.asarray(wt).astype(BF16))


def _outproj_kernel(x_ref, hr_ref, hc_ref, o_ref, cv_ref, fo_ref, w_ref, mod_ref, g2_ref, wr_ref,
                    x1_ref, xn_ref, aff_ref):
    half = MLSTM_WIDTH // 2
    a, b = MLSTM_WIDTH, MLSTM_WIDTH + CONV_CH
    og = jax.nn.sigmoid(o_ref[...])
    y = (_dot((hr_ref[...] * og[:, 0:half]).astype(BF16), w_ref[0:half, :])
         + _dot((hc_ref[...] * og[:, half:a]).astype(BF16), w_ref[half:a, :])
         + _dot(cv_ref[...], w_ref[a:b, :]) + _dot(fo_ref[...], w_ref[b:, :]))
    x1 = x_ref[...] + mod_ref[2:3, :] * y
    x1_ref[...] = x1
    xn = _modnorm(x1, g2_ref[...], mod_ref[3:4, :], mod_ref[4:5, :])
    xn_ref[...] = xn.astype(BF16)
    tm = xn.shape[0]
    r = _dot(jnp.concatenate(_split_bf16(xn), axis=0), jnp.concatenate(_split_bf16(wr_ref[...]), axis=1))
    logits = (r[:tm, :LANES] + r[:tm, LANES:]) + (r[tm:, :LANES] + r[tm:, LANES:])
    lane = lax.broadcasted_iota(I32, logits.shape, 1)
    logits = jnp.where(lane < N_EXPERTS, logits, -jnp.inf)
    e = jnp.exp(logits - jnp.max(logits, axis=-1, keepdims=True))
    aff_ref[...] = e / jnp.sum(e, axis=-1, keepdims=True)


def _outproj(x, hr, hc, o, cv, fo, w_out, mod, g2, wr):
    bsz, t, d = x.shape
    tm = min(t, 512)
    tok = lambda n: pl.BlockSpec((None, tm, n), lambda b, i: (b, i, 0))
    return pl.pallas_call(
        _outproj_kernel,
        grid=(bsz, t // tm),
        in_specs=[tok(d), tok(MLSTM_WIDTH // 2), tok(MLSTM_WIDTH // 2), tok(MLSTM_WIDTH), tok(CONV_CH), tok(FOURIER_CH),
                  pl.BlockSpec((d, d), lambda b, i: (0, 0)), _mod_spec(mod),
                  pl.BlockSpec((1, d), lambda b, i: (0, 0)),
                  pl.BlockSpec((d, LANES), lambda b, i: (0, 0))],
        out_specs=[tok(d), tok(d), tok(LANES)],
        out_shape=[jax.ShapeDtypeStruct((bsz, t, d), F32), jax.ShapeDtypeStruct((bsz, t, d), BF16),
                   jax.ShapeDtypeStruct((bsz, t, LANES), F32)],
        compiler_params=_params(("parallel", "parallel"), 40),
        name="outproj_router",
    )(x, hr, hc, o, cv, fo, w_out, mod[0], g2.reshape(1, d), wr)


def _select_kernel(a_ref, sp_ref, *, cap):
    a = a_ref[...]
    n = a.shape[0]
    blk = min(n, MXU_DIM)

    def count(mask):
        return jnp.sum(jnp.where(mask, 1.0, 0.0), axis=0, keepdims=True)

    def search(i, thr):
        cand = thr | jnp.left_shift(jnp.int32(1), 30 - i)
        return jnp.where(count(a >= lax.bitcast_convert_type(cand, F32)) >= cap, cand, thr)

    thr = lax.bitcast_convert_type(lax.fori_loop(0, 31, search, jnp.zeros((1, LANES), I32)), F32)
    gt = a > thr
    eq = a == thr
    before = jnp.where(lax.broadcasted_iota(I32, (blk, blk), 1) < lax.broadcasted_iota(I32, (blk, blk), 0), 1.0, 0.0).astype(BF16)

    def excl_cumsum(mask):
        x = jnp.where(mask, 1.0, 0.0)
        run = jnp.zeros((1, LANES), F32)
        parts = []
        for j in range(n // blk):
            xb = x[j * blk:(j + 1) * blk]
            parts.append(_dot(before, xb.astype(BF16)) + run)
            run = run + jnp.sum(xb, axis=0, keepdims=True)
        return jnp.concatenate(parts, axis=0)

    sel = gt | (eq & (excl_cumsum(eq) < cap - count(gt)))
    sp_ref[...] = jnp.where(sel, excl_cumsum(sel).astype(I32), -1)


def _select(aff_sets, cap):
    ng, n, _ = aff_sets.shape
    blk = pl.BlockSpec((None, n, LANES), lambda g: (g, 0, 0))
    return pl.pallas_call(
        functools.partial(_select_kernel, cap=cap),
        grid=(ng,),
        in_specs=[blk],
        out_specs=blk,
        out_shape=jax.ShapeDtypeStruct((ng, n, LANES), I32),
        compiler_params=_params(("parallel",), 32),
        name="ec_select",
    )(aff_sets)


def _ffn_kernel(sp_ref, ar_ref, xn_ref, wg_ref, wu_ref, wd_ref, ys_ref, wgu_b, wd_b, *, cap):
    g, j, k = pl.program_id(0), pl.program_id(1), pl.program_id(2)
    rows = wg_ref.shape[0]

    def cast_slice():
        rt = min(256, rows)
        dst = lax.rem(g, 2)
        for piece in range(rows // rt):
            src_rows = pl.ds(piece * rt, rt)
            dst_rows = pl.ds(pl.multiple_of(j * rows + piece * rt, rt), rt)
            wgu_b[dst, k, 0, dst_rows, :] = wg_ref[src_rows, :].astype(BF16)
            wgu_b[dst, k, 1, dst_rows, :] = wu_ref[src_rows, :].astype(BF16)
            wd_b[dst, k, dst_rows, :] = wd_ref[src_rows, :].astype(BF16)

    @pl.when(g == 0)
    def _():
        cast_slice()

    @pl.when(g > 0)
    def _():
        cast_slice()
        src = lax.rem(g + 1, 2)
        nb, _, n = sp_ref.shape
        slot = lax.broadcasted_iota(I32, (cap, n), 0)
        xs, gates = [], []
        for bi in range(nb):
            hit = slot == sp_ref[bi]
            gates.append(jnp.sum(jnp.where(hit, ar_ref[bi], 0.0), axis=1, keepdims=True))
            xs.append(_dot(jnp.where(hit, 1.0, 0.0).astype(BF16), xn_ref[bi]).astype(BF16))
        xs = xs[0] if nb == 1 else jnp.concatenate(xs, axis=0)
        gate = gates[0] if nb == 1 else jnp.concatenate(gates, axis=0)
        h1 = _dot(xs, wgu_b[src, k, 0])
        hid = (h1 * jax.nn.sigmoid(h1) * _dot(xs, wgu_b[src, k, 1])).astype(BF16)
        ys = (_dot(hid, wd_b[src, k]) * gate).astype(BF16)
        for bi in range(nb):
            ys_ref[bi] = ys[bi * cap:(bi + 1) * cap]


def _ffn(sp_row, aff_row, xn, wg, wu, wd, layer, cap):
    bsz, n, d = xn.shape
    _, ne, _, hid = wg.shape
    assert d == hid
    nb = min(bsz, max(1, MXU_DIM // cap))
    nj = bsz // nb
    gsz = 2 if nj > 1 else 1
    rows = d // nj
    expert = lambda g, k: jnp.maximum((g - 1) * gsz + k, 0)
    batch = lambda g, j: jnp.where(g == 0, 0, j)
    row = pl.BlockSpec((nb, None, 1, n), lambda g, j, k: (batch(g, j), expert(g, k), 0, 0))
    wspec = pl.BlockSpec((None, None, rows, hid), lambda g, j, k: (layer, jnp.minimum(g * gsz + k, ne - 1), j, 0))
    return pl.pallas_call(
        functools.partial(_ffn_kernel, cap=cap),
        grid=(ne // gsz + 1, nj, gsz),
        in_specs=[row, row, pl.BlockSpec((nb, n, d), lambda g, j, k: (batch(g, j), 0, 0)), wspec, wspec, wspec],
        out_specs=pl.BlockSpec((nb, None, cap, d), lambda g, j, k: (batch(g, j), expert(g, k), 0, 0)),
        out_shape=jax.ShapeDtypeStruct((bsz, ne, cap, d), BF16),
        scratch_shapes=[pltpu.VMEM((2, gsz, 2, d, hid), BF16), pltpu.VMEM((2, gsz, hid, d), BF16)],
        compiler_params=_params(("arbitrary", "arbitrary", "arbitrary"), 56),
        name="ec_ffn",
    )(sp_row, aff_row, xn, wg, wu, wd)


def _combine_kernel(x_ref, spc_ref, ys_ref, mod_ref, gf_ref, o_ref, *, cap, final):
    spc = spc_ref[...]
    tm = spc.shape[0]
    slot = lax.broadcasted_iota(I32, (tm, cap), 1)
    acc = jnp.zeros(x_ref.shape, F32)
    for e in range(N_EXPERTS):
        hit = jnp.where(spc[:, e:e + 1] == slot, 1.0, 0.0).astype(BF16)
        acc = acc + _dot(hit, ys_ref[e])
    out = x_ref[...] + mod_ref[5:6, :] * acc
    if final:
        out = out * lax.rsqrt(jnp.mean(out * out, axis=-1, keepdims=True) + EPS) * gf_ref[...]
    o_ref[...] = out


def _combine(x, sp_col, ys, mod, g_final, cap, final):
    bsz, n, d = x.shape
    tm = min(n, 1024)
    return pl.pallas_call(
        functools.partial(_combine_kernel, cap=cap, final=final),
        grid=(bsz, n // tm),
        in_specs=[pl.BlockSpec((None, tm, d), lambda b, i: (b, i, 0)),
                  pl.BlockSpec((None, tm, N_EXPERTS), lambda b, i: (b, i, 0)),
                  pl.BlockSpec((None, N_EXPERTS, cap, d), lambda b, i: (b, 0, 0, 0)),
                  _mod_spec(mod),
                  pl.BlockSpec((1, d), lambda b, i: (0, 0))],
        out_specs=pl.BlockSpec((None, tm, d), lambda b, i: (b, i, 0)),
        out_shape=jax.ShapeDtypeStruct((bsz, n, d), F32),
        compiler_params=_params(("parallel", "parallel"), 52),
        name="ec_combine_final" if final else "ec_combine",
    )(x, sp_col, ys, mod[0], g_final.reshape(1, d))


def _moe(x1, xn2, aff, wg, wu, wd, layer, mod, g_final, final):
    bsz, n, _ = x1.shape
    cap = EC_CAPACITY_FACTOR * n // N_EXPERTS
    per_group = LANES // N_EXPERTS
    ng = -(-bsz // per_group)
    aff_e = aff[:, :, :N_EXPERTS]
    sets = jnp.pad(aff_e, ((0, ng * per_group - bsz), (0, 0), (0, 0)))
    sets = sets.reshape(ng, per_group, n, N_EXPERTS).transpose(0, 2, 1, 3).reshape(ng, n, LANES)
    selpos = _select(sets, cap)
    sp_col = selpos.reshape(ng, n, per_group, N_EXPERTS).transpose(0, 2, 1, 3).reshape(ng * per_group, n, N_EXPERTS)[:bsz]
    sp_row = sp_col.transpose(0, 2, 1).reshape(bsz, N_EXPERTS, 1, n)
    aff_row = aff_e.transpose(0, 2, 1).reshape(bsz, N_EXPERTS, 1, n)
    ys = _ffn(sp_row, aff_row, xn2, wg, wu, wd, layer, cap)
    return _combine(x1, sp_col, ys, mod, g_final, cap, final)


def kernel(x, c, ctx, c_ctx, w_ada, b_ada, g_norm1, w_in, b_gates, g_hnorm, conv_w, conv_b, conv_ln_g, conv_ln_b,
           w_out, g_norm2, w_router, w_e_gate, w_e_up, w_e_down, g_final):
    bsz, t, d = x.shape
    depth = w_ada.shape[0]
    rows = t // GRID_W
    n_cond = 16
    cs = jnp.concatenate([c, c_ctx[None, :], jnp.zeros((n_cond - bsz - 1, d), F32)], axis=0)
    mods = _ada(cs, w_ada, b_ada).reshape(depth, n_cond, N_ADA, d)

    npair = N_HEADS // 2
    zero_states = (jnp.zeros((bsz, npair, 2, HEAD_DIM, HEAD_DIM), F32), jnp.zeros((bsz, npair, 8, HEAD_DIM), F32)) * 2
    h_ctx = ctx
    for l in range(depth):
        last = l == depth - 1
        lat, cmod = (mods, l, None), (mods, l, bsz)
        w_l = w_in[l]
        qkvg_cols = 3 * MLSTM_WIDTH + N_GATES * N_HEADS
        w_p = jnp.concatenate([w_l[:, :3 * MLSTM_WIDTH],
                               jnp.pad(w_l[:, 3 * MLSTM_WIDTH:qkvg_cols], ((0, 0), (0, LANES - N_GATES * N_HEADS))),
                               w_l[:, qkvg_cols:]], axis=1).astype(BF16)
        bg = jnp.pad(b_gates[l].reshape(1, N_HEADS * N_GATES), ((0, 0), (0, LANES - N_HEADS * N_GATES)))
        w_o = w_out[l].astype(BF16)
        w_r = jnp.pad(w_router[l], ((0, 0), (0, LANES - N_EXPERTS)))
        gh = g_hnorm[l].reshape(N_HEADS, HEAD_DIM, 1)

        def mlstm(xs, md, states, grid_rows, full):
            qkv, gates, extra = _inproj(xs, md, g_norm1[l], w_p, bg, full, grid_rows)
            return _mlstm(qkv, gates, states, gh, full, grid_rows), extra

        def sublayers(xs, md, states, grid_rows, final):
            (hr, hc, *st), (o, cacg, fr) = mlstm(xs, md, states, grid_rows, True)
            cv = _conv(cacg, conv_w[l], conv_b[l], conv_ln_g[l], conv_ln_b[l])
            fo = _fourier(fr)
            x1, xn2, aff = _outproj(xs, hr, hc, o, cv, fo, w_o, md, g_norm2[l], w_r)
            return _moe(x1, xn2, aff, w_e_gate, w_e_up, w_e_down, l, md, g_final, final), st

        if last:
            st, _ = mlstm(h_ctx, cmod, zero_states, None, False)
        else:
            h_ctx, st = sublayers(h_ctx, cmod, zero_states, None, False)
        x, _ = sublayers(x, lat, tuple(st), rows, last)
    return x
```

```python
import functools

import jax
import jax.numpy as jnp
import numpy as np
from jax import lax
from jax.experimental import pallas as pl
from jax.experimental.pallas import tpu as pltpu

F32 = jnp.float32
BF16 = jnp.bfloat16
I32 = jnp.int32

D_MODEL = 1024
GRID_W = 64
N_HEADS = 4
HEAD_DIM = 128
MLSTM_WIDTH = N_HEADS * HEAD_DIM
N_ROW_HEADS = N_HEADS // 2
CONV_CH = 256
CONV_K = 31
FOURIER_CH = 256
FOURIER_GROUP_CH = 64
N_GATES = 4
N_EXPERTS = 16
EC_CAPACITY_FACTOR = 2
N_ADA = 6
EPS = 1e-6

LANES = 128
MXU_DIM = 256
MLSTM_CHUNK = 128
MAX_CHUNKS = 16
CONV_HALO = 16
CONV_TAIL = 128
MIB = 1024 * 1024

SEC_QKV = (0, 3 * MLSTM_WIDTH)
SEC_G = (SEC_QKV[1], SEC_QKV[1] + LANES)
SEC_O = (SEC_G[1], SEC_G[1] + MLSTM_WIDTH)
SEC_CACG = (SEC_O[1], SEC_O[1] + 2 * CONV_CH)
SEC_FR = (SEC_CACG[1], SEC_CACG[1] + FOURIER_CH)
IN_COLS_PAD = SEC_FR[1]

ST_NF = 2 * HEAD_DIM
ST_NB = ST_NF + 16
ST_ROWS = ST_NB + 16
UP_ROWS = 2 * HEAD_DIM + 16


def _dot(a, b):
    return jnp.dot(a, b, preferred_element_type=F32)


def _dot_nt(a, b):
    return lax.dot_general(a, b, (((1,), (1,)), ((), ())), preferred_element_type=F32)


def _split_bf16(a):
    hi = a.astype(BF16)
    lo = (a - hi.astype(F32)).astype(BF16)
    return hi, lo


def _dot3(a, w):
    ah, al = _split_bf16(a)
    wh, wl = _split_bf16(w)
    return _dot(ah, wh) + (_dot(al, wh) + _dot(ah, wl))


def _modnorm(x, g, sh, sc):
    ms = jnp.mean(x * x, axis=-1, keepdims=True)
    return x * lax.rsqrt(ms + EPS) * g * (1.0 + sc) + sh


def _rows(n_rows, width, *vals):
    rid = lax.broadcasted_iota(I32, (n_rows, width), 0)
    out = jnp.zeros((n_rows, width), F32)
    for i, v in enumerate(vals):
        out = jnp.where(rid == i, v, out)
    return out


def _mod_spec(mod):
    mods, layer, row = mod
    _, _, n_ada, d = mods.shape
    return pl.BlockSpec((None, None, n_ada, d), lambda b, i, *_: (layer, b if row is None else row, 0, 0))


def _params(sem, vmem_mib):
    return pltpu.CompilerParams(dimension_semantics=sem, vmem_limit_bytes=vmem_mib * MIB)


def _ada_kernel(c_ref, w_ref, b_ref, o_ref):
    c = c_ref[...]
    o_ref[...] = _dot3(c * jax.nn.sigmoid(c), w_ref[...]) + b_ref[...]


def _ada(cs, w_ada, b_ada):
    n_layers, d, n6 = w_ada.shape
    r = cs.shape[0]
    tn = 1536
    return pl.pallas_call(
        _ada_kernel,
        grid=(n_layers, n6 // tn),
        in_specs=[
            pl.BlockSpec((r, d), lambda l, j: (0, 0)),
            pl.BlockSpec((None, d, tn), lambda l, j: (l, 0, j)),
            pl.BlockSpec((None, 1, tn), lambda l, j: (l, 0, j)),
        ],
        out_specs=pl.BlockSpec((None, r, tn), lambda l, j: (l, 0, j)),
        out_shape=jax.ShapeDtypeStruct((n_layers, r, n6), F32),
        compiler_params=_params(("arbitrary", "arbitrary"), 40),
        name="ada_mod",
    )(cs, w_ada, b_ada.reshape(n_layers, 1, n6))


def _inproj_kernel(x_ref, mod_ref, g_ref, w_ref, bg_ref, *out_refs, full, colmajor):
    qkv_refs, gates_ref = out_refs[:6], out_refs[6]
    xn = _modnorm(x_ref[...], g_ref[...], mod_ref[0:1, :], mod_ref[1:2, :]).astype(BF16)
    tm = xn.shape[0]
    w = MLSTM_WIDTH
    for i in range(3):
        r = _dot(xn, w_ref[:, i * w:(i + 1) * w])
        if i == 1:
            r = r * HEAD_DIM ** -0.5
        row_ref, col_ref = qkv_refs[2 * i], qkv_refs[2 * i + 1]
        for hh in range(N_ROW_HEADS):
            row_ref[hh] = r[:, hh * HEAD_DIM:(hh + 1) * HEAD_DIM].astype(BF16)
            piece = r[:, (N_ROW_HEADS + hh) * HEAD_DIM:(N_ROW_HEADS + hh + 1) * HEAD_DIM]
            if colmajor:
                piece = jnp.swapaxes(piece.reshape(tm // GRID_W, GRID_W, HEAD_DIM), 0, 1)
            col_ref[hh] = piece.astype(BF16)
    gates_ref[...] = _dot(xn, w_ref[:, SEC_G[0]:SEC_G[1]]) + bg_ref[...]
    if full:
        o_ref, cacg_ref, fr_ref = out_refs[7:]
        o_ref[...] = _dot(xn, w_ref[:, SEC_O[0]:SEC_O[1]])
        cacg_ref[...] = _dot(xn, w_ref[:, SEC_CACG[0]:SEC_CACG[1]])
        fr_ref[...] = _dot(xn, w_ref[:, SEC_FR[0]:SEC_FR[1]]).astype(BF16)


def _inproj(x, mod, g, w, bg, full, rows):
    bsz, t, d = x.shape
    tm = min(t, 1024)
    nh = N_ROW_HEADS
    ncols = IN_COLS_PAD if full else SEC_G[1]
    tok = lambda n: pl.BlockSpec((None, tm, n), lambda b, i: (b, i, 0))
    row_shape = jax.ShapeDtypeStruct((bsz, nh, t, HEAD_DIM), BF16)
    row_spec = pl.BlockSpec((None, nh, tm, HEAD_DIM), lambda b, i: (b, 0, i, 0))
    if rows is None:
        col_shape, col_spec = row_shape, row_spec
    else:
        col_shape = jax.ShapeDtypeStruct((bsz, nh, GRID_W, rows, HEAD_DIM), BF16)
        col_spec = pl.BlockSpec((None, nh, GRID_W, tm // GRID_W, HEAD_DIM), lambda b, i: (b, 0, 0, i, 0))
    out_shape = [row_shape, col_shape] * 3 + [jax.ShapeDtypeStruct((bsz, t, LANES), F32)]
    out_specs = [row_spec, col_spec] * 3 + [tok(LANES)]
    if full:
        out_shape += [jax.ShapeDtypeStruct((bsz, t, MLSTM_WIDTH), F32),
                      jax.ShapeDtypeStruct((bsz, t, 2 * CONV_CH), F32),
                      jax.ShapeDtypeStruct((bsz, t, FOURIER_CH), BF16)]
        out_specs += [tok(MLSTM_WIDTH), tok(2 * CONV_CH), tok(FOURIER_CH)]
    outs = pl.pallas_call(
        functools.partial(_inproj_kernel, full=full, colmajor=rows is not None),
        grid=(bsz, t // tm),
        in_specs=[tok(d), _mod_spec(mod),
                  pl.BlockSpec((1, d), lambda b, i: (0, 0)),
                  pl.BlockSpec((d, ncols), lambda b, i: (0, 0)),
                  pl.BlockSpec((1, LANES), lambda b, i: (0, 0))],
        out_specs=out_specs,
        out_shape=out_shape,
        compiler_params=_params(("parallel", "parallel"), 56),
        name="inproj_full" if full else "inproj_qkvg",
    )(x, mod[0], g.reshape(1, d), w, bg)
    qkv = [a.reshape(bsz, nh, t, HEAD_DIM) for a in outs[:6]]
    return qkv, outs[6], outs[7:]


def _logsig(x):
    return jnp.minimum(x, 0.0) - jnp.log1p(jnp.exp(-jnp.abs(x)))


def _tri_masks(n):
    a0 = lax.broadcasted_iota(I32, (n, n), 0)
    a1 = lax.broadcasted_iota(I32, (n, n), 1)
    return a0 <= a1, a0 >= a1


def _split3(x):
    hi = x.astype(BF16).astype(F32)
    r1 = x - hi
    mid = r1.astype(BF16).astype(F32)
    return hi, mid, r1 - mid


def _mlstm_consts(tri_scr, ones_scr):
    lc = MLSTM_CHUNK
    u = lax.broadcasted_iota(I32, (lc, 2 * lc), 0)
    t2 = lax.broadcasted_iota(I32, (lc, 2 * lc), 1)
    tri_scr[...] = jnp.where(((t2 < lc) & (u <= t2)) | ((t2 >= lc) & (u >= t2 - lc)), 1.0, 0.0).astype(BF16)
    r = lax.broadcasted_iota(I32, (16, 2 * lc), 0)
    t16 = lax.broadcasted_iota(I32, (16, 2 * lc), 1)
    ones_scr[...] = jnp.where(((r < 3) & (t16 < lc)) | ((r >= 3) & (r < 6) & (t16 >= lc)), 1.0, 0.0).astype(BF16)


def _mlstm_gates(head, g_ref, tri_scr, rowg_scr, stat_scr, n_chunks, cm_rows):
    lc = MLSTM_CHUNK
    pick = jnp.where(lax.broadcasted_iota(I32, (16, LANES), 1) == N_GATES * head + lax.broadcasted_iota(I32, (16, LANES), 0),
                     1.0, 0.0).astype(BF16)
    chunk_id = lax.broadcasted_iota(I32, (MAX_CHUNKS, lc), 0)

    def gather(c, acc):
        if cm_rows is None:
            g = g_ref[pl.ds(pl.multiple_of(c * lc, lc), lc), :]
        else:
            cols = lc // cm_rows
            g = jnp.concatenate([g_ref[pl.ds(c * cols + j, cm_rows, stride=GRID_W), :] for j in range(cols)], axis=0)
        rows = sum(_dot_nt(pick, part.astype(BF16)) for part in _split3(g))
        return tuple(jnp.where(chunk_id == c, rows[i:i + 1], a) for i, a in enumerate(acc))

    zeros = jnp.zeros((MAX_CHUNKS, lc), F32)
    i_f, f_f, i_b, f_b = lax.fori_loop(0, n_chunks, gather, (zeros,) * N_GATES, unroll=min(4, n_chunks))
    lf_f, lf_b = _logsig(f_f), _logsig(f_b)
    b_f = sum(_dot(part.astype(BF16), tri_scr[:, 0:lc]) for part in _split3(lf_f))
    b_b = sum(_dot(part.astype(BF16), tri_scr[:, lc:]) for part in _split3(lf_b))
    bl_f = jnp.sum(lf_f, axis=1, keepdims=True)
    bl_b = jnp.sum(lf_b, axis=1, keepdims=True)
    ci_f, ci_b = i_f - b_f, i_b - b_b
    ml_f = jnp.max(ci_f + bl_f, axis=1, keepdims=True)
    ml_b = jnp.max(ci_b + bl_b, axis=1, keepdims=True)
    for r, val in enumerate((jnp.exp(ci_f + bl_f - ml_f), jnp.exp(ci_b + bl_b - ml_b), b_f, b_b, ci_f, ci_b)):
        rowg_scr[:, r, :] = val
    for r, val in enumerate((bl_f, bl_b, ml_f, ml_b)):
        stat_scr[:, r, :] = jnp.broadcast_to(val, (MAX_CHUNKS, HEAD_DIM))


def _mlstm_increment(c, k_ref, v_ref, rowg_scr, up_scr, vt_scr):
    lc = MLSTM_CHUNK
    s = pl.multiple_of(c * lc, lc)
    v_t = v_ref[pl.ds(s, lc), :].astype(F32).T
    rg = rowg_scr[c]
    w_f, w_b = rg[0:1], rg[1:2]
    lhs = jnp.concatenate([(v_t * w_f).astype(BF16), (v_t * w_b).astype(BF16),
                           _rows(16, lc, w_f, w_b).astype(BF16)], axis=0)
    up_scr[c] = _dot(lhs, k_ref[pl.ds(s, lc), :])
    if vt_scr is not None:
        vt_scr[c] = v_t.astype(BF16)


def _mlstm_state_step(cc, carry, up_scr, stat_scr, st_scr, mst_scr, fwd):
    cmat, n, m = carry
    off, row = (0, 0) if fwd else (HEAD_DIM, 1)
    if st_scr is not None:
        st_scr[cc, off:off + HEAD_DIM, :] = cmat.astype(BF16)
        nrow = ST_NF if fwd else ST_NB
        st_scr[cc, nrow:nrow + 16, :] = _rows(16, HEAD_DIM, n).astype(BF16)
        mst_scr[cc, row:row + 1, :] = jnp.broadcast_to(m, (1, HEAD_DIM))
    st = stat_scr[cc]
    bl, ml = st[row:row + 1, 0:1], st[2 + row:3 + row, 0:1]
    m_new = jnp.maximum(bl + m, ml)
    keep = jnp.exp(bl + m - m_new)
    gain = jnp.exp(ml - m_new)
    cmat = keep * cmat + gain * up_scr[cc, off:off + HEAD_DIM, :]
    n = keep * n + gain * up_scr[cc, 2 * HEAD_DIM + row:2 * HEAD_DIM + row + 1, :]
    return cmat, n, m_new


def _mlstm_outputs(c, chains, ones_scr):
    lc = MLSTM_CHUNK
    s = pl.multiple_of(c * lc, lc)
    le, ge = _tri_masks(lc)
    qs = [ch[0][pl.ds(s, lc), :] for ch in chains]
    qk_ts = [_dot_nt(ch[1][pl.ds(s, lc), :], q) for ch, q in zip(chains, qs)]
    g_ts = [_dot_nt(ch[5][c], q) for ch, q in zip(chains, qs)]
    rgs = [ch[4][c] for ch in chains]
    cis = [lax.dot_general(_rows(16, lc, *_split3(rg[4:5]), *_split3(rg[5:6])).astype(BF16), ones_scr[...],
                           (((0,), (0,)), ((), ())), preferred_element_type=F32) for rg in rgs]

    def direction(qk_t, valid, b_row, ci, m_st, qn):
        dlog = jnp.where(valid, b_row + ci, -jnp.inf)
        inter = b_row + m_st
        m_t = jnp.maximum(inter, jnp.max(dlog, axis=0, keepdims=True))
        s_t = qk_t * jnp.exp(dlog - m_t)
        w_inter = jnp.exp(inter - m_t)
        den = w_inter * qn + jnp.sum(s_t, axis=0, keepdims=True)
        r = 1.0 / jnp.maximum(jnp.abs(den), jnp.exp(-m_t))
        return r * s_t, r * w_inter

    mixed = []
    for ch, qk_t, g_t, rg, ci in zip(chains, qk_ts, g_ts, rgs, cis):
        ms = ch[6][c]
        sf, cf = direction(qk_t, le, rg[2:3], ci[:, 0:lc], ms[0:1, 0:1], g_t[ST_NF:ST_NF + 1])
        sb, cb = direction(qk_t, ge, rg[3:4], ci[:, lc:], ms[1:2, 0:1], g_t[ST_NB:ST_NB + 1])
        mixed.append(((sf + sb).astype(BF16), cf * g_t[0:HEAD_DIM] + cb * g_t[HEAD_DIM:2 * HEAD_DIM]))
    h_ts = [inter_part + _dot(ch[7][c], s_mix) for ch, (s_mix, inter_part) in zip(chains, mixed)]
    for ch, h_t in zip(chains, h_ts):
        gh_ref, h_ref, cm_rows = ch[2], ch[3], ch[8]
        h = (h_t * lax.rsqrt(jnp.mean(h_t * h_t, axis=0, keepdims=True) + EPS) * gh_ref[...]).T
        if cm_rows is None:
            h_ref[pl.ds(s, lc), :] = h
        else:
            cols = lc // cm_rows
            for j in range(cols):
                h_ref[pl.ds(c * cols + j, cm_rows, stride=GRID_W), :] = h[j * cm_rows:(j + 1) * cm_rows]


def _mlstm_kernel(*refs, n_chunks, with_outputs, cm_rows):
    n_slot_in = 6
    g_ref = refs[0]
    slots_in = [refs[1:1 + n_slot_in], refs[1 + n_slot_in:1 + 2 * n_slot_in]]
    rest = refs[1 + 2 * n_slot_in:]
    if with_outputs:
        h_refs, rest = rest[:2], rest[2:]
    state_out, scratch = rest[:4], rest[4:]
    tri_scr, ones_scr, scratch = scratch[0], scratch[1], scratch[2:]
    n_scr = 6 if with_outputs else 3
    slots_scr = [scratch[:n_scr], scratch[n_scr:2 * n_scr]]
    slot_rows = [None, cm_rows]
    _mlstm_consts(tri_scr, ones_scr)
    for slot, scr in enumerate(slots_scr):
        _mlstm_gates(pl.program_id(1) + slot * N_ROW_HEADS, g_ref, tri_scr, scr[0], scr[1], n_chunks, slot_rows[slot])

    def increments(c, _):
        for (q_ref, k_ref, v_ref, c0_ref, n0_ref, gh_ref), scr in zip(slots_in, slots_scr):
            _mlstm_increment(c, k_ref, v_ref, scr[0], scr[2], scr[5] if with_outputs else None)
        return 0

    lax.fori_loop(0, n_chunks, increments, 0, unroll=min(4, n_chunks))

    def states(i, carry):
        out = []
        for slot, scr in enumerate(slots_scr):
            st_scr, mst_scr = (scr[3], scr[4]) if with_outputs else (None, None)
            out.append(_mlstm_state_step(i, carry[2 * slot], scr[2], scr[1], st_scr, mst_scr, True))
            out.append(_mlstm_state_step(n_chunks - 1 - i, carry[2 * slot + 1], scr[2], scr[1], st_scr, mst_scr, False))
        return tuple(out)

    init = []
    for (q_ref, k_ref, v_ref, c0_ref, n0_ref, gh_ref) in slots_in:
        init.append((c0_ref[0], n0_ref[0:1, :], n0_ref[2:3, 0:1]))
        init.append((c0_ref[1], n0_ref[1:2, :], n0_ref[3:4, 0:1]))
    final = lax.fori_loop(0, n_chunks, states, tuple(init))
    for slot in range(2):
        c_out, n_out = state_out[2 * slot], state_out[2 * slot + 1]
        (cf, nf, mf), (cb, nb, mb) = final[2 * slot], final[2 * slot + 1]
        c_out[0] = cf
        c_out[1] = cb
        n_out[...] = _rows(8, HEAD_DIM, nf, nb, mf, mb)

    if with_outputs:
        chains = [(q_ref, k_ref, gh_ref, h_refs[slot], scr[0], scr[3], scr[4], scr[5], slot_rows[slot])
                  for slot, ((q_ref, k_ref, v_ref, c0_ref, n0_ref, gh_ref), scr) in enumerate(zip(slots_in, slots_scr))]

        def outputs(c, _):
            _mlstm_outputs(c, chains, ones_scr)
            return 0

        lax.fori_loop(0, n_chunks, outputs, 0, unroll=min(4, n_chunks))


def _mlstm(qkv, gates, states, gh, with_outputs, cm_rows):
    bsz, npair, t, dh = qkv[0].shape
    nc = t // MLSTM_CHUNK
    lc = MLSTM_CHUNK
    seq = pl.BlockSpec((None, None, t, dh), lambda b, p: (b, p, 0, 0))
    cst = pl.BlockSpec((None, None, 2, dh, dh), lambda b, p: (b, p, 0, 0, 0))
    nst = pl.BlockSpec((None, None, 8, dh), lambda b, p: (b, p, 0, 0))

    def slot_specs(head_off):
        return [seq, seq, seq, cst, nst, pl.BlockSpec((None, dh, 1), lambda b, p: (p + head_off, 0, 0))]

    st_shapes = [jax.ShapeDtypeStruct((bsz, npair, 2, dh, dh), F32), jax.ShapeDtypeStruct((bsz, npair, 8, dh), F32)] * 2
    out_shape, out_specs = st_shapes, [cst, nst, cst, nst]
    assert nc <= MAX_CHUNKS
    slot_scratch = [pltpu.VMEM((MAX_CHUNKS, 8, lc), F32), pltpu.VMEM((MAX_CHUNKS, 8, dh), F32),
                    pltpu.VMEM((nc, UP_ROWS, dh), F32)]
    if with_outputs:
        h_shape = jax.ShapeDtypeStruct((bsz, t, npair * dh), F32)
        h_spec = pl.BlockSpec((None, t, dh), lambda b, p: (b, 0, p))
        out_shape = [h_shape, h_shape] + out_shape
        out_specs = [h_spec, h_spec] + out_specs
        slot_scratch += [pltpu.VMEM((nc, ST_ROWS, dh), BF16), pltpu.VMEM((nc, 8, dh), F32), pltpu.VMEM((nc, dh, lc), BF16)]
    q_r, q_c, k_r, k_c, v_r, v_c = qkv
    c_r, n_r, c_c, n_c = states
    return pl.pallas_call(
        functools.partial(_mlstm_kernel, n_chunks=nc, with_outputs=with_outputs, cm_rows=cm_rows),
        grid=(bsz, npair),
        in_specs=[pl.BlockSpec((None, t, LANES), lambda b, p: (b, 0, 0))] + slot_specs(0) + slot_specs(N_ROW_HEADS),
        out_specs=out_specs,
        out_shape=out_shape,
        scratch_shapes=[pltpu.VMEM((lc, 2 * lc), BF16), pltpu.VMEM((16, 2 * lc), BF16)] + slot_scratch * 2,
        compiler_params=_params(("parallel", "parallel"), 48),
        name="mlstm_full" if with_outputs else "mlstm_states",
    )(gates, q_r, k_r, v_r, c_r, n_r, gh, q_c, k_c, v_c, c_c, n_c, gh)


def _conv_kernel(cacg_ref, w_ref, cb_ref, lg_ref, lb_ref, o_ref, ysh, yconv, *, t):
    rt = 64
    ysh[0, 0:CONV_HALO, :] = jnp.zeros((CONV_HALO, CONV_CH), F32)
    ysh[0, CONV_HALO + t:t + CONV_TAIL, :] = jnp.zeros((CONV_TAIL - CONV_HALO, CONV_CH), F32)

    def fill(r, _):
        s = pl.multiple_of(r * rt, rt)
        a = cacg_ref[pl.ds(s, rt), 0:CONV_CH]
        g = cacg_ref[pl.ds(s, rt), CONV_CH:2 * CONV_CH]
        ysh[0, pl.ds(CONV_HALO + s, rt), :] = a * jax.nn.sigmoid(g)
        return 0

    lax.fori_loop(0, t // rt, fill, 0)

    def shift(r, _):
        s = pl.multiple_of(r * rt, rt)
        win = ysh[0, pl.ds(s, rt + 8), :]
        for res in range(1, 8):
            ysh[res, pl.ds(s, rt), :] = pltpu.roll(win, rt + 8 - res, axis=0)[0:rt, :]
        return 0

    lax.fori_loop(0, t // rt + 1, shift, 0)

    def tile(r, _):
        s = pl.multiple_of(r * rt, rt)
        acc = jnp.zeros((rt, CONV_CH), F32)
        for kk in range(CONV_K):
            off = kk + CONV_HALO - CONV_K // 2
            rows = ysh[off % 8, pl.ds(pl.multiple_of(s + 8 * (off // 8), 8), rt), :]
            acc = acc + w_ref[kk:kk + 1, :] * rows
        yconv[pl.ds(s, rt), :] = acc + cb_ref[...]
        return 0

    lax.fori_loop(0, t // rt, tile, 0)

    def norm(r, _):
        s = pl.multiple_of(r * rt, rt)
        y = yconv[pl.ds(s, rt), :]
        mu = jnp.mean(y, axis=-1, keepdims=True)
        yc = y - mu
        var = jnp.mean(yc * yc, axis=-1, keepdims=True)
        z = yc * lax.rsqrt(var + EPS) * lg_ref[...] + lb_ref[...]
        o_ref[pl.ds(s, rt), :] = (z * jax.nn.sigmoid(z)).astype(BF16)
        return 0

    lax.fori_loop(0, t // rt, norm, 0, unroll=4)


def _conv(cacg, conv_w, conv_b, ln_g, ln_b):
    bsz, t, _ = cacg.shape
    row = lambda a: a.reshape(1, CONV_CH)
    vec = pl.BlockSpec((1, CONV_CH), lambda b: (0, 0))
    return pl.pallas_call(
        functools.partial(_conv_kernel, t=t),
        grid=(bsz,),
        in_specs=[pl.BlockSpec((None, t, 2 * CONV_CH), lambda b: (b, 0, 0)),
                  pl.BlockSpec((CONV_K + 1, CONV_CH), lambda b: (0, 0)), vec, vec, vec],
        out_specs=pl.BlockSpec((None, t, CONV_CH), lambda b: (b, 0, 0)),
        out_shape=jax.ShapeDtypeStruct((bsz, t, CONV_CH), BF16),
        scratch_shapes=[pltpu.VMEM((8, t + CONV_TAIL, CONV_CH), F32), pltpu.VMEM((t, CONV_CH), F32)],
        compiler_params=_params(("parallel",), 40),
        name="conformer_conv",
    )(cacg, jnp.pad(conv_w, ((0, 1), (0, 0))), row(conv_b), row(ln_g), row(ln_b))


@functools.lru_cache(maxsize=None)
def _dft_tables(t):
    j = np.arange(t, dtype=np.int64)
    ang = 2.0 * np.pi * ((j[:, None] * j[None, :]) % t).astype(np.float64) / t
    wt = np.concatenate([np.cos(ang), -np.sin(ang)], axis=1).astype(np.float32)
    c = np.arange(FOURIER_CH, dtype=np.int64)
    grp, idx = c // FOURIER_GROUP_CH, c % FOURIER_GROUP_CH
    same = grp[:, None] == grp[None, :]
    angc = 2.0 * np.pi * ((idx[:, None] * idx[None, :]) % FOURIER_GROUP_CH).astype(np.float64) / FOURIER_GROUP_CH
    cc = np.where(same, np.cos(angc), 0.0).astype(np.float32)
    sc = np.where(same, np.sin(angc), 0.0).astype(np.float32)
    return wt, cc, sc


def _fourier_kernel(z_ref, cc_ref, sc_ref, wt_ref, o_ref, zcs, *, t, scale):
    bsz, _, ch = z_ref.shape

    @pl.when(pl.program_id(0) == 0)
    def _():
        for b in range(bsz):
            zcs[0:t, b * ch:(b + 1) * ch] = _dot(z_ref[b], cc_ref[...]).astype(BF16)
            zcs[t:2 * t, b * ch:(b + 1) * ch] = _dot(z_ref[b], sc_ref[...]).astype(BF16)

    r = _dot(wt_ref[...], zcs[...]) * scale
    for b in range(bsz):
        o_ref[b] = r[:, b * ch:(b + 1) * ch].astype(BF16)


def _fourier(z):
    bsz, t, ch = z.shape
    wt, cc, sc = _dft_tables(t)
    tm = min(t, 512)
    mat = pl.BlockSpec((ch, ch), lambda i: (0, 0))
    return pl.pallas_call(
        functools.partial(_fourier_kernel, t=t, scale=float((t * FOURIER_GROUP_CH) ** -0.5)),
        grid=(t // tm,),
        in_specs=[pl.BlockSpec((bsz, t, ch), lambda i: (0, 0, 0)), mat, mat,
                  pl.BlockSpec((tm, 2 * t), lambda i: (i, 0))],
        out_specs=pl.BlockSpec((bsz, tm, ch), lambda i: (0, i, 0)),
        out_shape=jax.ShapeDtypeStruct((bsz, t, ch), BF16),
        scratch_shapes=[pltpu.VMEM((2 * t, bsz * ch), BF16)],
        compiler_params=_params(("arbitrary",), 56),
        name="fourier_mix",
    )(z, jnp.asarray(cc).astype(BF16), jnp.asarray(sc).astype(BF16), jnp.asarray(wt).astype(BF16))


def _outproj_kernel(x_ref, hr_ref, hc_ref, o_ref, cv_ref, fo_ref, w_ref, mod_ref, g2_ref, wr_ref,
                    x1_ref, xn_ref, aff_ref):
    half = MLSTM_WIDTH // 2
    a, b = MLSTM_WIDTH, MLSTM_WIDTH + CONV_CH
    og = jax.nn.sigmoid(o_ref[...])
    y = (_dot((hr_ref[...] * og[:, 0:half]).astype(BF16), w_ref[0:half, :])
         + _dot((hc_ref[...] * og[:, half:a]).astype(BF16), w_ref[half:a, :])
         + _dot(cv_ref[...], w_ref[a:b, :]) + _dot(fo_ref[...], w_ref[b:, :]))
    x1 = x_ref[...] + mod_ref[2:3, :] * y
    x1_ref[...] = x1
    xn = _modnorm(x1, g2_ref[...], mod_ref[3:4, :], mod_ref[4:5, :])
    xn_ref[...] = xn.astype(BF16)
    tm = xn.shape[0]
    r = _dot(jnp.concatenate(_split_bf16(xn), axis=0), jnp.concatenate(_split_bf16(wr_ref[...]), axis=1))
    logits = (r[:tm, :LANES] + r[:tm, LANES:]) + (r[tm:, :LANES] + r[tm:, LANES:])
    lane = lax.broadcasted_iota(I32, logits.shape, 1)
    logits = jnp.where(lane < N_EXPERTS, logits, -jnp.inf)
    e = jnp.exp(logits - jnp.max(logits, axis=-1, keepdims=True))
    aff_ref[...] = e / jnp.sum(e, axis=-1, keepdims=True)


def _outproj(x, hr, hc, o, cv, fo, w_out, mod, g2, wr):
    bsz, t, d = x.shape
    tm = min(t, 512)
    tok = lambda n: pl.BlockSpec((None, tm, n), lambda b, i: (b, i, 0))
    return pl.pallas_call(
        _outproj_kernel,
        grid=(bsz, t // tm),
        in_specs=[tok(d), tok(MLSTM_WIDTH // 2), tok(MLSTM_WIDTH // 2), tok(MLSTM_WIDTH), tok(CONV_CH), tok(FOURIER_CH),
                  pl.BlockSpec((d, d), lambda b, i: (0, 0)), _mod_spec(mod),
                  pl.BlockSpec((1, d), lambda b, i: (0, 0)),
                  pl.BlockSpec((d, LANES), lambda b, i: (0, 0))],
        out_specs=[tok(d), tok(d), tok(LANES)],
        out_shape=[jax.ShapeDtypeStruct((bsz, t, d), F32), jax.ShapeDtypeStruct((bsz, t, d), BF16),
                   jax.ShapeDtypeStruct((bsz, t, LANES), F32)],
        compiler_params=_params(("parallel", "parallel"), 40),
        name="outproj_router",
    )(x, hr, hc, o, cv, fo, w_out, mod[0], g2.reshape(1, d), wr)


def _select_kernel(a_ref, sp_ref, win_ref, *, cap, tile):
    a = a_ref[...]
    n = a.shape[0]
    blk = min(n, MXU_DIM)

    def count(mask):
        return jnp.sum(jnp.where(mask, 1.0, 0.0), axis=0, keepdims=True)

    def search(i, thr):
        cand = thr | jnp.left_shift(jnp.int32(1), 30 - i)
        return jnp.where(count(a >= lax.bitcast_convert_type(cand, F32)) >= cap, cand, thr)

    thr = lax.bitcast_convert_type(lax.fori_loop(0, 31, search, jnp.zeros((1, LANES), I32)), F32)
    gt = a > thr
    eq = a == thr
    before = jnp.where(lax.broadcasted_iota(I32, (blk, blk), 1) < lax.broadcasted_iota(I32, (blk, blk), 0), 1.0, 0.0).astype(BF16)

    def excl_cumsum(mask):
        x = jnp.where(mask, 1.0, 0.0)
        run = jnp.zeros((1, LANES), F32)
        parts = []
        for j in range(n // blk):
            xb = x[j * blk:(j + 1) * blk]
            parts.append(_dot(before, xb.astype(BF16)) + run)
            run = run + jnp.sum(xb, axis=0, keepdims=True)
        return jnp.concatenate(parts, axis=0)

    sel = gt | (eq & (excl_cumsum(eq) < cap - count(gt)))
    pos = excl_cumsum(sel).astype(I32)
    sp_ref[...] = jnp.where(sel, pos, -1)
    rid = lax.broadcasted_iota(I32, win_ref.shape, 0)
    win = jnp.zeros(win_ref.shape, I32)
    for ti in range(n // tile):
        t_sel, t_pos = sel[ti * tile:(ti + 1) * tile], pos[ti * tile:(ti + 1) * tile]
        win = jnp.where(rid == 2 * ti, jnp.min(jnp.where(t_sel, t_pos, cap), axis=0, keepdims=True), win)
        win = jnp.where(rid == 2 * ti + 1, jnp.max(jnp.where(t_sel, t_pos + 1, 0), axis=0, keepdims=True), win)
    win_ref[...] = win


def _select(aff_sets, cap, tile):
    ng, n, _ = aff_sets.shape
    assert 2 * (n // tile) <= 8
    blk = pl.BlockSpec((None, n, LANES), lambda g: (g, 0, 0))
    return pl.pallas_call(
        functools.partial(_select_kernel, cap=cap, tile=tile),
        grid=(ng,),
        in_specs=[blk],
        out_specs=[blk, pl.BlockSpec((None, 8, LANES), lambda g: (g, 0, 0))],
        out_shape=[jax.ShapeDtypeStruct((ng, n, LANES), I32), jax.ShapeDtypeStruct((ng, 8, LANES), I32)],
        compiler_params=_params(("parallel",), 32),
        name="ec_select",
    )(aff_sets)


def _ffn_kernel(sp_ref, ar_ref, xn_ref, wg_ref, wu_ref, wd_ref, ys_ref, wgu_b, wd_b, *, cap):
    g, j, k = pl.program_id(0), pl.program_id(1), pl.program_id(2)
    rows = wg_ref.shape[0]

    def cast_slice():
        rt = min(256, rows)
        dst = lax.rem(g, 2)
        for piece in range(rows // rt):
            src_rows = pl.ds(piece * rt, rt)
            dst_rows = pl.ds(pl.multiple_of(j * rows + piece * rt, rt), rt)
            wgu_b[dst, k, 0, dst_rows, :] = wg_ref[src_rows, :].astype(BF16)
            wgu_b[dst, k, 1, dst_rows, :] = wu_ref[src_rows, :].astype(BF16)
            wd_b[dst, k, dst_rows, :] = wd_ref[src_rows, :].astype(BF16)

    @pl.when(g == 0)
    def _():
        cast_slice()

    @pl.when(g > 0)
    def _():
        cast_slice()
        src = lax.rem(g + 1, 2)
        nb, _, n = sp_ref.shape
        slot = lax.broadcasted_iota(I32, (cap, n), 0)
        xs, gates = [], []
        for bi in range(nb):
            hit = slot == sp_ref[bi]
            gates.append(jnp.sum(jnp.where(hit, ar_ref[bi], 0.0), axis=1, keepdims=True))
            xs.append(_dot(jnp.where(hit, 1.0, 0.0).astype(BF16), xn_ref[bi]).astype(BF16))
        xs = xs[0] if nb == 1 else jnp.concatenate(xs, axis=0)
        gate = gates[0] if nb == 1 else jnp.concatenate(gates, axis=0)
        h1 = _dot(xs, wgu_b[src, k, 0])
        hid = (h1 * jax.nn.sigmoid(h1) * _dot(xs, wgu_b[src, k, 1])).astype(BF16)
        ys = (_dot(hid, wd_b[src, k]) * gate).astype(BF16)
        for bi in range(nb):
            ys_ref[bi] = ys[bi * cap:(bi + 1) * cap]


def _ffn(sp_row, aff_row, xn, wg, wu, wd, layer, cap):
    bsz, n, d = xn.shape
    _, ne, _, hid = wg.shape
    assert d == hid
    nb = min(bsz, max(1, MXU_DIM // cap))
    nj = bsz // nb
    gsz = 2 if nj > 1 else 1
    rows = d // nj
    expert = lambda g, k: jnp.maximum((g - 1) * gsz + k, 0)
    batch = lambda g, j: jnp.where(g == 0, 0, j)
    row = pl.BlockSpec((nb, None, 1, n), lambda g, j, k: (batch(g, j), expert(g, k), 0, 0))
    wspec = pl.BlockSpec((None, None, rows, hid), lambda g, j, k: (layer, jnp.minimum(g * gsz + k, ne - 1), j, 0))
    return pl.pallas_call(
        functools.partial(_ffn_kernel, cap=cap),
        grid=(ne // gsz + 1, nj, gsz),
        in_specs=[row, row, pl.BlockSpec((nb, n, d), lambda g, j, k: (batch(g, j), 0, 0)), wspec, wspec, wspec],
        out_specs=pl.BlockSpec((nb, None, cap, d), lambda g, j, k: (batch(g, j), expert(g, k), 0, 0)),
        out_shape=jax.ShapeDtypeStruct((bsz, ne, cap, d), BF16),
        scratch_shapes=[pltpu.VMEM((2, gsz, 2, d, hid), BF16), pltpu.VMEM((2, gsz, hid, d), BF16)],
        compiler_params=_params(("arbitrary", "arbitrary", "arbitrary"), 56),
        name="ec_ffn",
    )(sp_row, aff_row, xn, wg, wu, wd)


def _combine_kernel(win_ref, x_ref, spc_ref, ys_ref, mod_ref, gf_ref, o_ref, *, cap, final, pack):
    b, i = pl.program_id(0), pl.program_id(1)
    spc = spc_ref[...]
    tm = spc.shape[0]

    def finish(acc):
        out = x_ref[...] + mod_ref[5:6, :] * acc
        if final:
            out = out * lax.rsqrt(jnp.mean(out * out, axis=-1, keepdims=True) + EPS) * gf_ref[...]
        o_ref[...] = out

    def full_width():
        slot = lax.broadcasted_iota(I32, (tm, cap), 1)
        acc = jnp.zeros(x_ref.shape, F32)
        for e in range(N_EXPERTS):
            hit = jnp.where(spc[:, e:e + 1] == slot, 1.0, 0.0).astype(BF16)
            acc = acc + _dot(hit, ys_ref[e])
        return acc

    if pack == 1:
        finish(full_width())
        return

    w = cap // pack
    starts, fits = [], None
    for e in range(N_EXPERTS):
        lo = win_ref[b, 2 * i * N_EXPERTS + e]
        hi = win_ref[b, (2 * i + 1) * N_EXPERTS + e]
        st = jnp.minimum(lax.shift_left(lax.shift_right_logical(lo, 4), 4), cap - w)
        ok = hi - st <= w
        fits = ok if fits is None else jnp.logical_and(fits, ok)
        starts.append(pl.multiple_of(st, 16))

    @pl.when(fits)
    def _():
        slot = lax.broadcasted_iota(I32, (tm, w), 1)
        acc = jnp.zeros(x_ref.shape, F32)
        for p in range(N_EXPERTS // pack):
            es = range(p * pack, (p + 1) * pack)
            hit = jnp.concatenate([jnp.where(spc[:, e:e + 1] - starts[e] == slot, 1.0, 0.0).astype(BF16) for e in es], axis=1)
            ysw = jnp.concatenate([ys_ref[e, pl.ds(starts[e], w), :] for e in es], axis=0)
            acc = acc + _dot(hit, ysw)
        finish(acc)

    @pl.when(jnp.logical_not(fits))
    def _():
        finish(full_width())


def _combine(x, sp_col, win, ys, mod, g_final, cap, tile, final):
    bsz, n, d = x.shape
    pack = 2 if cap // 2 >= LANES else 1
    tm = tile
    grid_spec = pltpu.PrefetchScalarGridSpec(
        num_scalar_prefetch=1,
        grid=(bsz, n // tm),
        in_specs=[pl.BlockSpec((None, tm, d), lambda b, i, *_: (b, i, 0)),
                  pl.BlockSpec((None, tm, N_EXPERTS), lambda b, i, *_: (b, i, 0)),
                  pl.BlockSpec((None, N_EXPERTS, cap, d), lambda b, i, *_: (b, 0, 0, 0)),
                  _mod_spec(mod),
                  pl.BlockSpec((1, d), lambda b, i, *_: (0, 0))],
        out_specs=pl.BlockSpec((None, tm, d), lambda b, i, *_: (b, i, 0)))
    return pl.pallas_call(
        functools.partial(_combine_kernel, cap=cap, final=final, pack=pack),
        grid_spec=grid_spec,
        out_shape=jax.ShapeDtypeStruct((bsz, n, d), F32),
        compiler_params=_params(("parallel", "parallel"), 48),
        name="ec_combine_final" if final else "ec_combine",
    )(win, x, sp_col, ys, mod[0], g_final.reshape(1, d))


def _moe(x1, xn2, aff, wg, wu, wd, layer, mod, g_final, final):
    bsz, n, _ = x1.shape
    cap = EC_CAPACITY_FACTOR * n // N_EXPERTS
    per_group = LANES // N_EXPERTS
    ng = -(-bsz // per_group)
    aff_e = aff[:, :, :N_EXPERTS]
    sets = jnp.pad(aff_e, ((0, ng * per_group - bsz), (0, 0), (0, 0)))
    sets = sets.reshape(ng, per_group, n, N_EXPERTS).transpose(0, 2, 1, 3).reshape(ng, n, LANES)
    tile = min(n, 512)
    selpos, win = _select(sets, cap, tile)
    ungroup = lambda a: a.reshape(ng, -1, per_group, N_EXPERTS).transpose(0, 2, 1, 3).reshape(ng * per_group, -1, N_EXPERTS)[:bsz]
    sp_col = ungroup(selpos)
    win = ungroup(win[:, :2 * (n // tile)]).reshape(bsz, -1)
    sp_row = sp_col.transpose(0, 2, 1).reshape(bsz, N_EXPERTS, 1, n)
    aff_row = aff_e.transpose(0, 2, 1).reshape(bsz, N_EXPERTS, 1, n)
    ys = _ffn(sp_row, aff_row, xn2, wg, wu, wd, layer, cap)
    return _combine(x1, sp_col, win, ys, mod, g_final, cap, tile, final)


def kernel(x, c, ctx, c_ctx, w_ada, b_ada, g_norm1, w_in, b_gates, g_hnorm, conv_w, conv_b, conv_ln_g, conv_ln_b,
           w_out, g_norm2, w_router, w_e_gate, w_e_up, w_e_down, g_final):
    bsz, t, d = x.shape
    depth = w_ada.shape[0]
    rows = t // GRID_W
    n_cond = 16
    cs = jnp.concatenate([c, c_ctx[None, :], jnp.zeros((n_cond - bsz - 1, d), F32)], axis=0)
    mods = _ada(cs, w_ada, b_ada).reshape(depth, n_cond, N_ADA, d)

    npair = N_HEADS // 2
    zero_states = (jnp.zeros((bsz, npair, 2, HEAD_DIM, HEAD_DIM), F32), jnp.zeros((bsz, npair, 8, HEAD_DIM), F32)) * 2
    h_ctx = ctx
    for l in range(depth):
        last = l == depth - 1
        lat, cmod = (mods, l, None), (mods, l, bsz)
        w_l = w_in[l]
        qkvg_cols = 3 * MLSTM_WIDTH + N_GATES * N_HEADS
        w_p = jnp.concatenate([w_l[:, :3 * MLSTM_WIDTH],
                               jnp.pad(w_l[:, 3 * MLSTM_WIDTH:qkvg_cols], ((0, 0), (0, LANES - N_GATES * N_HEADS))),
                               w_l[:, qkvg_cols:]], axis=1).astype(BF16)
        bg = jnp.pad(b_gates[l].reshape(1, N_HEADS * N_GATES), ((0, 0), (0, LANES - N_HEADS * N_GATES)))
        w_o = w_out[l].astype(BF16)
        w_r = jnp.pad(w_router[l], ((0, 0), (0, LANES - N_EXPERTS)))
        gh = g_hnorm[l].reshape(N_HEADS, HEAD_DIM, 1)

        def mlstm(xs, md, states, grid_rows, full):
            qkv, gates, extra = _inproj(xs, md, g_norm1[l], w_p, bg, full, grid_rows)
            return _mlstm(qkv, gates, states, gh, full, grid_rows), extra

        def sublayers(xs, md, states, grid_rows, final):
            (hr, hc, *st), (o, cacg, fr) = mlstm(xs, md, states, grid_rows, True)
            cv = _conv(cacg, conv_w[l], conv_b[l], conv_ln_g[l], conv_ln_b[l])
            fo = _fourier(fr)
            x1, xn2, aff = _outproj(xs, hr, hc, o, cv, fo, w_o, md, g_norm2[l], w_r)
            return _moe(x1, xn2, aff, w_e_gate, w_e_up, w_e_down, l, md, g_final, final), st

        if last:
            st, _ = mlstm(h_ctx, cmod, zero_states, None, False)
        else:
            h_ctx, st = sublayers(h_ctx, cmod, zero_states, None, False)
        x, _ = sublayers(x, lat, tuple(st), rows, last)
    return x
```

```python
import functools

import jax
import jax.numpy as jnp
import numpy as np
from jax import lax
from jax.experimental import pallas as pl
from jax.experimental.pallas import tpu as pltpu

F32 = jnp.float32
BF16 = jnp.bfloat16
I32 = jnp.int32

D_MODEL = 1024
GRID_W = 64
N_HEADS = 4
HEAD_DIM = 128
MLSTM_WIDTH = N_HEADS * HEAD_DIM
N_ROW_HEADS = N_HEADS // 2
CONV_CH = 256
CONV_K = 31
FOURIER_CH = 256
FOURIER_GROUP_CH = 64
N_GATES = 4
N_EXPERTS = 16
EC_CAPACITY_FACTOR = 2
N_ADA = 6
EPS = 1e-6

LANES = 128
MXU_DIM = 256
MLSTM_CHUNK = 128
MAX_CHUNKS = 16
CONV_HALO = 16
CONV_TAIL = 128
WIN_TILE = 256
MIB = 1024 * 1024

SEC_QKV = (0, 3 * MLSTM_WIDTH)
SEC_G = (SEC_QKV[1], SEC_QKV[1] + LANES)
SEC_O = (SEC_G[1], SEC_G[1] + MLSTM_WIDTH)
SEC_CACG = (SEC_O[1], SEC_O[1] + 2 * CONV_CH)
SEC_FR = (SEC_CACG[1], SEC_CACG[1] + FOURIER_CH)
IN_COLS_PAD = SEC_FR[1]

ST_NF = 2 * HEAD_DIM
ST_NB = ST_NF + 16
ST_ROWS = ST_NB + 16
UP_ROWS = 2 * HEAD_DIM + 16


def _dot(a, b):
    return jnp.dot(a, b, preferred_element_type=F32)


def _dot_nt(a, b):
    return lax.dot_general(a, b, (((1,), (1,)), ((), ())), preferred_element_type=F32)


def _split_bf16(a):
    hi = a.astype(BF16)
    lo = (a - hi.astype(F32)).astype(BF16)
    return hi, lo


def _dot3(a, w):
    ah, al = _split_bf16(a)
    wh, wl = _split_bf16(w)
    return _dot(ah, wh) + (_dot(al, wh) + _dot(ah, wl))


def _modnorm(x, g, sh, sc):
    ms = jnp.mean(x * x, axis=-1, keepdims=True)
    return x * lax.rsqrt(ms + EPS) * g * (1.0 + sc) + sh


def _rows(n_rows, width, *vals):
    rid = lax.broadcasted_iota(I32, (n_rows, width), 0)
    out = jnp.zeros((n_rows, width), F32)
    for i, v in enumerate(vals):
        out = jnp.where(rid == i, v, out)
    return out


def _mod_spec(mod):
    mods, layer, row = mod
    _, _, n_ada, d = mods.shape
    return pl.BlockSpec((None, None, n_ada, d), lambda b, i, *_: (layer, b if row is None else row, 0, 0))


def _params(sem, vmem_mib):
    return pltpu.CompilerParams(dimension_semantics=sem, vmem_limit_bytes=vmem_mib * MIB)


def _ada_kernel(c_ref, w_ref, b_ref, o_ref):
    c = c_ref[...]
    o_ref[...] = _dot3(c * jax.nn.sigmoid(c), w_ref[...]) + b_ref[...]


def _ada(cs, w_ada, b_ada):
    n_layers, d, n6 = w_ada.shape
    r = cs.shape[0]
    tn = 1536
    return pl.pallas_call(
        _ada_kernel,
        grid=(n_layers, n6 // tn),
        in_specs=[
            pl.BlockSpec((r, d), lambda l, j: (0, 0)),
            pl.BlockSpec((None, d, tn), lambda l, j: (l, 0, j)),
            pl.BlockSpec((None, 1, tn), lambda l, j: (l, 0, j)),
        ],
        out_specs=pl.BlockSpec((None, r, tn), lambda l, j: (l, 0, j)),
        out_shape=jax.ShapeDtypeStruct((n_layers, r, n6), F32),
        compiler_params=_params(("arbitrary", "arbitrary"), 40),
        name="ada_mod",
    )(cs, w_ada, b_ada.reshape(n_layers, 1, n6))


def _inproj_kernel(x_ref, mod_ref, g_ref, w_ref, bg_ref, *out_refs, full, colmajor):
    qkv_refs, gates_ref = out_refs[:6], out_refs[6]
    xn = _modnorm(x_ref[...], g_ref[...], mod_ref[0:1, :], mod_ref[1:2, :]).astype(BF16)
    tm = xn.shape[0]
    w = MLSTM_WIDTH
    for i in range(3):
        r = _dot(xn, w_ref[:, i * w:(i + 1) * w])
        if i == 1:
            r = r * HEAD_DIM ** -0.5
        row_ref, col_ref = qkv_refs[2 * i], qkv_refs[2 * i + 1]
        for hh in range(N_ROW_HEADS):
            row_ref[hh] = r[:, hh * HEAD_DIM:(hh + 1) * HEAD_DIM].astype(BF16)
            piece = r[:, (N_ROW_HEADS + hh) * HEAD_DIM:(N_ROW_HEADS + hh + 1) * HEAD_DIM]
            if colmajor:
                piece = jnp.swapaxes(piece.reshape(tm // GRID_W, GRID_W, HEAD_DIM), 0, 1)
            col_ref[hh] = piece.astype(BF16)
    gates_ref[...] = _dot(xn, w_ref[:, SEC_G[0]:SEC_G[1]]) + bg_ref[...]
    if full:
        o_ref, cacg_ref, fr_ref = out_refs[7:]
        o_ref[...] = _dot(xn, w_ref[:, SEC_O[0]:SEC_O[1]])
        cacg_ref[...] = _dot(xn, w_ref[:, SEC_CACG[0]:SEC_CACG[1]])
        fr_ref[...] = _dot(xn, w_ref[:, SEC_FR[0]:SEC_FR[1]]).astype(BF16)


def _inproj(x, mod, g, w, bg, full, rows):
    bsz, t, d = x.shape
    tm = min(t, 1024)
    nh = N_ROW_HEADS
    ncols = IN_COLS_PAD if full else SEC_G[1]
    tok = lambda n: pl.BlockSpec((None, tm, n), lambda b, i: (b, i, 0))
    row_shape = jax.ShapeDtypeStruct((bsz, nh, t, HEAD_DIM), BF16)
    row_spec = pl.BlockSpec((None, nh, tm, HEAD_DIM), lambda b, i: (b, 0, i, 0))
    if rows is None:
        col_shape, col_spec = row_shape, row_spec
    else:
        col_shape = jax.ShapeDtypeStruct((bsz, nh, GRID_W, rows, HEAD_DIM), BF16)
        col_spec = pl.BlockSpec((None, nh, GRID_W, tm // GRID_W, HEAD_DIM), lambda b, i: (b, 0, 0, i, 0))
    out_shape = [row_shape, col_shape] * 3 + [jax.ShapeDtypeStruct((bsz, t, LANES), F32)]
    out_specs = [row_spec, col_spec] * 3 + [tok(LANES)]
    if full:
        out_shape += [jax.ShapeDtypeStruct((bsz, t, MLSTM_WIDTH), F32),
                      jax.ShapeDtypeStruct((bsz, t, 2 * CONV_CH), F32),
                      jax.ShapeDtypeStruct((bsz, t, FOURIER_CH), BF16)]
        out_specs += [tok(MLSTM_WIDTH), tok(2 * CONV_CH), tok(FOURIER_CH)]
    outs = pl.pallas_call(
        functools.partial(_inproj_kernel, full=full, colmajor=rows is not None),
        grid=(bsz, t // tm),
        in_specs=[tok(d), _mod_spec(mod),
                  pl.BlockSpec((1, d), lambda b, i: (0, 0)),
                  pl.BlockSpec((d, ncols), lambda b, i: (0, 0)),
                  pl.BlockSpec((1, LANES), lambda b, i: (0, 0))],
        out_specs=out_specs,
        out_shape=out_shape,
        compiler_params=_params(("parallel", "parallel"), 56),
        name="inproj_full" if full else "inproj_qkvg",
    )(x, mod[0], g.reshape(1, d), w, bg)
    qkv = [a.reshape(bsz, nh, t, HEAD_DIM) for a in outs[:6]]
    return qkv, outs[6], outs[7:]


def _logsig(x):
    return jnp.minimum(x, 0.0) - jnp.log1p(jnp.exp(-jnp.abs(x)))


def _tri_masks(n):
    a0 = lax.broadcasted_iota(I32, (n, n), 0)
    a1 = lax.broadcasted_iota(I32, (n, n), 1)
    return a0 <= a1, a0 >= a1


def _split3(x):
    hi = x.astype(BF16).astype(F32)
    r1 = x - hi
    mid = r1.astype(BF16).astype(F32)
    return hi, mid, r1 - mid


def _mlstm_consts(tri_scr, ones_scr):
    lc = MLSTM_CHUNK
    u = lax.broadcasted_iota(I32, (lc, 2 * lc), 0)
    t2 = lax.broadcasted_iota(I32, (lc, 2 * lc), 1)
    tri_scr[...] = jnp.where(((t2 < lc) & (u <= t2)) | ((t2 >= lc) & (u >= t2 - lc)), 1.0, 0.0).astype(BF16)
    r = lax.broadcasted_iota(I32, (16, 2 * lc), 0)
    t16 = lax.broadcasted_iota(I32, (16, 2 * lc), 1)
    ones_scr[...] = jnp.where(((r < 3) & (t16 < lc)) | ((r >= 3) & (r < 6) & (t16 >= lc)), 1.0, 0.0).astype(BF16)


def _mlstm_gates(head, g_ref, tri_scr, rowg_scr, stat_scr, n_chunks, cm_rows):
    lc = MLSTM_CHUNK
    pick = jnp.where(lax.broadcasted_iota(I32, (16, LANES), 1) == N_GATES * head + lax.broadcasted_iota(I32, (16, LANES), 0),
                     1.0, 0.0).astype(BF16)
    chunk_id = lax.broadcasted_iota(I32, (MAX_CHUNKS, lc), 0)

    def gather(c, acc):
        if cm_rows is None:
            g = g_ref[pl.ds(pl.multiple_of(c * lc, lc), lc), :]
        else:
            cols = lc // cm_rows
            g = jnp.concatenate([g_ref[pl.ds(c * cols + j, cm_rows, stride=GRID_W), :] for j in range(cols)], axis=0)
        rows = sum(_dot_nt(pick, part.astype(BF16)) for part in _split3(g))
        return tuple(jnp.where(chunk_id == c, rows[i:i + 1], a) for i, a in enumerate(acc))

    zeros = jnp.zeros((MAX_CHUNKS, lc), F32)
    i_f, f_f, i_b, f_b = lax.fori_loop(0, n_chunks, gather, (zeros,) * N_GATES, unroll=min(4, n_chunks))
    lf_f, lf_b = _logsig(f_f), _logsig(f_b)
    b_f = sum(_dot(part.astype(BF16), tri_scr[:, 0:lc]) for part in _split3(lf_f))
    b_b = sum(_dot(part.astype(BF16), tri_scr[:, lc:]) for part in _split3(lf_b))
    bl_f = jnp.sum(lf_f, axis=1, keepdims=True)
    bl_b = jnp.sum(lf_b, axis=1, keepdims=True)
    ci_f, ci_b = i_f - b_f, i_b - b_b
    ml_f = jnp.max(ci_f + bl_f, axis=1, keepdims=True)
    ml_b = jnp.max(ci_b + bl_b, axis=1, keepdims=True)
    for r, val in enumerate((jnp.exp(ci_f + bl_f - ml_f), jnp.exp(ci_b + bl_b - ml_b), b_f, b_b, ci_f, ci_b)):
        rowg_scr[:, r, :] = val
    for r, val in enumerate((bl_f, bl_b, ml_f, ml_b)):
        stat_scr[:, r, :] = jnp.broadcast_to(val, (MAX_CHUNKS, HEAD_DIM))


def _mlstm_increment(c, k_ref, v_ref, rowg_scr, up_scr, vt_scr):
    lc = MLSTM_CHUNK
    s = pl.multiple_of(c * lc, lc)
    v_t = v_ref[pl.ds(s, lc), :].astype(F32).T
    rg = rowg_scr[c]
    w_f, w_b = rg[0:1], rg[1:2]
    lhs = jnp.concatenate([(v_t * w_f).astype(BF16), (v_t * w_b).astype(BF16),
                           _rows(16, lc, w_f, w_b).astype(BF16)], axis=0)
    up_scr[c] = _dot(lhs, k_ref[pl.ds(s, lc), :])
    if vt_scr is not None:
        vt_scr[c] = v_t.astype(BF16)


def _mlstm_state_step(cc, carry, up_scr, stat_scr, st_scr, mst_scr, fwd):
    cmat, n, m = carry
    off, row = (0, 0) if fwd else (HEAD_DIM, 1)
    if st_scr is not None:
        st_scr[cc, off:off + HEAD_DIM, :] = cmat.astype(BF16)
        nrow = ST_NF if fwd else ST_NB
        st_scr[cc, nrow:nrow + 16, :] = _rows(16, HEAD_DIM, n).astype(BF16)
        mst_scr[cc, row:row + 1, :] = jnp.broadcast_to(m, (1, HEAD_DIM))
    st = stat_scr[cc]
    bl, ml = st[row:row + 1, 0:1], st[2 + row:3 + row, 0:1]
    m_new = jnp.maximum(bl + m, ml)
    keep = jnp.exp(bl + m - m_new)
    gain = jnp.exp(ml - m_new)
    cmat = keep * cmat + gain * up_scr[cc, off:off + HEAD_DIM, :]
    n = keep * n + gain * up_scr[cc, 2 * HEAD_DIM + row:2 * HEAD_DIM + row + 1, :]
    return cmat, n, m_new


def _mlstm_outputs(c, chains, ones_scr):
    lc = MLSTM_CHUNK
    s = pl.multiple_of(c * lc, lc)
    le, ge = _tri_masks(lc)
    qs = [ch[0][pl.ds(s, lc), :] for ch in chains]
    qk_ts = [_dot_nt(ch[1][pl.ds(s, lc), :], q) for ch, q in zip(chains, qs)]
    g_ts = [_dot_nt(ch[5][c], q) for ch, q in zip(chains, qs)]
    rgs = [ch[4][c] for ch in chains]
    cis = [lax.dot_general(_rows(16, lc, *_split3(rg[4:5]), *_split3(rg[5:6])).astype(BF16), ones_scr[...],
                           (((0,), (0,)), ((), ())), preferred_element_type=F32) for rg in rgs]

    def direction(qk_t, valid, b_row, ci, m_st, qn):
        dlog = jnp.where(valid, b_row + ci, -jnp.inf)
        inter = b_row + m_st
        m_t = jnp.maximum(inter, jnp.max(dlog, axis=0, keepdims=True))
        s_t = qk_t * jnp.exp(dlog - m_t)
        w_inter = jnp.exp(inter - m_t)
        den = w_inter * qn + jnp.sum(s_t, axis=0, keepdims=True)
        r = 1.0 / jnp.maximum(jnp.abs(den), jnp.exp(-m_t))
        return r * s_t, r * w_inter

    mixed = []
    for ch, qk_t, g_t, rg, ci in zip(chains, qk_ts, g_ts, rgs, cis):
        ms = ch[6][c]
        sf, cf = direction(qk_t, le, rg[2:3], ci[:, 0:lc], ms[0:1, 0:1], g_t[ST_NF:ST_NF + 1])
        sb, cb = direction(qk_t, ge, rg[3:4], ci[:, lc:], ms[1:2, 0:1], g_t[ST_NB:ST_NB + 1])
        mixed.append(((sf + sb).astype(BF16), cf * g_t[0:HEAD_DIM] + cb * g_t[HEAD_DIM:2 * HEAD_DIM]))
    h_ts = [inter_part + _dot(ch[7][c], s_mix) for ch, (s_mix, inter_part) in zip(chains, mixed)]
    for ch, h_t in zip(chains, h_ts):
        gh_ref, h_ref, cm_rows = ch[2], ch[3], ch[8]
        h = (h_t * lax.rsqrt(jnp.mean(h_t * h_t, axis=0, keepdims=True) + EPS) * gh_ref[...]).T
        if cm_rows is None:
            h_ref[pl.ds(s, lc), :] = h
        else:
            cols = lc // cm_rows
            for j in range(cols):
                h_ref[pl.ds(c * cols + j, cm_rows, stride=GRID_W), :] = h[j * cm_rows:(j + 1) * cm_rows]


def _mlstm_kernel(*refs, n_chunks, with_outputs, cm_rows):
    n_slot_in = 6
    g_ref = refs[0]
    slots_in = [refs[1:1 + n_slot_in], refs[1 + n_slot_in:1 + 2 * n_slot_in]]
    rest = refs[1 + 2 * n_slot_in:]
    if with_outputs:
        h_refs, rest = rest[:2], rest[2:]
    state_out, scratch = rest[:4], rest[4:]
    tri_scr, ones_scr, scratch = scratch[0], scratch[1], scratch[2:]
    n_scr = 6 if with_outputs else 3
    slots_scr = [scratch[:n_scr], scratch[n_scr:2 * n_scr]]
    slot_rows = [None, cm_rows]
    _mlstm_consts(tri_scr, ones_scr)
    for slot, scr in enumerate(slots_scr):
        _mlstm_gates(pl.program_id(1) + slot * N_ROW_HEADS, g_ref, tri_scr, scr[0], scr[1], n_chunks, slot_rows[slot])

    def increments(c, _):
        for (q_ref, k_ref, v_ref, c0_ref, n0_ref, gh_ref), scr in zip(slots_in, slots_scr):
            _mlstm_increment(c, k_ref, v_ref, scr[0], scr[2], scr[5] if with_outputs else None)
        return 0

    lax.fori_loop(0, n_chunks, increments, 0, unroll=min(4, n_chunks))

    def states(i, carry):
        out = []
        for slot, scr in enumerate(slots_scr):
            st_scr, mst_scr = (scr[3], scr[4]) if with_outputs else (None, None)
            out.append(_mlstm_state_step(i, carry[2 * slot], scr[2], scr[1], st_scr, mst_scr, True))
            out.append(_mlstm_state_step(n_chunks - 1 - i, carry[2 * slot + 1], scr[2], scr[1], st_scr, mst_scr, False))
        return tuple(out)

    init = []
    for (q_ref, k_ref, v_ref, c0_ref, n0_ref, gh_ref) in slots_in:
        init.append((c0_ref[0], n0_ref[0:1, :], n0_ref[2:3, 0:1]))
        init.append((c0_ref[1], n0_ref[1:2, :], n0_ref[3:4, 0:1]))
    final = lax.fori_loop(0, n_chunks, states, tuple(init))
    for slot in range(2):
        c_out, n_out = state_out[2 * slot], state_out[2 * slot + 1]
        (cf, nf, mf), (cb, nb, mb) = final[2 * slot], final[2 * slot + 1]
        c_out[0] = cf
        c_out[1] = cb
        n_out[...] = _rows(8, HEAD_DIM, nf, nb, mf, mb)

    if with_outputs:
        chains = [(q_ref, k_ref, gh_ref, h_refs[slot], scr[0], scr[3], scr[4], scr[5], slot_rows[slot])
                  for slot, ((q_ref, k_ref, v_ref, c0_ref, n0_ref, gh_ref), scr) in enumerate(zip(slots_in, slots_scr))]

        def outputs(c, _):
            _mlstm_outputs(c, chains, ones_scr)
            return 0

        lax.fori_loop(0, n_chunks, outputs, 0, unroll=min(4, n_chunks))


def _mlstm(qkv, gates, states, gh, with_outputs, cm_rows):
    bsz, npair, t, dh = qkv[0].shape
    nc = t // MLSTM_CHUNK
    lc = MLSTM_CHUNK
    seq = pl.BlockSpec((None, None, t, dh), lambda b, p: (b, p, 0, 0))
    cst = pl.BlockSpec((None, None, 2, dh, dh), lambda b, p: (b, p, 0, 0, 0))
    nst = pl.BlockSpec((None, None, 8, dh), lambda b, p: (b, p, 0, 0))

    def slot_specs(head_off):
        return [seq, seq, seq, cst, nst, pl.BlockSpec((None, dh, 1), lambda b, p: (p + head_off, 0, 0))]

    st_shapes = [jax.ShapeDtypeStruct((bsz, npair, 2, dh, dh), F32), jax.ShapeDtypeStruct((bsz, npair, 8, dh), F32)] * 2
    out_shape, out_specs = st_shapes, [cst, nst, cst, nst]
    assert nc <= MAX_CHUNKS
    slot_scratch = [pltpu.VMEM((MAX_CHUNKS, 8, lc), F32), pltpu.VMEM((MAX_CHUNKS, 8, dh), F32),
                    pltpu.VMEM((nc, UP_ROWS, dh), F32)]
    if with_outputs:
        h_shape = jax.ShapeDtypeStruct((bsz, t, npair * dh), F32)
        h_spec = pl.BlockSpec((None, t, dh), lambda b, p: (b, 0, p))
        out_shape = [h_shape, h_shape] + out_shape
        out_specs = [h_spec, h_spec] + out_specs
        slot_scratch += [pltpu.VMEM((nc, ST_ROWS, dh), BF16), pltpu.VMEM((nc, 8, dh), F32), pltpu.VMEM((nc, dh, lc), BF16)]
    q_r, q_c, k_r, k_c, v_r, v_c = qkv
    c_r, n_r, c_c, n_c = states
    return pl.pallas_call(
        functools.partial(_mlstm_kernel, n_chunks=nc, with_outputs=with_outputs, cm_rows=cm_rows),
        grid=(bsz, npair),
        in_specs=[pl.BlockSpec((None, t, LANES), lambda b, p: (b, 0, 0))] + slot_specs(0) + slot_specs(N_ROW_HEADS),
        out_specs=out_specs,
        out_shape=out_shape,
        scratch_shapes=[pltpu.VMEM((lc, 2 * lc), BF16), pltpu.VMEM((16, 2 * lc), BF16)] + slot_scratch * 2,
        compiler_params=_params(("parallel", "parallel"), 48),
        name="mlstm_full" if with_outputs else "mlstm_states",
    )(gates, q_r, k_r, v_r, c_r, n_r, gh, q_c, k_c, v_c, c_c, n_c, gh)


def _conv_kernel(cacg_ref, w_ref, cb_ref, lg_ref, lb_ref, o_ref, ysh, yconv, *, t):
    rt = 64
    ysh[0, 0:CONV_HALO, :] = jnp.zeros((CONV_HALO, CONV_CH), F32)
    ysh[0, CONV_HALO + t:t + CONV_TAIL, :] = jnp.zeros((CONV_TAIL - CONV_HALO, CONV_CH), F32)

    def fill(r, _):
        s = pl.multiple_of(r * rt, rt)
        a = cacg_ref[pl.ds(s, rt), 0:CONV_CH]
        g = cacg_ref[pl.ds(s, rt), CONV_CH:2 * CONV_CH]
        ysh[0, pl.ds(CONV_HALO + s, rt), :] = a * jax.nn.sigmoid(g)
        return 0

    lax.fori_loop(0, t // rt, fill, 0)

    def shift(r, _):
        s = pl.multiple_of(r * rt, rt)
        win = ysh[0, pl.ds(s, rt + 8), :]
        for res in range(1, 8):
            ysh[res, pl.ds(s, rt), :] = pltpu.roll(win, rt + 8 - res, axis=0)[0:rt, :]
        return 0

    lax.fori_loop(0, t // rt + 1, shift, 0)

    def tile(r, _):
        s = pl.multiple_of(r * rt, rt)
        acc = jnp.zeros((rt, CONV_CH), F32)
        for kk in range(CONV_K):
            off = kk + CONV_HALO - CONV_K // 2
            rows = ysh[off % 8, pl.ds(pl.multiple_of(s + 8 * (off // 8), 8), rt), :]
            acc = acc + w_ref[kk:kk + 1, :] * rows
        yconv[pl.ds(s, rt), :] = acc + cb_ref[...]
        return 0

    lax.fori_loop(0, t // rt, tile, 0)

    def norm(r, _):
        s = pl.multiple_of(r * rt, rt)
        y = yconv[pl.ds(s, rt), :]
        mu = jnp.mean(y, axis=-1, keepdims=True)
        yc = y - mu
        var = jnp.mean(yc * yc, axis=-1, keepdims=True)
        z = yc * lax.rsqrt(var + EPS) * lg_ref[...] + lb_ref[...]
        o_ref[pl.ds(s, rt), :] = (z * jax.nn.sigmoid(z)).astype(BF16)
        return 0

    lax.fori_loop(0, t // rt, norm, 0, unroll=4)


def _conv(cacg, conv_w, conv_b, ln_g, ln_b):
    bsz, t, _ = cacg.shape
    row = lambda a: a.reshape(1, CONV_CH)
    vec = pl.BlockSpec((1, CONV_CH), lambda b: (0, 0))
    return pl.pallas_call(
        functools.partial(_conv_kernel, t=t),
        grid=(bsz,),
        in_specs=[pl.BlockSpec((None, t, 2 * CONV_CH), lambda b: (b, 0, 0)),
                  pl.BlockSpec((CONV_K + 1, CONV_CH), lambda b: (0, 0)), vec, vec, vec],
        out_specs=pl.BlockSpec((None, t, CONV_CH), lambda b: (b, 0, 0)),
        out_shape=jax.ShapeDtypeStruct((bsz, t, CONV_CH), BF16),
        scratch_shapes=[pltpu.VMEM((8, t + CONV_TAIL, CONV_CH), F32), pltpu.VMEM((t, CONV_CH), F32)],
        compiler_params=_params(("parallel",), 40),
        name="conformer_conv",
    )(cacg, jnp.pad(conv_w, ((0, 1), (0, 0))), row(conv_b), row(ln_g), row(ln_b))


@functools.lru_cache(maxsize=None)
def _dft_tables(t):
    j = np.arange(t, dtype=np.int64)
    ang = 2.0 * np.pi * ((j[:, None] * j[None, :]) % t).astype(np.float64) / t
    wt = np.concatenate([np.cos(ang), -np.sin(ang)], axis=1).astype(np.float32)
    c = np.arange(FOURIER_CH, dtype=np.int64)
    grp, idx = c // FOURIER_GROUP_CH, c % FOURIER_GROUP_CH
    same = grp[:, None] == grp[None, :]
    angc = 2.0 * np.pi * ((idx[:, None] * idx[None, :]) % FOURIER_GROUP_CH).astype(np.float64) / FOURIER_GROUP_CH
    cc = np.where(same, np.cos(angc), 0.0).astype(np.float32)
    sc = np.where(same, np.sin(angc), 0.0).astype(np.float32)
    return wt, cc, sc


def _fourier_kernel(z_ref, cc_ref, sc_ref, wt_ref, o_ref, zcs, *, t, scale):
    bsz, _, ch = z_ref.shape

    @pl.when(pl.program_id(0) == 0)
    def _():
        for b in range(bsz):
            zcs[0:t, b * ch:(b + 1) * ch] = _dot(z_ref[b], cc_ref[...]).astype(BF16)
            zcs[t:2 * t, b * ch:(b + 1) * ch] = _dot(z_ref[b], sc_ref[...]).astype(BF16)

    r = _dot(wt_ref[...], zcs[...]) * scale
    for b in range(bsz):
        o_ref[b] = r[:, b * ch:(b + 1) * ch].astype(BF16)


def _fourier(z):
    bsz, t, ch = z.shape
    wt, cc, sc = _dft_tables(t)
    tm = min(t, 512)
    mat = pl.BlockSpec((ch, ch), lambda i: (0, 0))
    return pl.pallas_call(
        functools.partial(_fourier_kernel, t=t, scale=float((t * FOURIER_GROUP_CH) ** -0.5)),
        grid=(t // tm,),
        in_specs=[pl.BlockSpec((bsz, t, ch), lambda i: (0, 0, 0)), mat, mat,
                  pl.BlockSpec((tm, 2 * t), lambda i: (i, 0))],
        out_specs=pl.BlockSpec((bsz, tm, ch), lambda i: (0, i, 0)),
        out_shape=jax.ShapeDtypeStruct((bsz, t, ch), BF16),
        scratch_shapes=[pltpu.VMEM((2 * t, bsz * ch), BF16)],
        compiler_params=_params(("arbitrary",), 56),
        name="fourier_mix",
    )(z, jnp.asarray(cc).astype(BF16), jnp.asarray(sc).astype(BF16), jnp.asarray(wt).astype(BF16))


def _outproj_kernel(x_ref, hr_ref, hc_ref, o_ref, cv_ref, fo_ref, w_ref, mod_ref, g2_ref, wr_ref,
                    x1_ref, xn_ref, aff_ref):
    half = MLSTM_WIDTH // 2
    a, b = MLSTM_WIDTH, MLSTM_WIDTH + CONV_CH
    og = jax.nn.sigmoid(o_ref[...])
    y = (_dot((hr_ref[...] * og[:, 0:half]).astype(BF16), w_ref[0:half, :])
         + _dot((hc_ref[...] * og[:, half:a]).astype(BF16), w_ref[half:a, :])
         + _dot(cv_ref[...], w_ref[a:b, :]) + _dot(fo_ref[...], w_ref[b:, :]))
    x1 = x_ref[...] + mod_ref[2:3, :] * y
    x1_ref[...] = x1
    xn = _modnorm(x1, g2_ref[...], mod_ref[3:4, :], mod_ref[4:5, :])
    xn_ref[...] = xn.astype(BF16)
    tm = xn.shape[0]
    r = _dot(jnp.concatenate(_split_bf16(xn), axis=0), jnp.concatenate(_split_bf16(wr_ref[...]), axis=1))
    logits = (r[:tm, :LANES] + r[:tm, LANES:]) + (r[tm:, :LANES] + r[tm:, LANES:])
    lane = lax.broadcasted_iota(I32, logits.shape, 1)
    logits = jnp.where(lane < N_EXPERTS, logits, -jnp.inf)
    e = jnp.exp(logits - jnp.max(logits, axis=-1, keepdims=True))
    aff_ref[...] = e / jnp.sum(e, axis=-1, keepdims=True)


def _outproj(x, hr, hc, o, cv, fo, w_out, mod, g2, wr):
    bsz, t, d = x.shape
    tm = min(t, 512)
    tok = lambda n: pl.BlockSpec((None, tm, n), lambda b, i: (b, i, 0))
    return pl.pallas_call(
        _outproj_kernel,
        grid=(bsz, t // tm),
        in_specs=[tok(d), tok(MLSTM_WIDTH // 2), tok(MLSTM_WIDTH // 2), tok(MLSTM_WIDTH), tok(CONV_CH), tok(FOURIER_CH),
                  pl.BlockSpec((d, d), lambda b, i: (0, 0)), _mod_spec(mod),
                  pl.BlockSpec((1, d), lambda b, i: (0, 0)),
                  pl.BlockSpec((d, LANES), lambda b, i: (0, 0))],
        out_specs=[tok(d), tok(d), tok(LANES)],
        out_shape=[jax.ShapeDtypeStruct((bsz, t, d), F32), jax.ShapeDtypeStruct((bsz, t, d), BF16),
                   jax.ShapeDtypeStruct((bsz, t, LANES), F32)],
        compiler_params=_params(("parallel", "parallel"), 40),
        name="outproj_router",
    )(x, hr, hc, o, cv, fo, w_out, mod[0], g2.reshape(1, d), wr)


def _select_kernel(a_ref, sp_ref, win_ref, *, cap, tile):
    a = a_ref[...]
    n = a.shape[0]
    blk = min(n, MXU_DIM)

    def count(mask):
        return jnp.sum(jnp.where(mask, 1.0, 0.0), axis=0, keepdims=True)

    def search(i, thr):
        cand = thr | jnp.left_shift(jnp.int32(1), 30 - i)
        return jnp.where(count(a >= lax.bitcast_convert_type(cand, F32)) >= cap, cand, thr)

    thr = lax.bitcast_convert_type(lax.fori_loop(0, 31, search, jnp.zeros((1, LANES), I32)), F32)
    gt = a > thr
    eq = a == thr
    before = jnp.where(lax.broadcasted_iota(I32, (blk, blk), 1) < lax.broadcasted_iota(I32, (blk, blk), 0), 1.0, 0.0).astype(BF16)

    def excl_cumsum(mask):
        x = jnp.where(mask, 1.0, 0.0)
        run = jnp.zeros((1, LANES), F32)
        parts = []
        for j in range(n // blk):
            xb = x[j * blk:(j + 1) * blk]
            parts.append(_dot(before, xb.astype(BF16)) + run)
            run = run + jnp.sum(xb, axis=0, keepdims=True)
        return jnp.concatenate(parts, axis=0)

    sel = gt | (eq & (excl_cumsum(eq) < cap - count(gt)))
    pos = excl_cumsum(sel).astype(I32)
    sp_ref[...] = jnp.where(sel, pos, -1)
    rid = lax.broadcasted_iota(I32, win_ref.shape, 0)
    win = jnp.zeros(win_ref.shape, I32)
    for ti in range(n // tile):
        t_sel, t_pos = sel[ti * tile:(ti + 1) * tile], pos[ti * tile:(ti + 1) * tile]
        win = jnp.where(rid == 2 * ti, jnp.min(jnp.where(t_sel, t_pos, cap), axis=0, keepdims=True), win)
        win = jnp.where(rid == 2 * ti + 1, jnp.max(jnp.where(t_sel, t_pos + 1, 0), axis=0, keepdims=True), win)
    win_ref[...] = win


def _select(aff_sets, cap, tile):
    ng, n, _ = aff_sets.shape
    win_rows = max(8, 2 * (n // tile))
    blk = pl.BlockSpec((None, n, LANES), lambda g: (g, 0, 0))
    return pl.pallas_call(
        functools.partial(_select_kernel, cap=cap, tile=tile),
        grid=(ng,),
        in_specs=[blk],
        out_specs=[blk, pl.BlockSpec((None, win_rows, LANES), lambda g: (g, 0, 0))],
        out_shape=[jax.ShapeDtypeStruct((ng, n, LANES), I32), jax.ShapeDtypeStruct((ng, win_rows, LANES), I32)],
        compiler_params=_params(("parallel",), 32),
        name="ec_select",
    )(aff_sets)


def _slot_window(win_ref, b, tile_idx, e, cap, width):
    lo = win_ref[b, 2 * tile_idx * N_EXPERTS + e]
    hi = win_ref[b, (2 * tile_idx + 1) * N_EXPERTS + e]
    st = jnp.minimum(lax.shift_left(lax.shift_right_logical(lo, 4), 4), cap - width)
    return pl.multiple_of(st, 16), hi - st <= width


def _ffn_kernel(win_ref, sp_ref, ar_ref, xn_ref, wg_ref, wu_ref, wd_ref, ys_ref, wgu_b, wd_b, xs_scr, gate_scr, *, cap, windowed):
    g, j, k = pl.program_id(0), pl.program_id(1), pl.program_id(2)
    rows = wg_ref.shape[0]

    def cast_slice():
        rt = min(256, rows)
        dst = lax.rem(g, 2)
        for piece in range(rows // rt):
            src_rows = pl.ds(piece * rt, rt)
            dst_rows = pl.ds(pl.multiple_of(j * rows + piece * rt, rt), rt)
            wgu_b[dst, k, 0, dst_rows, :] = wg_ref[src_rows, :].astype(BF16)
            wgu_b[dst, k, 1, dst_rows, :] = wu_ref[src_rows, :].astype(BF16)
            wd_b[dst, k, dst_rows, :] = wd_ref[src_rows, :].astype(BF16)

    @pl.when(g == 0)
    def _():
        cast_slice()

    @pl.when(g > 0)
    def _():
        cast_slice()
        src = lax.rem(g + 1, 2)
        nb, _, n = sp_ref.shape

        def full_gather(bi):
            hit = lax.broadcasted_iota(I32, (cap, n), 0) == sp_ref[bi]
            xs_scr[bi * cap:(bi + 1) * cap, :] = _dot(jnp.where(hit, 1.0, 0.0).astype(BF16), xn_ref[bi])
            gate_scr[bi * cap:(bi + 1) * cap, :] = jnp.broadcast_to(
                jnp.sum(jnp.where(hit, ar_ref[bi], 0.0), axis=1, keepdims=True), (cap, LANES))

        if not windowed:
            for bi in range(nb):
                full_gather(bi)
        else:
            assert nb == 1
            width = cap // 2
            expert = (g - 1) * pl.num_programs(2) + k
            wins = [_slot_window(win_ref, j, t, expert, cap, width) for t in range(n // WIN_TILE)]
            fits = functools.reduce(jnp.logical_and, [ok for _, ok in wins])

            @pl.when(fits)
            def _():
                xs_scr[...] = jnp.zeros(xs_scr.shape, F32)
                gate_scr[...] = jnp.zeros(gate_scr.shape, F32)
                for t, (st, _) in enumerate(wins):
                    tok = slice(t * WIN_TILE, (t + 1) * WIN_TILE)
                    hit = lax.broadcasted_iota(I32, (width, WIN_TILE), 0) + st == sp_ref[0, :, tok]
                    xs_scr[pl.ds(st, width), :] += _dot(jnp.where(hit, 1.0, 0.0).astype(BF16), xn_ref[0, tok, :])
                    gate_scr[pl.ds(st, width), :] += jnp.broadcast_to(
                        jnp.sum(jnp.where(hit, ar_ref[0, :, tok], 0.0), axis=1, keepdims=True), (width, LANES))

            @pl.when(jnp.logical_not(fits))
            def _():
                full_gather(0)

        xs = xs_scr[...].astype(BF16)
        h1 = _dot(xs, wgu_b[src, k, 0])
        hid = (h1 * jax.nn.sigmoid(h1) * _dot(xs, wgu_b[src, k, 1])).astype(BF16)
        ys = (_dot(hid, wd_b[src, k]) * gate_scr[:, 0:1]).astype(BF16)
        for bi in range(nb):
            ys_ref[bi] = ys[bi * cap:(bi + 1) * cap]


def _ffn(win, sp_row, aff_row, xn, wg, wu, wd, layer, cap):
    bsz, n, d = xn.shape
    _, ne, _, hid = wg.shape
    assert d == hid
    nb = min(bsz, max(1, MXU_DIM // cap))
    nj = bsz // nb
    gsz = 2 if nj > 1 else 1
    rows = d // nj
    windowed = nb == 1 and n > WIN_TILE and cap // 2 >= LANES
    expert = lambda g, k: jnp.maximum((g - 1) * gsz + k, 0)
    batch = lambda g, j: jnp.where(g == 0, 0, j)
    row = pl.BlockSpec((nb, None, 1, n), lambda g, j, k, *_: (batch(g, j), expert(g, k), 0, 0))
    wspec = pl.BlockSpec((None, None, rows, hid), lambda g, j, k, *_: (layer, jnp.minimum(g * gsz + k, ne - 1), j, 0))
    grid_spec = pltpu.PrefetchScalarGridSpec(
        num_scalar_prefetch=1,
        grid=(ne // gsz + 1, nj, gsz),
        in_specs=[row, row, pl.BlockSpec((nb, n, d), lambda g, j, k, *_: (batch(g, j), 0, 0)), wspec, wspec, wspec],
        out_specs=pl.BlockSpec((nb, None, cap, d), lambda g, j, k, *_: (batch(g, j), expert(g, k), 0, 0)),
        scratch_shapes=[pltpu.VMEM((2, gsz, 2, d, hid), BF16), pltpu.VMEM((2, gsz, hid, d), BF16),
                        pltpu.VMEM((nb * cap, d), F32), pltpu.VMEM((nb * cap, LANES), F32)])
    return pl.pallas_call(
        functools.partial(_ffn_kernel, cap=cap, windowed=windowed),
        grid_spec=grid_spec,
        out_shape=jax.ShapeDtypeStruct((bsz, ne, cap, d), BF16),
        compiler_params=_params(("arbitrary", "arbitrary", "arbitrary"), 56),
        name="ec_ffn",
    )(win, sp_row, aff_row, xn, wg, wu, wd)


def _combine_kernel(win_ref, x_ref, spc_ref, ys_ref, mod_ref, gf_ref, o_ref, *, cap, final, pack):
    b, i = pl.program_id(0), pl.program_id(1)
    tm = spc_ref.shape[0]
    sub = tm if pack == 1 else WIN_TILE

    def finish(rows, acc):
        out = x_ref[rows, :] + mod_ref[5:6, :] * acc
        if final:
            out = out * lax.rsqrt(jnp.mean(out * out, axis=-1, keepdims=True) + EPS) * gf_ref[...]
        o_ref[rows, :] = out

    def full_width(spc):
        slot = lax.broadcasted_iota(I32, (sub, cap), 1)
        acc = jnp.zeros((sub, x_ref.shape[1]), F32)
        for e in range(N_EXPERTS):
            hit = jnp.where(spc[:, e:e + 1] == slot, 1.0, 0.0).astype(BF16)
            acc = acc + _dot(hit, ys_ref[e])
        return acc

    for t in range(tm // sub):
        rows = slice(t * sub, (t + 1) * sub)
        spc = spc_ref[rows, :]
        if pack == 1:
            finish(rows, full_width(spc))
            continue
        w = cap // pack
        wins = [_slot_window(win_ref, b, i * (tm // sub) + t, e, cap, w) for e in range(N_EXPERTS)]
        fits = functools.reduce(jnp.logical_and, [ok for _, ok in wins])

        @pl.when(fits)
        def _():
            lane = lax.broadcasted_iota(I32, (sub, cap), 1)
            lane_row = lax.broadcasted_iota(I32, (1, cap), 1)
            acc = jnp.zeros((sub, x_ref.shape[1]), F32)
            for p in range(N_EXPERTS // pack):
                es = list(range(p * pack, (p + 1) * pack))
                col = spc[:, es[0]:es[0] + 1]
                off = jnp.zeros((1, cap), I32) + wins[es[0]][0]
                for q, e in enumerate(es[1:], 1):
                    col = jnp.where(lane >= q * w, spc[:, e:e + 1], col)
                    off = jnp.where(lane_row >= q * w, wins[e][0] - q * w, off)
                hit = jnp.where(col - off == lane, 1.0, 0.0).astype(BF16)
                ysw = jnp.concatenate([ys_ref[e, pl.ds(wins[e][0], w), :] for e in es], axis=0)
                acc = acc + _dot(hit, ysw)
            finish(rows, acc)

        @pl.when(jnp.logical_not(fits))
        def _():
            finish(rows, full_width(spc))


def _combine(x, sp_col, win, ys, mod, g_final, cap, final):
    bsz, n, d = x.shape
    pack = 4 if n > WIN_TILE and cap // 4 >= LANES // 2 else 1
    tm = min(n, 512)
    grid_spec = pltpu.PrefetchScalarGridSpec(
        num_scalar_prefetch=1,
        grid=(bsz, n // tm),
        in_specs=[pl.BlockSpec((None, tm, d), lambda b, i, *_: (b, i, 0)),
                  pl.BlockSpec((None, tm, N_EXPERTS), lambda b, i, *_: (b, i, 0)),
                  pl.BlockSpec((None, N_EXPERTS, cap, d), lambda b, i, *_: (b, 0, 0, 0)),
                  _mod_spec(mod),
                  pl.BlockSpec((1, d), lambda b, i, *_: (0, 0))],
        out_specs=pl.BlockSpec((None, tm, d), lambda b, i, *_: (b, i, 0)))
    return pl.pallas_call(
        functools.partial(_combine_kernel, cap=cap, final=final, pack=pack),
        grid_spec=grid_spec,
        out_shape=jax.ShapeDtypeStruct((bsz, n, d), F32),
        compiler_params=_params(("parallel", "parallel"), 48),
        name="ec_combine_final" if final else "ec_combine",
    )(win, x, sp_col, ys, mod[0], g_final.reshape(1, d))


def _moe(x1, xn2, aff, wg, wu, wd, layer, mod, g_final, final):
    bsz, n, _ = x1.shape
    cap = EC_CAPACITY_FACTOR * n // N_EXPERTS
    per_group = LANES // N_EXPERTS
    ng = -(-bsz // per_group)
    aff_e = aff[:, :, :N_EXPERTS]
    sets = jnp.pad(aff_e, ((0, ng * per_group - bsz), (0, 0), (0, 0)))
    sets = sets.reshape(ng, per_group, n, N_EXPERTS).transpose(0, 2, 1, 3).reshape(ng, n, LANES)
    tile = min(n, WIN_TILE)
    selpos, win = _select(sets, cap, tile)
    ungroup = lambda a: a.reshape(ng, -1, per_group, N_EXPERTS).transpose(0, 2, 1, 3).reshape(ng * per_group, -1, N_EXPERTS)[:bsz]
    sp_col = ungroup(selpos)
    win = ungroup(win[:, :2 * (n // tile)]).reshape(bsz, -1)
    sp_row = sp_col.transpose(0, 2, 1).reshape(bsz, N_EXPERTS, 1, n)
    aff_row = aff_e.transpose(0, 2, 1).reshape(bsz, N_EXPERTS, 1, n)
    ys = _ffn(win, sp_row, aff_row, xn2, wg, wu, wd, layer, cap)
    return _combine(x1, sp_col, win, ys, mod, g_final, cap, final)


def kernel(x, c, ctx, c_ctx, w_ada, b_ada, g_norm1, w_in, b_gates, g_hnorm, conv_w, conv_b, conv_ln_g, conv_ln_b,
           w_out, g_norm2, w_router, w_e_gate, w_e_up, w_e_down, g_final):
    bsz, t, d = x.shape
    depth = w_ada.shape[0]
    rows = t // GRID_W
    n_cond = 16
    cs = jnp.concatenate([c, c_ctx[None, :], jnp.zeros((n_cond - bsz - 1, d), F32)], axis=0)
    mods = _ada(cs, w_ada, b_ada).reshape(depth, n_cond, N_ADA, d)

    npair = N_HEADS // 2
    zero_states = (jnp.zeros((bsz, npair, 2, HEAD_DIM, HEAD_DIM), F32), jnp.zeros((bsz, npair, 8, HEAD_DIM), F32)) * 2
    h_ctx = ctx
    for l in range(depth):
        last = l == depth - 1
        lat, cmod = (mods, l, None), (mods, l, bsz)
        w_l = w_in[l]
        qkvg_cols = 3 * MLSTM_WIDTH + N_GATES * N_HEADS
        w_p = jnp.concatenate([w_l[:, :3 * MLSTM_WIDTH],
                               jnp.pad(w_l[:, 3 * MLSTM_WIDTH:qkvg_cols], ((0, 0), (0, LANES - N_GATES * N_HEADS))),
                               w_l[:, qkvg_cols:]], axis=1).astype(BF16)
        bg = jnp.pad(b_gates[l].reshape(1, N_HEADS * N_GATES), ((0, 0), (0, LANES - N_HEADS * N_GATES)))
        w_o = w_out[l].astype(BF16)
        w_r = jnp.pad(w_router[l], ((0, 0), (0, LANES - N_EXPERTS)))
        gh = g_hnorm[l].reshape(N_HEADS, HEAD_DIM, 1)

        def mlstm(xs, md, states, grid_rows, full):
            qkv, gates, extra = _inproj(xs, md, g_norm1[l], w_p, bg, full, grid_rows)
            return _mlstm(qkv, gates, states, gh, full, grid_rows), extra

        def sublayers(xs, md, states, grid_rows, final):
            (hr, hc, *st), (o, cacg, fr) = mlstm(xs, md, states, grid_rows, True)
            cv = _conv(cacg, conv_w[l], conv_b[l], conv_ln_g[l], conv_ln_b[l])
            fo = _fourier(fr)
            x1, xn2, aff = _outproj(xs, hr, hc, o, cv, fo, w_o, md, g_norm2[l], w_r)
            return _moe(x1, xn2, aff, w_e_gate, w_e_up, w_e_down, l, md, g_final, final), st

        if last:
            st, _ = mlstm(h_ctx, cmod, zero_states, None, False)
        else:
            h_ctx, st = sublayers(h_ctx, cmod, zero_states, None, False)
        x, _ = sublayers(x, lat, tuple(st), rows, last)
    return x
```

```python
import functools

import jax
import jax.numpy as jnp
import numpy as np
from jax import lax
from jax.experimental import pallas as pl
from jax.experimental.pallas import tpu as pltpu

F32 = jnp.float32
BF16 = jnp.bfloat16
I32 = jnp.int32

D_MODEL = 1024
GRID_W = 64
N_HEADS = 4
HEAD_DIM = 128
MLSTM_WIDTH = N_HEADS * HEAD_DIM
N_ROW_HEADS = N_HEADS // 2
CONV_CH = 256
CONV_K = 31
FOURIER_CH = 256
FOURIER_GROUP_CH = 64
N_GATES = 4
N_EXPERTS = 16
EC_CAPACITY_FACTOR = 2
N_ADA = 6
EPS = 1e-6

LANES = 128
MXU_DIM = 256
MLSTM_CHUNK = 128
MAX_CHUNKS = 16
CONV_HALO = 16
CONV_TAIL = 128
WIN_TILE = 256
MIB = 1024 * 1024

SEC_QKV = (0, 3 * MLSTM_WIDTH)
SEC_G = (SEC_QKV[1], SEC_QKV[1] + LANES)
SEC_O = (SEC_G[1], SEC_G[1] + MLSTM_WIDTH)
SEC_CACG = (SEC_O[1], SEC_O[1] + 2 * CONV_CH)
SEC_FR = (SEC_CACG[1], SEC_CACG[1] + FOURIER_CH)
IN_COLS_PAD = SEC_FR[1]

ST_NF = 2 * HEAD_DIM
ST_NB = ST_NF + 16
ST_ROWS = ST_NB + 16
UP_ROWS = 2 * HEAD_DIM + 16


def _dot(a, b):
    return jnp.dot(a, b, preferred_element_type=F32)


def _dot_nt(a, b):
    return lax.dot_general(a, b, (((1,), (1,)), ((), ())), preferred_element_type=F32)


def _split_bf16(a):
    hi = a.astype(BF16)
    lo = (a - hi.astype(F32)).astype(BF16)
    return hi, lo


def _dot3(a, w):
    ah, al = _split_bf16(a)
    wh, wl = _split_bf16(w)
    return _dot(ah, wh) + (_dot(al, wh) + _dot(ah, wl))


def _modnorm(x, g, sh, sc):
    ms = jnp.mean(x * x, axis=-1, keepdims=True)
    return x * lax.rsqrt(ms + EPS) * g * (1.0 + sc) + sh


def _rows(n_rows, width, *vals):
    rid = lax.broadcasted_iota(I32, (n_rows, width), 0)
    out = jnp.zeros((n_rows, width), F32)
    for i, v in enumerate(vals):
        out = jnp.where(rid == i, v, out)
    return out


def _mod_spec(mod):
    mods, layer, row = mod
    _, _, n_ada, d = mods.shape
    return pl.BlockSpec((None, None, n_ada, d), lambda b, i, *_: (layer, b if row is None else row, 0, 0))


def _params(sem, vmem_mib):
    return pltpu.CompilerParams(dimension_semantics=sem, vmem_limit_bytes=vmem_mib * MIB)


def _ada_kernel(c_ref, w_ref, b_ref, o_ref):
    c = c_ref[...]
    o_ref[...] = _dot3(c * jax.nn.sigmoid(c), w_ref[...]) + b_ref[...]


def _ada(cs, w_ada, b_ada):
    n_layers, d, n6 = w_ada.shape
    r = cs.shape[0]
    tn = 1536
    return pl.pallas_call(
        _ada_kernel,
        grid=(n_layers, n6 // tn),
        in_specs=[
            pl.BlockSpec((r, d), lambda l, j: (0, 0)),
            pl.BlockSpec((None, d, tn), lambda l, j: (l, 0, j)),
            pl.BlockSpec((None, 1, tn), lambda l, j: (l, 0, j)),
        ],
        out_specs=pl.BlockSpec((None, r, tn), lambda l, j: (l, 0, j)),
        out_shape=jax.ShapeDtypeStruct((n_layers, r, n6), F32),
        compiler_params=_params(("arbitrary", "arbitrary"), 40),
        name="ada_mod",
    )(cs, w_ada, b_ada.reshape(n_layers, 1, n6))


def _inproj_kernel(x_ref, mod_ref, g_ref, w_ref, bg_ref, *out_refs, full, colmajor):
    qkv_refs, gates_ref = out_refs[:6], out_refs[6]
    xn = _modnorm(x_ref[...], g_ref[...], mod_ref[0:1, :], mod_ref[1:2, :]).astype(BF16)
    tm = xn.shape[0]
    w = MLSTM_WIDTH
    for i in range(3):
        r = _dot(xn, w_ref[:, i * w:(i + 1) * w])
        if i == 1:
            r = r * HEAD_DIM ** -0.5
        row_ref, col_ref = qkv_refs[2 * i], qkv_refs[2 * i + 1]
        for hh in range(N_ROW_HEADS):
            row_ref[hh] = r[:, hh * HEAD_DIM:(hh + 1) * HEAD_DIM].astype(BF16)
            piece = r[:, (N_ROW_HEADS + hh) * HEAD_DIM:(N_ROW_HEADS + hh + 1) * HEAD_DIM]
            if colmajor:
                piece = jnp.swapaxes(piece.reshape(tm // GRID_W, GRID_W, HEAD_DIM), 0, 1)
            col_ref[hh] = piece.astype(BF16)
    gates_ref[...] = _dot(xn, w_ref[:, SEC_G[0]:SEC_G[1]]) + bg_ref[...]
    if full:
        o_ref, cacg_ref, fr_ref = out_refs[7:]
        o_ref[...] = _dot(xn, w_ref[:, SEC_O[0]:SEC_O[1]])
        cacg_ref[...] = _dot(xn, w_ref[:, SEC_CACG[0]:SEC_CACG[1]])
        fr_ref[...] = _dot(xn, w_ref[:, SEC_FR[0]:SEC_FR[1]]).astype(BF16)


def _inproj(x, mod, g, w, bg, full, rows):
    bsz, t, d = x.shape
    tm = min(t, 1024)
    nh = N_ROW_HEADS
    ncols = IN_COLS_PAD if full else SEC_G[1]
    tok = lambda n: pl.BlockSpec((None, tm, n), lambda b, i: (b, i, 0))
    row_shape = jax.ShapeDtypeStruct((bsz, nh, t, HEAD_DIM), BF16)
    row_spec = pl.BlockSpec((None, nh, tm, HEAD_DIM), lambda b, i: (b, 0, i, 0))
    if rows is None:
        col_shape, col_spec = row_shape, row_spec
    else:
        col_shape = jax.ShapeDtypeStruct((bsz, nh, GRID_W, rows, HEAD_DIM), BF16)
        col_spec = pl.BlockSpec((None, nh, GRID_W, tm // GRID_W, HEAD_DIM), lambda b, i: (b, 0, 0, i, 0))
    out_shape = [row_shape, col_shape] * 3 + [jax.ShapeDtypeStruct((bsz, t, LANES), F32)]
    out_specs = [row_spec, col_spec] * 3 + [tok(LANES)]
    if full:
        out_shape += [jax.ShapeDtypeStruct((bsz, t, MLSTM_WIDTH), F32),
                      jax.ShapeDtypeStruct((bsz, t, 2 * CONV_CH), F32),
                      jax.ShapeDtypeStruct((bsz, t, FOURIER_CH), BF16)]
        out_specs += [tok(MLSTM_WIDTH), tok(2 * CONV_CH), tok(FOURIER_CH)]
    outs = pl.pallas_call(
        functools.partial(_inproj_kernel, full=full, colmajor=rows is not None),
        grid=(bsz, t // tm),
        in_specs=[tok(d), _mod_spec(mod),
                  pl.BlockSpec((1, d), lambda b, i: (0, 0)),
                  pl.BlockSpec((d, ncols), lambda b, i: (0, 0)),
                  pl.BlockSpec((1, LANES), lambda b, i: (0, 0))],
        out_specs=out_specs,
        out_shape=out_shape,
        compiler_params=_params(("parallel", "parallel"), 56),
        name="inproj_full" if full else "inproj_qkvg",
    )(x, mod[0], g.reshape(1, d), w, bg)
    qkv = [a.reshape(bsz, nh, t, HEAD_DIM) for a in outs[:6]]
    return qkv, outs[6], outs[7:]


def _logsig(x):
    return jnp.minimum(x, 0.0) - jnp.log1p(jnp.exp(-jnp.abs(x)))


def _tri_masks(n):
    a0 = lax.broadcasted_iota(I32, (n, n), 0)
    a1 = lax.broadcasted_iota(I32, (n, n), 1)
    return a0 <= a1, a0 >= a1


def _split3(x):
    hi = x.astype(BF16).astype(F32)
    r1 = x - hi
    mid = r1.astype(BF16).astype(F32)
    return hi, mid, r1 - mid


def _mlstm_consts(tri_scr, ones_scr):
    lc = MLSTM_CHUNK
    u = lax.broadcasted_iota(I32, (lc, 2 * lc), 0)
    t2 = lax.broadcasted_iota(I32, (lc, 2 * lc), 1)
    tri_scr[...] = jnp.where(((t2 < lc) & (u <= t2)) | ((t2 >= lc) & (u >= t2 - lc)), 1.0, 0.0).astype(BF16)
    r = lax.broadcasted_iota(I32, (16, 2 * lc), 0)
    t16 = lax.broadcasted_iota(I32, (16, 2 * lc), 1)
    ones_scr[...] = jnp.where(((r < 3) & (t16 < lc)) | ((r >= 3) & (r < 6) & (t16 >= lc)), 1.0, 0.0).astype(BF16)


def _mlstm_gates(head, g_ref, tri_scr, rowg_scr, stat_scr, n_chunks, cm_rows):
    lc = MLSTM_CHUNK
    pick = jnp.where(lax.broadcasted_iota(I32, (16, LANES), 1) == N_GATES * head + lax.broadcasted_iota(I32, (16, LANES), 0),
                     1.0, 0.0).astype(BF16)
    chunk_id = lax.broadcasted_iota(I32, (MAX_CHUNKS, lc), 0)

    def gather(c, acc):
        if cm_rows is None:
            g = g_ref[pl.ds(pl.multiple_of(c * lc, lc), lc), :]
        else:
            cols = lc // cm_rows
            g = jnp.concatenate([g_ref[pl.ds(c * cols + j, cm_rows, stride=GRID_W), :] for j in range(cols)], axis=0)
        rows = sum(_dot_nt(pick, part.astype(BF16)) for part in _split3(g))
        return tuple(jnp.where(chunk_id == c, rows[i:i + 1], a) for i, a in enumerate(acc))

    zeros = jnp.zeros((MAX_CHUNKS, lc), F32)
    i_f, f_f, i_b, f_b = lax.fori_loop(0, n_chunks, gather, (zeros,) * N_GATES, unroll=min(4, n_chunks))
    lf_f, lf_b = _logsig(f_f), _logsig(f_b)
    b_f = sum(_dot(part.astype(BF16), tri_scr[:, 0:lc]) for part in _split3(lf_f))
    b_b = sum(_dot(part.astype(BF16), tri_scr[:, lc:]) for part in _split3(lf_b))
    bl_f = jnp.sum(lf_f, axis=1, keepdims=True)
    bl_b = jnp.sum(lf_b, axis=1, keepdims=True)
    ci_f, ci_b = i_f - b_f, i_b - b_b
    ml_f = jnp.max(ci_f + bl_f, axis=1, keepdims=True)
    ml_b = jnp.max(ci_b + bl_b, axis=1, keepdims=True)
    for r, val in enumerate((jnp.exp(ci_f + bl_f - ml_f), jnp.exp(ci_b + bl_b - ml_b), b_f, b_b, ci_f, ci_b)):
        rowg_scr[:, r, :] = val
    for r, val in enumerate((bl_f, bl_b, ml_f, ml_b)):
        stat_scr[:, r, :] = jnp.broadcast_to(val, (MAX_CHUNKS, HEAD_DIM))


def _mlstm_increment(c, k_ref, v_ref, rowg_scr, up_scr, vt_scr):
    lc = MLSTM_CHUNK
    s = pl.multiple_of(c * lc, lc)
    v_t = v_ref[pl.ds(s, lc), :].astype(F32).T
    rg = rowg_scr[c]
    w_f, w_b = rg[0:1], rg[1:2]
    lhs = jnp.concatenate([(v_t * w_f).astype(BF16), (v_t * w_b).astype(BF16),
                           _rows(16, lc, w_f, w_b).astype(BF16)], axis=0)
    up_scr[c] = _dot(lhs, k_ref[pl.ds(s, lc), :])
    if vt_scr is not None:
        vt_scr[c] = v_t.astype(BF16)


def _mlstm_state_step(cc, carry, up_scr, stat_scr, st_scr, mst_scr, fwd):
    cmat, n, m = carry
    off, row = (0, 0) if fwd else (HEAD_DIM, 1)
    if st_scr is not None:
        st_scr[cc, off:off + HEAD_DIM, :] = cmat.astype(BF16)
        nrow = ST_NF if fwd else ST_NB
        st_scr[cc, nrow:nrow + 16, :] = _rows(16, HEAD_DIM, n).astype(BF16)
        mst_scr[cc, row:row + 1, :] = jnp.broadcast_to(m, (1, HEAD_DIM))
    st = stat_scr[cc]
    bl, ml = st[row:row + 1, 0:1], st[2 + row:3 + row, 0:1]
    m_new = jnp.maximum(bl + m, ml)
    keep = jnp.exp(bl + m - m_new)
    gain = jnp.exp(ml - m_new)
    cmat = keep * cmat + gain * up_scr[cc, off:off + HEAD_DIM, :]
    n = keep * n + gain * up_scr[cc, 2 * HEAD_DIM + row:2 * HEAD_DIM + row + 1, :]
    return cmat, n, m_new


def _mlstm_outputs(c, chains, ones_scr):
    lc = MLSTM_CHUNK
    s = pl.multiple_of(c * lc, lc)
    le, ge = _tri_masks(lc)
    qs = [ch[0][pl.ds(s, lc), :] for ch in chains]
    qk_ts = [_dot_nt(ch[1][pl.ds(s, lc), :], q) for ch, q in zip(chains, qs)]
    g_ts = [_dot_nt(ch[5][c], q) for ch, q in zip(chains, qs)]
    rgs = [ch[4][c] for ch in chains]
    cis = [lax.dot_general(_rows(16, lc, *_split3(rg[4:5]), *_split3(rg[5:6])).astype(BF16), ones_scr[...],
                           (((0,), (0,)), ((), ())), preferred_element_type=F32) for rg in rgs]

    def direction(qk_t, valid, b_row, ci, m_st, qn):
        dlog = jnp.where(valid, b_row + ci, -jnp.inf)
        inter = b_row + m_st
        m_t = jnp.maximum(inter, jnp.max(dlog, axis=0, keepdims=True))
        s_t = qk_t * jnp.exp(dlog - m_t)
        w_inter = jnp.exp(inter - m_t)
        den = w_inter * qn + jnp.sum(s_t, axis=0, keepdims=True)
        r = 1.0 / jnp.maximum(jnp.abs(den), jnp.exp(-m_t))
        return r * s_t, r * w_inter

    mixed = []
    for ch, qk_t, g_t, rg, ci in zip(chains, qk_ts, g_ts, rgs, cis):
        ms = ch[6][c]
        sf, cf = direction(qk_t, le, rg[2:3], ci[:, 0:lc], ms[0:1, 0:1], g_t[ST_NF:ST_NF + 1])
        sb, cb = direction(qk_t, ge, rg[3:4], ci[:, lc:], ms[1:2, 0:1], g_t[ST_NB:ST_NB + 1])
        mixed.append(((sf + sb).astype(BF16), cf * g_t[0:HEAD_DIM] + cb * g_t[HEAD_DIM:2 * HEAD_DIM]))
    h_ts = [inter_part + _dot(ch[7][c], s_mix) for ch, (s_mix, inter_part) in zip(chains, mixed)]
    for ch, h_t in zip(chains, h_ts):
        gh_ref, h_ref, cm_rows = ch[2], ch[3], ch[8]
        h = (h_t * lax.rsqrt(jnp.mean(h_t * h_t, axis=0, keepdims=True) + EPS) * gh_ref[...]).T
        if cm_rows is None:
            h_ref[pl.ds(s, lc), :] = h
        else:
            cols = lc // cm_rows
            for j in range(cols):
                h_ref[pl.ds(c * cols + j, cm_rows, stride=GRID_W), :] = h[j * cm_rows:(j + 1) * cm_rows]


def _mlstm_kernel(*refs, n_chunks, with_outputs, cm_rows):
    n_slot_in = 6
    g_ref = refs[0]
    slots_in = [refs[1:1 + n_slot_in], refs[1 + n_slot_in:1 + 2 * n_slot_in]]
    rest = refs[1 + 2 * n_slot_in:]
    if with_outputs:
        h_refs, rest = rest[:2], rest[2:]
    state_out, scratch = rest[:4], rest[4:]
    tri_scr, ones_scr, scratch = scratch[0], scratch[1], scratch[2:]
    n_scr = 6 if with_outputs else 3
    slots_scr = [scratch[:n_scr], scratch[n_scr:2 * n_scr]]
    slot_rows = [None, cm_rows]
    _mlstm_consts(tri_scr, ones_scr)
    for slot, scr in enumerate(slots_scr):
        _mlstm_gates(pl.program_id(1) + slot * N_ROW_HEADS, g_ref, tri_scr, scr[0], scr[1], n_chunks, slot_rows[slot])

    def increments(c, _):
        for (q_ref, k_ref, v_ref, c0_ref, n0_ref, gh_ref), scr in zip(slots_in, slots_scr):
            _mlstm_increment(c, k_ref, v_ref, scr[0], scr[2], scr[5] if with_outputs else None)
        return 0

    lax.fori_loop(0, n_chunks, increments, 0, unroll=min(4, n_chunks))

    def states(i, carry):
        out = []
        for slot, scr in enumerate(slots_scr):
            st_scr, mst_scr = (scr[3], scr[4]) if with_outputs else (None, None)
            out.append(_mlstm_state_step(i, carry[2 * slot], scr[2], scr[1], st_scr, mst_scr, True))
            out.append(_mlstm_state_step(n_chunks - 1 - i, carry[2 * slot + 1], scr[2], scr[1], st_scr, mst_scr, False))
        return tuple(out)

    init = []
    for (q_ref, k_ref, v_ref, c0_ref, n0_ref, gh_ref) in slots_in:
        init.append((c0_ref[0], n0_ref[0:1, :], n0_ref[2:3, 0:1]))
        init.append((c0_ref[1], n0_ref[1:2, :], n0_ref[3:4, 0:1]))
    final = lax.fori_loop(0, n_chunks, states, tuple(init))
    for slot in range(2):
        c_out, n_out = state_out[2 * slot], state_out[2 * slot + 1]
        (cf, nf, mf), (cb, nb, mb) = final[2 * slot], final[2 * slot + 1]
        c_out[0] = cf
        c_out[1] = cb
        n_out[...] = _rows(8, HEAD_DIM, nf, nb, mf, mb)

    if with_outputs:
        chains = [(q_ref, k_ref, gh_ref, h_refs[slot], scr[0], scr[3], scr[4], scr[5], slot_rows[slot])
                  for slot, ((q_ref, k_ref, v_ref, c0_ref, n0_ref, gh_ref), scr) in enumerate(zip(slots_in, slots_scr))]

        def outputs(c, _):
            _mlstm_outputs(c, chains, ones_scr)
            return 0

        lax.fori_loop(0, n_chunks, outputs, 0, unroll=min(4, n_chunks))


def _mlstm(qkv, gates, states, gh, with_outputs, cm_rows):
    bsz, npair, t, dh = qkv[0].shape
    nc = t // MLSTM_CHUNK
    lc = MLSTM_CHUNK
    seq = pl.BlockSpec((None, None, t, dh), lambda b, p: (b, p, 0, 0))
    cst = pl.BlockSpec((None, None, 2, dh, dh), lambda b, p: (b, p, 0, 0, 0))
    nst = pl.BlockSpec((None, None, 8, dh), lambda b, p: (b, p, 0, 0))

    def slot_specs(head_off):
        return [seq, seq, seq, cst, nst, pl.BlockSpec((None, dh, 1), lambda b, p: (p + head_off, 0, 0))]

    st_shapes = [jax.ShapeDtypeStruct((bsz, npair, 2, dh, dh), F32), jax.ShapeDtypeStruct((bsz, npair, 8, dh), F32)] * 2
    out_shape, out_specs = st_shapes, [cst, nst, cst, nst]
    assert nc <= MAX_CHUNKS
    slot_scratch = [pltpu.VMEM((MAX_CHUNKS, 8, lc), F32), pltpu.VMEM((MAX_CHUNKS, 8, dh), F32),
                    pltpu.VMEM((nc, UP_ROWS, dh), F32)]
    if with_outputs:
        h_shape = jax.ShapeDtypeStruct((bsz, t, npair * dh), F32)
        h_spec = pl.BlockSpec((None, t, dh), lambda b, p: (b, 0, p))
        out_shape = [h_shape, h_shape] + out_shape
        out_specs = [h_spec, h_spec] + out_specs
        slot_scratch += [pltpu.VMEM((nc, ST_ROWS, dh), BF16), pltpu.VMEM((nc, 8, dh), F32), pltpu.VMEM((nc, dh, lc), BF16)]
    q_r, q_c, k_r, k_c, v_r, v_c = qkv
    c_r, n_r, c_c, n_c = states
    return pl.pallas_call(
        functools.partial(_mlstm_kernel, n_chunks=nc, with_outputs=with_outputs, cm_rows=cm_rows),
        grid=(bsz, npair),
        in_specs=[pl.BlockSpec((None, t, LANES), lambda b, p: (b, 0, 0))] + slot_specs(0) + slot_specs(N_ROW_HEADS),
        out_specs=out_specs,
        out_shape=out_shape,
        scratch_shapes=[pltpu.VMEM((lc, 2 * lc), BF16), pltpu.VMEM((16, 2 * lc), BF16)] + slot_scratch * 2,
        compiler_params=_params(("parallel", "parallel"), 48),
        name="mlstm_full" if with_outputs else "mlstm_states",
    )(gates, q_r, k_r, v_r, c_r, n_r, gh, q_c, k_c, v_c, c_c, n_c, gh)


def _conv_kernel(cacg_ref, w_ref, cb_ref, lg_ref, lb_ref, o_ref, ysh, yconv, *, t):
    rt = 64
    ysh[0, 0:CONV_HALO, :] = jnp.zeros((CONV_HALO, CONV_CH), F32)
    ysh[0, CONV_HALO + t:t + CONV_TAIL, :] = jnp.zeros((CONV_TAIL - CONV_HALO, CONV_CH), F32)

    def fill(r, _):
        s = pl.multiple_of(r * rt, rt)
        a = cacg_ref[pl.ds(s, rt), 0:CONV_CH]
        g = cacg_ref[pl.ds(s, rt), CONV_CH:2 * CONV_CH]
        ysh[0, pl.ds(CONV_HALO + s, rt), :] = a * jax.nn.sigmoid(g)
        return 0

    lax.fori_loop(0, t // rt, fill, 0)

    def shift(r, _):
        s = pl.multiple_of(r * rt, rt)
        win = ysh[0, pl.ds(s, rt + 8), :]
        for res in range(1, 8):
            ysh[res, pl.ds(s, rt), :] = pltpu.roll(win, rt + 8 - res, axis=0)[0:rt, :]
        return 0

    lax.fori_loop(0, t // rt + 1, shift, 0)

    def tile(r, _):
        s = pl.multiple_of(r * rt, rt)
        acc = jnp.zeros((rt, CONV_CH), F32)
        for kk in range(CONV_K):
            off = kk + CONV_HALO - CONV_K // 2
            rows = ysh[off % 8, pl.ds(pl.multiple_of(s + 8 * (off // 8), 8), rt), :]
            acc = acc + w_ref[kk:kk + 1, :] * rows
        yconv[pl.ds(s, rt), :] = acc + cb_ref[...]
        return 0

    lax.fori_loop(0, t // rt, tile, 0)

    def norm(r, _):
        s = pl.multiple_of(r * rt, rt)
        y = yconv[pl.ds(s, rt), :]
        mu = jnp.mean(y, axis=-1, keepdims=True)
        yc = y - mu
        var = jnp.mean(yc * yc, axis=-1, keepdims=True)
        z = yc * lax.rsqrt(var + EPS) * lg_ref[...] + lb_ref[...]
        o_ref[pl.ds(s, rt), :] = (z * jax.nn.sigmoid(z)).astype(BF16)
        return 0

    lax.fori_loop(0, t // rt, norm, 0, unroll=4)


def _conv(cacg, conv_w, conv_b, ln_g, ln_b):
    bsz, t, _ = cacg.shape
    row = lambda a: a.reshape(1, CONV_CH)
    vec = pl.BlockSpec((1, CONV_CH), lambda b: (0, 0))
    return pl.pallas_call(
        functools.partial(_conv_kernel, t=t),
        grid=(bsz,),
        in_specs=[pl.BlockSpec((None, t, 2 * CONV_CH), lambda b: (b, 0, 0)),
                  pl.BlockSpec((CONV_K + 1, CONV_CH), lambda b: (0, 0)), vec, vec, vec],
        out_specs=pl.BlockSpec((None, t, CONV_CH), lambda b: (b, 0, 0)),
        out_shape=jax.ShapeDtypeStruct((bsz, t, CONV_CH), BF16),
        scratch_shapes=[pltpu.VMEM((8, t + CONV_TAIL, CONV_CH), F32), pltpu.VMEM((t, CONV_CH), F32)],
        compiler_params=_params(("parallel",), 40),
        name="conformer_conv",
    )(cacg, jnp.pad(conv_w, ((0, 1), (0, 0))), row(conv_b), row(ln_g), row(ln_b))


@functools.lru_cache(maxsize=None)
def _dft_tables(t):
    j = np.arange(t, dtype=np.int64)
    ang = 2.0 * np.pi * ((j[:, None] * j[None, :]) % t).astype(np.float64) / t
    wt = np.concatenate([np.cos(ang), -np.sin(ang)], axis=1).astype(np.float32)
    c = np.arange(FOURIER_CH, dtype=np.int64)
    grp, idx = c // FOURIER_GROUP_CH, c % FOURIER_GROUP_CH
    same = grp[:, None] == grp[None, :]
    angc = 2.0 * np.pi * ((idx[:, None] * idx[None, :]) % FOURIER_GROUP_CH).astype(np.float64) / FOURIER_GROUP_CH
    cc = np.where(same, np.cos(angc), 0.0).astype(np.float32)
    sc = np.where(same, np.sin(angc), 0.0).astype(np.float32)
    return wt, cc, sc


def _fourier_kernel(z_ref, cc_ref, sc_ref, wt_ref, o_ref, zcs, *, t, scale):
    bsz, _, ch = z_ref.shape

    @pl.when(pl.program_id(0) == 0)
    def _():
        for b in range(bsz):
            zcs[0:t, b * ch:(b + 1) * ch] = _dot(z_ref[b], cc_ref[...]).astype(BF16)
            zcs[t:2 * t, b * ch:(b + 1) * ch] = _dot(z_ref[b], sc_ref[...]).astype(BF16)

    r = _dot(wt_ref[...], zcs[...]) * scale
    for b in range(bsz):
        o_ref[b] = r[:, b * ch:(b + 1) * ch].astype(BF16)


def _fourier(z):
    bsz, t, ch = z.shape
    wt, cc, sc = _dft_tables(t)
    tm = min(t, 512)
    mat = pl.BlockSpec((ch, ch), lambda i: (0, 0))
    return pl.pallas_call(
        functools.partial(_fourier_kernel, t=t, scale=float((t * FOURIER_GROUP_CH) ** -0.5)),
        grid=(t // tm,),
        in_specs=[pl.BlockSpec((bsz, t, ch), lambda i: (0, 0, 0)), mat, mat,
                  pl.BlockSpec((tm, 2 * t), lambda i: (i, 0))],
        out_specs=pl.BlockSpec((bsz, tm, ch), lambda i: (0, i, 0)),
        out_shape=jax.ShapeDtypeStruct((bsz, t, ch), BF16),
        scratch_shapes=[pltpu.VMEM((2 * t, bsz * ch), BF16)],
        compiler_params=_params(("arbitrary",), 56),
        name="fourier_mix",
    )(z, jnp.asarray(cc).astype(BF16), jnp.asarray(sc).astype(BF16), jnp.asarray(wt).astype(BF16))


def _outproj_kernel(x_ref, hr_ref, hc_ref, o_ref, cv_ref, fo_ref, w_ref, mod_ref, g2_ref, wr_ref,
                    x1_ref, xn_ref, aff_ref):
    half = MLSTM_WIDTH // 2
    a, b = MLSTM_WIDTH, MLSTM_WIDTH + CONV_CH
    og = jax.nn.sigmoid(o_ref[...])
    y = (_dot((hr_ref[...] * og[:, 0:half]).astype(BF16), w_ref[0:half, :])
         + _dot((hc_ref[...] * og[:, half:a]).astype(BF16), w_ref[half:a, :])
         + _dot(cv_ref[...], w_ref[a:b, :]) + _dot(fo_ref[...], w_ref[b:, :]))
    x1 = x_ref[...] + mod_ref[2:3, :] * y
    x1_ref[...] = x1
    xn = _modnorm(x1, g2_ref[...], mod_ref[3:4, :], mod_ref[4:5, :])
    xn_ref[...] = xn.astype(BF16)
    tm = xn.shape[0]
    r = _dot(jnp.concatenate(_split_bf16(xn), axis=0), jnp.concatenate(_split_bf16(wr_ref[...]), axis=1))
    logits = (r[:tm, :LANES] + r[:tm, LANES:]) + (r[tm:, :LANES] + r[tm:, LANES:])
    lane = lax.broadcasted_iota(I32, logits.shape, 1)
    logits = jnp.where(lane < N_EXPERTS, logits, -jnp.inf)
    e = jnp.exp(logits - jnp.max(logits, axis=-1, keepdims=True))
    aff_ref[...] = e / jnp.sum(e, axis=-1, keepdims=True)


def _outproj(x, hr, hc, o, cv, fo, w_out, mod, g2, wr):
    bsz, t, d = x.shape
    tm = min(t, 512)
    tok = lambda n: pl.BlockSpec((None, tm, n), lambda b, i: (b, i, 0))
    return pl.pallas_call(
        _outproj_kernel,
        grid=(bsz, t // tm),
        in_specs=[tok(d), tok(MLSTM_WIDTH // 2), tok(MLSTM_WIDTH // 2), tok(MLSTM_WIDTH), tok(CONV_CH), tok(FOURIER_CH),
                  pl.BlockSpec((d, d), lambda b, i: (0, 0)), _mod_spec(mod),
                  pl.BlockSpec((1, d), lambda b, i: (0, 0)),
                  pl.BlockSpec((d, LANES), lambda b, i: (0, 0))],
        out_specs=[tok(d), tok(d), tok(LANES)],
        out_shape=[jax.ShapeDtypeStruct((bsz, t, d), F32), jax.ShapeDtypeStruct((bsz, t, d), BF16),
                   jax.ShapeDtypeStruct((bsz, t, LANES), F32)],
        compiler_params=_params(("parallel", "parallel"), 40),
        name="outproj_router",
    )(x, hr, hc, o, cv, fo, w_out, mod[0], g2.reshape(1, d), wr)


def _select_kernel(a_ref, sp_ref, win_ref, *, cap, tile):
    a = a_ref[...]
    n = a.shape[0]
    blk = min(n, MXU_DIM)

    def count(mask):
        return jnp.sum(jnp.where(mask, 1.0, 0.0), axis=0, keepdims=True)

    def search(i, thr):
        cand = thr | jnp.left_shift(jnp.int32(1), 30 - i)
        return jnp.where(count(a >= lax.bitcast_convert_type(cand, F32)) >= cap, cand, thr)

    thr = lax.bitcast_convert_type(lax.fori_loop(0, 31, search, jnp.zeros((1, LANES), I32)), F32)
    gt = a > thr
    eq = a == thr
    before = jnp.where(lax.broadcasted_iota(I32, (blk, blk), 1) < lax.broadcasted_iota(I32, (blk, blk), 0), 1.0, 0.0).astype(BF16)

    def excl_cumsum(mask):
        x = jnp.where(mask, 1.0, 0.0)
        run = jnp.zeros((1, LANES), F32)
        parts = []
        for j in range(n // blk):
            xb = x[j * blk:(j + 1) * blk]
            parts.append(_dot(before, xb.astype(BF16)) + run)
            run = run + jnp.sum(xb, axis=0, keepdims=True)
        return jnp.concatenate(parts, axis=0)

    sel = gt | (eq & (excl_cumsum(eq) < cap - count(gt)))
    pos = excl_cumsum(sel).astype(I32)
    sp_ref[...] = jnp.where(sel, pos, -1)
    rid = lax.broadcasted_iota(I32, win_ref.shape, 0)
    win = jnp.zeros(win_ref.shape, I32)
    for ti in range(n // tile):
        t_sel, t_pos = sel[ti * tile:(ti + 1) * tile], pos[ti * tile:(ti + 1) * tile]
        win = jnp.where(rid == 2 * ti, jnp.min(jnp.where(t_sel, t_pos, cap), axis=0, keepdims=True), win)
        win = jnp.where(rid == 2 * ti + 1, jnp.max(jnp.where(t_sel, t_pos + 1, 0), axis=0, keepdims=True), win)
    win_ref[...] = win


def _select(aff_sets, cap, tile):
    ng, n, _ = aff_sets.shape
    win_rows = max(8, 2 * (n // tile))
    blk = pl.BlockSpec((None, n, LANES), lambda g: (g, 0, 0))
    return pl.pallas_call(
        functools.partial(_select_kernel, cap=cap, tile=tile),
        grid=(ng,),
        in_specs=[blk],
        out_specs=[blk, pl.BlockSpec((None, win_rows, LANES), lambda g: (g, 0, 0))],
        out_shape=[jax.ShapeDtypeStruct((ng, n, LANES), I32), jax.ShapeDtypeStruct((ng, win_rows, LANES), I32)],
        compiler_params=_params(("parallel",), 32),
        name="ec_select",
    )(aff_sets)


def _slot_window(win_ref, b, tile_idx, e, cap, width):
    lo = win_ref[b, 2 * tile_idx * N_EXPERTS + e]
    hi = win_ref[b, (2 * tile_idx + 1) * N_EXPERTS + e]
    st = jnp.minimum(lax.shift_left(lax.shift_right_logical(lo, 4), 4), cap - width)
    return pl.multiple_of(st, 16), hi - st <= width


def _ffn_kernel(win_ref, sp_ref, ar_ref, xn_ref, wg_ref, wu_ref, wd_ref, ys_ref, wgu_b, wd_b, xs_scr, gate_scr, *, cap, windowed):
    g, j = pl.program_id(0), pl.program_id(1)
    gsz, rows = wg_ref.shape[0], wg_ref.shape[1]
    nb, _, _, n = sp_ref.shape

    def cast_slice():
        rt = min(256, rows)
        dst = lax.rem(g, 2)
        for piece in range(rows // rt):
            src_rows = pl.ds(piece * rt, rt)
            dst_rows = pl.ds(pl.multiple_of(j * rows + piece * rt, rt), rt)
            for k in range(gsz):
                wgu_b[dst, k, 0, dst_rows, :] = wg_ref[k, src_rows, :].astype(BF16)
                wgu_b[dst, k, 1, dst_rows, :] = wu_ref[k, src_rows, :].astype(BF16)
                wd_b[dst, k, dst_rows, :] = wd_ref[k, src_rows, :].astype(BF16)

    def full_gather(k, bi):
        hit = lax.broadcasted_iota(I32, (cap, n), 0) == sp_ref[bi, k]
        xs_scr[k, bi * cap:(bi + 1) * cap, :] = _dot(jnp.where(hit, 1.0, 0.0).astype(BF16), xn_ref[bi])
        gate_scr[k, bi * cap:(bi + 1) * cap, :] = jnp.broadcast_to(
            jnp.sum(jnp.where(hit, ar_ref[bi, k], 0.0), axis=1, keepdims=True), (cap, LANES))

    def experts_ffn():
        src = lax.rem(g + 1, 2)
        xs = [xs_scr[k].astype(BF16) for k in range(gsz)]
        h1 = [_dot(xs[k], wgu_b[src, k, 0]) for k in range(gsz)]
        h2 = [_dot(xs[k], wgu_b[src, k, 1]) for k in range(gsz)]
        hid = [(h1[k] * jax.nn.sigmoid(h1[k]) * h2[k]).astype(BF16) for k in range(gsz)]
        ys = [(_dot(hid[k], wd_b[src, k]) * gate_scr[k, :, 0:1]).astype(BF16) for k in range(gsz)]
        for k in range(gsz):
            for bi in range(nb):
                ys_ref[bi, k] = ys[k][bi * cap:(bi + 1) * cap]

    def full_path():
        for k in range(gsz):
            for bi in range(nb):
                full_gather(k, bi)
        experts_ffn()

    @pl.when(g == 0)
    def _():
        cast_slice()

    if not windowed:
        @pl.when(g > 0)
        def _():
            cast_slice()
            full_path()
        return

    assert nb == 1
    width = cap // 2
    wins = [[_slot_window(win_ref, j, t, jnp.maximum(g - 1, 0) * gsz + k, cap, width) for t in range(n // WIN_TILE)]
            for k in range(gsz)]
    fits = functools.reduce(jnp.logical_and, [ok for per_expert in wins for _, ok in per_expert])

    @pl.when(jnp.logical_and(g > 0, fits))
    def _():
        cast_slice()
        xs_scr[...] = jnp.zeros(xs_scr.shape, F32)
        gate_scr[...] = jnp.zeros(gate_scr.shape, F32)
        for t in range(n // WIN_TILE):
            tok = slice(t * WIN_TILE, (t + 1) * WIN_TILE)
            for k in range(gsz):
                st = wins[k][t][0]
                hit = lax.broadcasted_iota(I32, (width, WIN_TILE), 0) + st == sp_ref[0, k, :, tok]
                xs_scr[k, pl.ds(st, width), :] += _dot(jnp.where(hit, 1.0, 0.0).astype(BF16), xn_ref[0, tok, :])
                gate_scr[k, pl.ds(st, width), :] += jnp.broadcast_to(
                    jnp.sum(jnp.where(hit, ar_ref[0, k, :, tok], 0.0), axis=1, keepdims=True), (width, LANES))
        experts_ffn()

    @pl.when(jnp.logical_and(g > 0, jnp.logical_not(fits)))
    def _():
        cast_slice()
        full_path()


def _ffn(win, sp_row, aff_row, xn, wg, wu, wd, layer, cap):
    bsz, n, d = xn.shape
    _, ne, _, hid = wg.shape
    assert d == hid
    nb = min(bsz, max(1, MXU_DIM // cap))
    nj = bsz // nb
    gsz = 2 if nj > 1 else 1
    rows = d // nj
    windowed = nb == 1 and n > WIN_TILE and cap // 2 >= LANES
    batch = lambda g, j: jnp.where(g == 0, 0, j)
    row = pl.BlockSpec((nb, gsz, 1, n), lambda g, j, *_: (batch(g, j), jnp.maximum(g - 1, 0), 0, 0))
    wspec = pl.BlockSpec((None, gsz, rows, hid), lambda g, j, *_: (layer, jnp.minimum(g, ne // gsz - 1), j, 0))
    grid_spec = pltpu.PrefetchScalarGridSpec(
        num_scalar_prefetch=1,
        grid=(ne // gsz + 1, nj),
        in_specs=[row, row, pl.BlockSpec((nb, n, d), lambda g, j, *_: (batch(g, j), 0, 0)), wspec, wspec, wspec],
        out_specs=pl.BlockSpec((nb, gsz, cap, d), lambda g, j, *_: (batch(g, j), jnp.maximum(g - 1, 0), 0, 0)),
        scratch_shapes=[pltpu.VMEM((2, gsz, 2, d, hid), BF16), pltpu.VMEM((2, gsz, hid, d), BF16),
                        pltpu.VMEM((gsz, nb * cap, d), F32), pltpu.VMEM((gsz, nb * cap, LANES), F32)])
    return pl.pallas_call(
        functools.partial(_ffn_kernel, cap=cap, windowed=windowed),
        grid_spec=grid_spec,
        out_shape=jax.ShapeDtypeStruct((bsz, ne, cap, d), BF16),
        compiler_params=_params(("arbitrary", "arbitrary"), 56),
        name="ec_ffn",
    )(win, sp_row, aff_row, xn, wg, wu, wd)


def _combine_kernel(win_ref, x_ref, spc_ref, ys_ref, mod_ref, gf_ref, o_ref, *, cap, final, pack):
    b, i = pl.program_id(0), pl.program_id(1)
    tm = spc_ref.shape[0]
    sub = tm if pack == 1 else WIN_TILE

    def finish(rows, acc):
        out = x_ref[rows, :] + mod_ref[5:6, :] * acc
        if final:
            out = out * lax.rsqrt(jnp.mean(out * out, axis=-1, keepdims=True) + EPS) * gf_ref[...]
        o_ref[rows, :] = out

    def full_width(spc):
        slot = lax.broadcasted_iota(I32, (sub, cap), 1)
        acc = jnp.zeros((sub, x_ref.shape[1]), F32)
        for e in range(N_EXPERTS):
            hit = jnp.where(spc[:, e:e + 1] == slot, 1.0, 0.0).astype(BF16)
            acc = acc + _dot(hit, ys_ref[e])
        return acc

    for t in range(tm // sub):
        rows = slice(t * sub, (t + 1) * sub)
        spc = spc_ref[rows, :]
        if pack == 1:
            finish(rows, full_width(spc))
            continue
        w = cap // pack
        wins = [_slot_window(win_ref, b, i * (tm // sub) + t, e, cap, w) for e in range(N_EXPERTS)]
        fits = functools.reduce(jnp.logical_and, [ok for _, ok in wins])

        @pl.when(fits)
        def _():
            lane = lax.broadcasted_iota(I32, (sub, cap), 1)
            lane_row = lax.broadcasted_iota(I32, (1, cap), 1)
            acc = jnp.zeros((sub, x_ref.shape[1]), F32)
            for p in range(N_EXPERTS // pack):
                es = list(range(p * pack, (p + 1) * pack))
                col = spc[:, es[0]:es[0] + 1]
                off = jnp.zeros((1, cap), I32) + wins[es[0]][0]
                for q, e in enumerate(es[1:], 1):
                    col = jnp.where(lane >= q * w, spc[:, e:e + 1], col)
                    off = jnp.where(lane_row >= q * w, wins[e][0] - q * w, off)
                hit = jnp.where(col - off == lane, 1.0, 0.0).astype(BF16)
                ysw = jnp.concatenate([ys_ref[e, pl.ds(wins[e][0], w), :] for e in es], axis=0)
                acc = acc + _dot(hit, ysw)
            finish(rows, acc)

        @pl.when(jnp.logical_not(fits))
        def _():
            finish(rows, full_width(spc))


def _combine(x, sp_col, win, ys, mod, g_final, cap, final):
    bsz, n, d = x.shape
    pack = 4 if n > WIN_TILE and cap // 4 >= LANES // 2 else 1
    tm = min(n, 512)
    grid_spec = pltpu.PrefetchScalarGridSpec(
        num_scalar_prefetch=1,
        grid=(bsz, n // tm),
        in_specs=[pl.BlockSpec((None, tm, d), lambda b, i, *_: (b, i, 0)),
                  pl.BlockSpec((None, tm, N_EXPERTS), lambda b, i, *_: (b, i, 0)),
                  pl.BlockSpec((None, N_EXPERTS, cap, d), lambda b, i, *_: (b, 0, 0, 0)),
                  _mod_spec(mod),
                  pl.BlockSpec((1, d), lambda b, i, *_: (0, 0))],
        out_specs=pl.BlockSpec((None, tm, d), lambda b, i, *_: (b, i, 0)))
    return pl.pallas_call(
        functools.partial(_combine_kernel, cap=cap, final=final, pack=pack),
        grid_spec=grid_spec,
        out_shape=jax.ShapeDtypeStruct((bsz, n, d), F32),
        compiler_params=_params(("parallel", "parallel"), 48),
        name="ec_combine_final" if final else "ec_combine",
    )(win, x, sp_col, ys, mod[0], g_final.reshape(1, d))


def _moe(x1, xn2, aff, wg, wu, wd, layer, mod, g_final, final):
    bsz, n, _ = x1.shape
    cap = EC_CAPACITY_FACTOR * n // N_EXPERTS
    per_group = LANES // N_EXPERTS
    ng = -(-bsz // per_group)
    aff_e = aff[:, :, :N_EXPERTS]
    sets = jnp.pad(aff_e, ((0, ng * per_group - bsz), (0, 0), (0, 0)))
    sets = sets.reshape(ng, per_group, n, N_EXPERTS).transpose(0, 2, 1, 3).reshape(ng, n, LANES)
    tile = min(n, WIN_TILE)
    selpos, win = _select(sets, cap, tile)
    ungroup = lambda a: a.reshape(ng, -1, per_group, N_EXPERTS).transpose(0, 2, 1, 3).reshape(ng * per_group, -1, N_EXPERTS)[:bsz]
    sp_col = ungroup(selpos)
    win = ungroup(win[:, :2 * (n // tile)]).reshape(bsz, -1)
    sp_row = sp_col.transpose(0, 2, 1).reshape(bsz, N_EXPERTS, 1, n)
    aff_row = aff_e.transpose(0, 2, 1).reshape(bsz, N_EXPERTS, 1, n)
    ys = _ffn(win, sp_row, aff_row, xn2, wg, wu, wd, layer, cap)
    return _combine(x1, sp_col, win, ys, mod, g_final, cap, final)


def kernel(x, c, ctx, c_ctx, w_ada, b_ada, g_norm1, w_in, b_gates, g_hnorm, conv_w, conv_b, conv_ln_g, conv_ln_b,
           w_out, g_norm2, w_router, w_e_gate, w_e_up, w_e_down, g_final):
    bsz, t, d = x.shape
    depth = w_ada.shape[0]
    rows = t // GRID_W
    n_cond = 16
    cs = jnp.concatenate([c, c_ctx[None, :], jnp.zeros((n_cond - bsz - 1, d), F32)], axis=0)
    mods = _ada(cs, w_ada, b_ada).reshape(depth, n_cond, N_ADA, d)

    npair = N_HEADS // 2
    zero_states = (jnp.zeros((bsz, npair, 2, HEAD_DIM, HEAD_DIM), F32), jnp.zeros((bsz, npair, 8, HEAD_DIM), F32)) * 2
    h_ctx = ctx
    for l in range(depth):
        last = l == depth - 1
        lat, cmod = (mods, l, None), (mods, l, bsz)
        w_l = w_in[l]
        qkvg_cols = 3 * MLSTM_WIDTH + N_GATES * N_HEADS
        w_p = jnp.concatenate([w_l[:, :3 * MLSTM_WIDTH],
                               jnp.pad(w_l[:, 3 * MLSTM_WIDTH:qkvg_cols], ((0, 0), (0, LANES - N_GATES * N_HEADS))),
                               w_l[:, qkvg_cols:]], axis=1).astype(BF16)
        bg = jnp.pad(b_gates[l].reshape(1, N_HEADS * N_GATES), ((0, 0), (0, LANES - N_HEADS * N_GATES)))
        w_o = w_out[l].astype(BF16)
        w_r = jnp.pad(w_router[l], ((0, 0), (0, LANES - N_EXPERTS)))
        gh = g_hnorm[l].reshape(N_HEADS, HEAD_DIM, 1)

        def mlstm(xs, md, states, grid_rows, full):
            qkv, gates, extra = _inproj(xs, md, g_norm1[l], w_p, bg, full, grid_rows)
            return _mlstm(qkv, gates, states, gh, full, grid_rows), extra

        def sublayers(xs, md, states, grid_rows, final):
            (hr, hc, *st), (o, cacg, fr) = mlstm(xs, md, states, grid_rows, True)
            cv = _conv(cacg, conv_w[l], conv_b[l], conv_ln_g[l], conv_ln_b[l])
            fo = _fourier(fr)
            x1, xn2, aff = _outproj(xs, hr, hc, o, cv, fo, w_o, md, g_norm2[l], w_r)
            return _moe(x1, xn2, aff, w_e_gate, w_e_up, w_e_down, l, md, g_final, final), st

        if last:
            st, _ = mlstm(h_ctx, cmod, zero_states, None, False)
        else:
            h_ctx, st = sublayers(h_ctx, cmod, zero_states, None, False)
        x, _ = sublayers(x, lat, tuple(st), rows, last)
    return x
```

```python
import functools

import jax
import jax.numpy as jnp
import numpy as np
from jax import lax
from jax.experimental import pallas as pl
from jax.experimental.pallas import tpu as pltpu

F32 = jnp.float32
BF16 = jnp.bfloat16
I32 = jnp.int32

D_MODEL = 1024
GRID_W = 64
N_HEADS = 4
HEAD_DIM = 128
MLSTM_WIDTH = N_HEADS * HEAD_DIM
N_ROW_HEADS = N_HEADS // 2
CONV_CH = 256
CONV_K = 31
FOURIER_CH = 256
FOURIER_GROUP_CH = 64
N_GATES = 4
N_EXPERTS = 16
EC_CAPACITY_FACTOR = 2
N_ADA = 6
EPS = 1e-6

LANES = 128
MXU_DIM = 256
MLSTM_CHUNK = 128
MAX_CHUNKS = 16
CONV_HALO = 16
CONV_TAIL = 128
WIN_TILE = 256
MIB = 1024 * 1024

SEC_QKV = (0, 3 * MLSTM_WIDTH)
SEC_G = (SEC_QKV[1], SEC_QKV[1] + LANES)
SEC_O = (SEC_G[1], SEC_G[1] + MLSTM_WIDTH)
SEC_CACG = (SEC_O[1], SEC_O[1] + 2 * CONV_CH)
SEC_FR = (SEC_CACG[1], SEC_CACG[1] + FOURIER_CH)
IN_COLS_PAD = SEC_FR[1]

ST_NF = 2 * HEAD_DIM
ST_NB = ST_NF + 16
ST_ROWS = ST_NB + 16
UP_ROWS = 2 * HEAD_DIM + 16


def _dot(a, b):
    return jnp.dot(a, b, preferred_element_type=F32)


def _dot_nt(a, b):
    return lax.dot_general(a, b, (((1,), (1,)), ((), ())), preferred_element_type=F32)


def _split_bf16(a):
    hi = a.astype(BF16)
    lo = (a - hi.astype(F32)).astype(BF16)
    return hi, lo


def _dot3(a, w):
    ah, al = _split_bf16(a)
    wh, wl = _split_bf16(w)
    return _dot(ah, wh) + (_dot(al, wh) + _dot(ah, wl))


def _modnorm(x, g, sh, sc):
    ms = jnp.mean(x * x, axis=-1, keepdims=True)
    return x * lax.rsqrt(ms + EPS) * g * (1.0 + sc) + sh


def _rows(n_rows, width, *vals):
    rid = lax.broadcasted_iota(I32, (n_rows, width), 0)
    out = jnp.zeros((n_rows, width), F32)
    for i, v in enumerate(vals):
        out = jnp.where(rid == i, v, out)
    return out


def _mod_spec(mod):
    mods, layer, row = mod
    _, _, n_ada, d = mods.shape
    return pl.BlockSpec((None, None, n_ada, d), lambda b, i, *_: (layer, b if row is None else row, 0, 0))


def _params(sem, vmem_mib):
    return pltpu.CompilerParams(dimension_semantics=sem, vmem_limit_bytes=vmem_mib * MIB)


def _ada_kernel(c_ref, w_ref, b_ref, o_ref):
    c = c_ref[...]
    o_ref[...] = _dot3(c * jax.nn.sigmoid(c), w_ref[...]) + b_ref[...]


def _ada(cs, w_ada, b_ada):
    n_layers, d, n6 = w_ada.shape
    r = cs.shape[0]
    tn = 1536
    return pl.pallas_call(
        _ada_kernel,
        grid=(n_layers, n6 // tn),
        in_specs=[
            pl.BlockSpec((r, d), lambda l, j: (0, 0)),
            pl.BlockSpec((None, d, tn), lambda l, j: (l, 0, j)),
            pl.BlockSpec((None, 1, tn), lambda l, j: (l, 0, j)),
        ],
        out_specs=pl.BlockSpec((None, r, tn), lambda l, j: (l, 0, j)),
        out_shape=jax.ShapeDtypeStruct((n_layers, r, n6), F32),
        compiler_params=_params(("arbitrary", "arbitrary"), 40),
        name="ada_mod",
    )(cs, w_ada, b_ada.reshape(n_layers, 1, n6))


def _inproj_kernel(x_ref, mod_ref, g_ref, w_ref, bg_ref, *out_refs, full, colmajor):
    qkv_refs, gates_ref = out_refs[:6], out_refs[6]
    xn = _modnorm(x_ref[...], g_ref[...], mod_ref[0:1, :], mod_ref[1:2, :]).astype(BF16)
    tm = xn.shape[0]
    w = MLSTM_WIDTH
    for i in range(3):
        r = _dot(xn, w_ref[:, i * w:(i + 1) * w])
        if i == 1:
            r = r * HEAD_DIM ** -0.5
        row_ref, col_ref = qkv_refs[2 * i], qkv_refs[2 * i + 1]
        for hh in range(N_ROW_HEADS):
            row_ref[hh] = r[:, hh * HEAD_DIM:(hh + 1) * HEAD_DIM].astype(BF16)
            piece = r[:, (N_ROW_HEADS + hh) * HEAD_DIM:(N_ROW_HEADS + hh + 1) * HEAD_DIM]
            if colmajor:
                piece = jnp.swapaxes(piece.reshape(tm // GRID_W, GRID_W, HEAD_DIM), 0, 1)
            col_ref[hh] = piece.astype(BF16)
    gates_ref[...] = _dot(xn, w_ref[:, SEC_G[0]:SEC_G[1]]) + bg_ref[...]
    if full:
        o_ref, cacg_ref, fr_ref = out_refs[7:]
        o_ref[...] = _dot(xn, w_ref[:, SEC_O[0]:SEC_O[1]])
        cacg_ref[...] = _dot(xn, w_ref[:, SEC_CACG[0]:SEC_CACG[1]])
        fr_ref[...] = _dot(xn, w_ref[:, SEC_FR[0]:SEC_FR[1]]).astype(BF16)


def _inproj(x, mod, g, w, bg, full, rows):
    bsz, t, d = x.shape
    tm = min(t, 1024)
    nh = N_ROW_HEADS
    ncols = IN_COLS_PAD if full else SEC_G[1]
    tok = lambda n: pl.BlockSpec((None, tm, n), lambda b, i: (b, i, 0))
    row_shape = jax.ShapeDtypeStruct((bsz, nh, t, HEAD_DIM), BF16)
    row_spec = pl.BlockSpec((None, nh, tm, HEAD_DIM), lambda b, i: (b, 0, i, 0))
    if rows is None:
        col_shape, col_spec = row_shape, row_spec
    else:
        col_shape = jax.ShapeDtypeStruct((bsz, nh, GRID_W, rows, HEAD_DIM), BF16)
        col_spec = pl.BlockSpec((None, nh, GRID_W, tm // GRID_W, HEAD_DIM), lambda b, i: (b, 0, 0, i, 0))
    out_shape = [row_shape, col_shape] * 3 + [jax.ShapeDtypeStruct((bsz, t, LANES), F32)]
    out_specs = [row_spec, col_spec] * 3 + [tok(LANES)]
    if full:
        out_shape += [jax.ShapeDtypeStruct((bsz, t, MLSTM_WIDTH), F32),
                      jax.ShapeDtypeStruct((bsz, t, 2 * CONV_CH), F32),
                      jax.ShapeDtypeStruct((bsz, t, FOURIER_CH), BF16)]
        out_specs += [tok(MLSTM_WIDTH), tok(2 * CONV_CH), tok(FOURIER_CH)]
    outs = pl.pallas_call(
        functools.partial(_inproj_kernel, full=full, colmajor=rows is not None),
        grid=(bsz, t // tm),
        in_specs=[tok(d), _mod_spec(mod),
                  pl.BlockSpec((1, d), lambda b, i: (0, 0)),
                  pl.BlockSpec((d, ncols), lambda b, i: (0, 0)),
                  pl.BlockSpec((1, LANES), lambda b, i: (0, 0))],
        out_specs=out_specs,
        out_shape=out_shape,
        compiler_params=_params(("parallel", "parallel"), 56),
        name="inproj_full" if full else "inproj_qkvg",
    )(x, mod[0], g.reshape(1, d), w, bg)
    qkv = [a.reshape(bsz, nh, t, HEAD_DIM) for a in outs[:6]]
    return qkv, outs[6], outs[7:]


def _logsig(x):
    return jnp.minimum(x, 0.0) - jnp.log1p(jnp.exp(-jnp.abs(x)))


def _tri_masks(n):
    a0 = lax.broadcasted_iota(I32, (n, n), 0)
    a1 = lax.broadcasted_iota(I32, (n, n), 1)
    return a0 <= a1, a0 >= a1


def _split3(x):
    hi = x.astype(BF16).astype(F32)
    r1 = x - hi
    mid = r1.astype(BF16).astype(F32)
    return hi, mid, r1 - mid


def _mlstm_consts(tri_scr, ones_scr):
    lc = MLSTM_CHUNK
    u = lax.broadcasted_iota(I32, (lc, 2 * lc), 0)
    t2 = lax.broadcasted_iota(I32, (lc, 2 * lc), 1)
    tri_scr[...] = jnp.where(((t2 < lc) & (u <= t2)) | ((t2 >= lc) & (u >= t2 - lc)), 1.0, 0.0).astype(BF16)
    r = lax.broadcasted_iota(I32, (16, 2 * lc), 0)
    t16 = lax.broadcasted_iota(I32, (16, 2 * lc), 1)
    ones_scr[...] = jnp.where(((r < 3) & (t16 < lc)) | ((r >= 3) & (r < 6) & (t16 >= lc)), 1.0, 0.0).astype(BF16)


def _mlstm_gates(head, g_ref, tri_scr, rowg_scr, stat_scr, n_chunks, cm_rows):
    lc = MLSTM_CHUNK
    pick = jnp.where(lax.broadcasted_iota(I32, (16, LANES), 1) == N_GATES * head + lax.broadcasted_iota(I32, (16, LANES), 0),
                     1.0, 0.0).astype(BF16)
    chunk_id = lax.broadcasted_iota(I32, (MAX_CHUNKS, lc), 0)

    def gather(c, acc):
        if cm_rows is None:
            g = g_ref[pl.ds(pl.multiple_of(c * lc, lc), lc), :]
        else:
            cols = lc // cm_rows
            g = jnp.concatenate([g_ref[pl.ds(c * cols + j, cm_rows, stride=GRID_W), :] for j in range(cols)], axis=0)
        rows = sum(_dot_nt(pick, part) for part in _split_bf16(g))
        return tuple(jnp.where(chunk_id == c, rows[i:i + 1], a) for i, a in enumerate(acc))

    zeros = jnp.zeros((MAX_CHUNKS, lc), F32)
    i_f, f_f, i_b, f_b = lax.fori_loop(0, n_chunks, gather, (zeros,) * N_GATES, unroll=min(4, n_chunks))
    lf_f, lf_b = _logsig(f_f), _logsig(f_b)
    b_f = sum(_dot(part.astype(BF16), tri_scr[:, 0:lc]) for part in _split3(lf_f))
    b_b = sum(_dot(part.astype(BF16), tri_scr[:, lc:]) for part in _split3(lf_b))
    bl_f = jnp.sum(lf_f, axis=1, keepdims=True)
    bl_b = jnp.sum(lf_b, axis=1, keepdims=True)
    ci_f, ci_b = i_f - b_f, i_b - b_b
    ml_f = jnp.max(ci_f + bl_f, axis=1, keepdims=True)
    ml_b = jnp.max(ci_b + bl_b, axis=1, keepdims=True)
    for r, val in enumerate((jnp.exp(ci_f + bl_f - ml_f), jnp.exp(ci_b + bl_b - ml_b), b_f, b_b, ci_f, ci_b)):
        rowg_scr[:, r, :] = val
    for r, val in enumerate((bl_f, bl_b, ml_f, ml_b)):
        stat_scr[:, r, :] = jnp.broadcast_to(val, (MAX_CHUNKS, HEAD_DIM))


def _mlstm_increment(c, k_ref, v_ref, rowg_scr, up_scr, vt_scr):
    lc = MLSTM_CHUNK
    s = pl.multiple_of(c * lc, lc)
    v_t = v_ref[pl.ds(s, lc), :].astype(F32).T
    rg = rowg_scr[c]
    w_f, w_b = rg[0:1], rg[1:2]
    lhs = jnp.concatenate([(v_t * w_f).astype(BF16), (v_t * w_b).astype(BF16),
                           _rows(16, lc, w_f, w_b).astype(BF16)], axis=0)
    up_scr[c] = _dot(lhs, k_ref[pl.ds(s, lc), :])
    if vt_scr is not None:
        vt_scr[c] = v_t.astype(BF16)


def _mlstm_state_step(cc, carry, up_scr, stat_scr, st_scr, mst_scr, fwd):
    cmat, n, m = carry
    off, row = (0, 0) if fwd else (HEAD_DIM, 1)
    if st_scr is not None:
        st_scr[cc, off:off + HEAD_DIM, :] = cmat.astype(BF16)
        nrow = ST_NF if fwd else ST_NB
        st_scr[cc, nrow:nrow + 16, :] = _rows(16, HEAD_DIM, n).astype(BF16)
        mst_scr[cc, row:row + 1, :] = jnp.broadcast_to(m, (1, HEAD_DIM))
    st = stat_scr[cc]
    bl, ml = st[row:row + 1, 0:1], st[2 + row:3 + row, 0:1]
    m_new = jnp.maximum(bl + m, ml)
    keep = jnp.exp(bl + m - m_new)
    gain = jnp.exp(ml - m_new)
    cmat = keep * cmat + gain * up_scr[cc, off:off + HEAD_DIM, :]
    n = keep * n + gain * up_scr[cc, 2 * HEAD_DIM + row:2 * HEAD_DIM + row + 1, :]
    return cmat, n, m_new


def _mlstm_outputs(c, chains, ones_scr):
    lc = MLSTM_CHUNK
    s = pl.multiple_of(c * lc, lc)
    le, ge = _tri_masks(lc)
    qs = [ch[0][pl.ds(s, lc), :] for ch in chains]
    qk_ts = [_dot_nt(ch[1][pl.ds(s, lc), :], q) for ch, q in zip(chains, qs)]
    g_ts = [_dot_nt(ch[5][c], q) for ch, q in zip(chains, qs)]
    rgs = [ch[4][c] for ch in chains]
    cis = [lax.dot_general(_rows(16, lc, *_split3(rg[4:5]), *_split3(rg[5:6])).astype(BF16), ones_scr[...],
                           (((0,), (0,)), ((), ())), preferred_element_type=F32) for rg in rgs]

    def direction(qk_t, valid, b_row, ci, m_st, qn):
        dlog = jnp.where(valid, b_row + ci, -jnp.inf)
        inter = b_row + m_st
        m_t = jnp.maximum(inter, jnp.max(dlog, axis=0, keepdims=True))
        s_t = qk_t * jnp.exp(dlog - m_t)
        w_inter = jnp.exp(inter - m_t)
        den = w_inter * qn + jnp.sum(s_t, axis=0, keepdims=True)
        r = 1.0 / jnp.maximum(jnp.abs(den), jnp.exp(-m_t))
        return r * s_t, r * w_inter

    mixed = []
    for ch, qk_t, g_t, rg, ci in zip(chains, qk_ts, g_ts, rgs, cis):
        ms = ch[6][c]
        sf, cf = direction(qk_t, le, rg[2:3], ci[:, 0:lc], ms[0:1, 0:1], g_t[ST_NF:ST_NF + 1])
        sb, cb = direction(qk_t, ge, rg[3:4], ci[:, lc:], ms[1:2, 0:1], g_t[ST_NB:ST_NB + 1])
        mixed.append(((sf + sb).astype(BF16), cf * g_t[0:HEAD_DIM] + cb * g_t[HEAD_DIM:2 * HEAD_DIM]))
    h_ts = [inter_part + _dot(ch[7][c], s_mix) for ch, (s_mix, inter_part) in zip(chains, mixed)]
    for ch, h_t in zip(chains, h_ts):
        gh_ref, h_ref, cm_rows = ch[2], ch[3], ch[8]
        h = (h_t * lax.rsqrt(jnp.mean(h_t * h_t, axis=0, keepdims=True) + EPS) * gh_ref[...]).T
        if cm_rows is None:
            h_ref[pl.ds(s, lc), :] = h
        else:
            cols = lc // cm_rows
            for j in range(cols):
                h_ref[pl.ds(c * cols + j, cm_rows, stride=GRID_W), :] = h[j * cm_rows:(j + 1) * cm_rows]


def _mlstm_kernel(*refs, n_chunks, with_outputs, cm_rows):
    n_slot_in = 6
    g_ref = refs[0]
    slots_in = [refs[1:1 + n_slot_in], refs[1 + n_slot_in:1 + 2 * n_slot_in]]
    rest = refs[1 + 2 * n_slot_in:]
    if with_outputs:
        h_refs, rest = rest[:2], rest[2:]
    state_out, scratch = rest[:4], rest[4:]
    tri_scr, ones_scr, scratch = scratch[0], scratch[1], scratch[2:]
    n_scr = 6 if with_outputs else 3
    slots_scr = [scratch[:n_scr], scratch[n_scr:2 * n_scr]]
    slot_rows = [None, cm_rows]
    _mlstm_consts(tri_scr, ones_scr)
    for slot, scr in enumerate(slots_scr):
        _mlstm_gates(pl.program_id(1) + slot * N_ROW_HEADS, g_ref, tri_scr, scr[0], scr[1], n_chunks, slot_rows[slot])

    def increments(c, _):
        for (q_ref, k_ref, v_ref, c0_ref, n0_ref, gh_ref), scr in zip(slots_in, slots_scr):
            _mlstm_increment(c, k_ref, v_ref, scr[0], scr[2], scr[5] if with_outputs else None)
        return 0

    lax.fori_loop(0, n_chunks, increments, 0, unroll=min(4, n_chunks))

    def states(i, carry):
        out = []
        for slot, scr in enumerate(slots_scr):
            st_scr, mst_scr = (scr[3], scr[4]) if with_outputs else (None, None)
            out.append(_mlstm_state_step(i, carry[2 * slot], scr[2], scr[1], st_scr, mst_scr, True))
            out.append(_mlstm_state_step(n_chunks - 1 - i, carry[2 * slot + 1], scr[2], scr[1], st_scr, mst_scr, False))
        return tuple(out)

    init = []
    for (q_ref, k_ref, v_ref, c0_ref, n0_ref, gh_ref) in slots_in:
        init.append((c0_ref[0], n0_ref[0:1, :], n0_ref[2:3, 0:1]))
        init.append((c0_ref[1], n0_ref[1:2, :], n0_ref[3:4, 0:1]))
    final = lax.fori_loop(0, n_chunks, states, tuple(init))
    for slot in range(2):
        c_out, n_out = state_out[2 * slot], state_out[2 * slot + 1]
        (cf, nf, mf), (cb, nb, mb) = final[2 * slot], final[2 * slot + 1]
        c_out[0] = cf
        c_out[1] = cb
        n_out[...] = _rows(8, HEAD_DIM, nf, nb, mf, mb)

    if with_outputs:
        chains = [(q_ref, k_ref, gh_ref, h_refs[slot], scr[0], scr[3], scr[4], scr[5], slot_rows[slot])
                  for slot, ((q_ref, k_ref, v_ref, c0_ref, n0_ref, gh_ref), scr) in enumerate(zip(slots_in, slots_scr))]

        def outputs(c, _):
            _mlstm_outputs(c, chains, ones_scr)
            return 0

        lax.fori_loop(0, n_chunks, outputs, 0, unroll=min(4, n_chunks))


def _mlstm(qkv, gates, states, gh, with_outputs, cm_rows):
    bsz, npair, t, dh = qkv[0].shape
    nc = t // MLSTM_CHUNK
    lc = MLSTM_CHUNK
    seq = pl.BlockSpec((None, None, t, dh), lambda b, p: (b, p, 0, 0))
    cst = pl.BlockSpec((None, None, 2, dh, dh), lambda b, p: (b, p, 0, 0, 0))
    nst = pl.BlockSpec((None, None, 8, dh), lambda b, p: (b, p, 0, 0))

    def slot_specs(head_off):
        return [seq, seq, seq, cst, nst, pl.BlockSpec((None, dh, 1), lambda b, p: (p + head_off, 0, 0))]

    st_shapes = [jax.ShapeDtypeStruct((bsz, npair, 2, dh, dh), F32), jax.ShapeDtypeStruct((bsz, npair, 8, dh), F32)] * 2
    out_shape, out_specs = st_shapes, [cst, nst, cst, nst]
    assert nc <= MAX_CHUNKS
    slot_scratch = [pltpu.VMEM((MAX_CHUNKS, 8, lc), F32), pltpu.VMEM((MAX_CHUNKS, 8, dh), F32),
                    pltpu.VMEM((nc, UP_ROWS, dh), F32)]
    if with_outputs:
        h_shape = jax.ShapeDtypeStruct((bsz, t, npair * dh), F32)
        h_spec = pl.BlockSpec((None, t, dh), lambda b, p: (b, 0, p))
        out_shape = [h_shape, h_shape] + out_shape
        out_specs = [h_spec, h_spec] + out_specs
        slot_scratch += [pltpu.VMEM((nc, ST_ROWS, dh), BF16), pltpu.VMEM((nc, 8, dh), F32), pltpu.VMEM((nc, dh, lc), BF16)]
    q_r, q_c, k_r, k_c, v_r, v_c = qkv
    c_r, n_r, c_c, n_c = states
    return pl.pallas_call(
        functools.partial(_mlstm_kernel, n_chunks=nc, with_outputs=with_outputs, cm_rows=cm_rows),
        grid=(bsz, npair),
        in_specs=[pl.BlockSpec((None, t, LANES), lambda b, p: (b, 0, 0))] + slot_specs(0) + slot_specs(N_ROW_HEADS),
        out_specs=out_specs,
        out_shape=out_shape,
        scratch_shapes=[pltpu.VMEM((lc, 2 * lc), BF16), pltpu.VMEM((16, 2 * lc), BF16)] + slot_scratch * 2,
        compiler_params=_params(("parallel", "parallel"), 48),
        name="mlstm_full" if with_outputs else "mlstm_states",
    )(gates, q_r, k_r, v_r, c_r, n_r, gh, q_c, k_c, v_c, c_c, n_c, gh)


def _conv_kernel(cacg_ref, w_ref, cb_ref, lg_ref, lb_ref, o_ref, ysh, yconv, *, t):
    rt = 64
    ysh[0, 0:CONV_HALO, :] = jnp.zeros((CONV_HALO, CONV_CH), F32)
    ysh[0, CONV_HALO + t:t + CONV_TAIL, :] = jnp.zeros((CONV_TAIL - CONV_HALO, CONV_CH), F32)

    def fill(r, _):
        s = pl.multiple_of(r * rt, rt)
        a = cacg_ref[pl.ds(s, rt), 0:CONV_CH]
        g = cacg_ref[pl.ds(s, rt), CONV_CH:2 * CONV_CH]
        ysh[0, pl.ds(CONV_HALO + s, rt), :] = a * jax.nn.sigmoid(g)
        return 0

    lax.fori_loop(0, t // rt, fill, 0)

    def shift(r, _):
        s = pl.multiple_of(r * rt, rt)
        win = ysh[0, pl.ds(s, rt + 8), :]
        for res in range(1, 8):
            ysh[res, pl.ds(s, rt), :] = pltpu.roll(win, rt + 8 - res, axis=0)[0:rt, :]
        return 0

    lax.fori_loop(0, t // rt + 1, shift, 0)

    def tile(r, _):
        s = pl.multiple_of(r * rt, rt)
        acc = jnp.zeros((rt, CONV_CH), F32)
        for kk in range(CONV_K):
            off = kk + CONV_HALO - CONV_K // 2
            rows = ysh[off % 8, pl.ds(pl.multiple_of(s + 8 * (off // 8), 8), rt), :]
            acc = acc + w_ref[kk:kk + 1, :] * rows
        yconv[pl.ds(s, rt), :] = acc + cb_ref[...]
        return 0

    lax.fori_loop(0, t // rt, tile, 0)

    def norm(r, _):
        s = pl.multiple_of(r * rt, rt)
        y = yconv[pl.ds(s, rt), :]
        mu = jnp.mean(y, axis=-1, keepdims=True)
        yc = y - mu
        var = jnp.mean(yc * yc, axis=-1, keepdims=True)
        z = yc * lax.rsqrt(var + EPS) * lg_ref[...] + lb_ref[...]
        o_ref[pl.ds(s, rt), :] = (z * jax.nn.sigmoid(z)).astype(BF16)
        return 0

    lax.fori_loop(0, t // rt, norm, 0, unroll=4)


def _conv(cacg, conv_w, conv_b, ln_g, ln_b):
    bsz, t, _ = cacg.shape
    row = lambda a: a.reshape(1, CONV_CH)
    vec = pl.BlockSpec((1, CONV_CH), lambda b: (0, 0))
    return pl.pallas_call(
        functools.partial(_conv_kernel, t=t),
        grid=(bsz,),
        in_specs=[pl.BlockSpec((None, t, 2 * CONV_CH), lambda b: (b, 0, 0)),
                  pl.BlockSpec((CONV_K + 1, CONV_CH), lambda b: (0, 0)), vec, vec, vec],
        out_specs=pl.BlockSpec((None, t, CONV_CH), lambda b: (b, 0, 0)),
        out_shape=jax.ShapeDtypeStruct((bsz, t, CONV_CH), BF16),
        scratch_shapes=[pltpu.VMEM((8, t + CONV_TAIL, CONV_CH), F32), pltpu.VMEM((t, CONV_CH), F32)],
        compiler_params=_params(("parallel",), 40),
        name="conformer_conv",
    )(cacg, jnp.pad(conv_w, ((0, 1), (0, 0))), row(conv_b), row(ln_g), row(ln_b))


@functools.lru_cache(maxsize=None)
def _dft_tables(t):
    j = np.arange(t, dtype=np.int64)
    ang = 2.0 * np.pi * ((j[:, None] * j[None, :]) % t).astype(np.float64) / t
    wt = np.concatenate([np.cos(ang), -np.sin(ang)], axis=1).astype(np.float32)
    c = np.arange(FOURIER_CH, dtype=np.int64)
    grp, idx = c // FOURIER_GROUP_CH, c % FOURIER_GROUP_CH
    same = grp[:, None] == grp[None, :]
    angc = 2.0 * np.pi * ((idx[:, None] * idx[None, :]) % FOURIER_GROUP_CH).astype(np.float64) / FOURIER_GROUP_CH
    cc = np.where(same, np.cos(angc), 0.0).astype(np.float32)
    sc = np.where(same, np.sin(angc), 0.0).astype(np.float32)
    return wt, cc, sc


def _fourier_kernel(z_ref, cc_ref, sc_ref, wt_ref, o_ref, zcs, *, t, scale):
    bsz, _, ch = z_ref.shape

    @pl.when(pl.program_id(0) == 0)
    def _():
        for b in range(bsz):
            zcs[0:t, b * ch:(b + 1) * ch] = _dot(z_ref[b], cc_ref[...]).astype(BF16)
            zcs[t:2 * t, b * ch:(b + 1) * ch] = _dot(z_ref[b], sc_ref[...]).astype(BF16)

    r = _dot(wt_ref[...], zcs[...]) * scale
    for b in range(bsz):
        o_ref[b] = r[:, b * ch:(b + 1) * ch].astype(BF16)


def _fourier(z):
    bsz, t, ch = z.shape
    wt, cc, sc = _dft_tables(t)
    tm = min(t, 512)
    mat = pl.BlockSpec((ch, ch), lambda i: (0, 0))
    return pl.pallas_call(
        functools.partial(_fourier_kernel, t=t, scale=float((t * FOURIER_GROUP_CH) ** -0.5)),
        grid=(t // tm,),
        in_specs=[pl.BlockSpec((bsz, t, ch), lambda i: (0, 0, 0)), mat, mat,
                  pl.BlockSpec((tm, 2 * t), lambda i: (i, 0))],
        out_specs=pl.BlockSpec((bsz, tm, ch), lambda i: (0, i, 0)),
        out_shape=jax.ShapeDtypeStruct((bsz, t, ch), BF16),
        scratch_shapes=[pltpu.VMEM((2 * t, bsz * ch), BF16)],
        compiler_params=_params(("arbitrary",), 56),
        name="fourier_mix",
    )(z, jnp.asarray(cc).astype(BF16), jnp.asarray(sc).astype(BF16), jnp.asarray(wt).astype(BF16))


def _outproj_kernel(x_ref, hr_ref, hc_ref, o_ref, cv_ref, fo_ref, w_ref, mod_ref, g2_ref, wr_ref,
                    x1_ref, xn_ref, aff_ref):
    half = MLSTM_WIDTH // 2
    a, b = MLSTM_WIDTH, MLSTM_WIDTH + CONV_CH
    og = jax.nn.sigmoid(o_ref[...])
    y = (_dot((hr_ref[...] * og[:, 0:half]).astype(BF16), w_ref[0:half, :])
         + _dot((hc_ref[...] * og[:, half:a]).astype(BF16), w_ref[half:a, :])
         + _dot(cv_ref[...], w_ref[a:b, :]) + _dot(fo_ref[...], w_ref[b:, :]))
    x1 = x_ref[...] + mod_ref[2:3, :] * y
    x1_ref[...] = x1
    xn = _modnorm(x1, g2_ref[...], mod_ref[3:4, :], mod_ref[4:5, :])
    xn_ref[...] = xn.astype(BF16)
    tm = xn.shape[0]
    r = _dot(jnp.concatenate(_split_bf16(xn), axis=0), jnp.concatenate(_split_bf16(wr_ref[...]), axis=1))
    logits = (r[:tm, :LANES] + r[:tm, LANES:]) + (r[tm:, :LANES] + r[tm:, LANES:])
    lane = lax.broadcasted_iota(I32, logits.shape, 1)
    logits = jnp.where(lane < N_EXPERTS, logits, -jnp.inf)
    e = jnp.exp(logits - jnp.max(logits, axis=-1, keepdims=True))
    aff_ref[...] = e / jnp.sum(e, axis=-1, keepdims=True)


def _outproj(x, hr, hc, o, cv, fo, w_out, mod, g2, wr):
    bsz, t, d = x.shape
    tm = min(t, 512)
    tok = lambda n: pl.BlockSpec((None, tm, n), lambda b, i: (b, i, 0))
    return pl.pallas_call(
        _outproj_kernel,
        grid=(bsz, t // tm),
        in_specs=[tok(d), tok(MLSTM_WIDTH // 2), tok(MLSTM_WIDTH // 2), tok(MLSTM_WIDTH), tok(CONV_CH), tok(FOURIER_CH),
                  pl.BlockSpec((d, d), lambda b, i: (0, 0)), _mod_spec(mod),
                  pl.BlockSpec((1, d), lambda b, i: (0, 0)),
                  pl.BlockSpec((d, LANES), lambda b, i: (0, 0))],
        out_specs=[tok(d), tok(d), tok(LANES)],
        out_shape=[jax.ShapeDtypeStruct((bsz, t, d), F32), jax.ShapeDtypeStruct((bsz, t, d), BF16),
                   jax.ShapeDtypeStruct((bsz, t, LANES), F32)],
        compiler_params=_params(("parallel", "parallel"), 40),
        name="outproj_router",
    )(x, hr, hc, o, cv, fo, w_out, mod[0], g2.reshape(1, d), wr)


def _select_kernel(a_ref, sp_ref, win_ref, *, cap, tile):
    a = a_ref[...]
    n = a.shape[0]
    blk = min(n, MXU_DIM)

    def count(mask):
        return jnp.sum(jnp.where(mask, 1.0, 0.0), axis=0, keepdims=True)

    def search(i, thr):
        cand = thr | jnp.left_shift(jnp.int32(1), 30 - i)
        return jnp.where(count(a >= lax.bitcast_convert_type(cand, F32)) >= cap, cand, thr)

    thr = lax.bitcast_convert_type(lax.fori_loop(0, 31, search, jnp.zeros((1, LANES), I32)), F32)
    gt = a > thr
    eq = a == thr
    before = jnp.where(lax.broadcasted_iota(I32, (blk, blk), 1) < lax.broadcasted_iota(I32, (blk, blk), 0), 1.0, 0.0).astype(BF16)

    def excl_cumsum(mask):
        x = jnp.where(mask, 1.0, 0.0)
        run = jnp.zeros((1, LANES), F32)
        parts = []
        for j in range(n // blk):
            xb = x[j * blk:(j + 1) * blk]
            parts.append(_dot(before, xb.astype(BF16)) + run)
            run = run + jnp.sum(xb, axis=0, keepdims=True)
        return jnp.concatenate(parts, axis=0)

    sel = gt | (eq & (excl_cumsum(eq) < cap - count(gt)))
    pos = excl_cumsum(sel).astype(I32)
    sp_ref[...] = jnp.where(sel, pos, -1)
    rid = lax.broadcasted_iota(I32, win_ref.shape, 0)
    win = jnp.zeros(win_ref.shape, I32)
    for ti in range(n // tile):
        t_sel, t_pos = sel[ti * tile:(ti + 1) * tile], pos[ti * tile:(ti + 1) * tile]
        win = jnp.where(rid == 2 * ti, jnp.min(jnp.where(t_sel, t_pos, cap), axis=0, keepdims=True), win)
        win = jnp.where(rid == 2 * ti + 1, jnp.max(jnp.where(t_sel, t_pos + 1, 0), axis=0, keepdims=True), win)
    win_ref[...] = win


def _select(aff_sets, cap, tile):
    ng, n, _ = aff_sets.shape
    win_rows = max(8, 2 * (n // tile))
    blk = pl.BlockSpec((None, n, LANES), lambda g: (g, 0, 0))
    return pl.pallas_call(
        functools.partial(_select_kernel, cap=cap, tile=tile),
        grid=(ng,),
        in_specs=[blk],
        out_specs=[blk, pl.BlockSpec((None, win_rows, LANES), lambda g: (g, 0, 0))],
        out_shape=[jax.ShapeDtypeStruct((ng, n, LANES), I32), jax.ShapeDtypeStruct((ng, win_rows, LANES), I32)],
        compiler_params=_params(("parallel",), 32),
        name="ec_select",
    )(aff_sets)


def _slot_window(win_ref, b, tile_idx, e, cap, width):
    lo = win_ref[b, 2 * tile_idx * N_EXPERTS + e]
    hi = win_ref[b, (2 * tile_idx + 1) * N_EXPERTS + e]
    st = jnp.minimum(lax.shift_left(lax.shift_right_logical(lo, 4), 4), cap - width)
    return pl.multiple_of(st, 16), hi - st <= width


def _ffn_kernel(win_ref, sp_ref, ar_ref, xn_ref, wg_ref, wu_ref, wd_ref, ys_ref, wgu_b, wd_b, xs_scr, gate_scr, *, cap, windowed):
    g, j = pl.program_id(0), pl.program_id(1)
    gsz, rows = wg_ref.shape[0], wg_ref.shape[1]
    nb, _, _, n = sp_ref.shape

    def cast_slice():
        rt = min(256, rows)
        dst = lax.rem(g, 2)
        for piece in range(rows // rt):
            src_rows = pl.ds(piece * rt, rt)
            dst_rows = pl.ds(pl.multiple_of(j * rows + piece * rt, rt), rt)
            for k in range(gsz):
                wgu_b[dst, k, 0, dst_rows, :] = wg_ref[k, src_rows, :].astype(BF16)
                wgu_b[dst, k, 1, dst_rows, :] = wu_ref[k, src_rows, :].astype(BF16)
                wd_b[dst, k, dst_rows, :] = wd_ref[k, src_rows, :].astype(BF16)

    def full_gather(k, bi):
        hit = lax.broadcasted_iota(I32, (cap, n), 0) == sp_ref[bi, k]
        xs_scr[k, bi * cap:(bi + 1) * cap, :] = _dot(jnp.where(hit, 1.0, 0.0).astype(BF16), xn_ref[bi])
        gate_scr[k, bi * cap:(bi + 1) * cap, :] = jnp.broadcast_to(
            jnp.sum(jnp.where(hit, ar_ref[bi, k], 0.0), axis=1, keepdims=True), (cap, LANES))

    def experts_ffn():
        src = lax.rem(g + 1, 2)
        xs = [xs_scr[k].astype(BF16) for k in range(gsz)]
        h1 = [_dot(xs[k], wgu_b[src, k, 0]) for k in range(gsz)]
        h2 = [_dot(xs[k], wgu_b[src, k, 1]) for k in range(gsz)]
        hid = [(h1[k] * jax.nn.sigmoid(h1[k]) * h2[k]).astype(BF16) for k in range(gsz)]
        ys = [(_dot(hid[k], wd_b[src, k]) * gate_scr[k, :, 0:1]).astype(BF16) for k in range(gsz)]
        for k in range(gsz):
            for bi in range(nb):
                ys_ref[bi, k] = ys[k][bi * cap:(bi + 1) * cap]

    def full_path():
        for k in range(gsz):
            for bi in range(nb):
                full_gather(k, bi)
        experts_ffn()

    @pl.when(g == 0)
    def _():
        cast_slice()

    if not windowed:
        @pl.when(g > 0)
        def _():
            cast_slice()
            full_path()
        return

    assert nb == 1
    width = cap // 2
    wins = [[_slot_window(win_ref, j, t, jnp.maximum(g - 1, 0) * gsz + k, cap, width) for t in range(n // WIN_TILE)]
            for k in range(gsz)]
    fits = functools.reduce(jnp.logical_and, [ok for per_expert in wins for _, ok in per_expert])

    @pl.when(jnp.logical_and(g > 0, fits))
    def _():
        cast_slice()
        xs_scr[...] = jnp.zeros(xs_scr.shape, F32)
        gate_scr[...] = jnp.zeros(gate_scr.shape, F32)
        for t in range(n // WIN_TILE):
            tok = slice(t * WIN_TILE, (t + 1) * WIN_TILE)
            for k in range(gsz):
                st = wins[k][t][0]
                hit = lax.broadcasted_iota(I32, (width, WIN_TILE), 0) + st == sp_ref[0, k, :, tok]
                xs_scr[k, pl.ds(st, width), :] += _dot(jnp.where(hit, 1.0, 0.0).astype(BF16), xn_ref[0, tok, :])
                gate_scr[k, pl.ds(st, width), :] += jnp.broadcast_to(
                    jnp.sum(jnp.where(hit, ar_ref[0, k, :, tok], 0.0), axis=1, keepdims=True), (width, LANES))
        experts_ffn()

    @pl.when(jnp.logical_and(g > 0, jnp.logical_not(fits)))
    def _():
        cast_slice()
        full_path()


def _ffn(win, sp_row, aff_row, xn, wg, wu, wd, layer, cap):
    bsz, n, d = xn.shape
    _, ne, _, hid = wg.shape
    assert d == hid
    nb = min(bsz, max(1, MXU_DIM // cap))
    nj = bsz // nb
    gsz = 2 if nj > 1 else 1
    rows = d // nj
    windowed = nb == 1 and n > WIN_TILE and cap // 2 >= LANES
    batch = lambda g, j: jnp.where(g == 0, 0, j)
    row = pl.BlockSpec((nb, gsz, 1, n), lambda g, j, *_: (batch(g, j), jnp.maximum(g - 1, 0), 0, 0))
    wspec = pl.BlockSpec((None, gsz, rows, hid), lambda g, j, *_: (layer, jnp.minimum(g, ne // gsz - 1), j, 0))
    grid_spec = pltpu.PrefetchScalarGridSpec(
        num_scalar_prefetch=1,
        grid=(ne // gsz + 1, nj),
        in_specs=[row, row, pl.BlockSpec((nb, n, d), lambda g, j, *_: (batch(g, j), 0, 0)), wspec, wspec, wspec],
        out_specs=pl.BlockSpec((nb, gsz, cap, d), lambda g, j, *_: (batch(g, j), jnp.maximum(g - 1, 0), 0, 0)),
        scratch_shapes=[pltpu.VMEM((2, gsz, 2, d, hid), BF16), pltpu.VMEM((2, gsz, hid, d), BF16),
                        pltpu.VMEM((gsz, nb * cap, d), F32), pltpu.VMEM((gsz, nb * cap, LANES), F32)])
    return pl.pallas_call(
        functools.partial(_ffn_kernel, cap=cap, windowed=windowed),
        grid_spec=grid_spec,
        out_shape=jax.ShapeDtypeStruct((bsz, ne, cap, d), BF16),
        compiler_params=_params(("arbitrary", "arbitrary"), 56),
        name="ec_ffn",
    )(win, sp_row, aff_row, xn, wg, wu, wd)


def _combine_kernel(win_ref, x_ref, spc_ref, ys_ref, mod_ref, gf_ref, o_ref, *, cap, final, pack):
    b, i = pl.program_id(0), pl.program_id(1)
    tm = spc_ref.shape[0]
    sub = tm if pack == 1 else WIN_TILE

    def finish(rows, acc):
        out = x_ref[rows, :] + mod_ref[5:6, :] * acc
        if final:
            out = out * lax.rsqrt(jnp.mean(out * out, axis=-1, keepdims=True) + EPS) * gf_ref[...]
        o_ref[rows, :] = out

    def full_width(spc):
        slot = lax.broadcasted_iota(I32, (sub, cap), 1)
        acc = jnp.zeros((sub, x_ref.shape[1]), F32)
        for e in range(N_EXPERTS):
            hit = jnp.where(spc[:, e:e + 1] == slot, 1.0, 0.0).astype(BF16)
            acc = acc + _dot(hit, ys_ref[e])
        return acc

    n_sub = tm // sub
    rows = [slice(t * sub, (t + 1) * sub) for t in range(n_sub)]
    spcs = [spc_ref[r, :] for r in rows]
    if pack == 1:
        for r, spc in zip(rows, spcs):
            finish(r, full_width(spc))
        return

    w = cap // pack
    wins = [[_slot_window(win_ref, b, i * n_sub + t, e, cap, w) for e in range(N_EXPERTS)] for t in range(n_sub)]
    fits = functools.reduce(jnp.logical_and, [ok for per_tile in wins for _, ok in per_tile])

    @pl.when(fits)
    def _():
        lane = lax.broadcasted_iota(I32, (sub, cap), 1)
        lane_row = lax.broadcasted_iota(I32, (1, cap), 1)
        accs = [jnp.zeros((sub, x_ref.shape[1]), F32) for _ in range(n_sub)]
        for p in range(N_EXPERTS // pack):
            es = list(range(p * pack, (p + 1) * pack))
            hits, ysws = [], []
            for t in range(n_sub):
                col = spcs[t][:, es[0]:es[0] + 1]
                off = jnp.zeros((1, cap), I32) + wins[t][es[0]][0]
                for q, e in enumerate(es[1:], 1):
                    col = jnp.where(lane >= q * w, spcs[t][:, e:e + 1], col)
                    off = jnp.where(lane_row >= q * w, wins[t][e][0] - q * w, off)
                hits.append(jnp.where(col - off == lane, 1.0, 0.0).astype(BF16))
                ysws.append(jnp.concatenate([ys_ref[e, pl.ds(wins[t][e][0], w), :] for e in es], axis=0))
            accs = [acc + _dot(hit, ysw) for acc, hit, ysw in zip(accs, hits, ysws)]
        for r, acc in zip(rows, accs):
            finish(r, acc)

    @pl.when(jnp.logical_not(fits))
    def _():
        for r, spc in zip(rows, spcs):
            finish(r, full_width(spc))


def _combine(x, sp_col, win, ys, mod, g_final, cap, final):
    bsz, n, d = x.shape
    pack = 4 if n > WIN_TILE and cap // 4 >= LANES // 2 else 1
    tm = min(n, 512)
    grid_spec = pltpu.PrefetchScalarGridSpec(
        num_scalar_prefetch=1,
        grid=(bsz, n // tm),
        in_specs=[pl.BlockSpec((None, tm, d), lambda b, i, *_: (b, i, 0)),
                  pl.BlockSpec((None, tm, N_EXPERTS), lambda b, i, *_: (b, i, 0)),
                  pl.BlockSpec((None, N_EXPERTS, cap, d), lambda b, i, *_: (b, 0, 0, 0)),
                  _mod_spec(mod),
                  pl.BlockSpec((1, d), lambda b, i, *_: (0, 0))],
        out_specs=pl.BlockSpec((None, tm, d), lambda b, i, *_: (b, i, 0)))
    return pl.pallas_call(
        functools.partial(_combine_kernel, cap=cap, final=final, pack=pack),
        grid_spec=grid_spec,
        out_shape=jax.ShapeDtypeStruct((bsz, n, d), F32),
        compiler_params=_params(("parallel", "parallel"), 48),
        name="ec_combine_final" if final else "ec_combine",
    )(win, x, sp_col, ys, mod[0], g_final.reshape(1, d))


def _moe(x1, xn2, aff, wg, wu, wd, layer, mod, g_final, final):
    bsz, n, _ = x1.shape
    cap = EC_CAPACITY_FACTOR * n // N_EXPERTS
    per_group = LANES // N_EXPERTS
    ng = -(-bsz // per_group)
    aff_e = aff[:, :, :N_EXPERTS]
    sets = jnp.pad(aff_e, ((0, ng * per_group - bsz), (0, 0), (0, 0)))
    sets = sets.reshape(ng, per_group, n, N_EXPERTS).transpose(0, 2, 1, 3).reshape(ng, n, LANES)
    tile = min(n, WIN_TILE)
    selpos, win = _select(sets, cap, tile)
    ungroup = lambda a: a.reshape(ng, -1, per_group, N_EXPERTS).transpose(0, 2, 1, 3).reshape(ng * per_group, -1, N_EXPERTS)[:bsz]
    sp_col = ungroup(selpos)
    win = ungroup(win[:, :2 * (n // tile)]).reshape(bsz, -1)
    sp_row = sp_col.transpose(0, 2, 1).reshape(bsz, N_EXPERTS, 1, n)
    aff_row = aff_e.transpose(0, 2, 1).reshape(bsz, N_EXPERTS, 1, n)
    ys = _ffn(win, sp_row, aff_row, xn2, wg, wu, wd, layer, cap)
    return _combine(x1, sp_col, win, ys, mod, g_final, cap, final)


def kernel(x, c, ctx, c_ctx, w_ada, b_ada, g_norm1, w_in, b_gates, g_hnorm, conv_w, conv_b, conv_ln_g, conv_ln_b,
           w_out, g_norm2, w_router, w_e_gate, w_e_up, w_e_down, g_final):
    bsz, t, d = x.shape
    depth = w_ada.shape[0]
    rows = t // GRID_W
    n_cond = 16
    cs = jnp.concatenate([c, c_ctx[None, :], jnp.zeros((n_cond - bsz - 1, d), F32)], axis=0)
    mods = _ada(cs, w_ada, b_ada).reshape(depth, n_cond, N_ADA, d)

    npair = N_HEADS // 2
    zero_states = (jnp.zeros((bsz, npair, 2, HEAD_DIM, HEAD_DIM), F32), jnp.zeros((bsz, npair, 8, HEAD_DIM), F32)) * 2
    h_ctx = ctx
    for l in range(depth):
        last = l == depth - 1
        lat, cmod = (mods, l, None), (mods, l, bsz)
        w_l = w_in[l]
        qkvg_cols = 3 * MLSTM_WIDTH + N_GATES * N_HEADS
        w_p = jnp.concatenate([w_l[:, :3 * MLSTM_WIDTH],
                               jnp.pad(w_l[:, 3 * MLSTM_WIDTH:qkvg_cols], ((0, 0), (0, LANES - N_GATES * N_HEADS))),
                               w_l[:, qkvg_cols:]], axis=1).astype(BF16)
        bg = jnp.pad(b_gates[l].reshape(1, N_HEADS * N_GATES), ((0, 0), (0, LANES - N_HEADS * N_GATES)))
        w_o = w_out[l].astype(BF16)
        w_r = jnp.pad(w_router[l], ((0, 0), (0, LANES - N_EXPERTS)))
        gh = g_hnorm[l].reshape(N_HEADS, HEAD_DIM, 1)

        def mlstm(xs, md, states, grid_rows, full):
            qkv, gates, extra = _inproj(xs, md, g_norm1[l], w_p, bg, full, grid_rows)
            return _mlstm(qkv, gates, states, gh, full, grid_rows), extra

        def sublayers(xs, md, states, grid_rows, final):
            (hr, hc, *st), (o, cacg, fr) = mlstm(xs, md, states, grid_rows, True)
            cv = _conv(cacg, conv_w[l], conv_b[l], conv_ln_g[l], conv_ln_b[l])
            fo = _fourier(fr)
            x1, xn2, aff = _outproj(xs, hr, hc, o, cv, fo, w_o, md, g_norm2[l], w_r)
            return _moe(x1, xn2, aff, w_e_gate, w_e_up, w_e_down, l, md, g_final, final), st

        if last:
            st, _ = mlstm(h_ctx, cmod, zero_states, None, False)
        else:
            h_ctx, st = sublayers(h_ctx, cmod, zero_states, None, False)
        x, _ = sublayers(x, lat, tuple(st), rows, last)
    return x
```

```python
import functools

import jax
import jax.numpy as jnp
import numpy as np
from jax import lax
from jax.experimental import pallas as pl
from jax.experimental.pallas import tpu as pltpu

F32 = jnp.float32
BF16 = jnp.bfloat16
I32 = jnp.int32

D_MODEL = 1024
GRID_W = 64
N_HEADS = 4
HEAD_DIM = 128
MLSTM_WIDTH = N_HEADS * HEAD_DIM
N_ROW_HEADS = N_HEADS // 2
CONV_CH = 256
CONV_K = 31
FOURIER_CH = 256
FOURIER_GROUP_CH = 64
N_GATES = 4
N_EXPERTS = 16
EC_CAPACITY_FACTOR = 2
N_ADA = 6
EPS = 1e-6

LANES = 128
MXU_DIM = 256
MLSTM_CHUNK = 128
MAX_CHUNKS = 16
CONV_HALO = 16
CONV_TAIL = 128
WIN_TILE = 256
MIB = 1024 * 1024

SEC_QKV = (0, 3 * MLSTM_WIDTH)
SEC_G = (SEC_QKV[1], SEC_QKV[1] + LANES)
SEC_O = (SEC_G[1], SEC_G[1] + MLSTM_WIDTH)
SEC_CACG = (SEC_O[1], SEC_O[1] + 2 * CONV_CH)
SEC_FR = (SEC_CACG[1], SEC_CACG[1] + FOURIER_CH)
IN_COLS_PAD = SEC_FR[1]

ST_NF = 2 * HEAD_DIM
ST_NB = ST_NF + 16
ST_ROWS = ST_NB + 16
UP_ROWS = 2 * HEAD_DIM + 16


def _dot(a, b):
    return jnp.dot(a, b, preferred_element_type=F32)


def _dot_nt(a, b):
    return lax.dot_general(a, b, (((1,), (1,)), ((), ())), preferred_element_type=F32)


def _split_bf16(a):
    hi = a.astype(BF16)
    lo = (a - hi.astype(F32)).astype(BF16)
    return hi, lo


def _dot3(a, w):
    ah, al = _split_bf16(a)
    wh, wl = _split_bf16(w)
    return _dot(ah, wh) + (_dot(al, wh) + _dot(ah, wl))


def _modnorm(x, g, sh, sc):
    ms = jnp.mean(x * x, axis=-1, keepdims=True)
    return x * lax.rsqrt(ms + EPS) * g * (1.0 + sc) + sh


def _rows(n_rows, width, *vals):
    rid = lax.broadcasted_iota(I32, (n_rows, width), 0)
    out = jnp.zeros((n_rows, width), F32)
    for i, v in enumerate(vals):
        out = jnp.where(rid == i, v, out)
    return out


def _mod_spec(mod):
    mods, layer, row = mod
    _, _, n_ada, d = mods.shape
    return pl.BlockSpec((None, None, n_ada, d), lambda b, i, *_: (layer, b if row is None else row, 0, 0))


def _params(sem, vmem_mib):
    return pltpu.CompilerParams(dimension_semantics=sem, vmem_limit_bytes=vmem_mib * MIB)


def _ada_kernel(c_ref, w_ref, b_ref, o_ref):
    c = c_ref[...]
    o_ref[...] = _dot3(c * jax.nn.sigmoid(c), w_ref[...]) + b_ref[...]


def _ada(cs, w_ada, b_ada):
    n_layers, d, n6 = w_ada.shape
    r = cs.shape[0]
    tn = 1536
    return pl.pallas_call(
        _ada_kernel,
        grid=(n_layers, n6 // tn),
        in_specs=[
            pl.BlockSpec((r, d), lambda l, j: (0, 0)),
            pl.BlockSpec((None, d, tn), lambda l, j: (l, 0, j)),
            pl.BlockSpec((None, 1, tn), lambda l, j: (l, 0, j)),
        ],
        out_specs=pl.BlockSpec((None, r, tn), lambda l, j: (l, 0, j)),
        out_shape=jax.ShapeDtypeStruct((n_layers, r, n6), F32),
        compiler_params=_params(("arbitrary", "arbitrary"), 40),
        name="ada_mod",
    )(cs, w_ada, b_ada.reshape(n_layers, 1, n6))


def _inproj_kernel(x_ref, mod_ref, g_ref, w_ref, bg_ref, *out_refs, full, colmajor):
    qkv_refs, gates_ref = out_refs[:6], out_refs[6]
    xn = _modnorm(x_ref[...], g_ref[...], mod_ref[0:1, :], mod_ref[1:2, :]).astype(BF16)
    tm = xn.shape[0]
    w = MLSTM_WIDTH
    for i in range(3):
        r = _dot(xn, w_ref[:, i * w:(i + 1) * w])
        if i == 1:
            r = r * HEAD_DIM ** -0.5
        row_ref, col_ref = qkv_refs[2 * i], qkv_refs[2 * i + 1]
        for hh in range(N_ROW_HEADS):
            row_ref[hh] = r[:, hh * HEAD_DIM:(hh + 1) * HEAD_DIM].astype(BF16)
            piece = r[:, (N_ROW_HEADS + hh) * HEAD_DIM:(N_ROW_HEADS + hh + 1) * HEAD_DIM]
            if colmajor:
                piece = jnp.swapaxes(piece.reshape(tm // GRID_W, GRID_W, HEAD_DIM), 0, 1)
            col_ref[hh] = piece.astype(BF16)
    gates_ref[...] = _dot(xn, w_ref[:, SEC_G[0]:SEC_G[1]]) + bg_ref[...]
    if full:
        o_ref, cacg_ref, fr_ref = out_refs[7:]
        o_ref[...] = _dot(xn, w_ref[:, SEC_O[0]:SEC_O[1]])
        cacg_ref[...] = _dot(xn, w_ref[:, SEC_CACG[0]:SEC_CACG[1]])
        fr_ref[...] = _dot(xn, w_ref[:, SEC_FR[0]:SEC_FR[1]]).astype(BF16)


def _inproj(x, mod, g, w, bg, full, rows):
    bsz, t, d = x.shape
    tm = min(t, 1024)
    nh = N_ROW_HEADS
    ncols = IN_COLS_PAD if full else SEC_G[1]
    tok = lambda n: pl.BlockSpec((None, tm, n), lambda b, i: (b, i, 0))
    row_shape = jax.ShapeDtypeStruct((bsz, nh, t, HEAD_DIM), BF16)
    row_spec = pl.BlockSpec((None, nh, tm, HEAD_DIM), lambda b, i: (b, 0, i, 0))
    if rows is None:
        col_shape, col_spec = row_shape, row_spec
    else:
        col_shape = jax.ShapeDtypeStruct((bsz, nh, GRID_W, rows, HEAD_DIM), BF16)
        col_spec = pl.BlockSpec((None, nh, GRID_W, tm // GRID_W, HEAD_DIM), lambda b, i: (b, 0, 0, i, 0))
    out_shape = [row_shape, col_shape] * 3 + [jax.ShapeDtypeStruct((bsz, t, LANES), F32)]
    out_specs = [row_spec, col_spec] * 3 + [tok(LANES)]
    if full:
        out_shape += [jax.ShapeDtypeStruct((bsz, t, MLSTM_WIDTH), F32),
                      jax.ShapeDtypeStruct((bsz, t, 2 * CONV_CH), F32),
                      jax.ShapeDtypeStruct((bsz, t, FOURIER_CH), BF16)]
        out_specs += [tok(MLSTM_WIDTH), tok(2 * CONV_CH), tok(FOURIER_CH)]
    outs = pl.pallas_call(
        functools.partial(_inproj_kernel, full=full, colmajor=rows is not None),
        grid=(bsz, t // tm),
        in_specs=[tok(d), _mod_spec(mod),
                  pl.BlockSpec((1, d), lambda b, i: (0, 0)),
                  pl.BlockSpec((d, ncols), lambda b, i: (0, 0)),
                  pl.BlockSpec((1, LANES), lambda b, i: (0, 0))],
        out_specs=out_specs,
        out_shape=out_shape,
        compiler_params=_params(("parallel", "parallel"), 56),
        name="inproj_full" if full else "inproj_qkvg",
    )(x, mod[0], g.reshape(1, d), w, bg)
    qkv = [a.reshape(bsz, nh, t, HEAD_DIM) for a in outs[:6]]
    return qkv, outs[6], outs[7:]


def _logsig(x):
    return jnp.minimum(x, 0.0) - jnp.log1p(jnp.exp(-jnp.abs(x)))


def _tri_masks(n):
    a0 = lax.broadcasted_iota(I32, (n, n), 0)
    a1 = lax.broadcasted_iota(I32, (n, n), 1)
    return a0 <= a1, a0 >= a1


def _split3(x):
    hi = x.astype(BF16).astype(F32)
    r1 = x - hi
    mid = r1.astype(BF16).astype(F32)
    return hi, mid, r1 - mid


def _mlstm_consts(tri_scr, ones_scr):
    lc = MLSTM_CHUNK
    u = lax.broadcasted_iota(I32, (lc, 2 * lc), 0)
    t2 = lax.broadcasted_iota(I32, (lc, 2 * lc), 1)
    tri_scr[...] = jnp.where(((t2 < lc) & (u <= t2)) | ((t2 >= lc) & (u >= t2 - lc)), 1.0, 0.0).astype(BF16)
    r = lax.broadcasted_iota(I32, (16, 2 * lc), 0)
    t16 = lax.broadcasted_iota(I32, (16, 2 * lc), 1)
    ones_scr[...] = jnp.where(((r < 3) & (t16 < lc)) | ((r >= 3) & (r < 6) & (t16 >= lc)), 1.0, 0.0).astype(BF16)


def _mlstm_gates(head, g_ref, tri_scr, rowg_scr, stat_scr, n_chunks, cm_rows):
    lc = MLSTM_CHUNK
    lane2 = lax.broadcasted_iota(I32, (16, 2 * LANES), 1)
    pick = jnp.where(jnp.bitwise_and(lane2, LANES - 1) == N_GATES * head + lax.broadcasted_iota(I32, (16, 2 * LANES), 0),
                     1.0, 0.0).astype(BF16)
    chunk_id = lax.broadcasted_iota(I32, (MAX_CHUNKS, lc), 0)

    def gather(c, acc):
        if cm_rows is None:
            g = g_ref[pl.ds(pl.multiple_of(c * lc, lc), lc), :]
        else:
            cols = lc // cm_rows
            g = jnp.concatenate([g_ref[pl.ds(c * cols + j, cm_rows, stride=GRID_W), :] for j in range(cols)], axis=0)
        rows = _dot_nt(pick, jnp.concatenate(_split_bf16(g), axis=1))
        return tuple(jnp.where(chunk_id == c, rows[i:i + 1], a) for i, a in enumerate(acc))

    zeros = jnp.zeros((MAX_CHUNKS, lc), F32)
    i_f, f_f, i_b, f_b = lax.fori_loop(0, n_chunks, gather, (zeros,) * N_GATES, unroll=min(4, n_chunks))
    lf_f, lf_b = _logsig(f_f), _logsig(f_b)
    b_f = sum(_dot(part.astype(BF16), tri_scr[:, 0:lc]) for part in _split3(lf_f))
    b_b = sum(_dot(part.astype(BF16), tri_scr[:, lc:]) for part in _split3(lf_b))
    bl_f = jnp.sum(lf_f, axis=1, keepdims=True)
    bl_b = jnp.sum(lf_b, axis=1, keepdims=True)
    ci_f, ci_b = i_f - b_f, i_b - b_b
    ml_f = jnp.max(ci_f + bl_f, axis=1, keepdims=True)
    ml_b = jnp.max(ci_b + bl_b, axis=1, keepdims=True)
    for r, val in enumerate((jnp.exp(ci_f + bl_f - ml_f), jnp.exp(ci_b + bl_b - ml_b), b_f, b_b, ci_f, ci_b)):
        rowg_scr[:, r, :] = val
    for r, val in enumerate((bl_f, bl_b, ml_f, ml_b)):
        stat_scr[:, r, :] = jnp.broadcast_to(val, (MAX_CHUNKS, HEAD_DIM))


def _mlstm_increment(c, k_ref, v_ref, rowg_scr, up_scr, vt_scr):
    lc = MLSTM_CHUNK
    s = pl.multiple_of(c * lc, lc)
    v_t = v_ref[pl.ds(s, lc), :].astype(F32).T
    rg = rowg_scr[c]
    w_f, w_b = rg[0:1], rg[1:2]
    lhs = jnp.concatenate([(v_t * w_f).astype(BF16), (v_t * w_b).astype(BF16),
                           _rows(16, lc, w_f, w_b).astype(BF16)], axis=0)
    up_scr[c] = _dot(lhs, k_ref[pl.ds(s, lc), :])
    if vt_scr is not None:
        vt_scr[c] = v_t.astype(BF16)


def _mlstm_state_step(cc, carry, up_scr, stat_scr, st_scr, mst_scr, fwd):
    cmat, n, m = carry
    off, row = (0, 0) if fwd else (HEAD_DIM, 1)
    if st_scr is not None:
        st_scr[cc, off:off + HEAD_DIM, :] = cmat.astype(BF16)
        nrow = ST_NF if fwd else ST_NB
        st_scr[cc, nrow:nrow + 16, :] = _rows(16, HEAD_DIM, n).astype(BF16)
        mst_scr[cc, row:row + 1, :] = jnp.broadcast_to(m, (1, HEAD_DIM))
    st = stat_scr[cc]
    bl, ml = st[row:row + 1, 0:1], st[2 + row:3 + row, 0:1]
    m_new = jnp.maximum(bl + m, ml)
    keep = jnp.exp(bl + m - m_new)
    gain = jnp.exp(ml - m_new)
    cmat = keep * cmat + gain * up_scr[cc, off:off + HEAD_DIM, :]
    n = keep * n + gain * up_scr[cc, 2 * HEAD_DIM + row:2 * HEAD_DIM + row + 1, :]
    return cmat, n, m_new


def _mlstm_outputs(work, ones_scr):
    lc = MLSTM_CHUNK
    starts = [pl.multiple_of(c * lc, lc) for _, c in work]
    le, ge = _tri_masks(lc)
    qs = [ch[0][pl.ds(s, lc), :] for (ch, _), s in zip(work, starts)]
    qk_ts = [_dot_nt(ch[1][pl.ds(s, lc), :], q) for (ch, _), s, q in zip(work, starts, qs)]
    g_ts = [_dot_nt(ch[5][c], q) for (ch, c), q in zip(work, qs)]
    rgs = [ch[4][c] for ch, c in work]
    cis = [lax.dot_general(_rows(16, lc, *_split3(rg[4:5]), *_split3(rg[5:6])).astype(BF16), ones_scr[...],
                           (((0,), (0,)), ((), ())), preferred_element_type=F32) for rg in rgs]

    def direction(qk_t, valid, b_row, ci, m_st, qn):
        dlog = jnp.where(valid, b_row + ci, -jnp.inf)
        inter = b_row + m_st
        m_t = jnp.maximum(inter, jnp.max(dlog, axis=0, keepdims=True))
        s_t = qk_t * jnp.exp(dlog - m_t)
        w_inter = jnp.exp(inter - m_t)
        den = w_inter * qn + jnp.sum(s_t, axis=0, keepdims=True)
        r = 1.0 / jnp.maximum(jnp.abs(den), jnp.exp(-m_t))
        return r * s_t, r * w_inter

    mixed = []
    for (ch, c), qk_t, g_t, rg, ci in zip(work, qk_ts, g_ts, rgs, cis):
        ms = ch[6][c]
        sf, cf = direction(qk_t, le, rg[2:3], ci[:, 0:lc], ms[0:1, 0:1], g_t[ST_NF:ST_NF + 1])
        sb, cb = direction(qk_t, ge, rg[3:4], ci[:, lc:], ms[1:2, 0:1], g_t[ST_NB:ST_NB + 1])
        mixed.append(((sf + sb).astype(BF16), cf * g_t[0:HEAD_DIM] + cb * g_t[HEAD_DIM:2 * HEAD_DIM]))
    h_ts = [inter_part + _dot(ch[7][c], s_mix) for (ch, c), (s_mix, inter_part) in zip(work, mixed)]
    for (ch, c), s, h_t in zip(work, starts, h_ts):
        gh_ref, h_ref, cm_rows = ch[2], ch[3], ch[8]
        h = (h_t * lax.rsqrt(jnp.mean(h_t * h_t, axis=0, keepdims=True) + EPS) * gh_ref[...]).T
        if cm_rows is None:
            h_ref[pl.ds(s, lc), :] = h
        else:
            cols = lc // cm_rows
            for j in range(cols):
                h_ref[pl.ds(c * cols + j, cm_rows, stride=GRID_W), :] = h[j * cm_rows:(j + 1) * cm_rows]


def _mlstm_kernel(*refs, n_chunks, with_outputs, cm_rows):
    n_slot_in = 6
    g_ref = refs[0]
    slots_in = [refs[1:1 + n_slot_in], refs[1 + n_slot_in:1 + 2 * n_slot_in]]
    rest = refs[1 + 2 * n_slot_in:]
    if with_outputs:
        h_refs, rest = rest[:2], rest[2:]
    state_out, scratch = rest[:4], rest[4:]
    tri_scr, ones_scr, scratch = scratch[0], scratch[1], scratch[2:]
    n_scr = 6 if with_outputs else 3
    slots_scr = [scratch[:n_scr], scratch[n_scr:2 * n_scr]]
    slot_rows = [None, cm_rows]
    _mlstm_consts(tri_scr, ones_scr)
    for slot, scr in enumerate(slots_scr):
        _mlstm_gates(pl.program_id(1) + slot * N_ROW_HEADS, g_ref, tri_scr, scr[0], scr[1], n_chunks, slot_rows[slot])

    def increments(c, _):
        for (q_ref, k_ref, v_ref, c0_ref, n0_ref, gh_ref), scr in zip(slots_in, slots_scr):
            _mlstm_increment(c, k_ref, v_ref, scr[0], scr[2], scr[5] if with_outputs else None)
        return 0

    lax.fori_loop(0, n_chunks, increments, 0, unroll=min(4, n_chunks))

    def states(i, carry):
        out = []
        for slot, scr in enumerate(slots_scr):
            st_scr, mst_scr = (scr[3], scr[4]) if with_outputs else (None, None)
            out.append(_mlstm_state_step(i, carry[2 * slot], scr[2], scr[1], st_scr, mst_scr, True))
            out.append(_mlstm_state_step(n_chunks - 1 - i, carry[2 * slot + 1], scr[2], scr[1], st_scr, mst_scr, False))
        return tuple(out)

    init = []
    for (q_ref, k_ref, v_ref, c0_ref, n0_ref, gh_ref) in slots_in:
        init.append((c0_ref[0], n0_ref[0:1, :], n0_ref[2:3, 0:1]))
        init.append((c0_ref[1], n0_ref[1:2, :], n0_ref[3:4, 0:1]))
    final = lax.fori_loop(0, n_chunks, states, tuple(init))
    for slot in range(2):
        c_out, n_out = state_out[2 * slot], state_out[2 * slot + 1]
        (cf, nf, mf), (cb, nb, mb) = final[2 * slot], final[2 * slot + 1]
        c_out[0] = cf
        c_out[1] = cb
        n_out[...] = _rows(8, HEAD_DIM, nf, nb, mf, mb)

    if with_outputs:
        chains = [(q_ref, k_ref, gh_ref, h_refs[slot], scr[0], scr[3], scr[4], scr[5], slot_rows[slot])
                  for slot, ((q_ref, k_ref, v_ref, c0_ref, n0_ref, gh_ref), scr) in enumerate(zip(slots_in, slots_scr))]

        per_iter = 4 if n_chunks % 4 == 0 else 2 if n_chunks % 2 == 0 else 1

        def outputs(i, _):
            _mlstm_outputs([(ch, i * per_iter + u) for u in range(per_iter) for ch in chains], ones_scr)
            return 0

        lax.fori_loop(0, n_chunks // per_iter, outputs, 0, unroll=min(2, n_chunks // per_iter))


def _mlstm(qkv, gates, states, gh, with_outputs, cm_rows):
    bsz, npair, t, dh = qkv[0].shape
    nc = t // MLSTM_CHUNK
    lc = MLSTM_CHUNK
    seq = pl.BlockSpec((None, None, t, dh), lambda b, p: (b, p, 0, 0))
    cst = pl.BlockSpec((None, None, 2, dh, dh), lambda b, p: (b, p, 0, 0, 0))
    nst = pl.BlockSpec((None, None, 8, dh), lambda b, p: (b, p, 0, 0))

    def slot_specs(head_off):
        return [seq, seq, seq, cst, nst, pl.BlockSpec((None, dh, 1), lambda b, p: (p + head_off, 0, 0))]

    st_shapes = [jax.ShapeDtypeStruct((bsz, npair, 2, dh, dh), F32), jax.ShapeDtypeStruct((bsz, npair, 8, dh), F32)] * 2
    out_shape, out_specs = st_shapes, [cst, nst, cst, nst]
    assert nc <= MAX_CHUNKS
    slot_scratch = [pltpu.VMEM((MAX_CHUNKS, 8, lc), F32), pltpu.VMEM((MAX_CHUNKS, 8, dh), F32),
                    pltpu.VMEM((nc, UP_ROWS, dh), F32)]
    if with_outputs:
        h_shape = jax.ShapeDtypeStruct((bsz, t, npair * dh), F32)
        h_spec = pl.BlockSpec((None, t, dh), lambda b, p: (b, 0, p))
        out_shape = [h_shape, h_shape] + out_shape
        out_specs = [h_spec, h_spec] + out_specs
        slot_scratch += [pltpu.VMEM((nc, ST_ROWS, dh), BF16), pltpu.VMEM((nc, 8, dh), F32), pltpu.VMEM((nc, dh, lc), BF16)]
    q_r, q_c, k_r, k_c, v_r, v_c = qkv
    c_r, n_r, c_c, n_c = states
    return pl.pallas_call(
        functools.partial(_mlstm_kernel, n_chunks=nc, with_outputs=with_outputs, cm_rows=cm_rows),
        grid=(bsz, npair),
        in_specs=[pl.BlockSpec((None, t, LANES), lambda b, p: (b, 0, 0))] + slot_specs(0) + slot_specs(N_ROW_HEADS),
        out_specs=out_specs,
        out_shape=out_shape,
        scratch_shapes=[pltpu.VMEM((lc, 2 * lc), BF16), pltpu.VMEM((16, 2 * lc), BF16)] + slot_scratch * 2,
        compiler_params=_params(("parallel", "parallel"), 48),
        name="mlstm_full" if with_outputs else "mlstm_states",
    )(gates, q_r, k_r, v_r, c_r, n_r, gh, q_c, k_c, v_c, c_c, n_c, gh)


def _conv_kernel(cacg_ref, w_ref, cb_ref, lg_ref, lb_ref, o_ref, ysh, yconv, *, t):
    rt = 64
    ysh[0, 0:CONV_HALO, :] = jnp.zeros((CONV_HALO, CONV_CH), F32)
    ysh[0, CONV_HALO + t:t + CONV_TAIL, :] = jnp.zeros((CONV_TAIL - CONV_HALO, CONV_CH), F32)

    def fill(r, _):
        s = pl.multiple_of(r * rt, rt)
        a = cacg_ref[pl.ds(s, rt), 0:CONV_CH]
        g = cacg_ref[pl.ds(s, rt), CONV_CH:2 * CONV_CH]
        ysh[0, pl.ds(CONV_HALO + s, rt), :] = a * jax.nn.sigmoid(g)
        return 0

    lax.fori_loop(0, t // rt, fill, 0)

    def shift(r, _):
        s = pl.multiple_of(r * rt, rt)
        win = ysh[0, pl.ds(s, rt + 8), :]
        for res in range(1, 8):
            ysh[res, pl.ds(s, rt), :] = pltpu.roll(win, rt + 8 - res, axis=0)[0:rt, :]
        return 0

    lax.fori_loop(0, t // rt + 1, shift, 0)

    def tile(r, _):
        s = pl.multiple_of(r * rt, rt)
        acc = jnp.zeros((rt, CONV_CH), F32)
        for kk in range(CONV_K):
            off = kk + CONV_HALO - CONV_K // 2
            rows = ysh[off % 8, pl.ds(pl.multiple_of(s + 8 * (off // 8), 8), rt), :]
            acc = acc + w_ref[kk:kk + 1, :] * rows
        yconv[pl.ds(s, rt), :] = acc + cb_ref[...]
        return 0

    lax.fori_loop(0, t // rt, tile, 0)

    def norm(r, _):
        s = pl.multiple_of(r * rt, rt)
        y = yconv[pl.ds(s, rt), :]
        mu = jnp.mean(y, axis=-1, keepdims=True)
        yc = y - mu
        var = jnp.mean(yc * yc, axis=-1, keepdims=True)
        z = yc * lax.rsqrt(var + EPS) * lg_ref[...] + lb_ref[...]
        o_ref[pl.ds(s, rt), :] = (z * jax.nn.sigmoid(z)).astype(BF16)
        return 0

    lax.fori_loop(0, t // rt, norm, 0, unroll=4)


def _conv(cacg, conv_w, conv_b, ln_g, ln_b):
    bsz, t, _ = cacg.shape
    row = lambda a: a.reshape(1, CONV_CH)
    vec = pl.BlockSpec((1, CONV_CH), lambda b: (0, 0))
    return pl.pallas_call(
        functools.partial(_conv_kernel, t=t),
        grid=(bsz,),
        in_specs=[pl.BlockSpec((None, t, 2 * CONV_CH), lambda b: (b, 0, 0)),
                  pl.BlockSpec((CONV_K + 1, CONV_CH), lambda b: (0, 0)), vec, vec, vec],
        out_specs=pl.BlockSpec((None, t, CONV_CH), lambda b: (b, 0, 0)),
        out_shape=jax.ShapeDtypeStruct((bsz, t, CONV_CH), BF16),
        scratch_shapes=[pltpu.VMEM((8, t + CONV_TAIL, CONV_CH), F32), pltpu.VMEM((t, CONV_CH), F32)],
        compiler_params=_params(("parallel",), 40),
        name="conformer_conv",
    )(cacg, jnp.pad(conv_w, ((0, 1), (0, 0))), row(conv_b), row(ln_g), row(ln_b))


@functools.lru_cache(maxsize=None)
def _dft_tables(t):
    j = np.arange(t, dtype=np.int64)
    ang = 2.0 * np.pi * ((j[:, None] * j[None, :]) % t).astype(np.float64) / t
    wt = np.concatenate([np.cos(ang), -np.sin(ang)], axis=1).astype(np.float32)
    c = np.arange(FOURIER_CH, dtype=np.int64)
    grp, idx = c // FOURIER_GROUP_CH, c % FOURIER_GROUP_CH
    same = grp[:, None] == grp[None, :]
    angc = 2.0 * np.pi * ((idx[:, None] * idx[None, :]) % FOURIER_GROUP_CH).astype(np.float64) / FOURIER_GROUP_CH
    cc = np.where(same, np.cos(angc), 0.0).astype(np.float32)
    sc = np.where(same, np.sin(angc), 0.0).astype(np.float32)
    return wt, cc, sc


def _fourier_kernel(z_ref, cc_ref, sc_ref, wt_ref, o_ref, zcs, *, t, scale):
    bsz, _, ch = z_ref.shape

    @pl.when(pl.program_id(0) == 0)
    def _():
        for b in range(bsz):
            zcs[0:t, b * ch:(b + 1) * ch] = _dot(z_ref[b], cc_ref[...]).astype(BF16)
            zcs[t:2 * t, b * ch:(b + 1) * ch] = _dot(z_ref[b], sc_ref[...]).astype(BF16)

    r = _dot(wt_ref[...], zcs[...]) * scale
    for b in range(bsz):
        o_ref[b] = r[:, b * ch:(b + 1) * ch].astype(BF16)


def _fourier(z):
    bsz, t, ch = z.shape
    wt, cc, sc = _dft_tables(t)
    tm = min(t, 512)
    mat = pl.BlockSpec((ch, ch), lambda i: (0, 0))
    return pl.pallas_call(
        functools.partial(_fourier_kernel, t=t, scale=float((t * FOURIER_GROUP_CH) ** -0.5)),
        grid=(t // tm,),
        in_specs=[pl.BlockSpec((bsz, t, ch), lambda i: (0, 0, 0)), mat, mat,
                  pl.BlockSpec((tm, 2 * t), lambda i: (i, 0))],
        out_specs=pl.BlockSpec((bsz, tm, ch), lambda i: (0, i, 0)),
        out_shape=jax.ShapeDtypeStruct((bsz, t, ch), BF16),
        scratch_shapes=[pltpu.VMEM((2 * t, bsz * ch), BF16)],
        compiler_params=_params(("arbitrary",), 56),
        name="fourier_mix",
    )(z, jnp.asarray(cc).astype(BF16), jnp.asarray(sc).astype(BF16), jnp.asarray(wt).astype(BF16))


def _outproj_kernel(x_ref, hr_ref, hc_ref, o_ref, cv_ref, fo_ref, w_ref, mod_ref, g2_ref, wr_ref,
                    x1_ref, xn_ref, aff_ref):
    half = MLSTM_WIDTH // 2
    a, b = MLSTM_WIDTH, MLSTM_WIDTH + CONV_CH
    og = jax.nn.sigmoid(o_ref[...])
    y = (_dot((hr_ref[...] * og[:, 0:half]).astype(BF16), w_ref[0:half, :])
         + _dot((hc_ref[...] * og[:, half:a]).astype(BF16), w_ref[half:a, :])
         + _dot(cv_ref[...], w_ref[a:b, :]) + _dot(fo_ref[...], w_ref[b:, :]))
    x1 = x_ref[...] + mod_ref[2:3, :] * y
    x1_ref[...] = x1
    xn = _modnorm(x1, g2_ref[...], mod_ref[3:4, :], mod_ref[4:5, :])
    xn_ref[...] = xn.astype(BF16)
    tm = xn.shape[0]
    r = _dot(jnp.concatenate(_split_bf16(xn), axis=0), jnp.concatenate(_split_bf16(wr_ref[...]), axis=1))
    logits = (r[:tm, :LANES] + r[:tm, LANES:]) + (r[tm:, :LANES] + r[tm:, LANES:])
    lane = lax.broadcasted_iota(I32, logits.shape, 1)
    logits = jnp.where(lane < N_EXPERTS, logits, -jnp.inf)
    e = jnp.exp(logits - jnp.max(logits, axis=-1, keepdims=True))
    aff_ref[...] = e / jnp.sum(e, axis=-1, keepdims=True)


def _outproj(x, hr, hc, o, cv, fo, w_out, mod, g2, wr):
    bsz, t, d = x.shape
    tm = min(t, 512)
    tok = lambda n: pl.BlockSpec((None, tm, n), lambda b, i: (b, i, 0))
    return pl.pallas_call(
        _outproj_kernel,
        grid=(bsz, t // tm),
        in_specs=[tok(d), tok(MLSTM_WIDTH // 2), tok(MLSTM_WIDTH // 2), tok(MLSTM_WIDTH), tok(CONV_CH), tok(FOURIER_CH),
                  pl.BlockSpec((d, d), lambda b, i: (0, 0)), _mod_spec(mod),
                  pl.BlockSpec((1, d), lambda b, i: (0, 0)),
                  pl.BlockSpec((d, LANES), lambda b, i: (0, 0))],
        out_specs=[tok(d), tok(d), tok(LANES)],
        out_shape=[jax.ShapeDtypeStruct((bsz, t, d), F32), jax.ShapeDtypeStruct((bsz, t, d), BF16),
                   jax.ShapeDtypeStruct((bsz, t, LANES), F32)],
        compiler_params=_params(("parallel", "parallel"), 40),
        name="outproj_router",
    )(x, hr, hc, o, cv, fo, w_out, mod[0], g2.reshape(1, d), wr)


def _select_kernel(a_ref, sp_ref, win_ref, *, cap, tile):
    a = a_ref[...]
    n = a.shape[0]
    blk = min(n, MXU_DIM)

    def count(mask):
        return jnp.sum(jnp.where(mask, 1.0, 0.0), axis=0, keepdims=True)

    def search(i, thr):
        cand = thr | jnp.left_shift(jnp.int32(1), 30 - i)
        return jnp.where(count(a >= lax.bitcast_convert_type(cand, F32)) >= cap, cand, thr)

    thr = lax.bitcast_convert_type(lax.fori_loop(0, 31, search, jnp.zeros((1, LANES), I32)), F32)
    gt = a > thr
    eq = a == thr
    before = jnp.where(lax.broadcasted_iota(I32, (blk, blk), 1) < lax.broadcasted_iota(I32, (blk, blk), 0), 1.0, 0.0).astype(BF16)

    def excl_cumsum(mask):
        x = jnp.where(mask, 1.0, 0.0)
        run = jnp.zeros((1, LANES), F32)
        parts = []
        for j in range(n // blk):
            xb = x[j * blk:(j + 1) * blk]
            parts.append(_dot(before, xb.astype(BF16)) + run)
            run = run + jnp.sum(xb, axis=0, keepdims=True)
        return jnp.concatenate(parts, axis=0)

    sel = gt | (eq & (excl_cumsum(eq) < cap - count(gt)))
    pos = excl_cumsum(sel).astype(I32)
    sp_ref[...] = jnp.where(sel, pos, -1)
    rid = lax.broadcasted_iota(I32, win_ref.shape, 0)
    win = jnp.zeros(win_ref.shape, I32)
    for ti in range(n // tile):
        t_sel, t_pos = sel[ti * tile:(ti + 1) * tile], pos[ti * tile:(ti + 1) * tile]
        win = jnp.where(rid == 2 * ti, jnp.min(jnp.where(t_sel, t_pos, cap), axis=0, keepdims=True), win)
        win = jnp.where(rid == 2 * ti + 1, jnp.max(jnp.where(t_sel, t_pos + 1, 0), axis=0, keepdims=True), win)
    win_ref[...] = win


def _select(aff_sets, cap, tile):
    ng, n, _ = aff_sets.shape
    win_rows = max(8, 2 * (n // tile))
    blk = pl.BlockSpec((None, n, LANES), lambda g: (g, 0, 0))
    return pl.pallas_call(
        functools.partial(_select_kernel, cap=cap, tile=tile),
        grid=(ng,),
        in_specs=[blk],
        out_specs=[blk, pl.BlockSpec((None, win_rows, LANES), lambda g: (g, 0, 0))],
        out_shape=[jax.ShapeDtypeStruct((ng, n, LANES), I32), jax.ShapeDtypeStruct((ng, win_rows, LANES), I32)],
        compiler_params=_params(("parallel",), 32),
        name="ec_select",
    )(aff_sets)


def _slot_window(win_ref, b, tile_idx, e, cap, width):
    lo = win_ref[b, 2 * tile_idx * N_EXPERTS + e]
    hi = win_ref[b, (2 * tile_idx + 1) * N_EXPERTS + e]
    st = jnp.minimum(lax.shift_left(lax.shift_right_logical(lo, 4), 4), cap - width)
    return pl.multiple_of(st, 16), hi - st <= width


def _ffn_kernel(win_ref, sp_ref, ar_ref, xn_ref, wg_ref, wu_ref, wd_ref, ys_ref, wgu_b, wd_b, xs_scr, gate_scr, *, cap, windowed):
    g, j = pl.program_id(0), pl.program_id(1)
    gsz, rows = wg_ref.shape[0], wg_ref.shape[1]
    nb, _, _, n = sp_ref.shape

    def cast_slice():
        rt = min(256, rows)
        dst = lax.rem(g, 2)
        for piece in range(rows // rt):
            src_rows = pl.ds(piece * rt, rt)
            dst_rows = pl.ds(pl.multiple_of(j * rows + piece * rt, rt), rt)
            for k in range(gsz):
                wgu_b[dst, k, 0, dst_rows, :] = wg_ref[k, src_rows, :].astype(BF16)
                wgu_b[dst, k, 1, dst_rows, :] = wu_ref[k, src_rows, :].astype(BF16)
                wd_b[dst, k, dst_rows, :] = wd_ref[k, src_rows, :].astype(BF16)

    def full_gather(k, bi):
        hit = lax.broadcasted_iota(I32, (cap, n), 0) == sp_ref[bi, k]
        xs_scr[k, bi * cap:(bi + 1) * cap, :] = _dot(jnp.where(hit, 1.0, 0.0).astype(BF16), xn_ref[bi])
        gate_scr[k, bi * cap:(bi + 1) * cap, :] = jnp.broadcast_to(
            jnp.sum(jnp.where(hit, ar_ref[bi, k], 0.0), axis=1, keepdims=True), (cap, LANES))

    def experts_ffn():
        src = lax.rem(g + 1, 2)
        xs = [xs_scr[k].astype(BF16) for k in range(gsz)]
        h1 = [_dot(xs[k], wgu_b[src, k, 0]) for k in range(gsz)]
        h2 = [_dot(xs[k], wgu_b[src, k, 1]) for k in range(gsz)]
        hid = [(h1[k] * jax.nn.sigmoid(h1[k]) * h2[k]).astype(BF16) for k in range(gsz)]
        ys = [(_dot(hid[k], wd_b[src, k]) * gate_scr[k, :, 0:1]).astype(BF16) for k in range(gsz)]
        for k in range(gsz):
            for bi in range(nb):
                ys_ref[bi, k] = ys[k][bi * cap:(bi + 1) * cap]

    def full_path():
        for k in range(gsz):
            for bi in range(nb):
                full_gather(k, bi)
        experts_ffn()

    @pl.when(g == 0)
    def _():
        cast_slice()

    if not windowed:
        @pl.when(g > 0)
        def _():
            cast_slice()
            full_path()
        return

    assert nb == 1
    width = cap // 2
    wins = [[_slot_window(win_ref, j, t, jnp.maximum(g - 1, 0) * gsz + k, cap, width) for t in range(n // WIN_TILE)]
            for k in range(gsz)]
    fits = functools.reduce(jnp.logical_and, [ok for per_expert in wins for _, ok in per_expert])

    @pl.when(jnp.logical_and(g > 0, fits))
    def _():
        cast_slice()
        xs_scr[...] = jnp.zeros(xs_scr.shape, F32)
        gate_scr[...] = jnp.zeros(gate_scr.shape, F32)
        for t in range(n // WIN_TILE):
            tok = slice(t * WIN_TILE, (t + 1) * WIN_TILE)
            for k in range(gsz):
                st = wins[k][t][0]
                hit = lax.broadcasted_iota(I32, (width, WIN_TILE), 0) + st == sp_ref[0, k, :, tok]
                xs_scr[k, pl.ds(st, width), :] += _dot(jnp.where(hit, 1.0, 0.0).astype(BF16), xn_ref[0, tok, :])
                gate_scr[k, pl.ds(st, width), :] += jnp.broadcast_to(
                    jnp.sum(jnp.where(hit, ar_ref[0, k, :, tok], 0.0), axis=1, keepdims=True), (width, LANES))
        experts_ffn()

    @pl.when(jnp.logical_and(g > 0, jnp.logical_not(fits)))
    def _():
        cast_slice()
        full_path()


def _ffn(win, sp_row, aff_row, xn, wg, wu, wd, layer, cap):
    bsz, n, d = xn.shape
    _, ne, _, hid = wg.shape
    assert d == hid
    nb = min(bsz, max(1, MXU_DIM // cap))
    nj = bsz // nb
    gsz = 2 if nj > 1 else 1
    rows = d // nj
    windowed = nb == 1 and n > WIN_TILE and cap // 2 >= LANES
    batch = lambda g, j: jnp.where(g == 0, 0, j)
    row = pl.BlockSpec((nb, gsz, 1, n), lambda g, j, *_: (batch(g, j), jnp.maximum(g - 1, 0), 0, 0))
    wspec = pl.BlockSpec((None, gsz, rows, hid), lambda g, j, *_: (layer, jnp.minimum(g, ne // gsz - 1), j, 0))
    grid_spec = pltpu.PrefetchScalarGridSpec(
        num_scalar_prefetch=1,
        grid=(ne // gsz + 1, nj),
        in_specs=[row, row, pl.BlockSpec((nb, n, d), lambda g, j, *_: (batch(g, j), 0, 0)), wspec, wspec, wspec],
        out_specs=pl.BlockSpec((nb, gsz, cap, d), lambda g, j, *_: (batch(g, j), jnp.maximum(g - 1, 0), 0, 0)),
        scratch_shapes=[pltpu.VMEM((2, gsz, 2, d, hid), BF16), pltpu.VMEM((2, gsz, hid, d), BF16),
                        pltpu.VMEM((gsz, nb * cap, d), F32), pltpu.VMEM((gsz, nb * cap, LANES), F32)])
    return pl.pallas_call(
        functools.partial(_ffn_kernel, cap=cap, windowed=windowed),
        grid_spec=grid_spec,
        out_shape=jax.ShapeDtypeStruct((bsz, ne, cap, d), BF16),
        compiler_params=_params(("arbitrary", "arbitrary"), 56),
        name="ec_ffn",
    )(win, sp_row, aff_row, xn, wg, wu, wd)


def _combine_kernel(win_ref, x_ref, spc_ref, ys_ref, mod_ref, gf_ref, o_ref, *, cap, final, pack):
    b, i = pl.program_id(0), pl.program_id(1)
    tm = spc_ref.shape[0]
    sub = tm if pack == 1 else WIN_TILE

    def finish(rows, acc):
        out = x_ref[rows, :] + mod_ref[5:6, :] * acc
        if final:
            out = out * lax.rsqrt(jnp.mean(out * out, axis=-1, keepdims=True) + EPS) * gf_ref[...]
        o_ref[rows, :] = out

    def full_width(spc):
        slot = lax.broadcasted_iota(I32, (sub, cap), 1)
        acc = jnp.zeros((sub, x_ref.shape[1]), F32)
        for e in range(N_EXPERTS):
            hit = jnp.where(spc[:, e:e + 1] == slot, 1.0, 0.0).astype(BF16)
            acc = acc + _dot(hit, ys_ref[e])
        return acc

    n_sub = tm // sub
    rows = [slice(t * sub, (t + 1) * sub) for t in range(n_sub)]
    spcs = [spc_ref[r, :] for r in rows]
    if pack == 1:
        for r, spc in zip(rows, spcs):
            finish(r, full_width(spc))
        return

    w = cap // pack
    wins = [[_slot_window(win_ref, b, i * n_sub + t, e, cap, w) for e in range(N_EXPERTS)] for t in range(n_sub)]
    fits = functools.reduce(jnp.logical_and, [ok for per_tile in wins for _, ok in per_tile])

    @pl.when(fits)
    def _():
        lane = lax.broadcasted_iota(I32, (sub, cap), 1)
        lane_row = lax.broadcasted_iota(I32, (1, cap), 1)
        accs = [jnp.zeros((sub, x_ref.shape[1]), F32) for _ in range(n_sub)]
        for p in range(N_EXPERTS // pack):
            es = list(range(p * pack, (p + 1) * pack))
            hits, ysws = [], []
            for t in range(n_sub):
                col = spcs[t][:, es[0]:es[0] + 1]
                off = jnp.zeros((1, cap), I32) + wins[t][es[0]][0]
                for q, e in enumerate(es[1:], 1):
                    col = jnp.where(lane >= q * w, spcs[t][:, e:e + 1], col)
                    off = jnp.where(lane_row >= q * w, wins[t][e][0] - q * w, off)
                hits.append(jnp.where(col - off == lane, 1.0, 0.0).astype(BF16))
                ysws.append(jnp.concatenate([ys_ref[e, pl.ds(wins[t][e][0], w), :] for e in es], axis=0))
            accs = [acc + _dot(hit, ysw) for acc, hit, ysw in zip(accs, hits, ysws)]
        for r, acc in zip(rows, accs):
            finish(r, acc)

    @pl.when(jnp.logical_not(fits))
    def _():
        for r, spc in zip(rows, spcs):
            finish(r, full_width(spc))


def _combine(x, sp_col, win, ys, mod, g_final, cap, final):
    bsz, n, d = x.shape
    pack = 4 if n > WIN_TILE and cap // 4 >= LANES // 2 else 1
    tm = min(n, 512)
    grid_spec = pltpu.PrefetchScalarGridSpec(
        num_scalar_prefetch=1,
        grid=(bsz, n // tm),
        in_specs=[pl.BlockSpec((None, tm, d), lambda b, i, *_: (b, i, 0)),
                  pl.BlockSpec((None, tm, N_EXPERTS), lambda b, i, *_: (b, i, 0)),
                  pl.BlockSpec((None, N_EXPERTS, cap, d), lambda b, i, *_: (b, 0, 0, 0)),
                  _mod_spec(mod),
                  pl.BlockSpec((1, d), lambda b, i, *_: (0, 0))],
        out_specs=pl.BlockSpec((None, tm, d), lambda b, i, *_: (b, i, 0)))
    return pl.pallas_call(
        functools.partial(_combine_kernel, cap=cap, final=final, pack=pack),
        grid_spec=grid_spec,
        out_shape=jax.ShapeDtypeStruct((bsz, n, d), F32),
        compiler_params=_params(("parallel", "parallel"), 48),
        name="ec_combine_final" if final else "ec_combine",
    )(win, x, sp_col, ys, mod[0], g_final.reshape(1, d))


def _moe(x1, xn2, aff, wg, wu, wd, layer, mod, g_final, final):
    bsz, n, _ = x1.shape
    cap = EC_CAPACITY_FACTOR * n // N_EXPERTS
    per_group = LANES // N_EXPERTS
    ng = -(-bsz // per_group)
    aff_e = aff[:, :, :N_EXPERTS]
    sets = jnp.pad(aff_e, ((0, ng * per_group - bsz), (0, 0), (0, 0)))
    sets = sets.reshape(ng, per_group, n, N_EXPERTS).transpose(0, 2, 1, 3).reshape(ng, n, LANES)
    tile = min(n, WIN_TILE)
    selpos, win = _select(sets, cap, tile)
    ungroup = lambda a: a.reshape(ng, -1, per_group, N_EXPERTS).transpose(0, 2, 1, 3).reshape(ng * per_group, -1, N_EXPERTS)[:bsz]
    sp_col = ungroup(selpos)
    win = ungroup(win[:, :2 * (n // tile)]).reshape(bsz, -1)
    sp_row = sp_col.transpose(0, 2, 1).reshape(bsz, N_EXPERTS, 1, n)
    aff_row = aff_e.transpose(0, 2, 1).reshape(bsz, N_EXPERTS, 1, n)
    ys = _ffn(win, sp_row, aff_row, xn2, wg, wu, wd, layer, cap)
    return _combine(x1, sp_col, win, ys, mod, g_final, cap, final)


def kernel(x, c, ctx, c_ctx, w_ada, b_ada, g_norm1, w_in, b_gates, g_hnorm, conv_w, conv_b, conv_ln_g, conv_ln_b,
           w_out, g_norm2, w_router, w_e_gate, w_e_up, w_e_down, g_final):
    bsz, t, d = x.shape
    depth = w_ada.shape[0]
    rows = t // GRID_W
    n_cond = 16
    cs = jnp.concatenate([c, c_ctx[None, :], jnp.zeros((n_cond - bsz - 1, d), F32)], axis=0)
    mods = _ada(cs, w_ada, b_ada).reshape(depth, n_cond, N_ADA, d)

    npair = N_HEADS // 2
    zero_states = (jnp.zeros((bsz, npair, 2, HEAD_DIM, HEAD_DIM), F32), jnp.zeros((bsz, npair, 8, HEAD_DIM), F32)) * 2
    h_ctx = ctx
    for l in range(depth):
        last = l == depth - 1
        lat, cmod = (mods, l, None), (mods, l, bsz)
        w_l = w_in[l]
        qkvg_cols = 3 * MLSTM_WIDTH + N_GATES * N_HEADS
        w_p = jnp.concatenate([w_l[:, :3 * MLSTM_WIDTH],
                               jnp.pad(w_l[:, 3 * MLSTM_WIDTH:qkvg_cols], ((0, 0), (0, LANES - N_GATES * N_HEADS))),
                               w_l[:, qkvg_cols:]], axis=1).astype(BF16)
        bg = jnp.pad(b_gates[l].reshape(1, N_HEADS * N_GATES), ((0, 0), (0, LANES - N_HEADS * N_GATES)))
        w_o = w_out[l].astype(BF16)
        w_r = jnp.pad(w_router[l], ((0, 0), (0, LANES - N_EXPERTS)))
        gh = g_hnorm[l].reshape(N_HEADS, HEAD_DIM, 1)

        def mlstm(xs, md, states, grid_rows, full):
            qkv, gates, extra = _inproj(xs, md, g_norm1[l], w_p, bg, full, grid_rows)
            return _mlstm(qkv, gates, states, gh, full, grid_rows), extra

        def sublayers(xs, md, states, grid_rows, final):
            (hr, hc, *st), (o, cacg, fr) = mlstm(xs, md, states, grid_rows, True)
            cv = _conv(cacg, conv_w[l], conv_b[l], conv_ln_g[l], conv_ln_b[l])
            fo = _fourier(fr)
            x1, xn2, aff = _outproj(xs, hr, hc, o, cv, fo, w_o, md, g_norm2[l], w_r)
            return _moe(x1, xn2, aff, w_e_gate, w_e_up, w_e_down, l, md, g_final, final), st

        if last:
            st, _ = mlstm(h_ctx, cmod, zero_states, None, False)
        else:
            h_ctx, st = sublayers(h_ctx, cmod, zero_states, None, False)
        x, _ = sublayers(x, lat, tuple(st), rows, last)
    return x
```

```python
import functools

import jax
import jax.numpy as jnp
import numpy as np
from jax import lax
from jax.experimental import pallas as pl
from jax.experimental.pallas import tpu as pltpu

F32 = jnp.float32
BF16 = jnp.bfloat16
I32 = jnp.int32

D_MODEL = 1024
GRID_W = 64
N_HEADS = 4
HEAD_DIM = 128
MLSTM_WIDTH = N_HEADS * HEAD_DIM
N_ROW_HEADS = N_HEADS // 2
CONV_CH = 256
CONV_K = 31
FOURIER_CH = 256
FOURIER_GROUP_CH = 64
N_GATES = 4
N_EXPERTS = 16
EC_CAPACITY_FACTOR = 2
N_ADA = 6
EPS = 1e-6

LANES = 128
MXU_DIM = 256
MLSTM_CHUNK = 128
MAX_CHUNKS = 16
CONV_HALO = 16
CONV_TAIL = 128
WIN_TILE = 256
MIB = 1024 * 1024

SEC_QKV = (0, 3 * MLSTM_WIDTH)
SEC_G = (SEC_QKV[1], SEC_QKV[1] + LANES)
SEC_O = (SEC_G[1], SEC_G[1] + MLSTM_WIDTH)
SEC_CACG = (SEC_O[1], SEC_O[1] + 2 * CONV_CH)
SEC_FR = (SEC_CACG[1], SEC_CACG[1] + FOURIER_CH)
IN_COLS_PAD = SEC_FR[1]

ST_NF = 2 * HEAD_DIM
ST_NB = ST_NF + 16
ST_ROWS = ST_NB + 16
UP_ROWS = 2 * HEAD_DIM + 16


def _dot(a, b):
    return jnp.dot(a, b, preferred_element_type=F32)


def _dot_nt(a, b):
    return lax.dot_general(a, b, (((1,), (1,)), ((), ())), preferred_element_type=F32)


def _split_bf16(a):
    hi = a.astype(BF16)
    lo = (a - hi.astype(F32)).astype(BF16)
    return hi, lo


def _dot3(a, w):
    ah, al = _split_bf16(a)
    wh, wl = _split_bf16(w)
    return _dot(ah, wh) + (_dot(al, wh) + _dot(ah, wl))


def _modnorm(x, g, sh, sc):
    ms = jnp.mean(x * x, axis=-1, keepdims=True)
    return x * lax.rsqrt(ms + EPS) * g * (1.0 + sc) + sh


def _rows(n_rows, width, *vals):
    rid = lax.broadcasted_iota(I32, (n_rows, width), 0)
    out = jnp.zeros((n_rows, width), F32)
    for i, v in enumerate(vals):
        out = jnp.where(rid == i, v, out)
    return out


def _mod_spec(mod):
    mods, layer, row = mod
    _, _, n_ada, d = mods.shape
    return pl.BlockSpec((None, None, n_ada, d), lambda b, i, *_: (layer, b if row is None else row, 0, 0))


def _params(sem, vmem_mib):
    return pltpu.CompilerParams(dimension_semantics=sem, vmem_limit_bytes=vmem_mib * MIB)


def _ada_kernel(c_ref, w_ref, b_ref, o_ref):
    c = c_ref[...]
    o_ref[...] = _dot3(c * jax.nn.sigmoid(c), w_ref[...]) + b_ref[...]


def _ada(cs, w_ada, b_ada):
    n_layers, d, n6 = w_ada.shape
    r = cs.shape[0]
    tn = 1536
    return pl.pallas_call(
        _ada_kernel,
        grid=(n_layers, n6 // tn),
        in_specs=[
            pl.BlockSpec((r, d), lambda l, j: (0, 0)),
            pl.BlockSpec((None, d, tn), lambda l, j: (l, 0, j)),
            pl.BlockSpec((None, 1, tn), lambda l, j: (l, 0, j)),
        ],
        out_specs=pl.BlockSpec((None, r, tn), lambda l, j: (l, 0, j)),
        out_shape=jax.ShapeDtypeStruct((n_layers, r, n6), F32),
        compiler_params=_params(("arbitrary", "arbitrary"), 40),
        name="ada_mod",
    )(cs, w_ada, b_ada.reshape(n_layers, 1, n6))


def _inproj_kernel(x_ref, mod_ref, g_ref, w_ref, bg_ref, *out_refs, full, colmajor):
    qkv_refs, gates_ref = out_refs[:6], out_refs[6]
    xn = _modnorm(x_ref[...], g_ref[...], mod_ref[0:1, :], mod_ref[1:2, :]).astype(BF16)
    tm = xn.shape[0]
    w = MLSTM_WIDTH
    for i in range(3):
        r = _dot(xn, w_ref[:, i * w:(i + 1) * w])
        if i == 1:
            r = r * HEAD_DIM ** -0.5
        row_ref, col_ref = qkv_refs[2 * i], qkv_refs[2 * i + 1]
        for hh in range(N_ROW_HEADS):
            row_ref[hh] = r[:, hh * HEAD_DIM:(hh + 1) * HEAD_DIM].astype(BF16)
            piece = r[:, (N_ROW_HEADS + hh) * HEAD_DIM:(N_ROW_HEADS + hh + 1) * HEAD_DIM]
            if colmajor:
                piece = jnp.swapaxes(piece.reshape(tm // GRID_W, GRID_W, HEAD_DIM), 0, 1)
            col_ref[hh] = piece.astype(BF16)
    gates_ref[...] = _dot(xn, w_ref[:, SEC_G[0]:SEC_G[1]]) + bg_ref[...]
    if full:
        o_ref, cacg_ref, fr_ref = out_refs[7:]
        o_ref[...] = _dot(xn, w_ref[:, SEC_O[0]:SEC_O[1]]).astype(BF16)
        cacg_ref[...] = _dot(xn, w_ref[:, SEC_CACG[0]:SEC_CACG[1]])
        fr_ref[...] = _dot(xn, w_ref[:, SEC_FR[0]:SEC_FR[1]]).astype(BF16)


def _inproj(x, mod, g, w, bg, full, rows):
    bsz, t, d = x.shape
    tm = min(t, 1024)
    nh = N_ROW_HEADS
    ncols = IN_COLS_PAD if full else SEC_G[1]
    tok = lambda n: pl.BlockSpec((None, tm, n), lambda b, i: (b, i, 0))
    row_shape = jax.ShapeDtypeStruct((bsz, nh, t, HEAD_DIM), BF16)
    row_spec = pl.BlockSpec((None, nh, tm, HEAD_DIM), lambda b, i: (b, 0, i, 0))
    if rows is None:
        col_shape, col_spec = row_shape, row_spec
    else:
        col_shape = jax.ShapeDtypeStruct((bsz, nh, GRID_W, rows, HEAD_DIM), BF16)
        col_spec = pl.BlockSpec((None, nh, GRID_W, tm // GRID_W, HEAD_DIM), lambda b, i: (b, 0, 0, i, 0))
    out_shape = [row_shape, col_shape] * 3 + [jax.ShapeDtypeStruct((bsz, t, LANES), F32)]
    out_specs = [row_spec, col_spec] * 3 + [tok(LANES)]
    if full:
        out_shape += [jax.ShapeDtypeStruct((bsz, t, MLSTM_WIDTH), BF16),
                      jax.ShapeDtypeStruct((bsz, t, 2 * CONV_CH), F32),
                      jax.ShapeDtypeStruct((bsz, t, FOURIER_CH), BF16)]
        out_specs += [tok(MLSTM_WIDTH), tok(2 * CONV_CH), tok(FOURIER_CH)]
    outs = pl.pallas_call(
        functools.partial(_inproj_kernel, full=full, colmajor=rows is not None),
        grid=(bsz, t // tm),
        in_specs=[tok(d), _mod_spec(mod),
                  pl.BlockSpec((1, d), lambda b, i: (0, 0)),
                  pl.BlockSpec((d, ncols), lambda b, i: (0, 0)),
                  pl.BlockSpec((1, LANES), lambda b, i: (0, 0))],
        out_specs=out_specs,
        out_shape=out_shape,
        compiler_params=_params(("parallel", "parallel"), 56),
        name="inproj_full" if full else "inproj_qkvg",
    )(x, mod[0], g.reshape(1, d), w, bg)
    qkv = [a.reshape(bsz, nh, t, HEAD_DIM) for a in outs[:6]]
    return qkv, outs[6], outs[7:]


def _logsig(x):
    return jnp.minimum(x, 0.0) - jnp.log1p(jnp.exp(-jnp.abs(x)))


def _tri_masks(n):
    a0 = lax.broadcasted_iota(I32, (n, n), 0)
    a1 = lax.broadcasted_iota(I32, (n, n), 1)
    return a0 <= a1, a0 >= a1


def _split3(x):
    hi = x.astype(BF16).astype(F32)
    r1 = x - hi
    mid = r1.astype(BF16).astype(F32)
    return hi, mid, r1 - mid


def _mlstm_consts(tri_scr, ones_scr):
    lc = MLSTM_CHUNK
    u = lax.broadcasted_iota(I32, (lc, 2 * lc), 0)
    t2 = lax.broadcasted_iota(I32, (lc, 2 * lc), 1)
    tri_scr[...] = jnp.where(((t2 < lc) & (u <= t2)) | ((t2 >= lc) & (u >= t2 - lc)), 1.0, 0.0).astype(BF16)
    r = lax.broadcasted_iota(I32, (16, 2 * lc), 0)
    t16 = lax.broadcasted_iota(I32, (16, 2 * lc), 1)
    ones_scr[...] = jnp.where(((r < 3) & (t16 < lc)) | ((r >= 3) & (r < 6) & (t16 >= lc)), 1.0, 0.0).astype(BF16)


def _mlstm_gates(head, g_ref, tri_scr, rowg_scr, stat_scr, n_chunks, cm_rows):
    lc = MLSTM_CHUNK
    lane2 = lax.broadcasted_iota(I32, (16, 2 * LANES), 1)
    pick = jnp.where(jnp.bitwise_and(lane2, LANES - 1) == N_GATES * head + lax.broadcasted_iota(I32, (16, 2 * LANES), 0),
                     1.0, 0.0).astype(BF16)
    chunk_id = lax.broadcasted_iota(I32, (MAX_CHUNKS, lc), 0)

    def gather(c, acc):
        if cm_rows is None:
            g = g_ref[pl.ds(pl.multiple_of(c * lc, lc), lc), :]
        else:
            cols = lc // cm_rows
            g = jnp.concatenate([g_ref[pl.ds(c * cols + j, cm_rows, stride=GRID_W), :] for j in range(cols)], axis=0)
        rows = _dot_nt(pick, jnp.concatenate(_split_bf16(g), axis=1))
        return tuple(jnp.where(chunk_id == c, rows[i:i + 1], a) for i, a in enumerate(acc))

    zeros = jnp.zeros((MAX_CHUNKS, lc), F32)
    i_f, f_f, i_b, f_b = lax.fori_loop(0, n_chunks, gather, (zeros,) * N_GATES, unroll=min(4, n_chunks))
    lf_f, lf_b = _logsig(f_f), _logsig(f_b)
    b_f = sum(_dot(part.astype(BF16), tri_scr[:, 0:lc]) for part in _split3(lf_f))
    b_b = sum(_dot(part.astype(BF16), tri_scr[:, lc:]) for part in _split3(lf_b))
    bl_f = jnp.sum(lf_f, axis=1, keepdims=True)
    bl_b = jnp.sum(lf_b, axis=1, keepdims=True)
    ci_f, ci_b = i_f - b_f, i_b - b_b
    ml_f = jnp.max(ci_f + bl_f, axis=1, keepdims=True)
    ml_b = jnp.max(ci_b + bl_b, axis=1, keepdims=True)
    for r, val in enumerate((jnp.exp(ci_f + bl_f - ml_f), jnp.exp(ci_b + bl_b - ml_b), b_f, b_b, ci_f, ci_b)):
        rowg_scr[:, r, :] = val
    for r, val in enumerate((bl_f, bl_b, ml_f, ml_b)):
        stat_scr[:, r, :] = jnp.broadcast_to(val, (MAX_CHUNKS, HEAD_DIM))


def _mlstm_increment(c, k_ref, v_ref, rowg_scr, up_scr, vt_scr):
    lc = MLSTM_CHUNK
    s = pl.multiple_of(c * lc, lc)
    v_t = v_ref[pl.ds(s, lc), :].astype(F32).T
    rg = rowg_scr[c]
    w_f, w_b = rg[0:1], rg[1:2]
    lhs = jnp.concatenate([(v_t * w_f).astype(BF16), (v_t * w_b).astype(BF16),
                           _rows(16, lc, w_f, w_b).astype(BF16)], axis=0)
    up_scr[c] = _dot(lhs, k_ref[pl.ds(s, lc), :])
    if vt_scr is not None:
        vt_scr[c] = v_t.astype(BF16)


def _mlstm_state_step(cc, carry, up_scr, stat_scr, st_scr, mst_scr, fwd):
    cmat, n, m = carry
    off, row = (0, 0) if fwd else (HEAD_DIM, 1)
    if st_scr is not None:
        st_scr[cc, off:off + HEAD_DIM, :] = cmat.astype(BF16)
        nrow = ST_NF if fwd else ST_NB
        st_scr[cc, nrow:nrow + 16, :] = _rows(16, HEAD_DIM, n).astype(BF16)
        mst_scr[cc, row:row + 1, :] = jnp.broadcast_to(m, (1, HEAD_DIM))
    st = stat_scr[cc]
    bl, ml = st[row:row + 1, 0:1], st[2 + row:3 + row, 0:1]
    m_new = jnp.maximum(bl + m, ml)
    keep = jnp.exp(bl + m - m_new)
    gain = jnp.exp(ml - m_new)
    cmat = keep * cmat + gain * up_scr[cc, off:off + HEAD_DIM, :]
    n = keep * n + gain * up_scr[cc, 2 * HEAD_DIM + row:2 * HEAD_DIM + row + 1, :]
    return cmat, n, m_new


def _mlstm_outputs(work, ones_scr):
    lc = MLSTM_CHUNK
    starts = [pl.multiple_of(c * lc, lc) for _, c in work]
    le, ge = _tri_masks(lc)
    qs = [ch[0][pl.ds(s, lc), :] for (ch, _), s in zip(work, starts)]
    qk_ts = [_dot_nt(ch[1][pl.ds(s, lc), :], q) for (ch, _), s, q in zip(work, starts, qs)]
    g_ts = [_dot_nt(ch[5][c], q) for (ch, c), q in zip(work, qs)]
    rgs = [ch[4][c] for ch, c in work]
    cis = [lax.dot_general(_rows(16, lc, *_split3(rg[4:5]), *_split3(rg[5:6])).astype(BF16), ones_scr[...],
                           (((0,), (0,)), ((), ())), preferred_element_type=F32) for rg in rgs]

    def direction(qk_t, valid, b_row, ci, m_st, qn):
        dlog = jnp.where(valid, b_row + ci, -jnp.inf)
        inter = b_row + m_st
        m_t = jnp.maximum(inter, jnp.max(dlog, axis=0, keepdims=True))
        s_t = qk_t * jnp.exp(dlog - m_t)
        w_inter = jnp.exp(inter - m_t)
        den = w_inter * qn + jnp.sum(s_t, axis=0, keepdims=True)
        r = 1.0 / jnp.maximum(jnp.abs(den), jnp.exp(-m_t))
        return r * s_t, r * w_inter

    mixed = []
    for (ch, c), qk_t, g_t, rg, ci in zip(work, qk_ts, g_ts, rgs, cis):
        ms = ch[6][c]
        sf, cf = direction(qk_t, le, rg[2:3], ci[:, 0:lc], ms[0:1, 0:1], g_t[ST_NF:ST_NF + 1])
        sb, cb = direction(qk_t, ge, rg[3:4], ci[:, lc:], ms[1:2, 0:1], g_t[ST_NB:ST_NB + 1])
        mixed.append(((sf + sb).astype(BF16), cf * g_t[0:HEAD_DIM] + cb * g_t[HEAD_DIM:2 * HEAD_DIM]))
    h_ts = [inter_part + _dot(ch[7][c], s_mix) for (ch, c), (s_mix, inter_part) in zip(work, mixed)]
    for (ch, c), s, h_t in zip(work, starts, h_ts):
        gh_ref, h_ref, cm_rows = ch[2], ch[3], ch[8]
        h = (h_t * lax.rsqrt(jnp.mean(h_t * h_t, axis=0, keepdims=True) + EPS) * gh_ref[...]).T
        if cm_rows is None:
            h_ref[pl.ds(s, lc), :] = h
        else:
            cols = lc // cm_rows
            for j in range(cols):
                h_ref[pl.ds(c * cols + j, cm_rows, stride=GRID_W), :] = h[j * cm_rows:(j + 1) * cm_rows]


def _mlstm_kernel(*refs, n_chunks, with_outputs, cm_rows):
    n_slot_in = 6
    g_ref = refs[0]
    slots_in = [refs[1:1 + n_slot_in], refs[1 + n_slot_in:1 + 2 * n_slot_in]]
    rest = refs[1 + 2 * n_slot_in:]
    if with_outputs:
        h_refs, rest = rest[:2], rest[2:]
    state_out, scratch = rest[:4], rest[4:]
    tri_scr, ones_scr, scratch = scratch[0], scratch[1], scratch[2:]
    n_scr = 6 if with_outputs else 3
    slots_scr = [scratch[:n_scr], scratch[n_scr:2 * n_scr]]
    slot_rows = [None, cm_rows]
    _mlstm_consts(tri_scr, ones_scr)
    for slot, scr in enumerate(slots_scr):
        _mlstm_gates(pl.program_id(1) + slot * N_ROW_HEADS, g_ref, tri_scr, scr[0], scr[1], n_chunks, slot_rows[slot])

    def increments(c, _):
        for (q_ref, k_ref, v_ref, c0_ref, n0_ref, gh_ref), scr in zip(slots_in, slots_scr):
            _mlstm_increment(c, k_ref, v_ref, scr[0], scr[2], scr[5] if with_outputs else None)
        return 0

    lax.fori_loop(0, n_chunks, increments, 0, unroll=min(4, n_chunks))

    def states(i, carry):
        out = []
        for slot, scr in enumerate(slots_scr):
            st_scr, mst_scr = (scr[3], scr[4]) if with_outputs else (None, None)
            out.append(_mlstm_state_step(i, carry[2 * slot], scr[2], scr[1], st_scr, mst_scr, True))
            out.append(_mlstm_state_step(n_chunks - 1 - i, carry[2 * slot + 1], scr[2], scr[1], st_scr, mst_scr, False))
        return tuple(out)

    init = []
    for (q_ref, k_ref, v_ref, c0_ref, n0_ref, gh_ref) in slots_in:
        init.append((c0_ref[0], n0_ref[0:1, :], n0_ref[2:3, 0:1]))
        init.append((c0_ref[1], n0_ref[1:2, :], n0_ref[3:4, 0:1]))
    final = lax.fori_loop(0, n_chunks, states, tuple(init))
    for slot in range(2):
        c_out, n_out = state_out[2 * slot], state_out[2 * slot + 1]
        (cf, nf, mf), (cb, nb, mb) = final[2 * slot], final[2 * slot + 1]
        c_out[0] = cf
        c_out[1] = cb
        n_out[...] = _rows(8, HEAD_DIM, nf, nb, mf, mb)

    if with_outputs:
        chains = [(q_ref, k_ref, gh_ref, h_refs[slot], scr[0], scr[3], scr[4], scr[5], slot_rows[slot])
                  for slot, ((q_ref, k_ref, v_ref, c0_ref, n0_ref, gh_ref), scr) in enumerate(zip(slots_in, slots_scr))]

        per_iter = 4 if n_chunks % 4 == 0 else 2 if n_chunks % 2 == 0 else 1

        def outputs(i, _):
            _mlstm_outputs([(ch, i * per_iter + u) for u in range(per_iter) for ch in chains], ones_scr)
            return 0

        lax.fori_loop(0, n_chunks // per_iter, outputs, 0, unroll=min(2, n_chunks // per_iter))


def _mlstm(qkv, gates, states, gh, with_outputs, cm_rows):
    bsz, npair, t, dh = qkv[0].shape
    nc = t // MLSTM_CHUNK
    lc = MLSTM_CHUNK
    seq = pl.BlockSpec((None, None, t, dh), lambda b, p: (b, p, 0, 0))
    cst = pl.BlockSpec((None, None, 2, dh, dh), lambda b, p: (b, p, 0, 0, 0))
    nst = pl.BlockSpec((None, None, 8, dh), lambda b, p: (b, p, 0, 0))

    def slot_specs(head_off):
        return [seq, seq, seq, cst, nst, pl.BlockSpec((None, dh, 1), lambda b, p: (p + head_off, 0, 0))]

    st_shapes = [jax.ShapeDtypeStruct((bsz, npair, 2, dh, dh), F32), jax.ShapeDtypeStruct((bsz, npair, 8, dh), F32)] * 2
    out_shape, out_specs = st_shapes, [cst, nst, cst, nst]
    assert nc <= MAX_CHUNKS
    slot_scratch = [pltpu.VMEM((MAX_CHUNKS, 8, lc), F32), pltpu.VMEM((MAX_CHUNKS, 8, dh), F32),
                    pltpu.VMEM((nc, UP_ROWS, dh), F32)]
    if with_outputs:
        h_shape = jax.ShapeDtypeStruct((bsz, t, npair * dh), F32)
        h_spec = pl.BlockSpec((None, t, dh), lambda b, p: (b, 0, p))
        out_shape = [h_shape, h_shape] + out_shape
        out_specs = [h_spec, h_spec] + out_specs
        slot_scratch += [pltpu.VMEM((nc, ST_ROWS, dh), BF16), pltpu.VMEM((nc, 8, dh), F32), pltpu.VMEM((nc, dh, lc), BF16)]
    q_r, q_c, k_r, k_c, v_r, v_c = qkv
    c_r, n_r, c_c, n_c = states
    return pl.pallas_call(
        functools.partial(_mlstm_kernel, n_chunks=nc, with_outputs=with_outputs, cm_rows=cm_rows),
        grid=(bsz, npair),
        in_specs=[pl.BlockSpec((None, t, LANES), lambda b, p: (b, 0, 0))] + slot_specs(0) + slot_specs(N_ROW_HEADS),
        out_specs=out_specs,
        out_shape=out_shape,
        scratch_shapes=[pltpu.VMEM((lc, 2 * lc), BF16), pltpu.VMEM((16, 2 * lc), BF16)] + slot_scratch * 2,
        compiler_params=_params(("parallel", "parallel"), 48),
        name="mlstm_full" if with_outputs else "mlstm_states",
    )(gates, q_r, k_r, v_r, c_r, n_r, gh, q_c, k_c, v_c, c_c, n_c, gh)


def _conv_kernel(cacg_ref, w_ref, cb_ref, lg_ref, lb_ref, o_ref, ysh, yconv, *, t):
    rt = 64
    ysh[0, 0:CONV_HALO, :] = jnp.zeros((CONV_HALO, CONV_CH), F32)
    ysh[0, CONV_HALO + t:t + CONV_TAIL, :] = jnp.zeros((CONV_TAIL - CONV_HALO, CONV_CH), F32)

    def fill(r, _):
        s = pl.multiple_of(r * rt, rt)
        a = cacg_ref[pl.ds(s, rt), 0:CONV_CH]
        g = cacg_ref[pl.ds(s, rt), CONV_CH:2 * CONV_CH]
        ysh[0, pl.ds(CONV_HALO + s, rt), :] = a * jax.nn.sigmoid(g)
        return 0

    lax.fori_loop(0, t // rt, fill, 0)

    def shift(r, _):
        s = pl.multiple_of(r * rt, rt)
        win = ysh[0, pl.ds(s, rt + 8), :]
        for res in range(1, 8):
            ysh[res, pl.ds(s, rt), :] = pltpu.roll(win, rt + 8 - res, axis=0)[0:rt, :]
        return 0

    lax.fori_loop(0, t // rt + 1, shift, 0)

    def tile(r, _):
        s = pl.multiple_of(r * rt, rt)
        acc = jnp.zeros((rt, CONV_CH), F32)
        for kk in range(CONV_K):
            off = kk + CONV_HALO - CONV_K // 2
            rows = ysh[off % 8, pl.ds(pl.multiple_of(s + 8 * (off // 8), 8), rt), :]
            acc = acc + w_ref[kk:kk + 1, :] * rows
        yconv[pl.ds(s, rt), :] = acc + cb_ref[...]
        return 0

    lax.fori_loop(0, t // rt, tile, 0)

    def norm(r, _):
        s = pl.multiple_of(r * rt, rt)
        y = yconv[pl.ds(s, rt), :]
        mu = jnp.mean(y, axis=-1, keepdims=True)
        yc = y - mu
        var = jnp.mean(yc * yc, axis=-1, keepdims=True)
        z = yc * lax.rsqrt(var + EPS) * lg_ref[...] + lb_ref[...]
        o_ref[pl.ds(s, rt), :] = (z * jax.nn.sigmoid(z)).astype(BF16)
        return 0

    lax.fori_loop(0, t // rt, norm, 0, unroll=4)


def _conv(cacg, conv_w, conv_b, ln_g, ln_b):
    bsz, t, _ = cacg.shape
    row = lambda a: a.reshape(1, CONV_CH)
    vec = pl.BlockSpec((1, CONV_CH), lambda b: (0, 0))
    return pl.pallas_call(
        functools.partial(_conv_kernel, t=t),
        grid=(bsz,),
        in_specs=[pl.BlockSpec((None, t, 2 * CONV_CH), lambda b: (b, 0, 0)),
                  pl.BlockSpec((CONV_K + 1, CONV_CH), lambda b: (0, 0)), vec, vec, vec],
        out_specs=pl.BlockSpec((None, t, CONV_CH), lambda b: (b, 0, 0)),
        out_shape=jax.ShapeDtypeStruct((bsz, t, CONV_CH), BF16),
        scratch_shapes=[pltpu.VMEM((8, t + CONV_TAIL, CONV_CH), F32), pltpu.VMEM((t, CONV_CH), F32)],
        compiler_params=_params(("parallel",), 40),
        name="conformer_conv",
    )(cacg, jnp.pad(conv_w, ((0, 1), (0, 0))), row(conv_b), row(ln_g), row(ln_b))


@functools.lru_cache(maxsize=None)
def _dft_tables(t):
    j = np.arange(t, dtype=np.int64)
    ang = 2.0 * np.pi * ((j[:, None] * j[None, :]) % t).astype(np.float64) / t
    wt = np.concatenate([np.cos(ang), -np.sin(ang)], axis=1).astype(np.float32)
    c = np.arange(FOURIER_CH, dtype=np.int64)
    grp, idx = c // FOURIER_GROUP_CH, c % FOURIER_GROUP_CH
    same = grp[:, None] == grp[None, :]
    angc = 2.0 * np.pi * ((idx[:, None] * idx[None, :]) % FOURIER_GROUP_CH).astype(np.float64) / FOURIER_GROUP_CH
    cc = np.where(same, np.cos(angc), 0.0).astype(np.float32)
    sc = np.where(same, np.sin(angc), 0.0).astype(np.float32)
    return wt, cc, sc


def _fourier_kernel(z_ref, cc_ref, sc_ref, wt_ref, o_ref, zcs, *, t, scale):
    bsz, _, ch = z_ref.shape

    @pl.when(pl.program_id(0) == 0)
    def _():
        for b in range(bsz):
            zcs[0:t, b * ch:(b + 1) * ch] = _dot(z_ref[b], cc_ref[...]).astype(BF16)
            zcs[t:2 * t, b * ch:(b + 1) * ch] = _dot(z_ref[b], sc_ref[...]).astype(BF16)

    r = _dot(wt_ref[...], zcs[...]) * scale
    for b in range(bsz):
        o_ref[b] = r[:, b * ch:(b + 1) * ch].astype(BF16)


def _fourier(z):
    bsz, t, ch = z.shape
    wt, cc, sc = _dft_tables(t)
    tm = min(t, 512)
    mat = pl.BlockSpec((ch, ch), lambda i: (0, 0))
    return pl.pallas_call(
        functools.partial(_fourier_kernel, t=t, scale=float((t * FOURIER_GROUP_CH) ** -0.5)),
        grid=(t // tm,),
        in_specs=[pl.BlockSpec((bsz, t, ch), lambda i: (0, 0, 0)), mat, mat,
                  pl.BlockSpec((tm, 2 * t), lambda i: (i, 0))],
        out_specs=pl.BlockSpec((bsz, tm, ch), lambda i: (0, i, 0)),
        out_shape=jax.ShapeDtypeStruct((bsz, t, ch), BF16),
        scratch_shapes=[pltpu.VMEM((2 * t, bsz * ch), BF16)],
        compiler_params=_params(("arbitrary",), 56),
        name="fourier_mix",
    )(z, jnp.asarray(cc).astype(BF16), jnp.asarray(sc).astype(BF16), jnp.asarray(wt).astype(BF16))


def _outproj_kernel(x_ref, hr_ref, hc_ref, o_ref, cv_ref, fo_ref, w_ref, mod_ref, g2_ref, wr_ref,
                    x1_ref, xn_ref, aff_ref):
    half = MLSTM_WIDTH // 2
    a, b = MLSTM_WIDTH, MLSTM_WIDTH + CONV_CH
    og = jax.nn.sigmoid(o_ref[...].astype(F32))
    y = (_dot((hr_ref[...] * og[:, 0:half]).astype(BF16), w_ref[0:half, :])
         + _dot((hc_ref[...] * og[:, half:a]).astype(BF16), w_ref[half:a, :])
         + _dot(cv_ref[...], w_ref[a:b, :]) + _dot(fo_ref[...], w_ref[b:, :]))
    x1 = x_ref[...] + mod_ref[2:3, :] * y
    x1_ref[...] = x1
    xn = _modnorm(x1, g2_ref[...], mod_ref[3:4, :], mod_ref[4:5, :])
    xn_ref[...] = xn.astype(BF16)
    tm = xn.shape[0]
    r = _dot(jnp.concatenate(_split_bf16(xn), axis=0), jnp.concatenate(_split_bf16(wr_ref[...]), axis=1))
    logits = (r[:tm, :LANES] + r[:tm, LANES:]) + (r[tm:, :LANES] + r[tm:, LANES:])
    lane = lax.broadcasted_iota(I32, logits.shape, 1)
    logits = jnp.where(lane < N_EXPERTS, logits, -jnp.inf)
    e = jnp.exp(logits - jnp.max(logits, axis=-1, keepdims=True))
    aff_ref[...] = e / jnp.sum(e, axis=-1, keepdims=True)


def _outproj(x, hr, hc, o, cv, fo, w_out, mod, g2, wr):
    bsz, t, d = x.shape
    tm = min(t, 512)
    tok = lambda n: pl.BlockSpec((None, tm, n), lambda b, i: (b, i, 0))
    return pl.pallas_call(
        _outproj_kernel,
        grid=(bsz, t // tm),
        in_specs=[tok(d), tok(MLSTM_WIDTH // 2), tok(MLSTM_WIDTH // 2), tok(MLSTM_WIDTH), tok(CONV_CH), tok(FOURIER_CH),
                  pl.BlockSpec((d, d), lambda b, i: (0, 0)), _mod_spec(mod),
                  pl.BlockSpec((1, d), lambda b, i: (0, 0)),
                  pl.BlockSpec((d, LANES), lambda b, i: (0, 0))],
        out_specs=[tok(d), tok(d), tok(LANES)],
        out_shape=[jax.ShapeDtypeStruct((bsz, t, d), F32), jax.ShapeDtypeStruct((bsz, t, d), BF16),
                   jax.ShapeDtypeStruct((bsz, t, LANES), F32)],
        compiler_params=_params(("parallel", "parallel"), 40),
        name="outproj_router",
    )(x, hr, hc, o, cv, fo, w_out, mod[0], g2.reshape(1, d), wr)


def _select_kernel(a_ref, sp_ref, win_ref, *, cap, tile):
    a = a_ref[...]
    n = a.shape[0]
    blk = min(n, MXU_DIM)

    def count(mask):
        return jnp.sum(jnp.where(mask, 1.0, 0.0), axis=0, keepdims=True)

    def search(i, thr):
        cand = thr | jnp.left_shift(jnp.int32(1), 30 - i)
        return jnp.where(count(a >= lax.bitcast_convert_type(cand, F32)) >= cap, cand, thr)

    thr = lax.bitcast_convert_type(lax.fori_loop(0, 31, search, jnp.zeros((1, LANES), I32)), F32)
    gt = a > thr
    eq = a == thr
    before = jnp.where(lax.broadcasted_iota(I32, (blk, blk), 1) < lax.broadcasted_iota(I32, (blk, blk), 0), 1.0, 0.0).astype(BF16)

    def excl_cumsum(mask):
        x = jnp.where(mask, 1.0, 0.0)
        run = jnp.zeros((1, LANES), F32)
        parts = []
        for j in range(n // blk):
            xb = x[j * blk:(j + 1) * blk]
            parts.append(_dot(before, xb.astype(BF16)) + run)
            run = run + jnp.sum(xb, axis=0, keepdims=True)
        return jnp.concatenate(parts, axis=0)

    sel = gt | (eq & (excl_cumsum(eq) < cap - count(gt)))
    pos = excl_cumsum(sel).astype(I32)
    sp_ref[...] = jnp.where(sel, pos, -1)
    rid = lax.broadcasted_iota(I32, win_ref.shape, 0)
    win = jnp.zeros(win_ref.shape, I32)
    for ti in range(n // tile):
        t_sel, t_pos = sel[ti * tile:(ti + 1) * tile], pos[ti * tile:(ti + 1) * tile]
        win = jnp.where(rid == 2 * ti, jnp.min(jnp.where(t_sel, t_pos, cap), axis=0, keepdims=True), win)
        win = jnp.where(rid == 2 * ti + 1, jnp.max(jnp.where(t_sel, t_pos + 1, 0), axis=0, keepdims=True), win)
    win_ref[...] = win


def _select(aff_sets, cap, tile):
    ng, n, _ = aff_sets.shape
    win_rows = max(8, 2 * (n // tile))
    blk = pl.BlockSpec((None, n, LANES), lambda g: (g, 0, 0))
    return pl.pallas_call(
        functools.partial(_select_kernel, cap=cap, tile=tile),
        grid=(ng,),
        in_specs=[blk],
        out_specs=[blk, pl.BlockSpec((None, win_rows, LANES), lambda g: (g, 0, 0))],
        out_shape=[jax.ShapeDtypeStruct((ng, n, LANES), I32), jax.ShapeDtypeStruct((ng, win_rows, LANES), I32)],
        compiler_params=_params(("parallel",), 32),
        name="ec_select",
    )(aff_sets)


def _slot_window(win_ref, b, tile_idx, e, cap, width):
    lo = win_ref[b, 2 * tile_idx * N_EXPERTS + e]
    hi = win_ref[b, (2 * tile_idx + 1) * N_EXPERTS + e]
    st = jnp.minimum(lax.shift_left(lax.shift_right_logical(lo, 4), 4), cap - width)
    return pl.multiple_of(st, 16), hi - st <= width


def _ffn_kernel(win_ref, sp_ref, ar_ref, xn_ref, wg_ref, wu_ref, wd_ref, ys_ref, wgu_b, wd_b, xs_scr, gate_scr, *, cap, windowed):
    g, j = pl.program_id(0), pl.program_id(1)
    gsz, rows = wg_ref.shape[0], wg_ref.shape[1]
    nb, _, _, n = sp_ref.shape

    def cast_slice():
        rt = min(256, rows)
        dst = lax.rem(g, 2)
        for piece in range(rows // rt):
            src_rows = pl.ds(piece * rt, rt)
            dst_rows = pl.ds(pl.multiple_of(j * rows + piece * rt, rt), rt)
            for k in range(gsz):
                wgu_b[dst, k, 0, dst_rows, :] = wg_ref[k, src_rows, :].astype(BF16)
                wgu_b[dst, k, 1, dst_rows, :] = wu_ref[k, src_rows, :].astype(BF16)
                wd_b[dst, k, dst_rows, :] = wd_ref[k, src_rows, :].astype(BF16)

    def full_gather(k, bi):
        hit = lax.broadcasted_iota(I32, (cap, n), 0) == sp_ref[bi, k]
        xs_scr[k, bi * cap:(bi + 1) * cap, :] = _dot(jnp.where(hit, 1.0, 0.0).astype(BF16), xn_ref[bi])
        gate_scr[k, bi * cap:(bi + 1) * cap, :] = jnp.broadcast_to(
            jnp.sum(jnp.where(hit, ar_ref[bi, k], 0.0), axis=1, keepdims=True), (cap, LANES))

    def experts_ffn():
        src = lax.rem(g + 1, 2)
        xs = [xs_scr[k].astype(BF16) for k in range(gsz)]
        h1 = [_dot(xs[k], wgu_b[src, k, 0]) for k in range(gsz)]
        h2 = [_dot(xs[k], wgu_b[src, k, 1]) for k in range(gsz)]
        hid = [(h1[k] * jax.nn.sigmoid(h1[k]) * h2[k]).astype(BF16) for k in range(gsz)]
        ys = [(_dot(hid[k], wd_b[src, k]) * gate_scr[k, :, 0:1]).astype(BF16) for k in range(gsz)]
        for k in range(gsz):
            for bi in range(nb):
                ys_ref[bi, k] = ys[k][bi * cap:(bi + 1) * cap]

    def full_path():
        for k in range(gsz):
            for bi in range(nb):
                full_gather(k, bi)
        experts_ffn()

    @pl.when(g == 0)
    def _():
        cast_slice()

    if not windowed:
        @pl.when(g > 0)
        def _():
            cast_slice()
            full_path()
        return

    assert nb == 1
    width = cap // 2
    wins = [[_slot_window(win_ref, j, t, jnp.maximum(g - 1, 0) * gsz + k, cap, width) for t in range(n // WIN_TILE)]
            for k in range(gsz)]
    fits = functools.reduce(jnp.logical_and, [ok for per_expert in wins for _, ok in per_expert])

    @pl.when(jnp.logical_and(g > 0, fits))
    def _():
        cast_slice()
        xs_scr[...] = jnp.zeros(xs_scr.shape, F32)
        gate_scr[...] = jnp.zeros(gate_scr.shape, F32)
        for t in range(n // WIN_TILE):
            tok = slice(t * WIN_TILE, (t + 1) * WIN_TILE)
            for k in range(gsz):
                st = wins[k][t][0]
                hit = lax.broadcasted_iota(I32, (width, WIN_TILE), 0) + st == sp_ref[0, k, :, tok]
                xs_scr[k, pl.ds(st, width), :] += _dot(jnp.where(hit, 1.0, 0.0).astype(BF16), xn_ref[0, tok, :])
                gate_scr[k, pl.ds(st, width), :] += jnp.broadcast_to(
                    jnp.sum(jnp.where(hit, ar_ref[0, k, :, tok], 0.0), axis=1, keepdims=True), (width, LANES))
        experts_ffn()

    @pl.when(jnp.logical_and(g > 0, jnp.logical_not(fits)))
    def _():
        cast_slice()
        full_path()


def _ffn(win, sp_row, aff_row, xn, wg, wu, wd, layer, cap):
    bsz, n, d = xn.shape
    _, ne, _, hid = wg.shape
    assert d == hid
    nb = min(bsz, max(1, MXU_DIM // (2 * cap)))
    nj = bsz // nb
    gsz = 2 if nj > 1 else 1
    rows = d // nj
    windowed = nb == 1 and n > WIN_TILE and cap // 2 >= LANES
    batch = lambda g, j: jnp.where(g == 0, 0, j)
    row = pl.BlockSpec((nb, gsz, 1, n), lambda g, j, *_: (batch(g, j), jnp.maximum(g - 1, 0), 0, 0))
    wspec = pl.BlockSpec((None, gsz, rows, hid), lambda g, j, *_: (layer, jnp.minimum(g, ne // gsz - 1), j, 0))
    grid_spec = pltpu.PrefetchScalarGridSpec(
        num_scalar_prefetch=1,
        grid=(ne // gsz + 1, nj),
        in_specs=[row, row, pl.BlockSpec((nb, n, d), lambda g, j, *_: (batch(g, j), 0, 0)), wspec, wspec, wspec],
        out_specs=pl.BlockSpec((nb, gsz, cap, d), lambda g, j, *_: (batch(g, j), jnp.maximum(g - 1, 0), 0, 0)),
        scratch_shapes=[pltpu.VMEM((2, gsz, 2, d, hid), BF16), pltpu.VMEM((2, gsz, hid, d), BF16),
                        pltpu.VMEM((gsz, nb * cap, d), F32), pltpu.VMEM((gsz, nb * cap, LANES), F32)])
    return pl.pallas_call(
        functools.partial(_ffn_kernel, cap=cap, windowed=windowed),
        grid_spec=grid_spec,
        out_shape=jax.ShapeDtypeStruct((bsz, ne, cap, d), BF16),
        compiler_params=_params(("arbitrary", "arbitrary"), 56),
        name="ec_ffn",
    )(win, sp_row, aff_row, xn, wg, wu, wd)


def _combine_kernel(win_ref, x_ref, spc_ref, ys_ref, mod_ref, gf_ref, o_ref, *, cap, final, pack):
    b, i = pl.program_id(0), pl.program_id(1)
    tm = spc_ref.shape[0]
    sub = tm if pack == 1 else WIN_TILE

    def finish(rows, acc):
        out = x_ref[rows, :] + mod_ref[5:6, :] * acc
        if final:
            out = out * lax.rsqrt(jnp.mean(out * out, axis=-1, keepdims=True) + EPS) * gf_ref[...]
        o_ref[rows, :] = out

    def full_width(spc):
        slot = lax.broadcasted_iota(I32, (sub, cap), 1)
        acc = jnp.zeros((sub, x_ref.shape[1]), F32)
        for e in range(N_EXPERTS):
            hit = jnp.where(spc[:, e:e + 1] == slot, 1.0, 0.0).astype(BF16)
            acc = acc + _dot(hit, ys_ref[e])
        return acc

    n_sub = tm // sub
    rows = [slice(t * sub, (t + 1) * sub) for t in range(n_sub)]
    spcs = [spc_ref[r, :] for r in rows]
    if pack == 1:
        for r, spc in zip(rows, spcs):
            finish(r, full_width(spc))
        return

    w = cap // pack
    wins = [[_slot_window(win_ref, b, i * n_sub + t, e, cap, w) for e in range(N_EXPERTS)] for t in range(n_sub)]
    fits = functools.reduce(jnp.logical_and, [ok for per_tile in wins for _, ok in per_tile])

    @pl.when(fits)
    def _():
        lane = lax.broadcasted_iota(I32, (sub, cap), 1)
        lane_row = lax.broadcasted_iota(I32, (1, cap), 1)
        accs = [jnp.zeros((sub, x_ref.shape[1]), F32) for _ in range(n_sub)]
        for p in range(N_EXPERTS // pack):
            es = list(range(p * pack, (p + 1) * pack))
            hits, ysws = [], []
            for t in range(n_sub):
                col = spcs[t][:, es[0]:es[0] + 1]
                off = jnp.zeros((1, cap), I32) + wins[t][es[0]][0]
                for q, e in enumerate(es[1:], 1):
                    col = jnp.where(lane >= q * w, spcs[t][:, e:e + 1], col)
                    off = jnp.where(lane_row >= q * w, wins[t][e][0] - q * w, off)
                hits.append(jnp.where(col - off == lane, 1.0, 0.0).astype(BF16))
                ysws.append(jnp.concatenate([ys_ref[e, pl.ds(wins[t][e][0], w), :] for e in es], axis=0))
            accs = [acc + _dot(hit, ysw) for acc, hit, ysw in zip(accs, hits, ysws)]
        for r, acc in zip(rows, accs):
            finish(r, acc)

    @pl.when(jnp.logical_not(fits))
    def _():
        for r, spc in zip(rows, spcs):
            finish(r, full_width(spc))


def _combine(x, sp_col, win, ys, mod, g_final, cap, final):
    bsz, n, d = x.shape
    pack = 4 if n > WIN_TILE and cap // 4 >= LANES // 2 else 1
    tm = min(n, 512)
    grid_spec = pltpu.PrefetchScalarGridSpec(
        num_scalar_prefetch=1,
        grid=(bsz, n // tm),
        in_specs=[pl.BlockSpec((None, tm, d), lambda b, i, *_: (b, i, 0)),
                  pl.BlockSpec((None, tm, N_EXPERTS), lambda b, i, *_: (b, i, 0)),
                  pl.BlockSpec((None, N_EXPERTS, cap, d), lambda b, i, *_: (b, 0, 0, 0)),
                  _mod_spec(mod),
                  pl.BlockSpec((1, d), lambda b, i, *_: (0, 0))],
        out_specs=pl.BlockSpec((None, tm, d), lambda b, i, *_: (b, i, 0)))
    return pl.pallas_call(
        functools.partial(_combine_kernel, cap=cap, final=final, pack=pack),
        grid_spec=grid_spec,
        out_shape=jax.ShapeDtypeStruct((bsz, n, d), F32),
        compiler_params=_params(("parallel", "parallel"), 48),
        name="ec_combine_final" if final else "ec_combine",
    )(win, x, sp_col, ys, mod[0], g_final.reshape(1, d))


def _moe(x1, xn2, aff, wg, wu, wd, layer, mod, g_final, final):
    bsz, n, _ = x1.shape
    cap = EC_CAPACITY_FACTOR * n // N_EXPERTS
    per_group = LANES // N_EXPERTS
    ng = -(-bsz // per_group)
    aff_e = aff[:, :, :N_EXPERTS]
    sets = jnp.pad(aff_e, ((0, ng * per_group - bsz), (0, 0), (0, 0)))
    sets = sets.reshape(ng, per_group, n, N_EXPERTS).transpose(0, 2, 1, 3).reshape(ng, n, LANES)
    tile = min(n, WIN_TILE)
    selpos, win = _select(sets, cap, tile)
    ungroup = lambda a: a.reshape(ng, -1, per_group, N_EXPERTS).transpose(0, 2, 1, 3).reshape(ng * per_group, -1, N_EXPERTS)[:bsz]
    sp_col = ungroup(selpos)
    win = ungroup(win[:, :2 * (n // tile)]).reshape(bsz, -1)
    sp_row = sp_col.transpose(0, 2, 1).reshape(bsz, N_EXPERTS, 1, n)
    aff_row = aff_e.transpose(0, 2, 1).reshape(bsz, N_EXPERTS, 1, n)
    ys = _ffn(win, sp_row, aff_row, xn2, wg, wu, wd, layer, cap)
    return _combine(x1, sp_col, win, ys, mod, g_final, cap, final)


def kernel(x, c, ctx, c_ctx, w_ada, b_ada, g_norm1, w_in, b_gates, g_hnorm, conv_w, conv_b, conv_ln_g, conv_ln_b,
           w_out, g_norm2, w_router, w_e_gate, w_e_up, w_e_down, g_final):
    bsz, t, d = x.shape
    depth = w_ada.shape[0]
    rows = t // GRID_W
    n_cond = 16
    cs = jnp.concatenate([c, c_ctx[None, :], jnp.zeros((n_cond - bsz - 1, d), F32)], axis=0)
    mods = _ada(cs, w_ada, b_ada).reshape(depth, n_cond, N_ADA, d)

    npair = N_HEADS // 2
    zero_states = (jnp.zeros((bsz, npair, 2, HEAD_DIM, HEAD_DIM), F32), jnp.zeros((bsz, npair, 8, HEAD_DIM), F32)) * 2
    h_ctx = ctx
    for l in range(depth):
        last = l == depth - 1
        lat, cmod = (mods, l, None), (mods, l, bsz)
        w_l = w_in[l]
        qkvg_cols = 3 * MLSTM_WIDTH + N_GATES * N_HEADS
        w_p = jnp.concatenate([w_l[:, :3 * MLSTM_WIDTH],
                               jnp.pad(w_l[:, 3 * MLSTM_WIDTH:qkvg_cols], ((0, 0), (0, LANES - N_GATES * N_HEADS))),
                               w_l[:, qkvg_cols:]], axis=1).astype(BF16)
        bg = jnp.pad(b_gates[l].reshape(1, N_HEADS * N_GATES), ((0, 0), (0, LANES - N_HEADS * N_GATES)))
        w_o = w_out[l].astype(BF16)
        w_r = jnp.pad(w_router[l], ((0, 0), (0, LANES - N_EXPERTS)))
        gh = g_hnorm[l].reshape(N_HEADS, HEAD_DIM, 1)

        def mlstm(xs, md, states, grid_rows, full):
            qkv, gates, extra = _inproj(xs, md, g_norm1[l], w_p, bg, full, grid_rows)
            return _mlstm(qkv, gates, states, gh, full, grid_rows), extra

        def sublayers(xs, md, states, grid_rows, final):
            (hr, hc, *st), (o, cacg, fr) = mlstm(xs, md, states, grid_rows, True)
            cv = _conv(cacg, conv_w[l], conv_b[l], conv_ln_g[l], conv_ln_b[l])
            fo = _fourier(fr)
            x1, xn2, aff = _outproj(xs, hr, hc, o, cv, fo, w_o, md, g_norm2[l], w_r)
            return _moe(x1, xn2, aff, w_e_gate, w_e_up, w_e_down, l, md, g_final, final), st

        if last:
            st, _ = mlstm(h_ctx, cmod, zero_states, None, False)
        else:
            h_ctx, st = sublayers(h_ctx, cmod, zero_states, None, False)
        x, _ = sublayers(x, lat, tuple(st), rows, last)
    return x
```

```python
import functools

import jax
import jax.numpy as jnp
import numpy as np
from jax import lax
from jax.experimental import pallas as pl
from jax.experimental.pallas import tpu as pltpu

F32 = jnp.float32
BF16 = jnp.bfloat16
I32 = jnp.int32

D_MODEL = 1024
GRID_W = 64
N_HEADS = 4
HEAD_DIM = 128
MLSTM_WIDTH = N_HEADS * HEAD_DIM
N_ROW_HEADS = N_HEADS // 2
CONV_CH = 256
CONV_K = 31
FOURIER_CH = 256
FOURIER_GROUP_CH = 64
N_GATES = 4
N_EXPERTS = 16
EC_CAPACITY_FACTOR = 2
N_ADA = 6
EPS = 1e-6

LANES = 128
MXU_DIM = 256
MLSTM_CHUNK = 128
MAX_CHUNKS = 16
CONV_HALO = 16
CONV_TAIL = 128
WIN_TILE = 256
MIB = 1024 * 1024

SEC_QKV = (0, 3 * MLSTM_WIDTH)
SEC_G = (SEC_QKV[1], SEC_QKV[1] + LANES)
SEC_O = (SEC_G[1], SEC_G[1] + MLSTM_WIDTH)
SEC_CACG = (SEC_O[1], SEC_O[1] + 2 * CONV_CH)
SEC_FR = (SEC_CACG[1], SEC_CACG[1] + FOURIER_CH)
IN_COLS_PAD = SEC_FR[1]

ST_NF = 2 * HEAD_DIM
ST_NB = ST_NF + 16
ST_ROWS = ST_NB + 16
UP_ROWS = 2 * HEAD_DIM + 16


def _dot(a, b):
    return jnp.dot(a, b, preferred_element_type=F32)


def _dot_nt(a, b):
    return lax.dot_general(a, b, (((1,), (1,)), ((), ())), preferred_element_type=F32)


def _split_bf16(a):
    hi = a.astype(BF16)
    lo = (a - hi.astype(F32)).astype(BF16)
    return hi, lo


def _dot3(a, w):
    ah, al = _split_bf16(a)
    wh, wl = _split_bf16(w)
    return _dot(ah, wh) + (_dot(al, wh) + _dot(ah, wl))


def _modnorm(x, g, sh, sc):
    ms = jnp.mean(x * x, axis=-1, keepdims=True)
    return x * lax.rsqrt(ms + EPS) * g * (1.0 + sc) + sh


def _rows(n_rows, width, *vals):
    rid = lax.broadcasted_iota(I32, (n_rows, width), 0)
    out = jnp.zeros((n_rows, width), F32)
    for i, v in enumerate(vals):
        out = jnp.where(rid == i, v, out)
    return out


def _mod_spec(mod):
    mods, layer, row = mod
    _, _, n_ada, d = mods.shape
    return pl.BlockSpec((None, None, n_ada, d), lambda b, i, *_: (layer, b if row is None else row, 0, 0))


def _params(sem, vmem_mib):
    return pltpu.CompilerParams(dimension_semantics=sem, vmem_limit_bytes=vmem_mib * MIB)


def _ada_kernel(c_ref, w_ref, b_ref, o_ref):
    c = c_ref[...]
    o_ref[...] = _dot3(c * jax.nn.sigmoid(c), w_ref[...]) + b_ref[...]


def _ada(cs, w_ada, b_ada):
    n_layers, d, n6 = w_ada.shape
    r = cs.shape[0]
    tn = 1536
    return pl.pallas_call(
        _ada_kernel,
        grid=(n_layers, n6 // tn),
        in_specs=[
            pl.BlockSpec((r, d), lambda l, j: (0, 0)),
            pl.BlockSpec((None, d, tn), lambda l, j: (l, 0, j)),
            pl.BlockSpec((None, 1, tn), lambda l, j: (l, 0, j)),
        ],
        out_specs=pl.BlockSpec((None, r, tn), lambda l, j: (l, 0, j)),
        out_shape=jax.ShapeDtypeStruct((n_layers, r, n6), F32),
        compiler_params=_params(("arbitrary", "arbitrary"), 40),
        name="ada_mod",
    )(cs, w_ada, b_ada.reshape(n_layers, 1, n6))


def _inproj_kernel(x_ref, mod_ref, g_ref, w_ref, bg_ref, *out_refs, full, colmajor):
    qkv_refs, gates_ref = out_refs[:6], out_refs[6]
    xn = _modnorm(x_ref[...], g_ref[...], mod_ref[0:1, :], mod_ref[1:2, :]).astype(BF16)
    tm = xn.shape[0]
    w = MLSTM_WIDTH
    for i in range(3):
        r = _dot(xn, w_ref[:, i * w:(i + 1) * w])
        if i == 1:
            r = r * HEAD_DIM ** -0.5
        row_ref, col_ref = qkv_refs[2 * i], qkv_refs[2 * i + 1]
        for hh in range(N_ROW_HEADS):
            row_ref[hh] = r[:, hh * HEAD_DIM:(hh + 1) * HEAD_DIM].astype(BF16)
            piece = r[:, (N_ROW_HEADS + hh) * HEAD_DIM:(N_ROW_HEADS + hh + 1) * HEAD_DIM]
            if colmajor:
                piece = jnp.swapaxes(piece.reshape(tm // GRID_W, GRID_W, HEAD_DIM), 0, 1)
            col_ref[hh] = piece.astype(BF16)
    gates_ref[...] = _dot(xn, w_ref[:, SEC_G[0]:SEC_G[1]]) + bg_ref[...]
    if full:
        o_ref, cacg_ref, fr_ref = out_refs[7:]
        o_ref[...] = _dot(xn, w_ref[:, SEC_O[0]:SEC_O[1]]).astype(BF16)
        cacg_ref[...] = _dot(xn, w_ref[:, SEC_CACG[0]:SEC_CACG[1]])
        fr_ref[...] = _dot(xn, w_ref[:, SEC_FR[0]:SEC_FR[1]]).astype(BF16)


def _inproj(x, mod, g, w, bg, full, rows):
    bsz, t, d = x.shape
    tm = min(t, 1024)
    nh = N_ROW_HEADS
    ncols = IN_COLS_PAD if full else SEC_G[1]
    tok = lambda n: pl.BlockSpec((None, tm, n), lambda b, i: (b, i, 0))
    row_shape = jax.ShapeDtypeStruct((bsz, nh, t, HEAD_DIM), BF16)
    row_spec = pl.BlockSpec((None, nh, tm, HEAD_DIM), lambda b, i: (b, 0, i, 0))
    if rows is None:
        col_shape, col_spec = row_shape, row_spec
    else:
        col_shape = jax.ShapeDtypeStruct((bsz, nh, GRID_W, rows, HEAD_DIM), BF16)
        col_spec = pl.BlockSpec((None, nh, GRID_W, tm // GRID_W, HEAD_DIM), lambda b, i: (b, 0, 0, i, 0))
    out_shape = [row_shape, col_shape] * 3 + [jax.ShapeDtypeStruct((bsz, t, LANES), F32)]
    out_specs = [row_spec, col_spec] * 3 + [tok(LANES)]
    if full:
        out_shape += [jax.ShapeDtypeStruct((bsz, t, MLSTM_WIDTH), BF16),
                      jax.ShapeDtypeStruct((bsz, t, 2 * CONV_CH), F32),
                      jax.ShapeDtypeStruct((bsz, t, FOURIER_CH), BF16)]
        out_specs += [tok(MLSTM_WIDTH), tok(2 * CONV_CH), tok(FOURIER_CH)]
    outs = pl.pallas_call(
        functools.partial(_inproj_kernel, full=full, colmajor=rows is not None),
        grid=(bsz, t // tm),
        in_specs=[tok(d), _mod_spec(mod),
                  pl.BlockSpec((1, d), lambda b, i: (0, 0)),
                  pl.BlockSpec((d, ncols), lambda b, i: (0, 0)),
                  pl.BlockSpec((1, LANES), lambda b, i: (0, 0))],
        out_specs=out_specs,
        out_shape=out_shape,
        compiler_params=_params(("parallel", "parallel"), 56),
        name="inproj_full" if full else "inproj_qkvg",
    )(x, mod[0], g.reshape(1, d), w, bg)
    qkv = [a.reshape(bsz, nh, t, HEAD_DIM) for a in outs[:6]]
    return qkv, outs[6], outs[7:]


def _logsig(x):
    return jnp.minimum(x, 0.0) - jnp.log1p(jnp.exp(-jnp.abs(x)))


def _tri_masks(n):
    a0 = lax.broadcasted_iota(I32, (n, n), 0)
    a1 = lax.broadcasted_iota(I32, (n, n), 1)
    return a0 <= a1, a0 >= a1


def _split3(x):
    hi = x.astype(BF16).astype(F32)
    r1 = x - hi
    mid = r1.astype(BF16).astype(F32)
    return hi, mid, r1 - mid


def _mlstm_consts(tri_scr, ones_scr):
    lc = MLSTM_CHUNK
    u = lax.broadcasted_iota(I32, (lc, 2 * lc), 0)
    t2 = lax.broadcasted_iota(I32, (lc, 2 * lc), 1)
    tri_scr[...] = jnp.where(((t2 < lc) & (u <= t2)) | ((t2 >= lc) & (u >= t2 - lc)), 1.0, 0.0).astype(BF16)
    r = lax.broadcasted_iota(I32, (16, 2 * lc), 0)
    t16 = lax.broadcasted_iota(I32, (16, 2 * lc), 1)
    ones_scr[...] = jnp.where(((r < 3) & (t16 < lc)) | ((r >= 3) & (r < 6) & (t16 >= lc)), 1.0, 0.0).astype(BF16)


def _mlstm_gates(head, g_ref, tri_scr, rowg_scr, stat_scr, n_chunks, cm_rows):
    lc = MLSTM_CHUNK
    lane2 = lax.broadcasted_iota(I32, (16, 2 * LANES), 1)
    pick = jnp.where(jnp.bitwise_and(lane2, LANES - 1) == N_GATES * head + lax.broadcasted_iota(I32, (16, 2 * LANES), 0),
                     1.0, 0.0).astype(BF16)
    chunk_id = lax.broadcasted_iota(I32, (MAX_CHUNKS, lc), 0)

    def gather(c, acc):
        if cm_rows is None:
            g = g_ref[pl.ds(pl.multiple_of(c * lc, lc), lc), :]
        else:
            cols = lc // cm_rows
            g = jnp.concatenate([g_ref[pl.ds(c * cols + j, cm_rows, stride=GRID_W), :] for j in range(cols)], axis=0)
        rows = _dot_nt(pick, jnp.concatenate(_split_bf16(g), axis=1))
        return tuple(jnp.where(chunk_id == c, rows[i:i + 1], a) for i, a in enumerate(acc))

    zeros = jnp.zeros((MAX_CHUNKS, lc), F32)
    i_f, f_f, i_b, f_b = lax.fori_loop(0, n_chunks, gather, (zeros,) * N_GATES, unroll=min(4, n_chunks))
    lf_f, lf_b = _logsig(f_f), _logsig(f_b)
    b_f = sum(_dot(part.astype(BF16), tri_scr[:, 0:lc]) for part in _split3(lf_f))
    b_b = sum(_dot(part.astype(BF16), tri_scr[:, lc:]) for part in _split3(lf_b))
    bl_f = jnp.sum(lf_f, axis=1, keepdims=True)
    bl_b = jnp.sum(lf_b, axis=1, keepdims=True)
    ci_f, ci_b = i_f - b_f, i_b - b_b
    ml_f = jnp.max(ci_f + bl_f, axis=1, keepdims=True)
    ml_b = jnp.max(ci_b + bl_b, axis=1, keepdims=True)
    for r, val in enumerate((jnp.exp(ci_f + bl_f - ml_f), jnp.exp(ci_b + bl_b - ml_b), b_f, b_b, ci_f, ci_b)):
        rowg_scr[:, r, :] = val
    for r, val in enumerate((bl_f, bl_b, ml_f, ml_b)):
        stat_scr[:, r, :] = jnp.broadcast_to(val, (MAX_CHUNKS, HEAD_DIM))


def _mlstm_increment(c, k_ref, v_ref, rowg_scr, up_scr, vt_scr):
    lc = MLSTM_CHUNK
    s = pl.multiple_of(c * lc, lc)
    v_t = v_ref[pl.ds(s, lc), :].astype(F32).T
    rg = rowg_scr[c]
    w_f, w_b = rg[0:1], rg[1:2]
    lhs = jnp.concatenate([(v_t * w_f).astype(BF16), (v_t * w_b).astype(BF16),
                           _rows(16, lc, w_f, w_b).astype(BF16)], axis=0)
    up_scr[c] = _dot(lhs, k_ref[pl.ds(s, lc), :])
    if vt_scr is not None:
        vt_scr[c] = v_t.astype(BF16)


def _mlstm_state_step(cc, carry, up_scr, stat_scr, st_scr, mst_scr, fwd):
    cmat, n, m = carry
    off, row = (0, 0) if fwd else (HEAD_DIM, 1)
    if st_scr is not None:
        st_scr[cc, off:off + HEAD_DIM, :] = cmat.astype(BF16)
        nrow = ST_NF if fwd else ST_NB
        st_scr[cc, nrow:nrow + 16, :] = _rows(16, HEAD_DIM, n).astype(BF16)
        mst_scr[cc, row:row + 1, :] = jnp.broadcast_to(m, (1, HEAD_DIM))
    st = stat_scr[cc]
    bl, ml = st[row:row + 1, 0:1], st[2 + row:3 + row, 0:1]
    m_new = jnp.maximum(bl + m, ml)
    keep = jnp.exp(bl + m - m_new)
    gain = jnp.exp(ml - m_new)
    cmat = keep * cmat + gain * up_scr[cc, off:off + HEAD_DIM, :]
    n = keep * n + gain * up_scr[cc, 2 * HEAD_DIM + row:2 * HEAD_DIM + row + 1, :]
    return cmat, n, m_new


def _mlstm_outputs(work, ones_scr):
    lc = MLSTM_CHUNK
    starts = [pl.multiple_of(c * lc, lc) for _, c in work]
    le, ge = _tri_masks(lc)
    qs = [ch[0][pl.ds(s, lc), :] for (ch, _), s in zip(work, starts)]
    qk_ts = [_dot_nt(ch[1][pl.ds(s, lc), :], q) for (ch, _), s, q in zip(work, starts, qs)]
    g_ts = [_dot_nt(ch[5][c], q) for (ch, c), q in zip(work, qs)]
    rgs = [ch[4][c] for ch, c in work]
    cis = [lax.dot_general(_rows(16, lc, *_split3(rg[4:5]), *_split3(rg[5:6])).astype(BF16), ones_scr[...],
                           (((0,), (0,)), ((), ())), preferred_element_type=F32) for rg in rgs]

    def direction(qk_t, valid, b_row, ci, m_st, qn):
        dlog = jnp.where(valid, b_row + ci, -jnp.inf)
        inter = b_row + m_st
        m_t = jnp.maximum(inter, jnp.max(dlog, axis=0, keepdims=True))
        s_t = qk_t * jnp.exp(dlog - m_t)
        w_inter = jnp.exp(inter - m_t)
        den = w_inter * qn + jnp.sum(s_t, axis=0, keepdims=True)
        r = 1.0 / jnp.maximum(jnp.abs(den), jnp.exp(-m_t))
        return r * s_t, r * w_inter

    mixed = []
    for (ch, c), qk_t, g_t, rg, ci in zip(work, qk_ts, g_ts, rgs, cis):
        ms = ch[6][c]
        sf, cf = direction(qk_t, le, rg[2:3], ci[:, 0:lc], ms[0:1, 0:1], g_t[ST_NF:ST_NF + 1])
        sb, cb = direction(qk_t, ge, rg[3:4], ci[:, lc:], ms[1:2, 0:1], g_t[ST_NB:ST_NB + 1])
        mixed.append(((sf + sb).astype(BF16), cf * g_t[0:HEAD_DIM] + cb * g_t[HEAD_DIM:2 * HEAD_DIM]))
    h_ts = [inter_part + _dot(ch[7][c], s_mix) for (ch, c), (s_mix, inter_part) in zip(work, mixed)]
    for (ch, c), s, h_t in zip(work, starts, h_ts):
        gh_ref, h_ref, cm_rows = ch[2], ch[3], ch[8]
        h = (h_t * lax.rsqrt(jnp.mean(h_t * h_t, axis=0, keepdims=True) + EPS) * gh_ref[...]).T
        if cm_rows is None:
            h_ref[pl.ds(s, lc), :] = h
        else:
            cols = lc // cm_rows
            for j in range(cols):
                h_ref[pl.ds(c * cols + j, cm_rows, stride=GRID_W), :] = h[j * cm_rows:(j + 1) * cm_rows]


def _mlstm_kernel(*refs, n_chunks, with_outputs, cm_rows):
    n_slot_in = 6
    g_ref = refs[0]
    slots_in = [refs[1:1 + n_slot_in], refs[1 + n_slot_in:1 + 2 * n_slot_in]]
    rest = refs[1 + 2 * n_slot_in:]
    if with_outputs:
        h_refs, rest = rest[:2], rest[2:]
    state_out, scratch = rest[:4], rest[4:]
    tri_scr, ones_scr, scratch = scratch[0], scratch[1], scratch[2:]
    n_scr = 6 if with_outputs else 3
    slots_scr = [scratch[:n_scr], scratch[n_scr:2 * n_scr]]
    slot_rows = [None, cm_rows]
    _mlstm_consts(tri_scr, ones_scr)
    for slot, scr in enumerate(slots_scr):
        _mlstm_gates(pl.program_id(1) + slot * N_ROW_HEADS, g_ref, tri_scr, scr[0], scr[1], n_chunks, slot_rows[slot])

    def increments(c, _):
        for (q_ref, k_ref, v_ref, c0_ref, n0_ref, gh_ref), scr in zip(slots_in, slots_scr):
            _mlstm_increment(c, k_ref, v_ref, scr[0], scr[2], scr[5] if with_outputs else None)
        return 0

    lax.fori_loop(0, n_chunks, increments, 0, unroll=min(4, n_chunks))

    def states(i, carry):
        out = []
        for slot, scr in enumerate(slots_scr):
            st_scr, mst_scr = (scr[3], scr[4]) if with_outputs else (None, None)
            out.append(_mlstm_state_step(i, carry[2 * slot], scr[2], scr[1], st_scr, mst_scr, True))
            out.append(_mlstm_state_step(n_chunks - 1 - i, carry[2 * slot + 1], scr[2], scr[1], st_scr, mst_scr, False))
        return tuple(out)

    init = []
    for (q_ref, k_ref, v_ref, c0_ref, n0_ref, gh_ref) in slots_in:
        init.append((c0_ref[0], n0_ref[0:1, :], n0_ref[2:3, 0:1]))
        init.append((c0_ref[1], n0_ref[1:2, :], n0_ref[3:4, 0:1]))
    final = lax.fori_loop(0, n_chunks, states, tuple(init))
    for slot in range(2):
        c_out, n_out = state_out[2 * slot], state_out[2 * slot + 1]
        (cf, nf, mf), (cb, nb, mb) = final[2 * slot], final[2 * slot + 1]
        c_out[0] = cf
        c_out[1] = cb
        n_out[...] = _rows(8, HEAD_DIM, nf, nb, mf, mb)

    if with_outputs:
        chains = [(q_ref, k_ref, gh_ref, h_refs[slot], scr[0], scr[3], scr[4], scr[5], slot_rows[slot])
                  for slot, ((q_ref, k_ref, v_ref, c0_ref, n0_ref, gh_ref), scr) in enumerate(zip(slots_in, slots_scr))]

        per_iter = 4 if n_chunks % 4 == 0 else 2 if n_chunks % 2 == 0 else 1

        def outputs(i, _):
            _mlstm_outputs([(ch, i * per_iter + u) for u in range(per_iter) for ch in chains], ones_scr)
            return 0

        lax.fori_loop(0, n_chunks // per_iter, outputs, 0, unroll=min(2, n_chunks // per_iter))


def _mlstm(qkv, gates, states, gh, with_outputs, cm_rows):
    bsz, npair, t, dh = qkv[0].shape
    nc = t // MLSTM_CHUNK
    lc = MLSTM_CHUNK
    seq = pl.BlockSpec((None, None, t, dh), lambda b, p: (b, p, 0, 0))
    cst = pl.BlockSpec((None, None, 2, dh, dh), lambda b, p: (b, p, 0, 0, 0))
    nst = pl.BlockSpec((None, None, 8, dh), lambda b, p: (b, p, 0, 0))

    def slot_specs(head_off):
        return [seq, seq, seq, cst, nst, pl.BlockSpec((None, dh, 1), lambda b, p: (p + head_off, 0, 0))]

    st_shapes = [jax.ShapeDtypeStruct((bsz, npair, 2, dh, dh), F32), jax.ShapeDtypeStruct((bsz, npair, 8, dh), F32)] * 2
    out_shape, out_specs = st_shapes, [cst, nst, cst, nst]
    assert nc <= MAX_CHUNKS
    slot_scratch = [pltpu.VMEM((MAX_CHUNKS, 8, lc), F32), pltpu.VMEM((MAX_CHUNKS, 8, dh), F32),
                    pltpu.VMEM((nc, UP_ROWS, dh), F32)]
    if with_outputs:
        h_shape = jax.ShapeDtypeStruct((bsz, t, npair * dh), F32)
        h_spec = pl.BlockSpec((None, t, dh), lambda b, p: (b, 0, p))
        out_shape = [h_shape, h_shape] + out_shape
        out_specs = [h_spec, h_spec] + out_specs
        slot_scratch += [pltpu.VMEM((nc, ST_ROWS, dh), BF16), pltpu.VMEM((nc, 8, dh), F32), pltpu.VMEM((nc, dh, lc), BF16)]
    q_r, q_c, k_r, k_c, v_r, v_c = qkv
    c_r, n_r, c_c, n_c = states
    return pl.pallas_call(
        functools.partial(_mlstm_kernel, n_chunks=nc, with_outputs=with_outputs, cm_rows=cm_rows),
        grid=(bsz, npair),
        in_specs=[pl.BlockSpec((None, t, LANES), lambda b, p: (b, 0, 0))] + slot_specs(0) + slot_specs(N_ROW_HEADS),
        out_specs=out_specs,
        out_shape=out_shape,
        scratch_shapes=[pltpu.VMEM((lc, 2 * lc), BF16), pltpu.VMEM((16, 2 * lc), BF16)] + slot_scratch * 2,
        compiler_params=_params(("parallel", "parallel"), 48),
        name="mlstm_full" if with_outputs else "mlstm_states",
    )(gates, q_r, k_r, v_r, c_r, n_r, gh, q_c, k_c, v_c, c_c, n_c, gh)


def _conv_kernel(cacg_ref, w_ref, cb_ref, lg_ref, lb_ref, o_ref, ysh, yconv, *, t):
    rt = 64
    ysh[0, 0:CONV_HALO, :] = jnp.zeros((CONV_HALO, CONV_CH), F32)
    ysh[0, CONV_HALO + t:t + CONV_TAIL, :] = jnp.zeros((CONV_TAIL - CONV_HALO, CONV_CH), F32)

    def fill(r, _):
        s = pl.multiple_of(r * rt, rt)
        a = cacg_ref[pl.ds(s, rt), 0:CONV_CH]
        g = cacg_ref[pl.ds(s, rt), CONV_CH:2 * CONV_CH]
        ysh[0, pl.ds(CONV_HALO + s, rt), :] = a * jax.nn.sigmoid(g)
        return 0

    lax.fori_loop(0, t // rt, fill, 0)

    def shift(r, _):
        s = pl.multiple_of(r * rt, rt)
        win = ysh[0, pl.ds(s, rt + 8), :]
        for res in range(1, 8):
            ysh[res, pl.ds(s, rt), :] = pltpu.roll(win, rt + 8 - res, axis=0)[0:rt, :]
        return 0

    lax.fori_loop(0, t // rt + 1, shift, 0)

    def tile(r, _):
        s = pl.multiple_of(r * rt, rt)
        acc = jnp.zeros((rt, CONV_CH), F32)
        for kk in range(CONV_K):
            off = kk + CONV_HALO - CONV_K // 2
            rows = ysh[off % 8, pl.ds(pl.multiple_of(s + 8 * (off // 8), 8), rt), :]
            acc = acc + w_ref[kk:kk + 1, :] * rows
        yconv[pl.ds(s, rt), :] = acc + cb_ref[...]
        return 0

    lax.fori_loop(0, t // rt, tile, 0)

    def norm(r, _):
        s = pl.multiple_of(r * rt, rt)
        y = yconv[pl.ds(s, rt), :]
        mu = jnp.mean(y, axis=-1, keepdims=True)
        yc = y - mu
        var = jnp.mean(yc * yc, axis=-1, keepdims=True)
        z = yc * lax.rsqrt(var + EPS) * lg_ref[...] + lb_ref[...]
        o_ref[pl.ds(s, rt), :] = (z * jax.nn.sigmoid(z)).astype(BF16)
        return 0

    lax.fori_loop(0, t // rt, norm, 0, unroll=4)


def _conv(cacg, conv_w, conv_b, ln_g, ln_b):
    bsz, t, _ = cacg.shape
    row = lambda a: a.reshape(1, CONV_CH)
    vec = pl.BlockSpec((1, CONV_CH), lambda b: (0, 0))
    return pl.pallas_call(
        functools.partial(_conv_kernel, t=t),
        grid=(bsz,),
        in_specs=[pl.BlockSpec((None, t, 2 * CONV_CH), lambda b: (b, 0, 0)),
                  pl.BlockSpec((CONV_K + 1, CONV_CH), lambda b: (0, 0)), vec, vec, vec],
        out_specs=pl.BlockSpec((None, t, CONV_CH), lambda b: (b, 0, 0)),
        out_shape=jax.ShapeDtypeStruct((bsz, t, CONV_CH), BF16),
        scratch_shapes=[pltpu.VMEM((8, t + CONV_TAIL, CONV_CH), F32), pltpu.VMEM((t, CONV_CH), F32)],
        compiler_params=_params(("parallel",), 40),
        name="conformer_conv",
    )(cacg, jnp.pad(conv_w, ((0, 1), (0, 0))), row(conv_b), row(ln_g), row(ln_b))


@functools.lru_cache(maxsize=None)
def _dft_tables(t):
    j = np.arange(t, dtype=np.int64)
    ang = 2.0 * np.pi * ((j[:, None] * j[None, :]) % t).astype(np.float64) / t
    wt = np.concatenate([np.cos(ang), -np.sin(ang)], axis=1).astype(np.float32)
    c = np.arange(FOURIER_CH, dtype=np.int64)
    grp, idx = c // FOURIER_GROUP_CH, c % FOURIER_GROUP_CH
    same = grp[:, None] == grp[None, :]
    angc = 2.0 * np.pi * ((idx[:, None] * idx[None, :]) % FOURIER_GROUP_CH).astype(np.float64) / FOURIER_GROUP_CH
    cc = np.where(same, np.cos(angc), 0.0).astype(np.float32)
    sc = np.where(same, np.sin(angc), 0.0).astype(np.float32)
    return wt, cc, sc


def _fourier_kernel(z_ref, cc_ref, sc_ref, wt_ref, o_ref, zcs, *, t, scale):
    bsz, _, ch = z_ref.shape

    @pl.when(pl.program_id(0) == 0)
    def _():
        for b in range(bsz):
            zcs[0:t, b * ch:(b + 1) * ch] = _dot(z_ref[b], cc_ref[...]).astype(BF16)
            zcs[t:2 * t, b * ch:(b + 1) * ch] = _dot(z_ref[b], sc_ref[...]).astype(BF16)

    r = _dot(wt_ref[...], zcs[...]) * scale
    for b in range(bsz):
        o_ref[b] = r[:, b * ch:(b + 1) * ch].astype(BF16)


def _fourier(z):
    bsz, t, ch = z.shape
    wt, cc, sc = _dft_tables(t)
    tm = min(t, 512)
    mat = pl.BlockSpec((ch, ch), lambda i: (0, 0))
    return pl.pallas_call(
        functools.partial(_fourier_kernel, t=t, scale=float((t * FOURIER_GROUP_CH) ** -0.5)),
        grid=(t // tm,),
        in_specs=[pl.BlockSpec((bsz, t, ch), lambda i: (0, 0, 0)), mat, mat,
                  pl.BlockSpec((tm, 2 * t), lambda i: (i, 0))],
        out_specs=pl.BlockSpec((bsz, tm, ch), lambda i: (0, i, 0)),
        out_shape=jax.ShapeDtypeStruct((bsz, t, ch), BF16),
        scratch_shapes=[pltpu.VMEM((2 * t, bsz * ch), BF16)],
        compiler_params=_params(("arbitrary",), 56),
        name="fourier_mix",
    )(z, jnp.asarray(cc).astype(BF16), jnp.asarray(sc).astype(BF16), jnp.asarray(wt).astype(BF16))


def _outproj_kernel(x_ref, hr_ref, hc_ref, o_ref, cv_ref, fo_ref, w_ref, mod_ref, g2_ref, wr_ref,
                    x1_ref, xn_ref, aff_ref):
    half = MLSTM_WIDTH // 2
    a, b = MLSTM_WIDTH, MLSTM_WIDTH + CONV_CH
    og = jax.nn.sigmoid(o_ref[...].astype(F32))
    y = (_dot((hr_ref[...] * og[:, 0:half]).astype(BF16), w_ref[0:half, :])
         + _dot((hc_ref[...] * og[:, half:a]).astype(BF16), w_ref[half:a, :])
         + _dot(cv_ref[...], w_ref[a:b, :]) + _dot(fo_ref[...], w_ref[b:, :]))
    x1 = x_ref[...] + mod_ref[2:3, :] * y
    x1_ref[...] = x1
    xn = _modnorm(x1, g2_ref[...], mod_ref[3:4, :], mod_ref[4:5, :])
    xn_ref[...] = xn.astype(BF16)
    tm = xn.shape[0]
    r = _dot(jnp.concatenate(_split_bf16(xn), axis=0), jnp.concatenate(_split_bf16(wr_ref[...]), axis=1))
    logits = (r[:tm, :LANES] + r[:tm, LANES:]) + (r[tm:, :LANES] + r[tm:, LANES:])
    lane = lax.broadcasted_iota(I32, logits.shape, 1)
    logits = jnp.where(lane < N_EXPERTS, logits, -jnp.inf)
    e = jnp.exp(logits - jnp.max(logits, axis=-1, keepdims=True))
    aff_ref[...] = e / jnp.sum(e, axis=-1, keepdims=True)


def _outproj(x, hr, hc, o, cv, fo, w_out, mod, g2, wr):
    bsz, t, d = x.shape
    tm = min(t, 512)
    tok = lambda n: pl.BlockSpec((None, tm, n), lambda b, i: (b, i, 0))
    return pl.pallas_call(
        _outproj_kernel,
        grid=(bsz, t // tm),
        in_specs=[tok(d), tok(MLSTM_WIDTH // 2), tok(MLSTM_WIDTH // 2), tok(MLSTM_WIDTH), tok(CONV_CH), tok(FOURIER_CH),
                  pl.BlockSpec((d, d), lambda b, i: (0, 0)), _mod_spec(mod),
                  pl.BlockSpec((1, d), lambda b, i: (0, 0)),
                  pl.BlockSpec((d, LANES), lambda b, i: (0, 0))],
        out_specs=[tok(d), tok(d), tok(LANES)],
        out_shape=[jax.ShapeDtypeStruct((bsz, t, d), F32), jax.ShapeDtypeStruct((bsz, t, d), BF16),
                   jax.ShapeDtypeStruct((bsz, t, LANES), F32)],
        compiler_params=_params(("parallel", "parallel"), 40),
        name="outproj_router",
    )(x, hr, hc, o, cv, fo, w_out, mod[0], g2.reshape(1, d), wr)


def _select_kernel(a_ref, sp_ref, win_ref, *, cap, tile):
    a = a_ref[...]
    n = a.shape[0]
    blk = min(n, MXU_DIM)

    def count(mask):
        return jnp.sum(jnp.where(mask, 1.0, 0.0), axis=0, keepdims=True)

    def search(i, thr):
        cand = thr | jnp.left_shift(jnp.int32(1), 30 - i)
        return jnp.where(count(a >= lax.bitcast_convert_type(cand, F32)) >= cap, cand, thr)

    thr = lax.bitcast_convert_type(lax.fori_loop(0, 31, search, jnp.zeros((1, LANES), I32)), F32)
    gt = a > thr
    eq = a == thr
    before = jnp.where(lax.broadcasted_iota(I32, (blk, blk), 1) < lax.broadcasted_iota(I32, (blk, blk), 0), 1.0, 0.0).astype(BF16)

    def excl_cumsum(mask):
        x = jnp.where(mask, 1.0, 0.0)
        run = jnp.zeros((1, LANES), F32)
        parts = []
        for j in range(n // blk):
            xb = x[j * blk:(j + 1) * blk]
            parts.append(_dot(before, xb.astype(BF16)) + run)
            run = run + jnp.sum(xb, axis=0, keepdims=True)
        return jnp.concatenate(parts, axis=0)

    sel = gt | (eq & (excl_cumsum(eq) < cap - count(gt)))
    pos = excl_cumsum(sel).astype(I32)
    sp_ref[...] = jnp.where(sel, pos, -1)
    rid = lax.broadcasted_iota(I32, win_ref.shape, 0)
    win = jnp.zeros(win_ref.shape, I32)
    for ti in range(n // tile):
        t_sel, t_pos = sel[ti * tile:(ti + 1) * tile], pos[ti * tile:(ti + 1) * tile]
        win = jnp.where(rid == 2 * ti, jnp.min(jnp.where(t_sel, t_pos, cap), axis=0, keepdims=True), win)
        win = jnp.where(rid == 2 * ti + 1, jnp.max(jnp.where(t_sel, t_pos + 1, 0), axis=0, keepdims=True), win)
    win_ref[...] = win


def _select(aff_sets, cap, tile):
    ng, n, _ = aff_sets.shape
    win_rows = max(8, 2 * (n // tile))
    blk = pl.BlockSpec((None, n, LANES), lambda g: (g, 0, 0))
    return pl.pallas_call(
        functools.partial(_select_kernel, cap=cap, tile=tile),
        grid=(ng,),
        in_specs=[blk],
        out_specs=[blk, pl.BlockSpec((None, win_rows, LANES), lambda g: (g, 0, 0))],
        out_shape=[jax.ShapeDtypeStruct((ng, n, LANES), I32), jax.ShapeDtypeStruct((ng, win_rows, LANES), I32)],
        compiler_params=_params(("parallel",), 32),
        name="ec_select",
    )(aff_sets)


def _slot_window(win_ref, b, tile_idx, e, cap, width):
    lo = win_ref[b, 2 * tile_idx * N_EXPERTS + e]
    hi = win_ref[b, (2 * tile_idx + 1) * N_EXPERTS + e]
    st = jnp.minimum(lax.shift_left(lax.shift_right_logical(lo, 4), 4), cap - width)
    return pl.multiple_of(st, 16), hi - st <= width


def _ffn_kernel(win_ref, sp_ref, ar_ref, xn_ref, wg_ref, wu_ref, wd_ref, ys_ref, wgu_b, wd_b, xs_scr, gate_scr, *, cap, windowed):
    g, j = pl.program_id(0), pl.program_id(1)
    gsz, rows = wg_ref.shape[0], wg_ref.shape[1]
    nb, _, _, n = sp_ref.shape

    def cast_slice():
        rt = min(256, rows)
        dst = lax.rem(g, 2)
        for piece in range(rows // rt):
            src_rows = pl.ds(piece * rt, rt)
            dst_rows = pl.ds(pl.multiple_of(j * rows + piece * rt, rt), rt)
            for k in range(gsz):
                wgu_b[dst, k, 0, dst_rows, :] = wg_ref[k, src_rows, :].astype(BF16)
                wgu_b[dst, k, 1, dst_rows, :] = wu_ref[k, src_rows, :].astype(BF16)
                wd_b[dst, k, dst_rows, :] = wd_ref[k, src_rows, :].astype(BF16)

    def full_gather(k, bi):
        hit = lax.broadcasted_iota(I32, (cap, n), 0) == sp_ref[bi, k]
        xs_scr[k, bi * cap:(bi + 1) * cap, :] = _dot(jnp.where(hit, 1.0, 0.0).astype(BF16), xn_ref[bi])
        gate_scr[k, bi * cap:(bi + 1) * cap, :] = jnp.broadcast_to(
            jnp.sum(jnp.where(hit, ar_ref[bi, k], 0.0), axis=1, keepdims=True), (cap, LANES))

    def experts_ffn():
        src = lax.rem(g + 1, 2)
        xs = [xs_scr[k].astype(BF16) for k in range(gsz)]
        h1 = [_dot(xs[k], wgu_b[src, k, 0]) for k in range(gsz)]
        h2 = [_dot(xs[k], wgu_b[src, k, 1]) for k in range(gsz)]
        hid = [(h1[k] * jax.nn.sigmoid(h1[k]) * h2[k]).astype(BF16) for k in range(gsz)]
        ys = [(_dot(hid[k], wd_b[src, k]) * gate_scr[k, :, 0:1]).astype(BF16) for k in range(gsz)]
        for k in range(gsz):
            for bi in range(nb):
                ys_ref[bi, k] = ys[k][bi * cap:(bi + 1) * cap]

    def full_path():
        for k in range(gsz):
            for bi in range(nb):
                full_gather(k, bi)
        experts_ffn()

    @pl.when(g == 0)
    def _():
        cast_slice()

    if not windowed:
        @pl.when(g > 0)
        def _():
            cast_slice()
            full_path()
        return

    assert nb == 1
    width = cap // 2
    wins = [[_slot_window(win_ref, j, t, jnp.maximum(g - 1, 0) * gsz + k, cap, width) for t in range(n // WIN_TILE)]
            for k in range(gsz)]
    fits = functools.reduce(jnp.logical_and, [ok for per_expert in wins for _, ok in per_expert])

    @pl.when(jnp.logical_and(g > 0, fits))
    def _():
        cast_slice()
        xs_scr[...] = jnp.zeros(xs_scr.shape, F32)
        gate_scr[...] = jnp.zeros(gate_scr.shape, F32)
        for t in range(n // WIN_TILE):
            tok = slice(t * WIN_TILE, (t + 1) * WIN_TILE)
            for k in range(gsz):
                st = wins[k][t][0]
                hit = lax.broadcasted_iota(I32, (width, WIN_TILE), 0) + st == sp_ref[0, k, :, tok]
                xs_scr[k, pl.ds(st, width), :] += _dot(jnp.where(hit, 1.0, 0.0).astype(BF16), xn_ref[0, tok, :])
                gate_scr[k, pl.ds(st, width), :] += jnp.broadcast_to(
                    jnp.sum(jnp.where(hit, ar_ref[0, k, :, tok], 0.0), axis=1, keepdims=True), (width, LANES))
        experts_ffn()

    @pl.when(jnp.logical_and(g > 0, jnp.logical_not(fits)))
    def _():
        cast_slice()
        full_path()


def _ffn(win, sp_row, aff_row, xn, wg, wu, wd, layer, cap):
    bsz, n, d = xn.shape
    _, ne, _, hid = wg.shape
    assert d == hid
    nb = min(bsz, max(1, MXU_DIM // cap))
    nj = bsz // nb
    gsz = 2 if nj > 1 else 1
    rows = d // nj
    windowed = nb == 1 and n > WIN_TILE and cap // 2 >= LANES
    batch = lambda g, j: jnp.where(g == 0, 0, j)
    row = pl.BlockSpec((nb, gsz, 1, n), lambda g, j, *_: (batch(g, j), jnp.maximum(g - 1, 0), 0, 0))
    wspec = pl.BlockSpec((None, gsz, rows, hid), lambda g, j, *_: (layer, jnp.minimum(g, ne // gsz - 1), j, 0))
    grid_spec = pltpu.PrefetchScalarGridSpec(
        num_scalar_prefetch=1,
        grid=(ne // gsz + 1, nj),
        in_specs=[row, row, pl.BlockSpec((nb, n, d), lambda g, j, *_: (batch(g, j), 0, 0)), wspec, wspec, wspec],
        out_specs=pl.BlockSpec((nb, gsz, cap, d), lambda g, j, *_: (batch(g, j), jnp.maximum(g - 1, 0), 0, 0)),
        scratch_shapes=[pltpu.VMEM((2, gsz, 2, d, hid), BF16), pltpu.VMEM((2, gsz, hid, d), BF16),
                        pltpu.VMEM((gsz, nb * cap, d), F32), pltpu.VMEM((gsz, nb * cap, LANES), F32)])
    return pl.pallas_call(
        functools.partial(_ffn_kernel, cap=cap, windowed=windowed),
        grid_spec=grid_spec,
        out_shape=jax.ShapeDtypeStruct((bsz, ne, cap, d), BF16),
        compiler_params=_params(("arbitrary", "arbitrary"), 56),
        name="ec_ffn",
    )(win, sp_row, aff_row, xn, wg, wu, wd)


def _combine_kernel(win_ref, x_ref, spc_ref, ys_ref, mod_ref, gf_ref, o_ref, *, cap, final, pack):
    b, i = pl.program_id(0), pl.program_id(1)
    tm = spc_ref.shape[0]
    sub = tm if pack == 1 else WIN_TILE

    def finish(rows, acc):
        out = x_ref[rows, :] + mod_ref[5:6, :] * acc
        if final:
            out = out * lax.rsqrt(jnp.mean(out * out, axis=-1, keepdims=True) + EPS) * gf_ref[...]
        o_ref[rows, :] = out

    def full_width(spc):
        slot = lax.broadcasted_iota(I32, (sub, cap), 1)
        acc = jnp.zeros((sub, x_ref.shape[1]), F32)
        for e in range(N_EXPERTS):
            hit = jnp.where(spc[:, e:e + 1] == slot, 1.0, 0.0).astype(BF16)
            acc = acc + _dot(hit, ys_ref[e])
        return acc

    n_sub = tm // sub
    rows = [slice(t * sub, (t + 1) * sub) for t in range(n_sub)]
    spcs = [spc_ref[r, :] for r in rows]
    if pack == 1:
        for r, spc in zip(rows, spcs):
            finish(r, full_width(spc))
        return

    w = cap // pack
    wins = [[_slot_window(win_ref, b, i * n_sub + t, e, cap, w) for e in range(N_EXPERTS)] for t in range(n_sub)]
    fits = functools.reduce(jnp.logical_and, [ok for per_tile in wins for _, ok in per_tile])

    @pl.when(fits)
    def _():
        lane = lax.broadcasted_iota(I32, (sub, cap), 1)
        lane_row = lax.broadcasted_iota(I32, (1, cap), 1)
        accs = [jnp.zeros((sub, x_ref.shape[1]), F32) for _ in range(n_sub)]
        for p in range(N_EXPERTS // pack):
            es = list(range(p * pack, (p + 1) * pack))
            hits, ysws = [], []
            for t in range(n_sub):
                col = spcs[t][:, es[0]:es[0] + 1]
                off = jnp.zeros((1, cap), I32) + wins[t][es[0]][0]
                for q, e in enumerate(es[1:], 1):
                    col = jnp.where(lane >= q * w, spcs[t][:, e:e + 1], col)
                    off = jnp.where(lane_row >= q * w, wins[t][e][0] - q * w, off)
                hits.append(jnp.where(col - off == lane, 1.0, 0.0).astype(BF16))
                ysws.append(jnp.concatenate([ys_ref[e, pl.ds(wins[t][e][0], w), :] for e in es], axis=0))
            accs = [acc + _dot(hit, ysw) for acc, hit, ysw in zip(accs, hits, ysws)]
        for r, acc in zip(rows, accs):
            finish(r, acc)

    @pl.when(jnp.logical_not(fits))
    def _():
        for r, spc in zip(rows, spcs):
            finish(r, full_width(spc))


def _combine(x, sp_col, win, ys, mod, g_final, cap, final):
    bsz, n, d = x.shape
    pack = 4 if n > WIN_TILE and cap // 4 >= LANES // 2 else 1
    tm = min(n, 1024)
    grid_spec = pltpu.PrefetchScalarGridSpec(
        num_scalar_prefetch=1,
        grid=(bsz, n // tm),
        in_specs=[pl.BlockSpec((None, tm, d), lambda b, i, *_: (b, i, 0)),
                  pl.BlockSpec((None, tm, N_EXPERTS), lambda b, i, *_: (b, i, 0)),
                  pl.BlockSpec((None, N_EXPERTS, cap, d), lambda b, i, *_: (b, 0, 0, 0)),
                  _mod_spec(mod),
                  pl.BlockSpec((1, d), lambda b, i, *_: (0, 0))],
        out_specs=pl.BlockSpec((None, tm, d), lambda b, i, *_: (b, i, 0)))
    return pl.pallas_call(
        functools.partial(_combine_kernel, cap=cap, final=final, pack=pack),
        grid_spec=grid_spec,
        out_shape=jax.ShapeDtypeStruct((bsz, n, d), F32),
        compiler_params=_params(("parallel", "parallel"), 48),
        name="ec_combine_final" if final else "ec_combine",
    )(win, x, sp_col, ys, mod[0], g_final.reshape(1, d))


def _moe(x1, xn2, aff, wg, wu, wd, layer, mod, g_final, final):
    bsz, n, _ = x1.shape
    cap = EC_CAPACITY_FACTOR * n // N_EXPERTS
    per_group = LANES // N_EXPERTS
    ng = -(-bsz // per_group)
    aff_e = aff[:, :, :N_EXPERTS]
    sets = jnp.pad(aff_e, ((0, ng * per_group - bsz), (0, 0), (0, 0)))
    sets = sets.reshape(ng, per_group, n, N_EXPERTS).transpose(0, 2, 1, 3).reshape(ng, n, LANES)
    tile = min(n, WIN_TILE)
    selpos, win = _select(sets, cap, tile)
    ungroup = lambda a: a.reshape(ng, -1, per_group, N_EXPERTS).transpose(0, 2, 1, 3).reshape(ng * per_group, -1, N_EXPERTS)[:bsz]
    sp_col = ungroup(selpos)
    win = ungroup(win[:, :2 * (n // tile)]).reshape(bsz, -1)
    sp_row = sp_col.transpose(0, 2, 1).reshape(bsz, N_EXPERTS, 1, n)
    aff_row = aff_e.transpose(0, 2, 1).reshape(bsz, N_EXPERTS, 1, n)
    ys = _ffn(win, sp_row, aff_row, xn2, wg, wu, wd, layer, cap)
    return _combine(x1, sp_col, win, ys, mod, g_final, cap, final)


def kernel(x, c, ctx, c_ctx, w_ada, b_ada, g_norm1, w_in, b_gates, g_hnorm, conv_w, conv_b, conv_ln_g, conv_ln_b,
           w_out, g_norm2, w_router, w_e_gate, w_e_up, w_e_down, g_final):
    bsz, t, d = x.shape
    depth = w_ada.shape[0]
    rows = t // GRID_W
    n_cond = 16
    cs = jnp.concatenate([c, c_ctx[None, :], jnp.zeros((n_cond - bsz - 1, d), F32)], axis=0)
    mods = _ada(cs, w_ada, b_ada).reshape(depth, n_cond, N_ADA, d)

    npair = N_HEADS // 2
    zero_states = (jnp.zeros((bsz, npair, 2, HEAD_DIM, HEAD_DIM), F32), jnp.zeros((bsz, npair, 8, HEAD_DIM), F32)) * 2
    h_ctx = ctx
    for l in range(depth):
        last = l == depth - 1
        lat, cmod = (mods, l, None), (mods, l, bsz)
        w_l = w_in[l]
        qkvg_cols = 3 * MLSTM_WIDTH + N_GATES * N_HEADS
        w_p = jnp.concatenate([w_l[:, :3 * MLSTM_WIDTH],
                               jnp.pad(w_l[:, 3 * MLSTM_WIDTH:qkvg_cols], ((0, 0), (0, LANES - N_GATES * N_HEADS))),
                               w_l[:, qkvg_cols:]], axis=1).astype(BF16)
        bg = jnp.pad(b_gates[l].reshape(1, N_HEADS * N_GATES), ((0, 0), (0, LANES - N_HEADS * N_GATES)))
        w_o = w_out[l].astype(BF16)
        w_r = jnp.pad(w_router[l], ((0, 0), (0, LANES - N_EXPERTS)))
        gh = g_hnorm[l].reshape(N_HEADS, HEAD_DIM, 1)

        def mlstm(xs, md, states, grid_rows, full):
            qkv, gates, extra = _inproj(xs, md, g_norm1[l], w_p, bg, full, grid_rows)
            return _mlstm(qkv, gates, states, gh, full, grid_rows), extra

        def sublayers(xs, md, states, grid_rows, final):
            (hr, hc, *st), (o, cacg, fr) = mlstm(xs, md, states, grid_rows, True)
            cv = _conv(cacg, conv_w[l], conv_b[l], conv_ln_g[l], conv_ln_b[l])
            fo = _fourier(fr)
            x1, xn2, aff = _outproj(xs, hr, hc, o, cv, fo, w_o, md, g_norm2[l], w_r)
            return _moe(x1, xn2, aff, w_e_gate, w_e_up, w_e_down, l, md, g_final, final), st

        if last:
            st, _ = mlstm(h_ctx, cmod, zero_states, None, False)
        else:
            h_ctx, st = sublayers(h_ctx, cmod, zero_states, None, False)
        x, _ = sublayers(x, lat, tuple(st), rows, last)
    return x
```

```python
import functools

import jax
import jax.numpy as jnp
import numpy as np
from jax import lax
from jax.experimental import pallas as pl
from jax.experimental.pallas import tpu as pltpu

F32 = jnp.float32
BF16 = jnp.bfloat16
I32 = jnp.int32

D_MODEL = 1024
GRID_W = 64
N_HEADS = 4
HEAD_DIM = 128
MLSTM_WIDTH = N_HEADS * HEAD_DIM
N_ROW_HEADS = N_HEADS // 2
CONV_CH = 256
CONV_K = 31
FOURIER_CH = 256
FOURIER_GROUP_CH = 64
N_GATES = 4
N_EXPERTS = 16
EC_CAPACITY_FACTOR = 2
N_ADA = 6
EPS = 1e-6

LANES = 128
MXU_DIM = 256
MLSTM_CHUNK = 128
MAX_CHUNKS = 16
CONV_HALO = 16
CONV_TAIL = 128
WIN_TILE = 256
MIB = 1024 * 1024

SEC_QKV = (0, 3 * MLSTM_WIDTH)
SEC_G = (SEC_QKV[1], SEC_QKV[1] + LANES)
SEC_O = (SEC_G[1], SEC_G[1] + MLSTM_WIDTH)
SEC_CACG = (SEC_O[1], SEC_O[1] + 2 * CONV_CH)
SEC_FR = (SEC_CACG[1], SEC_CACG[1] + FOURIER_CH)
IN_COLS_PAD = SEC_FR[1]

ST_NF = 2 * HEAD_DIM
ST_NB = ST_NF + 16
ST_ROWS = ST_NB + 16
UP_ROWS = 2 * HEAD_DIM + 16


def _dot(a, b):
    return jnp.dot(a, b, preferred_element_type=F32)


def _dot_nt(a, b):
    return lax.dot_general(a, b, (((1,), (1,)), ((), ())), preferred_element_type=F32)


def _split_bf16(a):
    hi = a.astype(BF16)
    lo = (a - hi.astype(F32)).astype(BF16)
    return hi, lo


def _dot3(a, w):
    ah, al = _split_bf16(a)
    wh, wl = _split_bf16(w)
    return _dot(ah, wh) + (_dot(al, wh) + _dot(ah, wl))


def _modnorm(x, g, sh, sc):
    ms = jnp.mean(x * x, axis=-1, keepdims=True)
    return x * lax.rsqrt(ms + EPS) * g * (1.0 + sc) + sh


def _rows(n_rows, width, *vals):
    rid = lax.broadcasted_iota(I32, (n_rows, width), 0)
    out = jnp.zeros((n_rows, width), F32)
    for i, v in enumerate(vals):
        out = jnp.where(rid == i, v, out)
    return out


def _mod_spec(mod):
    mods, layer, row = mod
    _, _, n_ada, d = mods.shape
    return pl.BlockSpec((None, None, n_ada, d), lambda b, i, *_: (layer, b if row is None else row, 0, 0))


def _params(sem, vmem_mib):
    return pltpu.CompilerParams(dimension_semantics=sem, vmem_limit_bytes=vmem_mib * MIB)


def _ada_kernel(c_ref, w_ref, b_ref, o_ref):
    c = c_ref[...]
    o_ref[...] = _dot3(c * jax.nn.sigmoid(c), w_ref[...]) + b_ref[...]


def _ada(cs, w_ada, b_ada):
    n_layers, d, n6 = w_ada.shape
    r = cs.shape[0]
    tn = 1536
    return pl.pallas_call(
        _ada_kernel,
        grid=(n_layers, n6 // tn),
        in_specs=[
            pl.BlockSpec((r, d), lambda l, j: (0, 0)),
            pl.BlockSpec((None, d, tn), lambda l, j: (l, 0, j)),
            pl.BlockSpec((None, 1, tn), lambda l, j: (l, 0, j)),
        ],
        out_specs=pl.BlockSpec((None, r, tn), lambda l, j: (l, 0, j)),
        out_shape=jax.ShapeDtypeStruct((n_layers, r, n6), F32),
        compiler_params=_params(("arbitrary", "arbitrary"), 40),
        name="ada_mod",
    )(cs, w_ada, b_ada.reshape(n_layers, 1, n6))


def _inproj_kernel(x_ref, mod_ref, g_ref, w_ref, bg_ref, *out_refs, full, colmajor):
    qkv_refs, gates_ref = out_refs[:6], out_refs[6]
    xn = _modnorm(x_ref[...], g_ref[...], mod_ref[0:1, :], mod_ref[1:2, :]).astype(BF16)
    tm = xn.shape[0]
    w = MLSTM_WIDTH
    for i in range(3):
        r = _dot(xn, w_ref[:, i * w:(i + 1) * w])
        if i == 1:
            r = r * HEAD_DIM ** -0.5
        row_ref, col_ref = qkv_refs[2 * i], qkv_refs[2 * i + 1]
        for hh in range(N_ROW_HEADS):
            row_ref[hh] = r[:, hh * HEAD_DIM:(hh + 1) * HEAD_DIM].astype(BF16)
            piece = r[:, (N_ROW_HEADS + hh) * HEAD_DIM:(N_ROW_HEADS + hh + 1) * HEAD_DIM]
            if colmajor:
                piece = jnp.swapaxes(piece.reshape(tm // GRID_W, GRID_W, HEAD_DIM), 0, 1)
            col_ref[hh] = piece.astype(BF16)
    gates_ref[...] = _dot(xn, w_ref[:, SEC_G[0]:SEC_G[1]]) + bg_ref[...]
    if full:
        o_ref, cacg_ref, fr_ref = out_refs[7:]
        o_ref[...] = _dot(xn, w_ref[:, SEC_O[0]:SEC_O[1]]).astype(BF16)
        cacg_ref[...] = _dot(xn, w_ref[:, SEC_CACG[0]:SEC_CACG[1]])
        fr_ref[...] = _dot(xn, w_ref[:, SEC_FR[0]:SEC_FR[1]]).astype(BF16)


def _inproj(x, mod, g, w, bg, full, rows):
    bsz, t, d = x.shape
    tm = min(t, 1024)
    nh = N_ROW_HEADS
    ncols = IN_COLS_PAD if full else SEC_G[1]
    tok = lambda n: pl.BlockSpec((None, tm, n), lambda b, i: (b, i, 0))
    row_shape = jax.ShapeDtypeStruct((bsz, nh, t, HEAD_DIM), BF16)
    row_spec = pl.BlockSpec((None, nh, tm, HEAD_DIM), lambda b, i: (b, 0, i, 0))
    if rows is None:
        col_shape, col_spec = row_shape, row_spec
    else:
        col_shape = jax.ShapeDtypeStruct((bsz, nh, GRID_W, rows, HEAD_DIM), BF16)
        col_spec = pl.BlockSpec((None, nh, GRID_W, tm // GRID_W, HEAD_DIM), lambda b, i: (b, 0, 0, i, 0))
    out_shape = [row_shape, col_shape] * 3 + [jax.ShapeDtypeStruct((bsz, t, LANES), F32)]
    out_specs = [row_spec, col_spec] * 3 + [tok(LANES)]
    if full:
        out_shape += [jax.ShapeDtypeStruct((bsz, t, MLSTM_WIDTH), BF16),
                      jax.ShapeDtypeStruct((bsz, t, 2 * CONV_CH), F32),
                      jax.ShapeDtypeStruct((bsz, t, FOURIER_CH), BF16)]
        out_specs += [tok(MLSTM_WIDTH), tok(2 * CONV_CH), tok(FOURIER_CH)]
    outs = pl.pallas_call(
        functools.partial(_inproj_kernel, full=full, colmajor=rows is not None),
        grid=(bsz, t // tm),
        in_specs=[tok(d), _mod_spec(mod),
                  pl.BlockSpec((1, d), lambda b, i: (0, 0)),
                  pl.BlockSpec((d, ncols), lambda b, i: (0, 0)),
                  pl.BlockSpec((1, LANES), lambda b, i: (0, 0))],
        out_specs=out_specs,
        out_shape=out_shape,
        compiler_params=_params(("parallel", "parallel"), 56),
        name="inproj_full" if full else "inproj_qkvg",
    )(x, mod[0], g.reshape(1, d), w, bg)
    qkv = [a.reshape(bsz, nh, t, HEAD_DIM) for a in outs[:6]]
    return qkv, outs[6], outs[7:]


def _logsig(x):
    return jnp.minimum(x, 0.0) - jnp.log1p(jnp.exp(-jnp.abs(x)))


def _tri_masks(n):
    a0 = lax.broadcasted_iota(I32, (n, n), 0)
    a1 = lax.broadcasted_iota(I32, (n, n), 1)
    return a0 <= a1, a0 >= a1


def _split3(x):
    hi = x.astype(BF16).astype(F32)
    r1 = x - hi
    mid = r1.astype(BF16).astype(F32)
    return hi, mid, r1 - mid


def _mlstm_consts(tri_scr, ones_scr):
    lc = MLSTM_CHUNK
    u = lax.broadcasted_iota(I32, (lc, 2 * lc), 0)
    t2 = lax.broadcasted_iota(I32, (lc, 2 * lc), 1)
    tri_scr[...] = jnp.where(((t2 < lc) & (u <= t2)) | ((t2 >= lc) & (u >= t2 - lc)), 1.0, 0.0).astype(BF16)
    r = lax.broadcasted_iota(I32, (16, 2 * lc), 0)
    t16 = lax.broadcasted_iota(I32, (16, 2 * lc), 1)
    ones_scr[...] = jnp.where(((r < 3) & (t16 < lc)) | ((r >= 3) & (r < 6) & (t16 >= lc)), 1.0, 0.0).astype(BF16)


def _mlstm_gates(head, g_ref, tri_scr, rowg_scr, stat_scr, n_chunks, cm_rows):
    lc = MLSTM_CHUNK
    lane2 = lax.broadcasted_iota(I32, (16, 2 * LANES), 1)
    pick = jnp.where(jnp.bitwise_and(lane2, LANES - 1) == N_GATES * head + lax.broadcasted_iota(I32, (16, 2 * LANES), 0),
                     1.0, 0.0).astype(BF16)
    chunk_id = lax.broadcasted_iota(I32, (MAX_CHUNKS, lc), 0)

    def gather(c, acc):
        if cm_rows is None:
            g = g_ref[pl.ds(pl.multiple_of(c * lc, lc), lc), :]
        else:
            cols = lc // cm_rows
            g = jnp.concatenate([g_ref[pl.ds(c * cols + j, cm_rows, stride=GRID_W), :] for j in range(cols)], axis=0)
        rows = _dot_nt(pick, jnp.concatenate(_split_bf16(g), axis=1))
        return tuple(jnp.where(chunk_id == c, rows[i:i + 1], a) for i, a in enumerate(acc))

    zeros = jnp.zeros((MAX_CHUNKS, lc), F32)
    i_f, f_f, i_b, f_b = lax.fori_loop(0, n_chunks, gather, (zeros,) * N_GATES, unroll=min(4, n_chunks))
    lf_f, lf_b = _logsig(f_f), _logsig(f_b)
    b_f = sum(_dot(part.astype(BF16), tri_scr[:, 0:lc]) for part in _split3(lf_f))
    b_b = sum(_dot(part.astype(BF16), tri_scr[:, lc:]) for part in _split3(lf_b))
    bl_f = jnp.sum(lf_f, axis=1, keepdims=True)
    bl_b = jnp.sum(lf_b, axis=1, keepdims=True)
    ci_f, ci_b = i_f - b_f, i_b - b_b
    ml_f = jnp.max(ci_f + bl_f, axis=1, keepdims=True)
    ml_b = jnp.max(ci_b + bl_b, axis=1, keepdims=True)
    for r, val in enumerate((jnp.exp(ci_f + bl_f - ml_f), jnp.exp(ci_b + bl_b - ml_b), b_f, b_b, ci_f, ci_b)):
        rowg_scr[:, r, :] = val
    for r, val in enumerate((bl_f, bl_b, ml_f, ml_b)):
        stat_scr[:, r, :] = jnp.broadcast_to(val, (MAX_CHUNKS, HEAD_DIM))


def _mlstm_increment(c, k_ref, v_ref, rowg_scr, up_scr, vt_scr):
    lc = MLSTM_CHUNK
    s = pl.multiple_of(c * lc, lc)
    v_t = v_ref[pl.ds(s, lc), :].astype(F32).T
    rg = rowg_scr[c]
    w_f, w_b = rg[0:1], rg[1:2]
    lhs = jnp.concatenate([(v_t * w_f).astype(BF16), (v_t * w_b).astype(BF16),
                           _rows(16, lc, w_f, w_b).astype(BF16)], axis=0)
    up_scr[c] = _dot(lhs, k_ref[pl.ds(s, lc), :])
    if vt_scr is not None:
        vt_scr[c] = v_t.astype(BF16)


def _mlstm_state_step(cc, carry, up_scr, stat_scr, st_scr, mst_scr, fwd):
    cmat, n, m = carry
    off, row = (0, 0) if fwd else (HEAD_DIM, 1)
    if st_scr is not None:
        st_scr[cc, off:off + HEAD_DIM, :] = cmat.astype(BF16)
        nrow = ST_NF if fwd else ST_NB
        st_scr[cc, nrow:nrow + 16, :] = _rows(16, HEAD_DIM, n).astype(BF16)
        mst_scr[cc, row:row + 1, :] = jnp.broadcast_to(m, (1, HEAD_DIM))
    st = stat_scr[cc]
    bl, ml = st[row:row + 1, 0:1], st[2 + row:3 + row, 0:1]
    m_new = jnp.maximum(bl + m, ml)
    keep = jnp.exp(bl + m - m_new)
    gain = jnp.exp(ml - m_new)
    cmat = keep * cmat + gain * up_scr[cc, off:off + HEAD_DIM, :]
    n = keep * n + gain * up_scr[cc, 2 * HEAD_DIM + row:2 * HEAD_DIM + row + 1, :]
    return cmat, n, m_new


def _mlstm_outputs(work, ones_scr):
    lc = MLSTM_CHUNK
    starts = [pl.multiple_of(c * lc, lc) for _, c in work]
    le, ge = _tri_masks(lc)
    qs = [ch[0][pl.ds(s, lc), :] for (ch, _), s in zip(work, starts)]
    qk_ts = [_dot_nt(ch[1][pl.ds(s, lc), :], q) for (ch, _), s, q in zip(work, starts, qs)]
    g_ts = [_dot_nt(ch[5][c], q) for (ch, c), q in zip(work, qs)]
    rgs = [ch[4][c] for ch, c in work]
    cis = [lax.dot_general(_rows(16, lc, *_split3(rg[4:5]), *_split3(rg[5:6])).astype(BF16), ones_scr[...],
                           (((0,), (0,)), ((), ())), preferred_element_type=F32) for rg in rgs]

    def direction(qk_t, valid, b_row, ci, m_st, qn):
        dlog = jnp.where(valid, b_row + ci, -jnp.inf)
        inter = b_row + m_st
        m_t = jnp.maximum(inter, jnp.max(dlog, axis=0, keepdims=True))
        s_t = qk_t * jnp.exp(dlog - m_t)
        w_inter = jnp.exp(inter - m_t)
        den = w_inter * qn + jnp.sum(s_t, axis=0, keepdims=True)
        r = 1.0 / jnp.maximum(jnp.abs(den), jnp.exp(-m_t))
        return r * s_t, r * w_inter

    mixed = []
    for (ch, c), qk_t, g_t, rg, ci in zip(work, qk_ts, g_ts, rgs, cis):
        ms = ch[6][c]
        sf, cf = direction(qk_t, le, rg[2:3], ci[:, 0:lc], ms[0:1, 0:1], g_t[ST_NF:ST_NF + 1])
        sb, cb = direction(qk_t, ge, rg[3:4], ci[:, lc:], ms[1:2, 0:1], g_t[ST_NB:ST_NB + 1])
        mixed.append(((sf + sb).astype(BF16), cf * g_t[0:HEAD_DIM] + cb * g_t[HEAD_DIM:2 * HEAD_DIM]))
    h_ts = [inter_part + _dot(ch[7][c], s_mix) for (ch, c), (s_mix, inter_part) in zip(work, mixed)]
    for (ch, c), s, h_t in zip(work, starts, h_ts):
        gh_ref, h_ref, cm_rows = ch[2], ch[3], ch[8]
        h = (h_t * lax.rsqrt(jnp.mean(h_t * h_t, axis=0, keepdims=True) + EPS) * gh_ref[...]).T
        if cm_rows is None:
            h_ref[pl.ds(s, lc), :] = h
        else:
            cols = lc // cm_rows
            for j in range(cols):
                h_ref[pl.ds(c * cols + j, cm_rows, stride=GRID_W), :] = h[j * cm_rows:(j + 1) * cm_rows]


def _mlstm_kernel(*refs, n_chunks, with_outputs, cm_rows):
    n_slot_in = 6
    g_ref = refs[0]
    slots_in = [refs[1:1 + n_slot_in], refs[1 + n_slot_in:1 + 2 * n_slot_in]]
    rest = refs[1 + 2 * n_slot_in:]
    if with_outputs:
        h_refs, rest = rest[:2], rest[2:]
    state_out, scratch = rest[:4], rest[4:]
    tri_scr, ones_scr, scratch = scratch[0], scratch[1], scratch[2:]
    n_scr = 6 if with_outputs else 3
    slots_scr = [scratch[:n_scr], scratch[n_scr:2 * n_scr]]
    slot_rows = [None, cm_rows]
    _mlstm_consts(tri_scr, ones_scr)
    for slot, scr in enumerate(slots_scr):
        _mlstm_gates(pl.program_id(1) + slot * N_ROW_HEADS, g_ref, tri_scr, scr[0], scr[1], n_chunks, slot_rows[slot])

    def increments(c, _):
        for (q_ref, k_ref, v_ref, c0_ref, n0_ref, gh_ref), scr in zip(slots_in, slots_scr):
            _mlstm_increment(c, k_ref, v_ref, scr[0], scr[2], scr[5] if with_outputs else None)
        return 0

    lax.fori_loop(0, n_chunks, increments, 0, unroll=min(4, n_chunks))

    def states(i, carry):
        out = []
        for slot, scr in enumerate(slots_scr):
            st_scr, mst_scr = (scr[3], scr[4]) if with_outputs else (None, None)
            out.append(_mlstm_state_step(i, carry[2 * slot], scr[2], scr[1], st_scr, mst_scr, True))
            out.append(_mlstm_state_step(n_chunks - 1 - i, carry[2 * slot + 1], scr[2], scr[1], st_scr, mst_scr, False))
        return tuple(out)

    init = []
    for (q_ref, k_ref, v_ref, c0_ref, n0_ref, gh_ref) in slots_in:
        init.append((c0_ref[0], n0_ref[0:1, :], n0_ref[2:3, 0:1]))
        init.append((c0_ref[1], n0_ref[1:2, :], n0_ref[3:4, 0:1]))
    final = lax.fori_loop(0, n_chunks, states, tuple(init))
    for slot in range(2):
        c_out, n_out = state_out[2 * slot], state_out[2 * slot + 1]
        (cf, nf, mf), (cb, nb, mb) = final[2 * slot], final[2 * slot + 1]
        c_out[0] = cf
        c_out[1] = cb
        n_out[...] = _rows(8, HEAD_DIM, nf, nb, mf, mb)

    if with_outputs:
        chains = [(q_ref, k_ref, gh_ref, h_refs[slot], scr[0], scr[3], scr[4], scr[5], slot_rows[slot])
                  for slot, ((q_ref, k_ref, v_ref, c0_ref, n0_ref, gh_ref), scr) in enumerate(zip(slots_in, slots_scr))]

        per_iter = 4 if n_chunks % 4 == 0 else 2 if n_chunks % 2 == 0 else 1

        def outputs(i, _):
            _mlstm_outputs([(ch, i * per_iter + u) for u in range(per_iter) for ch in chains], ones_scr)
            return 0

        lax.fori_loop(0, n_chunks // per_iter, outputs, 0, unroll=min(2, n_chunks // per_iter))


def _mlstm(qkv, gates, states, gh, with_outputs, cm_rows):
    bsz, npair, t, dh = qkv[0].shape
    nc = t // MLSTM_CHUNK
    lc = MLSTM_CHUNK
    seq = pl.BlockSpec((None, None, t, dh), lambda b, p: (b, p, 0, 0))
    cst = pl.BlockSpec((None, None, 2, dh, dh), lambda b, p: (b, p, 0, 0, 0))
    nst = pl.BlockSpec((None, None, 8, dh), lambda b, p: (b, p, 0, 0))

    def slot_specs(head_off):
        return [seq, seq, seq, cst, nst, pl.BlockSpec((None, dh, 1), lambda b, p: (p + head_off, 0, 0))]

    st_shapes = [jax.ShapeDtypeStruct((bsz, npair, 2, dh, dh), F32), jax.ShapeDtypeStruct((bsz, npair, 8, dh), F32)] * 2
    out_shape, out_specs = st_shapes, [cst, nst, cst, nst]
    assert nc <= MAX_CHUNKS
    slot_scratch = [pltpu.VMEM((MAX_CHUNKS, 8, lc), F32), pltpu.VMEM((MAX_CHUNKS, 8, dh), F32),
                    pltpu.VMEM((nc, UP_ROWS, dh), F32)]
    if with_outputs:
        h_shape = jax.ShapeDtypeStruct((bsz, t, npair * dh), F32)
        h_spec = pl.BlockSpec((None, t, dh), lambda b, p: (b, 0, p))
        out_shape = [h_shape, h_shape] + out_shape
        out_specs = [h_spec, h_spec] + out_specs
        slot_scratch += [pltpu.VMEM((nc, ST_ROWS, dh), BF16), pltpu.VMEM((nc, 8, dh), F32), pltpu.VMEM((nc, dh, lc), BF16)]
    q_r, q_c, k_r, k_c, v_r, v_c = qkv
    c_r, n_r, c_c, n_c = states
    return pl.pallas_call(
        functools.partial(_mlstm_kernel, n_chunks=nc, with_outputs=with_outputs, cm_rows=cm_rows),
        grid=(bsz, npair),
        in_specs=[pl.BlockSpec((None, t, LANES), lambda b, p: (b, 0, 0))] + slot_specs(0) + slot_specs(N_ROW_HEADS),
        out_specs=out_specs,
        out_shape=out_shape,
        scratch_shapes=[pltpu.VMEM((lc, 2 * lc), BF16), pltpu.VMEM((16, 2 * lc), BF16)] + slot_scratch * 2,
        compiler_params=_params(("parallel", "parallel"), 48),
        name="mlstm_full" if with_outputs else "mlstm_states",
    )(gates, q_r, k_r, v_r, c_r, n_r, gh, q_c, k_c, v_c, c_c, n_c, gh)


def _conv_kernel(cacg_ref, w_ref, cb_ref, lg_ref, lb_ref, o_ref, ysh, yconv, *, t):
    rt = 64
    ysh[0, 0:CONV_HALO, :] = jnp.zeros((CONV_HALO, CONV_CH), F32)
    ysh[0, CONV_HALO + t:t + CONV_TAIL, :] = jnp.zeros((CONV_TAIL - CONV_HALO, CONV_CH), F32)

    def fill(r, _):
        s = pl.multiple_of(r * rt, rt)
        a = cacg_ref[pl.ds(s, rt), 0:CONV_CH]
        g = cacg_ref[pl.ds(s, rt), CONV_CH:2 * CONV_CH]
        ysh[0, pl.ds(CONV_HALO + s, rt), :] = a * jax.nn.sigmoid(g)
        return 0

    lax.fori_loop(0, t // rt, fill, 0)

    def shift(r, _):
        s = pl.multiple_of(r * rt, rt)
        win = ysh[0, pl.ds(s, rt + 8), :]
        for res in range(1, 8):
            ysh[res, pl.ds(s, rt), :] = pltpu.roll(win, rt + 8 - res, axis=0)[0:rt, :]
        return 0

    lax.fori_loop(0, t // rt + 1, shift, 0)

    def tile(r, _):
        s = pl.multiple_of(r * rt, rt)
        acc = jnp.zeros((rt, CONV_CH), F32)
        for kk in range(CONV_K):
            off = kk + CONV_HALO - CONV_K // 2
            rows = ysh[off % 8, pl.ds(pl.multiple_of(s + 8 * (off // 8), 8), rt), :]
            acc = acc + w_ref[kk:kk + 1, :] * rows
        yconv[pl.ds(s, rt), :] = acc + cb_ref[...]
        return 0

    lax.fori_loop(0, t // rt, tile, 0)

    def norm(r, _):
        s = pl.multiple_of(r * rt, rt)
        y = yconv[pl.ds(s, rt), :]
        mu = jnp.mean(y, axis=-1, keepdims=True)
        yc = y - mu
        var = jnp.mean(yc * yc, axis=-1, keepdims=True)
        z = yc * lax.rsqrt(var + EPS) * lg_ref[...] + lb_ref[...]
        o_ref[pl.ds(s, rt), :] = (z * jax.nn.sigmoid(z)).astype(BF16)
        return 0

    lax.fori_loop(0, t // rt, norm, 0, unroll=4)


def _conv(cacg, conv_w, conv_b, ln_g, ln_b):
    bsz, t, _ = cacg.shape
    row = lambda a: a.reshape(1, CONV_CH)
    vec = pl.BlockSpec((1, CONV_CH), lambda b: (0, 0))
    return pl.pallas_call(
        functools.partial(_conv_kernel, t=t),
        grid=(bsz,),
        in_specs=[pl.BlockSpec((None, t, 2 * CONV_CH), lambda b: (b, 0, 0)),
                  pl.BlockSpec((CONV_K + 1, CONV_CH), lambda b: (0, 0)), vec, vec, vec],
        out_specs=pl.BlockSpec((None, t, CONV_CH), lambda b: (b, 0, 0)),
        out_shape=jax.ShapeDtypeStruct((bsz, t, CONV_CH), BF16),
        scratch_shapes=[pltpu.VMEM((8, t + CONV_TAIL, CONV_CH), F32), pltpu.VMEM((t, CONV_CH), F32)],
        compiler_params=_params(("parallel",), 40),
        name="conformer_conv",
    )(cacg, jnp.pad(conv_w, ((0, 1), (0, 0))), row(conv_b), row(ln_g), row(ln_b))


@functools.lru_cache(maxsize=None)
def _dft_tables(t):
    j = np.arange(t, dtype=np.int64)
    ang = 2.0 * np.pi * ((j[:, None] * j[None, :]) % t).astype(np.float64) / t
    wt = np.concatenate([np.cos(ang), -np.sin(ang)], axis=1).astype(np.float32)
    c = np.arange(FOURIER_CH, dtype=np.int64)
    grp, idx = c // FOURIER_GROUP_CH, c % FOURIER_GROUP_CH
    same = grp[:, None] == grp[None, :]
    angc = 2.0 * np.pi * ((idx[:, None] * idx[None, :]) % FOURIER_GROUP_CH).astype(np.float64) / FOURIER_GROUP_CH
    cc = np.where(same, np.cos(angc), 0.0).astype(np.float32)
    sc = np.where(same, np.sin(angc), 0.0).astype(np.float32)
    return wt, cc, sc


def _fourier_kernel(z_ref, cc_ref, sc_ref, wt_ref, o_ref, zcs, *, t, scale):
    bsz, _, ch = z_ref.shape

    @pl.when(pl.program_id(0) == 0)
    def _():
        for b in range(bsz):
            zcs[0:t, b * ch:(b + 1) * ch] = _dot(z_ref[b], cc_ref[...]).astype(BF16)
            zcs[t:2 * t, b * ch:(b + 1) * ch] = _dot(z_ref[b], sc_ref[...]).astype(BF16)

    r = _dot(wt_ref[...], zcs[...]) * scale
    for b in range(bsz):
        o_ref[b] = r[:, b * ch:(b + 1) * ch].astype(BF16)


def _fourier(z):
    bsz, t, ch = z.shape
    wt, cc, sc = _dft_tables(t)
    tm = min(t, 512)
    mat = pl.BlockSpec((ch, ch), lambda i: (0, 0))
    return pl.pallas_call(
        functools.partial(_fourier_kernel, t=t, scale=float((t * FOURIER_GROUP_CH) ** -0.5)),
        grid=(t // tm,),
        in_specs=[pl.BlockSpec((bsz, t, ch), lambda i: (0, 0, 0)), mat, mat,
                  pl.BlockSpec((tm, 2 * t), lambda i: (i, 0))],
        out_specs=pl.BlockSpec((bsz, tm, ch), lambda i: (0, i, 0)),
        out_shape=jax.ShapeDtypeStruct((bsz, t, ch), BF16),
        scratch_shapes=[pltpu.VMEM((2 * t, bsz * ch), BF16)],
        compiler_params=_params(("arbitrary",), 56),
        name="fourier_mix",
    )(z, jnp.asarray(cc).astype(BF16), jnp.asarray(sc).astype(BF16), jnp.asarray(wt).astype(BF16))


def _outproj_kernel(x_ref, hr_ref, hc_ref, o_ref, cv_ref, fo_ref, w_ref, mod_ref, g2_ref, wr_ref,
                    x1_ref, xn_ref, aff_ref):
    half = MLSTM_WIDTH // 2
    a, b = MLSTM_WIDTH, MLSTM_WIDTH + CONV_CH
    og = jax.nn.sigmoid(o_ref[...].astype(F32))
    y = (_dot((hr_ref[...] * og[:, 0:half]).astype(BF16), w_ref[0:half, :])
         + _dot((hc_ref[...] * og[:, half:a]).astype(BF16), w_ref[half:a, :])
         + _dot(cv_ref[...], w_ref[a:b, :]) + _dot(fo_ref[...], w_ref[b:, :]))
    x1 = x_ref[...] + mod_ref[2:3, :] * y
    x1_ref[...] = x1
    xn = _modnorm(x1, g2_ref[...], mod_ref[3:4, :], mod_ref[4:5, :])
    xn_ref[...] = xn.astype(BF16)
    tm = xn.shape[0]
    r = _dot(jnp.concatenate(_split_bf16(xn), axis=0), jnp.concatenate(_split_bf16(wr_ref[...]), axis=1))
    logits = (r[:tm, :LANES] + r[:tm, LANES:]) + (r[tm:, :LANES] + r[tm:, LANES:])
    lane = lax.broadcasted_iota(I32, logits.shape, 1)
    logits = jnp.where(lane < N_EXPERTS, logits, -jnp.inf)
    e = jnp.exp(logits - jnp.max(logits, axis=-1, keepdims=True))
    aff_ref[...] = e / jnp.sum(e, axis=-1, keepdims=True)


def _outproj(x, hr, hc, o, cv, fo, w_out, mod, g2, wr):
    bsz, t, d = x.shape
    tm = min(t, 1024)
    tok = lambda n: pl.BlockSpec((None, tm, n), lambda b, i: (b, i, 0))
    return pl.pallas_call(
        _outproj_kernel,
        grid=(bsz, t // tm),
        in_specs=[tok(d), tok(MLSTM_WIDTH // 2), tok(MLSTM_WIDTH // 2), tok(MLSTM_WIDTH), tok(CONV_CH), tok(FOURIER_CH),
                  pl.BlockSpec((d, d), lambda b, i: (0, 0)), _mod_spec(mod),
                  pl.BlockSpec((1, d), lambda b, i: (0, 0)),
                  pl.BlockSpec((d, LANES), lambda b, i: (0, 0))],
        out_specs=[tok(d), tok(d), tok(LANES)],
        out_shape=[jax.ShapeDtypeStruct((bsz, t, d), F32), jax.ShapeDtypeStruct((bsz, t, d), BF16),
                   jax.ShapeDtypeStruct((bsz, t, LANES), F32)],
        compiler_params=_params(("parallel", "parallel"), 56),
        name="outproj_router",
    )(x, hr, hc, o, cv, fo, w_out, mod[0], g2.reshape(1, d), wr)


def _select_kernel(a_ref, sp_ref, win_ref, *, cap, tile):
    a = a_ref[...]
    n = a.shape[0]
    blk = min(n, MXU_DIM)

    def count(mask):
        return jnp.sum(jnp.where(mask, 1.0, 0.0), axis=0, keepdims=True)

    def search(i, thr):
        cand = thr | jnp.left_shift(jnp.int32(1), 30 - i)
        return jnp.where(count(a >= lax.bitcast_convert_type(cand, F32)) >= cap, cand, thr)

    thr = lax.bitcast_convert_type(lax.fori_loop(0, 31, search, jnp.zeros((1, LANES), I32)), F32)
    gt = a > thr
    eq = a == thr
    before = jnp.where(lax.broadcasted_iota(I32, (blk, blk), 1) < lax.broadcasted_iota(I32, (blk, blk), 0), 1.0, 0.0).astype(BF16)

    def excl_cumsum(mask):
        x = jnp.where(mask, 1.0, 0.0)
        run = jnp.zeros((1, LANES), F32)
        parts = []
        for j in range(n // blk):
            xb = x[j * blk:(j + 1) * blk]
            parts.append(_dot(before, xb.astype(BF16)) + run)
            run = run + jnp.sum(xb, axis=0, keepdims=True)
        return jnp.concatenate(parts, axis=0)

    sel = gt | (eq & (excl_cumsum(eq) < cap - count(gt)))
    pos = excl_cumsum(sel).astype(I32)
    sp_ref[...] = jnp.where(sel, pos, -1)
    rid = lax.broadcasted_iota(I32, win_ref.shape, 0)
    win = jnp.zeros(win_ref.shape, I32)
    for ti in range(n // tile):
        t_sel, t_pos = sel[ti * tile:(ti + 1) * tile], pos[ti * tile:(ti + 1) * tile]
        win = jnp.where(rid == 2 * ti, jnp.min(jnp.where(t_sel, t_pos, cap), axis=0, keepdims=True), win)
        win = jnp.where(rid == 2 * ti + 1, jnp.max(jnp.where(t_sel, t_pos + 1, 0), axis=0, keepdims=True), win)
    win_ref[...] = win


def _select(aff_sets, cap, tile):
    ng, n, _ = aff_sets.shape
    win_rows = max(8, 2 * (n // tile))
    blk = pl.BlockSpec((None, n, LANES), lambda g: (g, 0, 0))
    return pl.pallas_call(
        functools.partial(_select_kernel, cap=cap, tile=tile),
        grid=(ng,),
        in_specs=[blk],
        out_specs=[blk, pl.BlockSpec((None, win_rows, LANES), lambda g: (g, 0, 0))],
        out_shape=[jax.ShapeDtypeStruct((ng, n, LANES), I32), jax.ShapeDtypeStruct((ng, win_rows, LANES), I32)],
        compiler_params=_params(("parallel",), 32),
        name="ec_select",
    )(aff_sets)


def _slot_window(win_ref, b, tile_idx, e, cap, width):
    lo = win_ref[b, 2 * tile_idx * N_EXPERTS + e]
    hi = win_ref[b, (2 * tile_idx + 1) * N_EXPERTS + e]
    st = jnp.minimum(lax.shift_left(lax.shift_right_logical(lo, 4), 4), cap - width)
    return pl.multiple_of(st, 16), hi - st <= width


def _ffn_kernel(win_ref, sp_ref, ar_ref, xn_ref, wg_ref, wu_ref, wd_ref, ys_ref, wgu_b, wd_b, xs_scr, gate_scr, *, cap, windowed):
    g, j = pl.program_id(0), pl.program_id(1)
    gsz, rows = wg_ref.shape[0], wg_ref.shape[1]
    nb, _, _, n = sp_ref.shape

    def cast_slice():
        rt = min(256, rows)
        dst = lax.rem(g, 2)
        for piece in range(rows // rt):
            src_rows = pl.ds(piece * rt, rt)
            dst_rows = pl.ds(pl.multiple_of(j * rows + piece * rt, rt), rt)
            for k in range(gsz):
                wgu_b[dst, k, 0, dst_rows, :] = wg_ref[k, src_rows, :].astype(BF16)
                wgu_b[dst, k, 1, dst_rows, :] = wu_ref[k, src_rows, :].astype(BF16)
                wd_b[dst, k, dst_rows, :] = wd_ref[k, src_rows, :].astype(BF16)

    def full_gather(k, bi):
        hit = lax.broadcasted_iota(I32, (cap, n), 0) == sp_ref[bi, k]
        xs_scr[k, bi * cap:(bi + 1) * cap, :] = _dot(jnp.where(hit, 1.0, 0.0).astype(BF16), xn_ref[bi])
        gate_scr[k, bi * cap:(bi + 1) * cap, :] = jnp.broadcast_to(
            jnp.sum(jnp.where(hit, ar_ref[bi, k], 0.0), axis=1, keepdims=True), (cap, LANES))

    def experts_ffn():
        src = lax.rem(g + 1, 2)
        xs = [xs_scr[k].astype(BF16) for k in range(gsz)]
        h1 = [_dot(xs[k], wgu_b[src, k, 0]) for k in range(gsz)]
        h2 = [_dot(xs[k], wgu_b[src, k, 1]) for k in range(gsz)]
        hid = [(h1[k] * jax.nn.sigmoid(h1[k]) * h2[k]).astype(BF16) for k in range(gsz)]
        ys = [(_dot(hid[k], wd_b[src, k]) * gate_scr[k, :, 0:1]).astype(BF16) for k in range(gsz)]
        for k in range(gsz):
            for bi in range(nb):
                ys_ref[bi, k] = ys[k][bi * cap:(bi + 1) * cap]

    def full_path():
        for k in range(gsz):
            for bi in range(nb):
                full_gather(k, bi)
        experts_ffn()

    @pl.when(g == 0)
    def _():
        cast_slice()

    if not windowed:
        @pl.when(g > 0)
        def _():
            cast_slice()
            full_path()
        return

    assert nb == 1
    width = cap // 2
    wins = [[_slot_window(win_ref, j, t, jnp.maximum(g - 1, 0) * gsz + k, cap, width) for t in range(n // WIN_TILE)]
            for k in range(gsz)]
    fits = functools.reduce(jnp.logical_and, [ok for per_expert in wins for _, ok in per_expert])

    @pl.when(jnp.logical_and(g > 0, fits))
    def _():
        cast_slice()
        xs_scr[...] = jnp.zeros(xs_scr.shape, F32)
        gate_scr[...] = jnp.zeros(gate_scr.shape, F32)
        for t in range(n // WIN_TILE):
            tok = slice(t * WIN_TILE, (t + 1) * WIN_TILE)
            for k in range(gsz):
                st = wins[k][t][0]
                hit = lax.broadcasted_iota(I32, (width, WIN_TILE), 0) + st == sp_ref[0, k, :, tok]
                xs_scr[k, pl.ds(st, width), :] += _dot(jnp.where(hit, 1.0, 0.0).astype(BF16), xn_ref[0, tok, :])
                gate_scr[k, pl.ds(st, width), :] += jnp.broadcast_to(
                    jnp.sum(jnp.where(hit, ar_ref[0, k, :, tok], 0.0), axis=1, keepdims=True), (width, LANES))
        experts_ffn()

    @pl.when(jnp.logical_and(g > 0, jnp.logical_not(fits)))
    def _():
        cast_slice()
        full_path()


def _ffn(win, sp_row, aff_row, xn, wg, wu, wd, layer, cap):
    bsz, n, d = xn.shape
    _, ne, _, hid = wg.shape
    assert d == hid
    nb = min(bsz, max(1, MXU_DIM // cap))
    nj = bsz // nb
    gsz = 2 if nj > 1 else 1
    rows = d // nj
    windowed = nb == 1 and n > WIN_TILE and cap // 2 >= LANES
    batch = lambda g, j: jnp.where(g == 0, 0, j)
    row = pl.BlockSpec((nb, gsz, 1, n), lambda g, j, *_: (batch(g, j), jnp.maximum(g - 1, 0), 0, 0))
    wspec = pl.BlockSpec((None, gsz, rows, hid), lambda g, j, *_: (layer, jnp.minimum(g, ne // gsz - 1), j, 0))
    grid_spec = pltpu.PrefetchScalarGridSpec(
        num_scalar_prefetch=1,
        grid=(ne // gsz + 1, nj),
        in_specs=[row, row, pl.BlockSpec((nb, n, d), lambda g, j, *_: (batch(g, j), 0, 0)), wspec, wspec, wspec],
        out_specs=pl.BlockSpec((nb, gsz, cap, d), lambda g, j, *_: (batch(g, j), jnp.maximum(g - 1, 0), 0, 0)),
        scratch_shapes=[pltpu.VMEM((2, gsz, 2, d, hid), BF16), pltpu.VMEM((2, gsz, hid, d), BF16),
                        pltpu.VMEM((gsz, nb * cap, d), F32), pltpu.VMEM((gsz, nb * cap, LANES), F32)])
    return pl.pallas_call(
        functools.partial(_ffn_kernel, cap=cap, windowed=windowed),
        grid_spec=grid_spec,
        out_shape=jax.ShapeDtypeStruct((bsz, ne, cap, d), BF16),
        compiler_params=_params(("arbitrary", "arbitrary"), 56),
        name="ec_ffn",
    )(win, sp_row, aff_row, xn, wg, wu, wd)


def _combine_kernel(win_ref, x_ref, spc_ref, ys_ref, mod_ref, gf_ref, o_ref, *, cap, final, pack):
    b, i = pl.program_id(0), pl.program_id(1)
    tm = spc_ref.shape[0]
    sub = tm if pack == 1 else WIN_TILE

    def finish(rows, acc):
        out = x_ref[rows, :] + mod_ref[5:6, :] * acc
        if final:
            out = out * lax.rsqrt(jnp.mean(out * out, axis=-1, keepdims=True) + EPS) * gf_ref[...]
        o_ref[rows, :] = out

    def full_width(spc):
        slot = lax.broadcasted_iota(I32, (sub, cap), 1)
        acc = jnp.zeros((sub, x_ref.shape[1]), F32)
        for e in range(N_EXPERTS):
            hit = jnp.where(spc[:, e:e + 1] == slot, 1.0, 0.0).astype(BF16)
            acc = acc + _dot(hit, ys_ref[e])
        return acc

    n_sub = tm // sub
    rows = [slice(t * sub, (t + 1) * sub) for t in range(n_sub)]
    spcs = [spc_ref[r, :] for r in rows]
    if pack == 1:
        for r, spc in zip(rows, spcs):
            finish(r, full_width(spc))
        return

    w = cap // pack
    wins = [[_slot_window(win_ref, b, i * n_sub + t, e, cap, w) for e in range(N_EXPERTS)] for t in range(n_sub)]
    fits = functools.reduce(jnp.logical_and, [ok for per_tile in wins for _, ok in per_tile])

    @pl.when(fits)
    def _():
        lane = lax.broadcasted_iota(I32, (sub, cap), 1)
        lane_row = lax.broadcasted_iota(I32, (1, cap), 1)
        accs = [jnp.zeros((sub, x_ref.shape[1]), F32) for _ in range(n_sub)]
        for p in range(N_EXPERTS // pack):
            es = list(range(p * pack, (p + 1) * pack))
            hits, ysws = [], []
            for t in range(n_sub):
                col = spcs[t][:, es[0]:es[0] + 1]
                off = jnp.zeros((1, cap), I32) + wins[t][es[0]][0]
                for q, e in enumerate(es[1:], 1):
                    col = jnp.where(lane >= q * w, spcs[t][:, e:e + 1], col)
                    off = jnp.where(lane_row >= q * w, wins[t][e][0] - q * w, off)
                hits.append(jnp.where(col - off == lane, 1.0, 0.0).astype(BF16))
                ysws.append(jnp.concatenate([ys_ref[e, pl.ds(wins[t][e][0], w), :] for e in es], axis=0))
            accs = [acc + _dot(hit, ysw) for acc, hit, ysw in zip(accs, hits, ysws)]
        for r, acc in zip(rows, accs):
            finish(r, acc)

    @pl.when(jnp.logical_not(fits))
    def _():
        for r, spc in zip(rows, spcs):
            finish(r, full_width(spc))


def _combine(x, sp_col, win, ys, mod, g_final, cap, final):
    bsz, n, d = x.shape
    pack = 4 if n > WIN_TILE and cap // 4 >= LANES // 2 else 1
    tm = min(n, 1024)
    grid_spec = pltpu.PrefetchScalarGridSpec(
        num_scalar_prefetch=1,
        grid=(bsz, n // tm),
        in_specs=[pl.BlockSpec((None, tm, d), lambda b, i, *_: (b, i, 0)),
                  pl.BlockSpec((None, tm, N_EXPERTS), lambda b, i, *_: (b, i, 0)),
                  pl.BlockSpec((None, N_EXPERTS, cap, d), lambda b, i, *_: (b, 0, 0, 0)),
                  _mod_spec(mod),
                  pl.BlockSpec((1, d), lambda b, i, *_: (0, 0))],
        out_specs=pl.BlockSpec((None, tm, d), lambda b, i, *_: (b, i, 0)))
    return pl.pallas_call(
        functools.partial(_combine_kernel, cap=cap, final=final, pack=pack),
        grid_spec=grid_spec,
        out_shape=jax.ShapeDtypeStruct((bsz, n, d), F32),
        compiler_params=_params(("parallel", "parallel"), 48),
        name="ec_combine_final" if final else "ec_combine",
    )(win, x, sp_col, ys, mod[0], g_final.reshape(1, d))


def _moe(x1, xn2, aff, wg, wu, wd, layer, mod, g_final, final):
    bsz, n, _ = x1.shape
    cap = EC_CAPACITY_FACTOR * n // N_EXPERTS
    per_group = LANES // N_EXPERTS
    ng = -(-bsz // per_group)
    aff_e = aff[:, :, :N_EXPERTS]
    sets = jnp.pad(aff_e, ((0, ng * per_group - bsz), (0, 0), (0, 0)))
    sets = sets.reshape(ng, per_group, n, N_EXPERTS).transpose(0, 2, 1, 3).reshape(ng, n, LANES)
    tile = min(n, WIN_TILE)
    selpos, win = _select(sets, cap, tile)
    ungroup = lambda a: a.reshape(ng, -1, per_group, N_EXPERTS).transpose(0, 2, 1, 3).reshape(ng * per_group, -1, N_EXPERTS)[:bsz]
    sp_col = ungroup(selpos)
    win = ungroup(win[:, :2 * (n // tile)]).reshape(bsz, -1)
    sp_row = sp_col.transpose(0, 2, 1).reshape(bsz, N_EXPERTS, 1, n)
    aff_row = aff_e.transpose(0, 2, 1).reshape(bsz, N_EXPERTS, 1, n)
    ys = _ffn(win, sp_row, aff_row, xn2, wg, wu, wd, layer, cap)
    return _combine(x1, sp_col, win, ys, mod, g_final, cap, final)


def kernel(x, c, ctx, c_ctx, w_ada, b_ada, g_norm1, w_in, b_gates, g_hnorm, conv_w, conv_b, conv_ln_g, conv_ln_b,
           w_out, g_norm2, w_router, w_e_gate, w_e_up, w_e_down, g_final):
    bsz, t, d = x.shape
    depth = w_ada.shape[0]
    rows = t // GRID_W
    n_cond = 16
    cs = jnp.concatenate([c, c_ctx[None, :], jnp.zeros((n_cond - bsz - 1, d), F32)], axis=0)
    mods = _ada(cs, w_ada, b_ada).reshape(depth, n_cond, N_ADA, d)

    npair = N_HEADS // 2
    zero_states = (jnp.zeros((bsz, npair, 2, HEAD_DIM, HEAD_DIM), F32), jnp.zeros((bsz, npair, 8, HEAD_DIM), F32)) * 2
    h_ctx = ctx
    for l in range(depth):
        last = l == depth - 1
        lat, cmod = (mods, l, None), (mods, l, bsz)
        w_l = w_in[l]
        qkvg_cols = 3 * MLSTM_WIDTH + N_GATES * N_HEADS
        w_p = jnp.concatenate([w_l[:, :3 * MLSTM_WIDTH],
                               jnp.pad(w_l[:, 3 * MLSTM_WIDTH:qkvg_cols], ((0, 0), (0, LANES - N_GATES * N_HEADS))),
                               w_l[:, qkvg_cols:]], axis=1).astype(BF16)
        bg = jnp.pad(b_gates[l].reshape(1, N_HEADS * N_GATES), ((0, 0), (0, LANES - N_HEADS * N_GATES)))
        w_o = w_out[l].astype(BF16)
        w_r = jnp.pad(w_router[l], ((0, 0), (0, LANES - N_EXPERTS)))
        gh = g_hnorm[l].reshape(N_HEADS, HEAD_DIM, 1)

        def mlstm(xs, md, states, grid_rows, full):
            qkv, gates, extra = _inproj(xs, md, g_norm1[l], w_p, bg, full, grid_rows)
            return _mlstm(qkv, gates, states, gh, full, grid_rows), extra

        def sublayers(xs, md, states, grid_rows, final):
            (hr, hc, *st), (o, cacg, fr) = mlstm(xs, md, states, grid_rows, True)
            cv = _conv(cacg, conv_w[l], conv_b[l], conv_ln_g[l], conv_ln_b[l])
            fo = _fourier(fr)
            x1, xn2, aff = _outproj(xs, hr, hc, o, cv, fo, w_o, md, g_norm2[l], w_r)
            return _moe(x1, xn2, aff, w_e_gate, w_e_up, w_e_down, l, md, g_final, final), st

        if last:
            st, _ = mlstm(h_ctx, cmod, zero_states, None, False)
        else:
            h_ctx, st = sublayers(h_ctx, cmod, zero_states, None, False)
        x, _ = sublayers(x, lat, tuple(st), rows, last)
    return x
```

```python
import functools

import jax
import jax.numpy as jnp
import numpy as np
from jax import lax
from jax.experimental import pallas as pl
from jax.experimental.pallas import tpu as pltpu

F32 = jnp.float32
BF16 = jnp.bfloat16
I32 = jnp.int32

D_MODEL = 1024
GRID_W = 64
N_HEADS = 4
HEAD_DIM = 128
MLSTM_WIDTH = N_HEADS * HEAD_DIM
N_ROW_HEADS = N_HEADS // 2
CONV_CH = 256
CONV_K = 31
FOURIER_CH = 256
FOURIER_GROUP_CH = 64
N_GATES = 4
N_EXPERTS = 16
EC_CAPACITY_FACTOR = 2
N_ADA = 6
EPS = 1e-6

LANES = 128
MXU_DIM = 256
MLSTM_CHUNK = 128
MAX_CHUNKS = 16
CONV_HALO = 16
CONV_TAIL = 128
WIN_TILE = 256
MIB = 1024 * 1024

SEC_QKV = (0, 3 * MLSTM_WIDTH)
SEC_G = (SEC_QKV[1], SEC_QKV[1] + LANES)
SEC_O = (SEC_G[1], SEC_G[1] + MLSTM_WIDTH)
SEC_CACG = (SEC_O[1], SEC_O[1] + 2 * CONV_CH)
SEC_FR = (SEC_CACG[1], SEC_CACG[1] + FOURIER_CH)
IN_COLS_PAD = SEC_FR[1]

ST_NF = 2 * HEAD_DIM
ST_NB = ST_NF + 16
ST_ROWS = ST_NB + 16
UP_ROWS = 2 * HEAD_DIM + 16


def _dot(a, b):
    return jnp.dot(a, b, preferred_element_type=F32)


def _dot_nt(a, b):
    return lax.dot_general(a, b, (((1,), (1,)), ((), ())), preferred_element_type=F32)


def _split_bf16(a):
    hi = a.astype(BF16)
    lo = (a - hi.astype(F32)).astype(BF16)
    return hi, lo


def _dot3(a, w):
    ah, al = _split_bf16(a)
    wh, wl = _split_bf16(w)
    return _dot(ah, wh) + (_dot(al, wh) + _dot(ah, wl))


def _modnorm(x, g, sh, sc):
    ms = jnp.mean(x * x, axis=-1, keepdims=True)
    return x * lax.rsqrt(ms + EPS) * g * (1.0 + sc) + sh


def _rows(n_rows, width, *vals):
    rid = lax.broadcasted_iota(I32, (n_rows, width), 0)
    out = jnp.zeros((n_rows, width), F32)
    for i, v in enumerate(vals):
        out = jnp.where(rid == i, v, out)
    return out


def _mod_spec(mod):
    mods, layer, row = mod
    _, _, n_ada, d = mods.shape
    return pl.BlockSpec((None, None, n_ada, d), lambda b, i, *_: (layer, b if row is None else row, 0, 0))


def _params(sem, vmem_mib):
    return pltpu.CompilerParams(dimension_semantics=sem, vmem_limit_bytes=vmem_mib * MIB)


def _ada_kernel(c_ref, w_ref, b_ref, o_ref):
    c = c_ref[...]
    r = c.shape[0]
    ah, al = _split_bf16(c * jax.nn.sigmoid(c))
    wh, wl = _split_bf16(w_ref[...])
    both = _dot(jnp.concatenate([ah, al], axis=0), wh)
    o_ref[...] = both[:r] + (both[r:] + _dot(ah, wl)) + b_ref[...]


def _ada(cs, w_ada, b_ada):
    n_layers, d, n6 = w_ada.shape
    r = cs.shape[0]
    tn = 1536
    return pl.pallas_call(
        _ada_kernel,
        grid=(n_layers, n6 // tn),
        in_specs=[
            pl.BlockSpec((r, d), lambda l, j: (0, 0)),
            pl.BlockSpec((None, d, tn), lambda l, j: (l, 0, j)),
            pl.BlockSpec((None, 1, tn), lambda l, j: (l, 0, j)),
        ],
        out_specs=pl.BlockSpec((None, r, tn), lambda l, j: (l, 0, j)),
        out_shape=jax.ShapeDtypeStruct((n_layers, r, n6), F32),
        compiler_params=_params(("arbitrary", "arbitrary"), 40),
        name="ada_mod",
    )(cs, w_ada, b_ada.reshape(n_layers, 1, n6))


def _inproj_kernel(x_ref, mod_ref, g_ref, w_ref, bg_ref, *out_refs, full, colmajor):
    qkv_refs, gates_ref = out_refs[:6], out_refs[6]
    xn = _modnorm(x_ref[...], g_ref[...], mod_ref[0:1, :], mod_ref[1:2, :]).astype(BF16)
    tm = xn.shape[0]
    w = MLSTM_WIDTH
    for i in range(3):
        r = _dot(xn, w_ref[:, i * w:(i + 1) * w])
        if i == 1:
            r = r * HEAD_DIM ** -0.5
        row_ref, col_ref = qkv_refs[2 * i], qkv_refs[2 * i + 1]
        for hh in range(N_ROW_HEADS):
            row_ref[hh] = r[:, hh * HEAD_DIM:(hh + 1) * HEAD_DIM].astype(BF16)
            piece = r[:, (N_ROW_HEADS + hh) * HEAD_DIM:(N_ROW_HEADS + hh + 1) * HEAD_DIM]
            if colmajor:
                piece = jnp.swapaxes(piece.reshape(tm // GRID_W, GRID_W, HEAD_DIM), 0, 1)
            col_ref[hh] = piece.astype(BF16)
    gates_ref[...] = _dot(xn, w_ref[:, SEC_G[0]:SEC_G[1]]) + bg_ref[...]
    if full:
        o_ref, cacg_ref, fr_ref = out_refs[7:]
        o_ref[...] = _dot(xn, w_ref[:, SEC_O[0]:SEC_O[1]]).astype(BF16)
        cacg_ref[...] = _dot(xn, w_ref[:, SEC_CACG[0]:SEC_CACG[1]])
        fr_ref[...] = _dot(xn, w_ref[:, SEC_FR[0]:SEC_FR[1]]).astype(BF16)


def _inproj(x, mod, g, w, bg, full, rows):
    bsz, t, d = x.shape
    tm = min(t, 1024)
    nh = N_ROW_HEADS
    ncols = IN_COLS_PAD if full else SEC_G[1]
    tok = lambda n: pl.BlockSpec((None, tm, n), lambda b, i: (b, i, 0))
    row_shape = jax.ShapeDtypeStruct((bsz, nh, t, HEAD_DIM), BF16)
    row_spec = pl.BlockSpec((None, nh, tm, HEAD_DIM), lambda b, i: (b, 0, i, 0))
    if rows is None:
        col_shape, col_spec = row_shape, row_spec
    else:
        col_shape = jax.ShapeDtypeStruct((bsz, nh, GRID_W, rows, HEAD_DIM), BF16)
        col_spec = pl.BlockSpec((None, nh, GRID_W, tm // GRID_W, HEAD_DIM), lambda b, i: (b, 0, 0, i, 0))
    out_shape = [row_shape, col_shape] * 3 + [jax.ShapeDtypeStruct((bsz, t, LANES), F32)]
    out_specs = [row_spec, col_spec] * 3 + [tok(LANES)]
    if full:
        out_shape += [jax.ShapeDtypeStruct((bsz, t, MLSTM_WIDTH), BF16),
                      jax.ShapeDtypeStruct((bsz, t, 2 * CONV_CH), F32),
                      jax.ShapeDtypeStruct((bsz, t, FOURIER_CH), BF16)]
        out_specs += [tok(MLSTM_WIDTH), tok(2 * CONV_CH), tok(FOURIER_CH)]
    outs = pl.pallas_call(
        functools.partial(_inproj_kernel, full=full, colmajor=rows is not None),
        grid=(bsz, t // tm),
        in_specs=[tok(d), _mod_spec(mod),
                  pl.BlockSpec((1, d), lambda b, i: (0, 0)),
                  pl.BlockSpec((d, ncols), lambda b, i: (0, 0)),
                  pl.BlockSpec((1, LANES), lambda b, i: (0, 0))],
        out_specs=out_specs,
        out_shape=out_shape,
        compiler_params=_params(("parallel", "parallel"), 56),
        name="inproj_full" if full else "inproj_qkvg",
    )(x, mod[0], g.reshape(1, d), w, bg)
    qkv = [a.reshape(bsz, nh, t, HEAD_DIM) for a in outs[:6]]
    return qkv, outs[6], outs[7:]


def _logsig(x):
    return jnp.minimum(x, 0.0) - jnp.log1p(jnp.exp(-jnp.abs(x)))


def _tri_masks(n):
    a0 = lax.broadcasted_iota(I32, (n, n), 0)
    a1 = lax.broadcasted_iota(I32, (n, n), 1)
    return a0 <= a1, a0 >= a1


def _split3(x):
    hi = x.astype(BF16).astype(F32)
    r1 = x - hi
    mid = r1.astype(BF16).astype(F32)
    return hi, mid, r1 - mid


def _mlstm_consts(tri_scr, ones_scr):
    lc = MLSTM_CHUNK
    u = lax.broadcasted_iota(I32, (lc, 2 * lc), 0)
    t2 = lax.broadcasted_iota(I32, (lc, 2 * lc), 1)
    tri_scr[...] = jnp.where(((t2 < lc) & (u <= t2)) | ((t2 >= lc) & (u >= t2 - lc)), 1.0, 0.0).astype(BF16)
    r = lax.broadcasted_iota(I32, (16, 2 * lc), 0)
    t16 = lax.broadcasted_iota(I32, (16, 2 * lc), 1)
    ones_scr[...] = jnp.where(((r < 3) & (t16 < lc)) | ((r >= 3) & (r < 6) & (t16 >= lc)), 1.0, 0.0).astype(BF16)


def _mlstm_gates(head, g_ref, tri_scr, rowg_scr, stat_scr, n_chunks, cm_rows):
    lc = MLSTM_CHUNK
    lane2 = lax.broadcasted_iota(I32, (16, 2 * LANES), 1)
    pick = jnp.where(jnp.bitwise_and(lane2, LANES - 1) == N_GATES * head + lax.broadcasted_iota(I32, (16, 2 * LANES), 0),
                     1.0, 0.0).astype(BF16)
    chunk_id = lax.broadcasted_iota(I32, (MAX_CHUNKS, lc), 0)

    def gather(c, acc):
        if cm_rows is None:
            g = g_ref[pl.ds(pl.multiple_of(c * lc, lc), lc), :]
        else:
            cols = lc // cm_rows
            g = jnp.concatenate([g_ref[pl.ds(c * cols + j, cm_rows, stride=GRID_W), :] for j in range(cols)], axis=0)
        rows = _dot_nt(pick, jnp.concatenate(_split_bf16(g), axis=1))
        return tuple(jnp.where(chunk_id == c, rows[i:i + 1], a) for i, a in enumerate(acc))

    zeros = jnp.zeros((MAX_CHUNKS, lc), F32)
    i_f, f_f, i_b, f_b = lax.fori_loop(0, n_chunks, gather, (zeros,) * N_GATES, unroll=min(4, n_chunks))
    lf_f, lf_b = _logsig(f_f), _logsig(f_b)
    b_f = sum(_dot(part.astype(BF16), tri_scr[:, 0:lc]) for part in _split3(lf_f))
    b_b = sum(_dot(part.astype(BF16), tri_scr[:, lc:]) for part in _split3(lf_b))
    bl_f = jnp.sum(lf_f, axis=1, keepdims=True)
    bl_b = jnp.sum(lf_b, axis=1, keepdims=True)
    ci_f, ci_b = i_f - b_f, i_b - b_b
    ml_f = jnp.max(ci_f + bl_f, axis=1, keepdims=True)
    ml_b = jnp.max(ci_b + bl_b, axis=1, keepdims=True)
    for r, val in enumerate((jnp.exp(ci_f + bl_f - ml_f), jnp.exp(ci_b + bl_b - ml_b), b_f, b_b, ci_f, ci_b)):
        rowg_scr[:, r, :] = val
    for r, val in enumerate((bl_f, bl_b, ml_f, ml_b)):
        stat_scr[:, r, :] = jnp.broadcast_to(val, (MAX_CHUNKS, HEAD_DIM))


def _mlstm_increment(c, k_ref, v_ref, rowg_scr, up_scr, vt_scr):
    lc = MLSTM_CHUNK
    s = pl.multiple_of(c * lc, lc)
    v_t = v_ref[pl.ds(s, lc), :].astype(F32).T
    rg = rowg_scr[c]
    w_f, w_b = rg[0:1], rg[1:2]
    lhs = jnp.concatenate([(v_t * w_f).astype(BF16), (v_t * w_b).astype(BF16),
                           _rows(16, lc, w_f, w_b).astype(BF16)], axis=0)
    up_scr[c] = _dot(lhs, k_ref[pl.ds(s, lc), :])
    if vt_scr is not None:
        vt_scr[c] = v_t.astype(BF16)


def _mlstm_state_step(cc, carry, up_scr, stat_scr, st_scr, mst_scr, fwd):
    cmat, n, m = carry
    off, row = (0, 0) if fwd else (HEAD_DIM, 1)
    if st_scr is not None:
        st_scr[cc, off:off + HEAD_DIM, :] = cmat.astype(BF16)
        nrow = ST_NF if fwd else ST_NB
        st_scr[cc, nrow:nrow + 16, :] = _rows(16, HEAD_DIM, n).astype(BF16)
        mst_scr[cc, row:row + 1, :] = jnp.broadcast_to(m, (1, HEAD_DIM))
    st = stat_scr[cc]
    bl, ml = st[row:row + 1, 0:1], st[2 + row:3 + row, 0:1]
    m_new = jnp.maximum(bl + m, ml)
    keep = jnp.exp(bl + m - m_new)
    gain = jnp.exp(ml - m_new)
    cmat = keep * cmat + gain * up_scr[cc, off:off + HEAD_DIM, :]
    n = keep * n + gain * up_scr[cc, 2 * HEAD_DIM + row:2 * HEAD_DIM + row + 1, :]
    return cmat, n, m_new


def _mlstm_outputs(work, ones_scr):
    lc = MLSTM_CHUNK
    starts = [pl.multiple_of(c * lc, lc) for _, c in work]
    le, ge = _tri_masks(lc)
    qs = [ch[0][pl.ds(s, lc), :] for (ch, _), s in zip(work, starts)]
    qk_ts = [_dot_nt(ch[1][pl.ds(s, lc), :], q) for (ch, _), s, q in zip(work, starts, qs)]
    g_ts = [_dot_nt(ch[5][c], q) for (ch, c), q in zip(work, qs)]
    rgs = [ch[4][c] for ch, c in work]
    cis = [lax.dot_general(_rows(16, lc, *_split3(rg[4:5]), *_split3(rg[5:6])).astype(BF16), ones_scr[...],
                           (((0,), (0,)), ((), ())), preferred_element_type=F32) for rg in rgs]

    def direction(qk_t, valid, b_row, ci, m_st, qn):
        dlog = jnp.where(valid, b_row + ci, -jnp.inf)
        inter = b_row + m_st
        m_t = jnp.maximum(inter, jnp.max(dlog, axis=0, keepdims=True))
        s_t = qk_t * jnp.exp(dlog - m_t)
        w_inter = jnp.exp(inter - m_t)
        den = w_inter * qn + jnp.sum(s_t, axis=0, keepdims=True)
        r = 1.0 / jnp.maximum(jnp.abs(den), jnp.exp(-m_t))
        return r * s_t, r * w_inter

    mixed = []
    for (ch, c), qk_t, g_t, rg, ci in zip(work, qk_ts, g_ts, rgs, cis):
        ms = ch[6][c]
        sf, cf = direction(qk_t, le, rg[2:3], ci[:, 0:lc], ms[0:1, 0:1], g_t[ST_NF:ST_NF + 1])
        sb, cb = direction(qk_t, ge, rg[3:4], ci[:, lc:], ms[1:2, 0:1], g_t[ST_NB:ST_NB + 1])
        mixed.append(((sf + sb).astype(BF16), cf * g_t[0:HEAD_DIM] + cb * g_t[HEAD_DIM:2 * HEAD_DIM]))
    h_ts = [inter_part + _dot(ch[7][c], s_mix) for (ch, c), (s_mix, inter_part) in zip(work, mixed)]
    for (ch, c), s, h_t in zip(work, starts, h_ts):
        gh_ref, h_ref, cm_rows = ch[2], ch[3], ch[8]
        h = (h_t * lax.rsqrt(jnp.mean(h_t * h_t, axis=0, keepdims=True) + EPS) * gh_ref[...]).T
        if cm_rows is None:
            h_ref[pl.ds(s, lc), :] = h
        else:
            cols = lc // cm_rows
            for j in range(cols):
                h_ref[pl.ds(c * cols + j, cm_rows, stride=GRID_W), :] = h[j * cm_rows:(j + 1) * cm_rows]


def _mlstm_kernel(*refs, n_chunks, with_outputs, cm_rows):
    n_slot_in = 6
    g_ref = refs[0]
    slots_in = [refs[1:1 + n_slot_in], refs[1 + n_slot_in:1 + 2 * n_slot_in]]
    rest = refs[1 + 2 * n_slot_in:]
    if with_outputs:
        h_refs, rest = rest[:2], rest[2:]
    state_out, scratch = rest[:4], rest[4:]
    tri_scr, ones_scr, scratch = scratch[0], scratch[1], scratch[2:]
    n_scr = 6 if with_outputs else 3
    slots_scr = [scratch[:n_scr], scratch[n_scr:2 * n_scr]]
    slot_rows = [None, cm_rows]
    _mlstm_consts(tri_scr, ones_scr)
    for slot, scr in enumerate(slots_scr):
        _mlstm_gates(pl.program_id(1) + slot * N_ROW_HEADS, g_ref, tri_scr, scr[0], scr[1], n_chunks, slot_rows[slot])

    def increments(c, _):
        for (q_ref, k_ref, v_ref, c0_ref, n0_ref, gh_ref), scr in zip(slots_in, slots_scr):
            _mlstm_increment(c, k_ref, v_ref, scr[0], scr[2], scr[5] if with_outputs else None)
        return 0

    lax.fori_loop(0, n_chunks, increments, 0, unroll=min(4, n_chunks))

    def states(i, carry):
        out = []
        for slot, scr in enumerate(slots_scr):
            st_scr, mst_scr = (scr[3], scr[4]) if with_outputs else (None, None)
            out.append(_mlstm_state_step(i, carry[2 * slot], scr[2], scr[1], st_scr, mst_scr, True))
            out.append(_mlstm_state_step(n_chunks - 1 - i, carry[2 * slot + 1], scr[2], scr[1], st_scr, mst_scr, False))
        return tuple(out)

    init = []
    for (q_ref, k_ref, v_ref, c0_ref, n0_ref, gh_ref) in slots_in:
        init.append((c0_ref[0], n0_ref[0:1, :], n0_ref[2:3, 0:1]))
        init.append((c0_ref[1], n0_ref[1:2, :], n0_ref[3:4, 0:1]))
    final = lax.fori_loop(0, n_chunks, states, tuple(init))
    for slot in range(2):
        c_out, n_out = state_out[2 * slot], state_out[2 * slot + 1]
        (cf, nf, mf), (cb, nb, mb) = final[2 * slot], final[2 * slot + 1]
        c_out[0] = cf
        c_out[1] = cb
        n_out[...] = _rows(8, HEAD_DIM, nf, nb, mf, mb)

    if with_outputs:
        chains = [(q_ref, k_ref, gh_ref, h_refs[slot], scr[0], scr[3], scr[4], scr[5], slot_rows[slot])
                  for slot, ((q_ref, k_ref, v_ref, c0_ref, n0_ref, gh_ref), scr) in enumerate(zip(slots_in, slots_scr))]

        per_iter = 4 if n_chunks % 4 == 0 else 2 if n_chunks % 2 == 0 else 1

        def outputs(i, _):
            _mlstm_outputs([(ch, i * per_iter + u) for u in range(per_iter) for ch in chains], ones_scr)
            return 0

        lax.fori_loop(0, n_chunks // per_iter, outputs, 0, unroll=min(2, n_chunks // per_iter))


def _mlstm(qkv, gates, states, gh, with_outputs, cm_rows):
    bsz, npair, t, dh = qkv[0].shape
    nc = t // MLSTM_CHUNK
    lc = MLSTM_CHUNK
    seq = pl.BlockSpec((None, None, t, dh), lambda b, p: (b, p, 0, 0))
    cst = pl.BlockSpec((None, None, 2, dh, dh), lambda b, p: (b, p, 0, 0, 0))
    nst = pl.BlockSpec((None, None, 8, dh), lambda b, p: (b, p, 0, 0))

    def slot_specs(head_off):
        return [seq, seq, seq, cst, nst, pl.BlockSpec((None, dh, 1), lambda b, p: (p + head_off, 0, 0))]

    st_shapes = [jax.ShapeDtypeStruct((bsz, npair, 2, dh, dh), F32), jax.ShapeDtypeStruct((bsz, npair, 8, dh), F32)] * 2
    out_shape, out_specs = st_shapes, [cst, nst, cst, nst]
    assert nc <= MAX_CHUNKS
    slot_scratch = [pltpu.VMEM((MAX_CHUNKS, 8, lc), F32), pltpu.VMEM((MAX_CHUNKS, 8, dh), F32),
                    pltpu.VMEM((nc, UP_ROWS, dh), F32)]
    if with_outputs:
        h_shape = jax.ShapeDtypeStruct((bsz, t, npair * dh), F32)
        h_spec = pl.BlockSpec((None, t, dh), lambda b, p: (b, 0, p))
        out_shape = [h_shape, h_shape] + out_shape
        out_specs = [h_spec, h_spec] + out_specs
        slot_scratch += [pltpu.VMEM((nc, ST_ROWS, dh), BF16), pltpu.VMEM((nc, 8, dh), F32), pltpu.VMEM((nc, dh, lc), BF16)]
    q_r, q_c, k_r, k_c, v_r, v_c = qkv
    c_r, n_r, c_c, n_c = states
    return pl.pallas_call(
        functools.partial(_mlstm_kernel, n_chunks=nc, with_outputs=with_outputs, cm_rows=cm_rows),
        grid=(bsz, npair),
        in_specs=[pl.BlockSpec((None, t, LANES), lambda b, p: (b, 0, 0))] + slot_specs(0) + slot_specs(N_ROW_HEADS),
        out_specs=out_specs,
        out_shape=out_shape,
        scratch_shapes=[pltpu.VMEM((lc, 2 * lc), BF16), pltpu.VMEM((16, 2 * lc), BF16)] + slot_scratch * 2,
        compiler_params=_params(("parallel", "parallel"), 48),
        name="mlstm_full" if with_outputs else "mlstm_states",
    )(gates, q_r, k_r, v_r, c_r, n_r, gh, q_c, k_c, v_c, c_c, n_c, gh)


def _conv_kernel(cacg_ref, w_ref, cb_ref, lg_ref, lb_ref, o_ref, ysh, yconv, *, t):
    rt = 64
    ysh[0, 0:CONV_HALO, :] = jnp.zeros((CONV_HALO, CONV_CH), F32)
    ysh[0, CONV_HALO + t:t + CONV_TAIL, :] = jnp.zeros((CONV_TAIL - CONV_HALO, CONV_CH), F32)

    def fill(r, _):
        s = pl.multiple_of(r * rt, rt)
        a = cacg_ref[pl.ds(s, rt), 0:CONV_CH]
        g = cacg_ref[pl.ds(s, rt), CONV_CH:2 * CONV_CH]
        ysh[0, pl.ds(CONV_HALO + s, rt), :] = a * jax.nn.sigmoid(g)
        return 0

    lax.fori_loop(0, t // rt, fill, 0)

    def shift(r, _):
        s = pl.multiple_of(r * rt, rt)
        win = ysh[0, pl.ds(s, rt + 8), :]
        for res in range(1, 8):
            ysh[res, pl.ds(s, rt), :] = pltpu.roll(win, rt + 8 - res, axis=0)[0:rt, :]
        return 0

    lax.fori_loop(0, t // rt + 1, shift, 0)

    def tile(r, _):
        s = pl.multiple_of(r * rt, rt)
        acc = jnp.zeros((rt, CONV_CH), F32)
        for kk in range(CONV_K):
            off = kk + CONV_HALO - CONV_K // 2
            rows = ysh[off % 8, pl.ds(pl.multiple_of(s + 8 * (off // 8), 8), rt), :]
            acc = acc + w_ref[kk:kk + 1, :] * rows
        yconv[pl.ds(s, rt), :] = acc + cb_ref[...]
        return 0

    lax.fori_loop(0, t // rt, tile, 0)

    def norm(r, _):
        s = pl.multiple_of(r * rt, rt)
        y = yconv[pl.ds(s, rt), :]
        mu = jnp.mean(y, axis=-1, keepdims=True)
        yc = y - mu
        var = jnp.mean(yc * yc, axis=-1, keepdims=True)
        z = yc * lax.rsqrt(var + EPS) * lg_ref[...] + lb_ref[...]
        o_ref[pl.ds(s, rt), :] = (z * jax.nn.sigmoid(z)).astype(BF16)
        return 0

    lax.fori_loop(0, t // rt, norm, 0, unroll=4)


def _conv(cacg, conv_w, conv_b, ln_g, ln_b):
    bsz, t, _ = cacg.shape
    row = lambda a: a.reshape(1, CONV_CH)
    vec = pl.BlockSpec((1, CONV_CH), lambda b: (0, 0))
    return pl.pallas_call(
        functools.partial(_conv_kernel, t=t),
        grid=(bsz,),
        in_specs=[pl.BlockSpec((None, t, 2 * CONV_CH), lambda b: (b, 0, 0)),
                  pl.BlockSpec((CONV_K + 1, CONV_CH), lambda b: (0, 0)), vec, vec, vec],
        out_specs=pl.BlockSpec((None, t, CONV_CH), lambda b: (b, 0, 0)),
        out_shape=jax.ShapeDtypeStruct((bsz, t, CONV_CH), BF16),
        scratch_shapes=[pltpu.VMEM((8, t + CONV_TAIL, CONV_CH), F32), pltpu.VMEM((t, CONV_CH), F32)],
        compiler_params=_params(("parallel",), 40),
        name="conformer_conv",
    )(cacg, jnp.pad(conv_w, ((0, 1), (0, 0))), row(conv_b), row(ln_g), row(ln_b))


@functools.lru_cache(maxsize=None)
def _dft_tables(t):
    j = np.arange(t, dtype=np.int64)
    ang = 2.0 * np.pi * ((j[:, None] * j[None, :]) % t).astype(np.float64) / t
    wt = np.concatenate([np.cos(ang), -np.sin(ang)], axis=1).astype(np.float32)
    c = np.arange(FOURIER_CH, dtype=np.int64)
    grp, idx = c // FOURIER_GROUP_CH, c % FOURIER_GROUP_CH
    same = grp[:, None] == grp[None, :]
    angc = 2.0 * np.pi * ((idx[:, None] * idx[None, :]) % FOURIER_GROUP_CH).astype(np.float64) / FOURIER_GROUP_CH
    cc = np.where(same, np.cos(angc), 0.0).astype(np.float32)
    sc = np.where(same, np.sin(angc), 0.0).astype(np.float32)
    return wt, cc, sc


def _fourier_kernel(z_ref, cc_ref, sc_ref, wt_ref, o_ref, zcs, *, t, scale):
    bsz, _, ch = z_ref.shape

    @pl.when(pl.program_id(0) == 0)
    def _():
        for b in range(bsz):
            zcs[0:t, b * ch:(b + 1) * ch] = _dot(z_ref[b], cc_ref[...]).astype(BF16)
            zcs[t:2 * t, b * ch:(b + 1) * ch] = _dot(z_ref[b], sc_ref[...]).astype(BF16)

    r = _dot(wt_ref[...], zcs[...]) * scale
    for b in range(bsz):
        o_ref[b] = r[:, b * ch:(b + 1) * ch].astype(BF16)


def _fourier(z):
    bsz, t, ch = z.shape
    wt, cc, sc = _dft_tables(t)
    tm = min(t, 512)
    mat = pl.BlockSpec((ch, ch), lambda i: (0, 0))
    return pl.pallas_call(
        functools.partial(_fourier_kernel, t=t, scale=float((t * FOURIER_GROUP_CH) ** -0.5)),
        grid=(t // tm,),
        in_specs=[pl.BlockSpec((bsz, t, ch), lambda i: (0, 0, 0)), mat, mat,
                  pl.BlockSpec((tm, 2 * t), lambda i: (i, 0))],
        out_specs=pl.BlockSpec((bsz, tm, ch), lambda i: (0, i, 0)),
        out_shape=jax.ShapeDtypeStruct((bsz, t, ch), BF16),
        scratch_shapes=[pltpu.VMEM((2 * t, bsz * ch), BF16)],
        compiler_params=_params(("arbitrary",), 56),
        name="fourier_mix",
    )(z, jnp.asarray(cc).astype(BF16), jnp.asarray(sc).astype(BF16), jnp.asarray(wt).astype(BF16))


def _outproj_kernel(x_ref, hr_ref, hc_ref, o_ref, cv_ref, fo_ref, w_ref, mod_ref, g2_ref, wr_ref,
                    x1_ref, xn_ref, aff_ref):
    half = MLSTM_WIDTH // 2
    a, b = MLSTM_WIDTH, MLSTM_WIDTH + CONV_CH
    og = jax.nn.sigmoid(o_ref[...].astype(F32))
    y = (_dot((hr_ref[...] * og[:, 0:half]).astype(BF16), w_ref[0:half, :])
         + _dot((hc_ref[...] * og[:, half:a]).astype(BF16), w_ref[half:a, :])
         + _dot(cv_ref[...], w_ref[a:b, :]) + _dot(fo_ref[...], w_ref[b:, :]))
    x1 = x_ref[...] + mod_ref[2:3, :] * y
    x1_ref[...] = x1
    xn = _modnorm(x1, g2_ref[...], mod_ref[3:4, :], mod_ref[4:5, :])
    xn_ref[...] = xn.astype(BF16)
    tm = xn.shape[0]
    r = _dot(jnp.concatenate(_split_bf16(xn), axis=0), jnp.concatenate(_split_bf16(wr_ref[...]), axis=1))
    logits = (r[:tm, :LANES] + r[:tm, LANES:]) + (r[tm:, :LANES] + r[tm:, LANES:])
    lane = lax.broadcasted_iota(I32, logits.shape, 1)
    logits = jnp.where(lane < N_EXPERTS, logits, -jnp.inf)
    e = jnp.exp(logits - jnp.max(logits, axis=-1, keepdims=True))
    aff_ref[...] = e / jnp.sum(e, axis=-1, keepdims=True)


def _outproj(x, hr, hc, o, cv, fo, w_out, mod, g2, wr):
    bsz, t, d = x.shape
    tm = min(t, 512)
    tok = lambda n: pl.BlockSpec((None, tm, n), lambda b, i: (b, i, 0))
    return pl.pallas_call(
        _outproj_kernel,
        grid=(bsz, t // tm),
        in_specs=[tok(d), tok(MLSTM_WIDTH // 2), tok(MLSTM_WIDTH // 2), tok(MLSTM_WIDTH), tok(CONV_CH), tok(FOURIER_CH),
                  pl.BlockSpec((d, d), lambda b, i: (0, 0)), _mod_spec(mod),
                  pl.BlockSpec((1, d), lambda b, i: (0, 0)),
                  pl.BlockSpec((d, LANES), lambda b, i: (0, 0))],
        out_specs=[tok(d), tok(d), tok(LANES)],
        out_shape=[jax.ShapeDtypeStruct((bsz, t, d), F32), jax.ShapeDtypeStruct((bsz, t, d), BF16),
                   jax.ShapeDtypeStruct((bsz, t, LANES), F32)],
        compiler_params=_params(("parallel", "parallel"), 40),
        name="outproj_router",
    )(x, hr, hc, o, cv, fo, w_out, mod[0], g2.reshape(1, d), wr)


def _select_kernel(a_ref, sp_ref, win_ref, *, cap, tile):
    a = a_ref[...]
    n = a.shape[0]
    blk = min(n, MXU_DIM)

    def count(mask):
        return jnp.sum(jnp.where(mask, 1.0, 0.0), axis=0, keepdims=True)

    def search(i, thr):
        cand = thr | jnp.left_shift(jnp.int32(1), 30 - i)
        return jnp.where(count(a >= lax.bitcast_convert_type(cand, F32)) >= cap, cand, thr)

    thr = lax.bitcast_convert_type(lax.fori_loop(0, 31, search, jnp.zeros((1, LANES), I32)), F32)
    gt = a > thr
    eq = a == thr
    before = jnp.where(lax.broadcasted_iota(I32, (blk, blk), 1) < lax.broadcasted_iota(I32, (blk, blk), 0), 1.0, 0.0).astype(BF16)

    def excl_cumsum(mask):
        x = jnp.where(mask, 1.0, 0.0)
        run = jnp.zeros((1, LANES), F32)
        parts = []
        for j in range(n // blk):
            xb = x[j * blk:(j + 1) * blk]
            parts.append(_dot(before, xb.astype(BF16)) + run)
            run = run + jnp.sum(xb, axis=0, keepdims=True)
        return jnp.concatenate(parts, axis=0)

    sel = gt | (eq & (excl_cumsum(eq) < cap - count(gt)))
    pos = excl_cumsum(sel).astype(I32)
    sp_ref[...] = jnp.where(sel, pos, -1)
    rid = lax.broadcasted_iota(I32, win_ref.shape, 0)
    win = jnp.zeros(win_ref.shape, I32)
    for ti in range(n // tile):
        t_sel, t_pos = sel[ti * tile:(ti + 1) * tile], pos[ti * tile:(ti + 1) * tile]
        win = jnp.where(rid == 2 * ti, jnp.min(jnp.where(t_sel, t_pos, cap), axis=0, keepdims=True), win)
        win = jnp.where(rid == 2 * ti + 1, jnp.max(jnp.where(t_sel, t_pos + 1, 0), axis=0, keepdims=True), win)
    win_ref[...] = win


def _select(aff_sets, cap, tile):
    ng, n, _ = aff_sets.shape
    win_rows = max(8, 2 * (n // tile))
    blk = pl.BlockSpec((None, n, LANES), lambda g: (g, 0, 0))
    return pl.pallas_call(
        functools.partial(_select_kernel, cap=cap, tile=tile),
        grid=(ng,),
        in_specs=[blk],
        out_specs=[blk, pl.BlockSpec((None, win_rows, LANES), lambda g: (g, 0, 0))],
        out_shape=[jax.ShapeDtypeStruct((ng, n, LANES), I32), jax.ShapeDtypeStruct((ng, win_rows, LANES), I32)],
        compiler_params=_params(("parallel",), 32),
        name="ec_select",
    )(aff_sets)


def _slot_window(win_ref, b, tile_idx, e, cap, width):
    lo = win_ref[b, 2 * tile_idx * N_EXPERTS + e]
    hi = win_ref[b, (2 * tile_idx + 1) * N_EXPERTS + e]
    st = jnp.minimum(lax.shift_left(lax.shift_right_logical(lo, 4), 4), cap - width)
    return pl.multiple_of(st, 16), hi - st <= width


def _ffn_kernel(win_ref, sp_ref, ar_ref, xn_ref, wg_ref, wu_ref, wd_ref, ys_ref, wgu_b, wd_b, xs_scr, gate_scr, *, cap, windowed):
    g, j = pl.program_id(0), pl.program_id(1)
    gsz, rows = wg_ref.shape[0], wg_ref.shape[1]
    nb, _, _, n = sp_ref.shape

    def cast_slice():
        rt = min(256, rows)
        dst = lax.rem(g, 2)
        for piece in range(rows // rt):
            src_rows = pl.ds(piece * rt, rt)
            dst_rows = pl.ds(pl.multiple_of(j * rows + piece * rt, rt), rt)
            for k in range(gsz):
                wgu_b[dst, k, 0, dst_rows, :] = wg_ref[k, src_rows, :].astype(BF16)
                wgu_b[dst, k, 1, dst_rows, :] = wu_ref[k, src_rows, :].astype(BF16)
                wd_b[dst, k, dst_rows, :] = wd_ref[k, src_rows, :].astype(BF16)

    def full_gather(k, bi):
        hit = lax.broadcasted_iota(I32, (cap, n), 0) == sp_ref[bi, k]
        xs_scr[k, bi * cap:(bi + 1) * cap, :] = _dot(jnp.where(hit, 1.0, 0.0).astype(BF16), xn_ref[bi])
        gate_scr[k, bi * cap:(bi + 1) * cap, :] = jnp.broadcast_to(
            jnp.sum(jnp.where(hit, ar_ref[bi, k], 0.0), axis=1, keepdims=True), (cap, LANES))

    def experts_ffn():
        src = lax.rem(g + 1, 2)
        xs = [xs_scr[k].astype(BF16) for k in range(gsz)]
        h1 = [_dot(xs[k], wgu_b[src, k, 0]) for k in range(gsz)]
        h2 = [_dot(xs[k], wgu_b[src, k, 1]) for k in range(gsz)]
        hid = [(h1[k] * jax.nn.sigmoid(h1[k]) * h2[k]).astype(BF16) for k in range(gsz)]
        ys = [(_dot(hid[k], wd_b[src, k]) * gate_scr[k, :, 0:1]).astype(BF16) for k in range(gsz)]
        for k in range(gsz):
            for bi in range(nb):
                ys_ref[bi, k] = ys[k][bi * cap:(bi + 1) * cap]

    def full_path():
        for k in range(gsz):
            for bi in range(nb):
                full_gather(k, bi)
        experts_ffn()

    @pl.when(g == 0)
    def _():
        cast_slice()

    if not windowed:
        @pl.when(g > 0)
        def _():
            cast_slice()
            full_path()
        return

    assert nb == 1
    width = cap // 2
    wins = [[_slot_window(win_ref, j, t, jnp.maximum(g - 1, 0) * gsz + k, cap, width) for t in range(n // WIN_TILE)]
            for k in range(gsz)]
    fits = functools.reduce(jnp.logical_and, [ok for per_expert in wins for _, ok in per_expert])

    @pl.when(jnp.logical_and(g > 0, fits))
    def _():
        cast_slice()
        xs_scr[...] = jnp.zeros(xs_scr.shape, F32)
        gate_scr[...] = jnp.zeros(gate_scr.shape, F32)
        for t in range(n // WIN_TILE):
            tok = slice(t * WIN_TILE, (t + 1) * WIN_TILE)
            for k in range(gsz):
                st = wins[k][t][0]
                hit = lax.broadcasted_iota(I32, (width, WIN_TILE), 0) + st == sp_ref[0, k, :, tok]
                xs_scr[k, pl.ds(st, width), :] += _dot(jnp.where(hit, 1.0, 0.0).astype(BF16), xn_ref[0, tok, :])
                gate_scr[k, pl.ds(st, width), :] += jnp.broadcast_to(
                    jnp.sum(jnp.where(hit, ar_ref[0, k, :, tok], 0.0), axis=1, keepdims=True), (width, LANES))
        experts_ffn()

    @pl.when(jnp.logical_and(g > 0, jnp.logical_not(fits)))
    def _():
        cast_slice()
        full_path()


def _ffn(win, sp_row, aff_row, xn, wg, wu, wd, layer, cap):
    bsz, n, d = xn.shape
    _, ne, _, hid = wg.shape
    assert d == hid
    nb = min(bsz, max(1, MXU_DIM // cap))
    nj = bsz // nb
    gsz = 2 if nj > 1 else 1
    rows = d // nj
    windowed = nb == 1 and n > WIN_TILE and cap // 2 >= LANES
    batch = lambda g, j: jnp.where(g == 0, 0, j)
    row = pl.BlockSpec((nb, gsz, 1, n), lambda g, j, *_: (batch(g, j), jnp.maximum(g - 1, 0), 0, 0))
    wspec = pl.BlockSpec((None, gsz, rows, hid), lambda g, j, *_: (layer, jnp.minimum(g, ne // gsz - 1), j, 0))
    grid_spec = pltpu.PrefetchScalarGridSpec(
        num_scalar_prefetch=1,
        grid=(ne // gsz + 1, nj),
        in_specs=[row, row, pl.BlockSpec((nb, n, d), lambda g, j, *_: (batch(g, j), 0, 0)), wspec, wspec, wspec],
        out_specs=pl.BlockSpec((nb, gsz, cap, d), lambda g, j, *_: (batch(g, j), jnp.maximum(g - 1, 0), 0, 0)),
        scratch_shapes=[pltpu.VMEM((2, gsz, 2, d, hid), BF16), pltpu.VMEM((2, gsz, hid, d), BF16),
                        pltpu.VMEM((gsz, nb * cap, d), F32), pltpu.VMEM((gsz, nb * cap, LANES), F32)])
    return pl.pallas_call(
        functools.partial(_ffn_kernel, cap=cap, windowed=windowed),
        grid_spec=grid_spec,
        out_shape=jax.ShapeDtypeStruct((bsz, ne, cap, d), BF16),
        compiler_params=_params(("arbitrary", "arbitrary"), 56),
        name="ec_ffn",
    )(win, sp_row, aff_row, xn, wg, wu, wd)


def _combine_kernel(win_ref, x_ref, spc_ref, ys_ref, mod_ref, gf_ref, o_ref, *, cap, final, pack):
    b, i = pl.program_id(0), pl.program_id(1)
    tm = spc_ref.shape[0]
    sub = tm if pack == 1 else WIN_TILE

    def finish(rows, acc):
        out = x_ref[rows, :] + mod_ref[5:6, :] * acc
        if final:
            out = out * lax.rsqrt(jnp.mean(out * out, axis=-1, keepdims=True) + EPS) * gf_ref[...]
        o_ref[rows, :] = out

    def full_width(spc):
        slot = lax.broadcasted_iota(I32, (sub, cap), 1)
        acc = jnp.zeros((sub, x_ref.shape[1]), F32)
        for e in range(N_EXPERTS):
            hit = jnp.where(spc[:, e:e + 1] == slot, 1.0, 0.0).astype(BF16)
            acc = acc + _dot(hit, ys_ref[e])
        return acc

    n_sub = tm // sub
    rows = [slice(t * sub, (t + 1) * sub) for t in range(n_sub)]
    spcs = [spc_ref[r, :] for r in rows]
    if pack == 1:
        for r, spc in zip(rows, spcs):
            finish(r, full_width(spc))
        return

    w = cap // pack
    wins = [[_slot_window(win_ref, b, i * n_sub + t, e, cap, w) for e in range(N_EXPERTS)] for t in range(n_sub)]
    fits = functools.reduce(jnp.logical_and, [ok for per_tile in wins for _, ok in per_tile])

    @pl.when(fits)
    def _():
        lane = lax.broadcasted_iota(I32, (sub, cap), 1)
        lane_row = lax.broadcasted_iota(I32, (1, cap), 1)
        accs = [jnp.zeros((sub, x_ref.shape[1]), F32) for _ in range(n_sub)]
        for p in range(N_EXPERTS // pack):
            es = list(range(p * pack, (p + 1) * pack))
            hits, ysws = [], []
            for t in range(n_sub):
                col = spcs[t][:, es[0]:es[0] + 1]
                off = jnp.zeros((1, cap), I32) + wins[t][es[0]][0]
                for q, e in enumerate(es[1:], 1):
                    col = jnp.where(lane >= q * w, spcs[t][:, e:e + 1], col)
                    off = jnp.where(lane_row >= q * w, wins[t][e][0] - q * w, off)
                hits.append(jnp.where(col - off == lane, 1.0, 0.0).astype(BF16))
                ysws.append(jnp.concatenate([ys_ref[e, pl.ds(wins[t][e][0], w), :] for e in es], axis=0))
            accs = [acc + _dot(hit, ysw) for acc, hit, ysw in zip(accs, hits, ysws)]
        for r, acc in zip(rows, accs):
            finish(r, acc)

    @pl.when(jnp.logical_not(fits))
    def _():
        for r, spc in zip(rows, spcs):
            finish(r, full_width(spc))


def _combine(x, sp_col, win, ys, mod, g_final, cap, final):
    bsz, n, d = x.shape
    pack = 4 if n > WIN_TILE and cap // 4 >= LANES // 2 else 1
    tm = min(n, 1024)
    grid_spec = pltpu.PrefetchScalarGridSpec(
        num_scalar_prefetch=1,
        grid=(bsz, n // tm),
        in_specs=[pl.BlockSpec((None, tm, d), lambda b, i, *_: (b, i, 0)),
                  pl.BlockSpec((None, tm, N_EXPERTS), lambda b, i, *_: (b, i, 0)),
                  pl.BlockSpec((None, N_EXPERTS, cap, d), lambda b, i, *_: (b, 0, 0, 0)),
                  _mod_spec(mod),
                  pl.BlockSpec((1, d), lambda b, i, *_: (0, 0))],
        out_specs=pl.BlockSpec((None, tm, d), lambda b, i, *_: (b, i, 0)))
    return pl.pallas_call(
        functools.partial(_combine_kernel, cap=cap, final=final, pack=pack),
        grid_spec=grid_spec,
        out_shape=jax.ShapeDtypeStruct((bsz, n, d), F32),
        compiler_params=_params(("parallel", "parallel"), 48),
        name="ec_combine_final" if final else "ec_combine",
    )(win, x, sp_col, ys, mod[0], g_final.reshape(1, d))


def _moe(x1, xn2, aff, wg, wu, wd, layer, mod, g_final, final):
    bsz, n, _ = x1.shape
    cap = EC_CAPACITY_FACTOR * n // N_EXPERTS
    per_group = LANES // N_EXPERTS
    ng = -(-bsz // per_group)
    aff_e = aff[:, :, :N_EXPERTS]
    sets = jnp.pad(aff_e, ((0, ng * per_group - bsz), (0, 0), (0, 0)))
    sets = sets.reshape(ng, per_group, n, N_EXPERTS).transpose(0, 2, 1, 3).reshape(ng, n, LANES)
    tile = min(n, WIN_TILE)
    selpos, win = _select(sets, cap, tile)
    ungroup = lambda a: a.reshape(ng, -1, per_group, N_EXPERTS).transpose(0, 2, 1, 3).reshape(ng * per_group, -1, N_EXPERTS)[:bsz]
    sp_col = ungroup(selpos)
    win = ungroup(win[:, :2 * (n // tile)]).reshape(bsz, -1)
    sp_row = sp_col.transpose(0, 2, 1).reshape(bsz, N_EXPERTS, 1, n)
    aff_row = aff_e.transpose(0, 2, 1).reshape(bsz, N_EXPERTS, 1, n)
    ys = _ffn(win, sp_row, aff_row, xn2, wg, wu, wd, layer, cap)
    return _combine(x1, sp_col, win, ys, mod, g_final, cap, final)


def kernel(x, c, ctx, c_ctx, w_ada, b_ada, g_norm1, w_in, b_gates, g_hnorm, conv_w, conv_b, conv_ln_g, conv_ln_b,
           w_out, g_norm2, w_router, w_e_gate, w_e_up, w_e_down, g_final):
    bsz, t, d = x.shape
    depth = w_ada.shape[0]
    rows = t // GRID_W
    n_cond = 16
    cs = jnp.concatenate([c, c_ctx[None, :], jnp.zeros((n_cond - bsz - 1, d), F32)], axis=0)
    mods = _ada(cs, w_ada, b_ada).reshape(depth, n_cond, N_ADA, d)

    npair = N_HEADS // 2
    zero_states = (jnp.zeros((bsz, npair, 2, HEAD_DIM, HEAD_DIM), F32), jnp.zeros((bsz, npair, 8, HEAD_DIM), F32)) * 2
    h_ctx = ctx
    for l in range(depth):
        last = l == depth - 1
        lat, cmod = (mods, l, None), (mods, l, bsz)
        w_l = w_in[l]
        qkvg_cols = 3 * MLSTM_WIDTH + N_GATES * N_HEADS
        w_p = jnp.concatenate([w_l[:, :3 * MLSTM_WIDTH],
                               jnp.pad(w_l[:, 3 * MLSTM_WIDTH:qkvg_cols], ((0, 0), (0, LANES - N_GATES * N_HEADS))),
                               w_l[:, qkvg_cols:]], axis=1).astype(BF16)
        bg = jnp.pad(b_gates[l].reshape(1, N_HEADS * N_GATES), ((0, 0), (0, LANES - N_HEADS * N_GATES)))
        w_o = w_out[l].astype(BF16)
        w_r = jnp.pad(w_router[l], ((0, 0), (0, LANES - N_EXPERTS)))
        gh = g_hnorm[l].reshape(N_HEADS, HEAD_DIM, 1)

        def mlstm(xs, md, states, grid_rows, full):
            qkv, gates, extra = _inproj(xs, md, g_norm1[l], w_p, bg, full, grid_rows)
            return _mlstm(qkv, gates, states, gh, full, grid_rows), extra

        def sublayers(xs, md, states, grid_rows, final):
            (hr, hc, *st), (o, cacg, fr) = mlstm(xs, md, states, grid_rows, True)
            cv = _conv(cacg, conv_w[l], conv_b[l], conv_ln_g[l], conv_ln_b[l])
            fo = _fourier(fr)
            x1, xn2, aff = _outproj(xs, hr, hc, o, cv, fo, w_o, md, g_norm2[l], w_r)
            return _moe(x1, xn2, aff, w_e_gate, w_e_up, w_e_down, l, md, g_final, final), st

        if last:
            st, _ = mlstm(h_ctx, cmod, zero_states, None, False)
        else:
            h_ctx, st = sublayers(h_ctx, cmod, zero_states, None, False)
        x, _ = sublayers(x, lat, tuple(st), rows, last)
    return x
```
